```python
import math
import jax, jax.numpy as jnp
from jax import lax
import numpy as np

D_MODEL = 1024
BATCH = 1
SEQ = 16384
DEPTH = 2
DEC_BATCH = 8
DEC_SEQ = 64
PAST_LEN = 4096

CHUNK = 64
QBLOCK = 128
N_BRANCH = 4
N_HEADS = 4
HEAD_DIM = 64
BRANCH_WIDTH = N_HEADS * HEAD_DIM
MLA_D_C = 128
MLA_D_NOPE = 64
MLA_D_ROPE = 32
MLA_D_V = 64
MLA_THETA = 10000.0
ROPE_THETA = 500000.0
BAND_LEFT_CHUNKS = 8
BAND_WINDOW = BAND_LEFT_CHUNKS * CHUNK
REL_CLIP = 128
IDX_HEADS = 8
IDX_DIM = 64
DSA_TOPK = 256
D_FF = ((8 * D_MODEL // 3 + 255) // 256) * 256
ALPHA = (2 * DEPTH) ** 0.25
BETA = (8 * DEPTH) ** -0.25
NEG_INF = -1e30
LN_EPS = 1e-5
IN_SIZES = (N_HEADS * (MLA_D_NOPE + MLA_D_ROPE), MLA_D_C, MLA_D_ROPE,
            3 * BRANCH_WIDTH, 3 * BRANCH_WIDTH, 3 * BRANCH_WIDTH,
            IDX_HEADS * IDX_DIM, IDX_DIM, IDX_HEADS, N_BRANCH * D_MODEL)
IN_WIDTH = sum(IN_SIZES)

kernel_name = 'hybrid_streaming_encoder_step'


def split_cols(proj):
    offs = np.cumsum(IN_SIZES)[:-1].tolist()
    return jnp.split(proj, offs, axis=-1)


def layer_norm(x, g, b):
    xf = x.astype(jnp.float32)
    mu = jnp.mean(xf, axis=-1, keepdims=True)
    var = jnp.mean(jnp.square(xf - mu), axis=-1, keepdims=True)
    return ((xf - mu) * lax.rsqrt(var + LN_EPS)).astype(x.dtype) * g + b


def rms_norm(x, g):
    xf = x.astype(jnp.float32)
    return (xf * lax.rsqrt(jnp.mean(jnp.square(xf), axis=-1, keepdims=True) + LN_EPS)).astype(x.dtype) * g


def rope(x, pos, theta):
    r = x.shape[-1]
    half = r // 2
    inv = jnp.exp(jnp.arange(half, dtype=jnp.float32) * (-2.0 * math.log(theta) / r))
    ang = pos.astype(jnp.float32)[:, None] * inv[None, :]
    cos = jnp.cos(ang)[None, :, None, :].astype(x.dtype)
    sin = jnp.sin(ang)[None, :, None, :].astype(x.dtype)
    x1, x2 = x[..., :half], x[..., half:]
    return jnp.concatenate([x1 * cos - x2 * sin, x2 * cos + x1 * sin], axis=-1)


def partial_rope(x, pos):
    r = x.shape[-1] // 4
    return jnp.concatenate([rope(x[..., :r], pos, ROPE_THETA), x[..., r:]], axis=-1)


def chunk_visible(q_pos, k_pos):
    return (k_pos[None, :] // CHUNK) <= (q_pos[:, None] // CHUNK)


def map_query_blocks(fn, q_pos, *q_arrays):
    sq = q_pos.shape[0]
    qb = QBLOCK if sq % QBLOCK == 0 else sq
    nb = sq // qb

    def to_blocks(a):
        return jnp.moveaxis(a.reshape((a.shape[0], nb, qb) + a.shape[2:]), 1, 0)

    out = lax.map(fn, (q_pos.reshape(nb, qb),) + tuple(to_blocks(a) for a in q_arrays))
    out = jnp.moveaxis(out, 0, 1)
    return out.reshape((out.shape[0], sq) + out.shape[3:])


def mla_attention(q_nope, q_rope, k_nope, k_rope, v, q_pos, k_pos):
    scale = (MLA_D_NOPE + MLA_D_ROPE) ** -0.5

    def blk(args):
        qp, qn, qr = args
        s = (jnp.einsum('bqhd,bkhd->bhqk', qn, k_nope)
             + jnp.einsum('bqhr,bkr->bhqk', qr, k_rope)).astype(jnp.float32) * scale
        s = jnp.where(chunk_visible(qp, k_pos)[None, None], s, NEG_INF)
        p = jax.nn.softmax(s, axis=-1).astype(v.dtype)
        return jnp.einsum('bhqk,bkhd->bqhd', p, v)

    return map_query_blocks(blk, q_pos, q_nope, q_rope)


def stick_breaking_attention(q, k, v, q_pos, k_pos):
    scale = HEAD_DIM ** -0.5

    def blk(args):
        qp, qq = args
        z = jnp.einsum('bqhd,bkhd->bhqk', qq, k).astype(jnp.float32) * scale
        strict = (k_pos[None, :] < qp[:, None])[None, None]
        log_1m = jnp.where(strict, jax.nn.log_sigmoid(-z), 0.0)
        after = lax.cumsum(log_1m, axis=3, reverse=True) - log_1m
        a = jnp.where(strict, jnp.exp(jax.nn.log_sigmoid(z) + after), 0.0).astype(v.dtype)
        return jnp.einsum('bhqk,bkhd->bqhd', a, v)

    return map_query_blocks(blk, q_pos, q)


def band_attention(qc, kb, vb, q_pos, k_pos, rel_bias):
    s = jnp.einsum('bnqhd,bnkhd->bnhqk', qc, kb).astype(jnp.float32) * HEAD_DIM ** -0.5
    rel = jnp.clip(q_pos[:, :, None] - k_pos[:, None, :], -REL_CLIP, REL_CLIP) + REL_CLIP
    bias = jnp.moveaxis(rel_bias[:, rel], 0, 1).astype(jnp.float32)
    qch = q_pos[:, :, None] // CHUNK
    kch = k_pos[:, None, :] // CHUNK
    ok = (k_pos[:, None, :] >= 0) & (kch <= qch) & (kch >= qch - BAND_LEFT_CHUNKS)
    s = jnp.where(ok[None, :, None], s + bias[None], NEG_INF)
    p = jax.nn.softmax(s, axis=-1).astype(vb.dtype)
    return jnp.einsum('bnhqk,bnkhd->bnqhd', p, vb)


def band_prompt(q, kv, rel_bias):
    bsz, s_len = q.shape[0], q.shape[1]
    nc = s_len // CHUNK
    pad = BAND_LEFT_CHUNKS * CHUNK
    qc = q.reshape(bsz, nc, CHUNK, N_HEADS, HEAD_DIM)
    kvp = jnp.pad(kv, ((0, 0), (pad, 0), (0, 0), (0, 0), (0, 0)))
    kvp = kvp.reshape(bsz, nc + BAND_LEFT_CHUNKS, CHUNK, 2, N_HEADS, HEAD_DIM)
    kvb = jnp.concatenate([kvp[:, i:i + nc] for i in range(BAND_LEFT_CHUNKS + 1)], axis=2)
    pp = jnp.arange(-pad, s_len, dtype=jnp.int32).reshape(nc + BAND_LEFT_CHUNKS, CHUNK)
    kpos = jnp.concatenate([pp[i:i + nc] for i in range(BAND_LEFT_CHUNKS + 1)], axis=1)
    qpos = jnp.arange(s_len, dtype=jnp.int32).reshape(nc, CHUNK)
    out = band_attention(qc, kvb[:, :, :, 0], kvb[:, :, :, 1], qpos, kpos, rel_bias)
    return out.reshape(bsz, s_len, N_HEADS, HEAD_DIM)


def dsa_attention(q, q_idx, w_idx, kv, k_idx, q_pos, k_pos):
    n_sel = min(DSA_TOPK, k_pos.shape[0] // 4)
    scale = HEAD_DIM ** -0.5

    def blk(args):
        qp, qq, qi, wi = args
        rel = jax.nn.relu(jnp.einsum('bqjd,bkd->bqjk', qi, k_idx).astype(jnp.float32) * IDX_DIM ** -0.5)
        score = jnp.einsum('bqj,bqjk->bqk', wi.astype(jnp.float32) * IDX_HEADS ** -0.5, rel)
        vis = chunk_visible(qp, k_pos)
        score = jnp.where(vis[None], score, NEG_INF)
        _, idx = lax.top_k(score, n_sel)
        sel = jax.vmap(lambda kv_b, i_b: kv_b[i_b])(kv, idx)
        ok = (k_pos[idx] // CHUNK) <= (qp[None, :, None] // CHUNK)
        s = jnp.einsum('bqhd,bqnhd->bhqn', qq, sel[:, :, :, 0]).astype(jnp.float32) * scale
        s = jnp.where(ok[:, None], s, NEG_INF)
        p = jax.nn.softmax(s, axis=-1).astype(kv.dtype)
        return jnp.einsum('bhqn,bqnhd->bqhd', p, sel[:, :, :, 1])

    return map_query_blocks(blk, q_pos, q, q_idx, w_idx)


def trunk_layer(x, pos, past, w_in, mla_kv_norm, mla_w_uk, mla_w_uv, band_rel_bias, w_branch,
                w_out, ln1_g, ln1_b, w_gate_up, w_down, ln2_g, ln2_b):
    bsz, s_len, _ = x.shape
    (a_q, a_ckv, a_kr, sb_qkv, bd_qkv, ds_qkv, ix_q, ix_k, ix_w, gate_logits) = split_cols(x @ w_in)

    a_q = a_q.reshape(bsz, s_len, N_HEADS, MLA_D_NOPE + MLA_D_ROPE)
    q_nope = a_q[..., :MLA_D_NOPE]
    q_rope = rope(a_q[..., MLA_D_NOPE:], pos, MLA_THETA)
    k_rope_new = rope(a_kr[:, :, None, :], pos, MLA_THETA)[:, :, 0]
    lat_new = jnp.concatenate([rms_norm(a_ckv, mla_kv_norm), k_rope_new], axis=-1)

    sb = sb_qkv.reshape(bsz, s_len, 3, N_HEADS, HEAD_DIM)
    sb_q, sb_kv_new = sb[:, :, 0], sb[:, :, 1:]

    bd = bd_qkv.reshape(bsz, s_len, 3, N_HEADS, HEAD_DIM)
    bd_q, bd_kv_new = bd[:, :, 0], bd[:, :, 1:]

    ds = ds_qkv.reshape(bsz, s_len, 3, N_HEADS, HEAD_DIM)
    ds_q = partial_rope(ds[:, :, 0], pos)
    ds_kv_new = jnp.stack([partial_rope(ds[:, :, 1], pos), ds[:, :, 2]], axis=2)
    ix_q = partial_rope(ix_q.reshape(bsz, s_len, IDX_HEADS, IDX_DIM), pos)
    kidx_new = partial_rope(ix_k[:, :, None, :], pos)[:, :, 0]

    if past is None:
        k_pos = pos
        lat_all, sb_kv_all, ds_kv_all, kidx_all = lat_new, sb_kv_new, ds_kv_new, kidx_new
        o_c = band_prompt(bd_q, bd_kv_new, band_rel_bias)
        band_rows = bd_kv_new[:, s_len - min(BAND_WINDOW, s_len):]
    else:
        past_lat, past_sb, past_band, past_ds, past_kidx = past
        p_len = past_lat.shape[1]
        w_len = past_band.shape[1]
        k_pos = jnp.concatenate([jnp.arange(p_len, dtype=jnp.int32), pos])
        lat_all = jnp.concatenate([past_lat, lat_new], axis=1)
        sb_kv_all = jnp.concatenate([past_sb, sb_kv_new], axis=1)
        ds_kv_all = jnp.concatenate([past_ds, ds_kv_new], axis=1)
        kidx_all = jnp.concatenate([past_kidx, kidx_new], axis=1)
        band_kv = jnp.concatenate([past_band, bd_kv_new], axis=1)
        band_pos = jnp.concatenate([jnp.arange(p_len - w_len, p_len, dtype=jnp.int32), pos])
        o_c = band_attention(bd_q[:, None], band_kv[:, None, :, 0], band_kv[:, None, :, 1],
                             pos[None], band_pos[None], band_rel_bias)[:, 0]
        band_rows = bd_kv_new

    ckv_all, kr_all = lat_all[..., :MLA_D_C], lat_all[..., MLA_D_C:]
    k_nope = jnp.einsum('bkc,chd->bkhd', ckv_all, mla_w_uk)
    v_a = jnp.einsum('bkc,chd->bkhd', ckv_all, mla_w_uv)
    o_a = mla_attention(q_nope, q_rope, k_nope, kr_all, v_a, pos, k_pos)
    o_b = stick_breaking_attention(sb_q, sb_kv_all[:, :, 0], sb_kv_all[:, :, 1], pos, k_pos)
    o_d = dsa_attention(ds_q, ix_q, ix_w, ds_kv_all, kidx_all, pos, k_pos)

    branches = jnp.stack([o_a, o_b, o_c, o_d], axis=2).reshape(bsz, s_len, N_BRANCH, BRANCH_WIDTH)
    gates = jax.nn.sigmoid(gate_logits.reshape(bsz, s_len, N_BRANCH, D_MODEL))
    merged = jnp.einsum('bsnd,bsnd->bsd', gates, jnp.einsum('bsnv,nvd->bsnd', branches, w_branch))
    x = layer_norm(ALPHA * x + merged @ w_out, ln1_g, ln1_b)

    gu = x @ w_gate_up
    ffn = (jax.nn.silu(gu[..., :D_FF]) * gu[..., D_FF:]) @ w_down
    x = layer_norm(ALPHA * x + ffn, ln2_g, ln2_b)
    return x, (lat_new, sb_kv_new, band_rows, ds_kv_new, kidx_new)


def setup_inputs(seed: int = 0) -> dict:
    key = jax.random.key(seed)
    ks = jax.random.split(key, 20)
    f32 = jnp.float32

    def nrm(k, shape, scale):
        return jax.random.normal(k, shape, f32) * scale

    band_rows = min(BAND_WINDOW, PAST_LEN)
    return {
        'x_prompt': nrm(ks[0], (BATCH, SEQ, D_MODEL), 1.0),
        'x_sample': nrm(ks[1], (DEC_BATCH, DEC_SEQ, D_MODEL), 1.0),
        'cache_mla_latent': nrm(ks[2], (DEPTH, DEC_BATCH, PAST_LEN, MLA_D_C + MLA_D_ROPE), 1.0),
        'cache_sb_kv': nrm(ks[3], (DEPTH, DEC_BATCH, PAST_LEN, 2, N_HEADS, HEAD_DIM), 1.0),
        'cache_band_kv': nrm(ks[4], (DEPTH, DEC_BATCH, band_rows, 2, N_HEADS, HEAD_DIM), 1.0),
        'cache_dsa_kv': nrm(ks[5], (DEPTH, DEC_BATCH, PAST_LEN, 2, N_HEADS, HEAD_DIM), 1.0),
        'cache_dsa_kidx': nrm(ks[6], (DEPTH, DEC_BATCH, PAST_LEN, IDX_DIM), 1.0),
        'w_in': nrm(ks[7], (DEPTH, D_MODEL, IN_WIDTH), D_MODEL ** -0.5),
        'mla_kv_norm': 1.0 + nrm(ks[8], (DEPTH, MLA_D_C), 0.01),
        'mla_w_uk': nrm(ks[9], (DEPTH, MLA_D_C, N_HEADS, MLA_D_NOPE), MLA_D_C ** -0.5),
        'mla_w_uv': nrm(ks[10], (DEPTH, MLA_D_C, N_HEADS, MLA_D_V), MLA_D_C ** -0.5),
        'band_rel_bias': nrm(ks[11], (DEPTH, N_HEADS, 2 * REL_CLIP + 1), 0.1),
        'w_branch': nrm(ks[12], (DEPTH, N_BRANCH, BRANCH_WIDTH, D_MODEL), BRANCH_WIDTH ** -0.5),
        'w_out': nrm(ks[13], (DEPTH, D_MODEL, D_MODEL), BETA * D_MODEL ** -0.5),
        'ln1_g': 1.0 + nrm(ks[14], (DEPTH, D_MODEL), 0.01),
        'ln1_b': nrm(ks[15], (DEPTH, D_MODEL), 0.01),
        'w_gate_up': nrm(ks[16], (DEPTH, D_MODEL, 2 * D_FF), D_MODEL ** -0.5),
        'w_down': nrm(ks[17], (DEPTH, D_FF, D_MODEL), BETA * D_FF ** -0.5),
        'ln2_g': 1.0 + nrm(ks[18], (DEPTH, D_MODEL), 0.01),
        'ln2_b': nrm(ks[19], (DEPTH, D_MODEL), 0.01),
    }


def reference(x_prompt, x_sample, cache_mla_latent, cache_sb_kv, cache_band_kv, cache_dsa_kv,
              cache_dsa_kidx, w_in, mla_kv_norm, mla_w_uk, mla_w_uv, band_rel_bias, w_branch,
              w_out, ln1_g, ln1_b, w_gate_up, w_down, ln2_g, ln2_b):
    past_len = cache_mla_latent.shape[2]
    pos_p = jnp.arange(x_prompt.shape[1], dtype=jnp.int32)
    pos_s = past_len + jnp.arange(x_sample.shape[1], dtype=jnp.int32)
    xp, xs = x_prompt, x_sample
    st_p, st_s = [], []
    for l in range(DEPTH):
        params = (w_in[l], mla_kv_norm[l], mla_w_uk[l], mla_w_uv[l], band_rel_bias[l], w_branch[l],
                  w_out[l], ln1_g[l], ln1_b[l], w_gate_up[l], w_down[l], ln2_g[l], ln2_b[l])
        xp, new_p = trunk_layer(xp, pos_p, None, *params)
        past = (cache_mla_latent[l], cache_sb_kv[l], cache_band_kv[l], cache_dsa_kv[l], cache_dsa_kidx[l])
        xs, new_s = trunk_layer(xs, pos_s, past, *params)
        st_p.append(new_p)
        st_s.append(new_s)
    lat_p = jnp.stack([s[0] for s in st_p])
    sb_p = jnp.stack([s[1] for s in st_p])
    band_p = jnp.stack([s[2] for s in st_p])
    dsa_p = jnp.stack([s[3] for s in st_p])
    kidx_p = jnp.stack([s[4] for s in st_p])
    lat_s = jnp.stack([s[0] for s in st_s])
    sb_s = jnp.stack([s[1] for s in st_s])
    band_s = jnp.stack([s[2] for s in st_s])
    dsa_s = jnp.stack([s[3] for s in st_s])
    kidx_s = jnp.stack([s[4] for s in st_s])
    return (xp, xs, lat_p, sb_p, band_p, dsa_p, kidx_p, lat_s, sb_s, band_s, dsa_s, kidx_s)
```

```python
import functools
import math

import numpy as np
import jax
import jax.numpy as jnp
from jax import lax
from jax.experimental import pallas as pl
from jax.experimental.pallas import tpu as pltpu

D_MODEL = 1024
CHUNK = 64
CHUNK_SHIFT = 6
N_BRANCH = 4
N_HEADS = 4
HEAD_DIM = 64
BRANCH_WIDTH = N_HEADS * HEAD_DIM
MLA_D_C = 128
MLA_D_NOPE = 64
MLA_D_ROPE = 32
MLA_THETA = 10000.0
ROPE_THETA = 500000.0
BAND_LEFT_CHUNKS = 8
BAND_WINDOW = BAND_LEFT_CHUNKS * CHUNK
REL_CLIP = 128
IDX_HEADS = 8
IDX_DIM = 64
DSA_TOPK = 256
D_FF = ((8 * D_MODEL // 3 + 255) // 256) * 256
DEPTH = 2
ALPHA = (2 * DEPTH) ** 0.25
NEG_INF = -1e30
LN_EPS = 1e-5
IN_SIZES = (N_HEADS * (MLA_D_NOPE + MLA_D_ROPE), MLA_D_C, MLA_D_ROPE,
            3 * BRANCH_WIDTH, 3 * BRANCH_WIDTH, 3 * BRANCH_WIDTH,
            IDX_HEADS * IDX_DIM, IDX_DIM, IDX_HEADS, N_BRANCH * D_MODEL)
REST_WIDTH = sum(IN_SIZES[:-1])
GATE_WIDTH = IN_SIZES[-1]

LANES = 128
KEY_ALIGN = 512
INT_MIN = -2 ** 31
VMEM_LIMIT = 56 * 1024 * 1024

F32 = jnp.float32
BF16 = jnp.bfloat16


def _cparams(n_axes):
    return pltpu.CompilerParams(dimension_semantics=("arbitrary",) * n_axes,
                                vmem_limit_bytes=VMEM_LIMIT)


def _pick(n, candidates):
    for c in candidates:
        if n % c == 0:
            return c
    return n


def _dot_nt(a, b):
    return lax.dot_general(a, b, (((1,), (1,)), ((), ())), preferred_element_type=F32)


def _mm_kernel(a_ref, b_ref, o_ref):
    o_ref[...] = jnp.dot(a_ref[...].astype(BF16), b_ref[...], preferred_element_type=F32)


def _matmul(a, b):
    m, k = a.shape
    n = b.shape[1]
    tm = _pick(m, (1024, 512, 256, 128, 64, 32, 16, 8))
    tn = _pick(n, (512, 384, 256, 128))
    return pl.pallas_call(
        _mm_kernel,
        grid=(m // tm, n // tn),
        in_specs=[pl.BlockSpec((tm, k), lambda i, j: (i, 0)),
                  pl.BlockSpec((k, tn), lambda i, j: (0, j))],
        out_specs=pl.BlockSpec((tm, tn), lambda i, j: (i, j)),
        out_shape=jax.ShapeDtypeStruct((m, n), F32),
        compiler_params=_cparams(2),
        name="matmul",
    )(a, b)


def _mla_kernel(q_ref, k_ref, v_ref, o_ref, *, qb, kb, q_off, skp):
    q0 = pl.program_id(1) * qb
    hi = jnp.minimum((((q_off + q0 + qb - 1) >> CHUNK_SHIFT) + 1) * CHUNK, skp)
    nkb = (hi + kb - 1) // kb
    qpos = q_off + q0 + lax.broadcasted_iota(jnp.int32, (qb, 1), 0)
    cend = ((qpos >> CHUNK_SHIFT) + 1) * CHUNK
    col = lax.broadcasted_iota(jnp.int32, (qb, kb), 1)
    lane = lax.broadcasted_iota(jnp.int32, (qb, LANES), 1)
    outs = []
    for h in range(N_HEADS):
        qh = q_ref[0, :, h * LANES:(h + 1) * LANES]
        pair = h // 2

        def body(i, carry, qh=qh, h=h, pair=pair):
            m, l, acc = carry
            k0 = pl.multiple_of(i * kb, kb)
            kblk = k_ref[0, pl.ds(k0, kb), h * LANES:(h + 1) * LANES]
            s = _dot_nt(qh, kblk)
            s = jnp.where(col + k0 < cend, s, NEG_INF)
            m_new = jnp.maximum(m, jnp.max(s, axis=1, keepdims=True))
            alpha = jnp.exp(m - m_new)
            p = jnp.exp(s - m_new)
            l = alpha * l + jnp.sum(p, axis=1, keepdims=True)
            vblk = v_ref[0, pl.ds(k0, kb), pair * LANES:(pair + 1) * LANES]
            acc = alpha * acc + jnp.dot(p.astype(BF16), vblk, preferred_element_type=F32)
            return m_new, l, acc

        init = (jnp.full((qb, 1), NEG_INF, F32), jnp.zeros((qb, 1), F32),
                jnp.zeros((qb, LANES), F32))
        m, l, acc = lax.fori_loop(0, nkb, body, init)
        outs.append(acc / l)
    for pair in range(N_HEADS // 2):
        o_ref[0, :, pair * LANES:(pair + 1) * LANES] = jnp.where(
            lane < HEAD_DIM, outs[2 * pair], outs[2 * pair + 1])


def _mla_attention(q, k, v, q_off):
    b, sq, _ = q.shape
    skp = k.shape[1]
    qb = _pick(sq, (256, 128, 64))
    kb = KEY_ALIGN
    kern = functools.partial(_mla_kernel, qb=qb, kb=kb, q_off=q_off, skp=skp)
    return pl.pallas_call(
        kern,
        grid=(b, sq // qb),
        in_specs=[pl.BlockSpec((1, qb, N_HEADS * LANES), lambda bi, qi: (bi, qi, 0)),
                  pl.BlockSpec((1, skp, N_HEADS * LANES), lambda bi, qi: (bi, 0, 0),
                               pipeline_mode=pl.Buffered(1)),
                  pl.BlockSpec((1, skp, BRANCH_WIDTH), lambda bi, qi: (bi, 0, 0),
                               pipeline_mode=pl.Buffered(1))],
        out_specs=pl.BlockSpec((1, qb, BRANCH_WIDTH), lambda bi, qi: (bi, qi, 0)),
        out_shape=jax.ShapeDtypeStruct((b, sq, BRANCH_WIDTH), F32),
        compiler_params=_cparams(2),
        name="mla_attention",
    )(q, k, v)


def _head_mask(x_pair, h):
    lane = lax.broadcasted_iota(jnp.int32, x_pair.shape, 1)
    keep = (lane < HEAD_DIM) if h % 2 == 0 else (lane >= HEAD_DIM)
    return jnp.where(keep, x_pair, jnp.zeros_like(x_pair))


def _sb_kernel(q_ref, k_ref, v_ref, t_ref, o_ref, *, qb, kb, q_off):
    q0 = pl.program_id(1) * qb
    hi = q_off + q0 + qb - 1
    nkb = (hi + kb - 1) // kb
    qpos = q_off + q0 + lax.broadcasted_iota(jnp.int32, (qb, 1), 0)
    col = lax.broadcasted_iota(jnp.int32, (qb, kb), 1)
    lane = lax.broadcasted_iota(jnp.int32, (qb, LANES), 1)
    tri = t_ref[...]
    outs = []
    for h in range(N_HEADS):
        pair = h // 2
        qh = _head_mask(q_ref[0, :, pair * LANES:(pair + 1) * LANES], h)

        def body(i, carry, qh=qh, pair=pair):
            run, acc = carry
            k0 = pl.multiple_of((nkb - 1 - i) * kb, kb)
            kblk = k_ref[0, pl.ds(k0, kb), pair * LANES:(pair + 1) * LANES]
            z = _dot_nt(qh, kblk)
            strict = col + k0 < qpos
            log_1m = -(jnp.maximum(z, 0.0) + jnp.log1p(jnp.exp(-jnp.abs(z))))
            log_1m = jnp.where(strict, log_1m, 0.0)
            hi_part = log_1m.astype(BF16)
            lo_part = (log_1m - hi_part.astype(F32)).astype(BF16)
            suffix = (jnp.dot(hi_part, tri, preferred_element_type=F32)
                      + jnp.dot(lo_part, tri, preferred_element_type=F32))
            logw = z + log_1m + suffix + run
            a = jnp.where(strict, jnp.exp(logw), 0.0)
            vblk = v_ref[0, pl.ds(k0, kb), pair * LANES:(pair + 1) * LANES]
            acc = acc + jnp.dot(a.astype(BF16), vblk, preferred_element_type=F32)
            run = run + jnp.sum(log_1m, axis=1, keepdims=True)
            return run, acc

        init = (jnp.zeros((qb, 1), F32), jnp.zeros((qb, LANES), F32))
        _, acc = lax.fori_loop(0, nkb, body, init)
        outs.append(acc)
    for pair in range(N_HEADS // 2):
        o_ref[0, :, pair * LANES:(pair + 1) * LANES] = jnp.where(
            lane < HEAD_DIM, outs[2 * pair], outs[2 * pair + 1])


def _sb_attention(q, k, v, q_off):
    b, sq, _ = q.shape
    skp = k.shape[1]
    qb = _pick(sq, (256, 128, 64))
    kb = 256
    tri = jnp.asarray(np.tril(np.ones((kb, kb), np.float32), -1), BF16)
    kern = functools.partial(_sb_kernel, qb=qb, kb=kb, q_off=q_off)
    return pl.pallas_call(
        kern,
        grid=(b, sq // qb),
        in_specs=[pl.BlockSpec((1, qb, BRANCH_WIDTH), lambda bi, qi: (bi, qi, 0)),
                  pl.BlockSpec((1, skp, BRANCH_WIDTH), lambda bi, qi: (bi, 0, 0),
                               pipeline_mode=pl.Buffered(1)),
                  pl.BlockSpec((1, skp, BRANCH_WIDTH), lambda bi, qi: (bi, 0, 0),
                               pipeline_mode=pl.Buffered(1)),
                  pl.BlockSpec((kb, kb), lambda bi, qi: (0, 0))],
        out_specs=pl.BlockSpec((1, qb, BRANCH_WIDTH), lambda bi, qi: (bi, qi, 0)),
        out_shape=jax.ShapeDtypeStruct((b, sq, BRANCH_WIDTH), F32),
        compiler_params=_cparams(2),
        name="stick_breaking_attention",
    )(q, k, v, tri)


def _band_kernel(bias_ref, q_ref, k_ref, v_ref, o_ref, tile_ref, *, qb, win, q_off):
    first = (pl.program_id(0) == 0) & (pl.program_id(1) == 0)
    row = lax.broadcasted_iota(jnp.int32, (qb, win), 0)
    col = lax.broadcasted_iota(jnp.int32, (qb, win), 1)

    @pl.when(first)
    def _():
        rel = jnp.clip(row + BAND_WINDOW - col, -REL_CLIP, REL_CLIP) + REL_CLIP
        qch = row >> CHUNK_SHIFT
        kch = col >> CHUNK_SHIFT
        in_band = (kch >= qch) & (kch <= qch + BAND_LEFT_CHUNKS)
        for h in range(N_HEADS):
            tile_ref[h] = jnp.full((qb, win), bias_ref[h, 0], F32)

        def fill(r, c):
            hit = rel == r
            for h in range(N_HEADS):
                tile_ref[h] = jnp.where(hit, bias_ref[h, r], tile_ref[h])
            return c

        lax.fori_loop(1, 2 * REL_CLIP + 1, fill, 0)
        for h in range(N_HEADS):
            tile_ref[h] = jnp.where(in_band, tile_ref[h], NEG_INF)

    q0 = pl.multiple_of(pl.program_id(1) * qb, qb)
    kpos = col + (q_off + q0 - BAND_WINDOW)
    lane = lax.broadcasted_iota(jnp.int32, (qb, LANES), 1)
    outs = []
    for h in range(N_HEADS):
        pair = h // 2
        qh = _head_mask(q_ref[0, :, pair * LANES:(pair + 1) * LANES], h)
        kwin = k_ref[0, pl.ds(q0, win), pair * LANES:(pair + 1) * LANES]
        s = _dot_nt(qh, kwin) + tile_ref[h]
        s = jnp.where(kpos >= 0, s, NEG_INF)
        m = jnp.max(s, axis=1, keepdims=True)
        p = jnp.exp(s - m)
        l = jnp.sum(p, axis=1, keepdims=True)
        vwin = v_ref[0, pl.ds(q0, win), pair * LANES:(pair + 1) * LANES]
        outs.append(jnp.dot(p.astype(BF16), vwin, preferred_element_type=F32) / l)
    for pair in range(N_HEADS // 2):
        o_ref[0, :, pair * LANES:(pair + 1) * LANES] = jnp.where(
            lane < HEAD_DIM, outs[2 * pair], outs[2 * pair + 1])


def _band_attention(q, k, v, rel_bias, q_off):
    b, sq, _ = q.shape
    sk = k.shape[1]
    qb = _pick(sq, (256, 128, 64))
    win = qb + BAND_WINDOW
    kern = functools.partial(_band_kernel, qb=qb, win=win, q_off=q_off)
    return pl.pallas_call(
        kern,
        grid=(b, sq // qb),
        in_specs=[pl.BlockSpec(memory_space=pltpu.SMEM),
                  pl.BlockSpec((1, qb, BRANCH_WIDTH), lambda bi, qi: (bi, qi, 0)),
                  pl.BlockSpec((1, sk, BRANCH_WIDTH), lambda bi, qi: (bi, 0, 0),
                               pipeline_mode=pl.Buffered(1)),
                  pl.BlockSpec((1, sk, BRANCH_WIDTH), lambda bi, qi: (bi, 0, 0),
                               pipeline_mode=pl.Buffered(1))],
        out_specs=pl.BlockSpec((1, qb, BRANCH_WIDTH), lambda bi, qi: (bi, qi, 0)),
        out_shape=jax.ShapeDtypeStruct((b, sq, BRANCH_WIDTH), F32),
        scratch_shapes=[pltpu.VMEM((N_HEADS, qb, win), F32)],
        compiler_params=_cparams(2),
        name="band_attention",
    )(rel_bias, q, k, v)


def _dsa_kernel(qi_ref, w_ref, kx_ref, q_ref, k_ref, vt_ref, o_ref, key_ref, cut_ref,
                *, qb, kb, q_off, skp, n_sel):
    q0 = pl.program_id(1) * qb
    hi = jnp.minimum((((q_off + q0 + qb - 1) >> CHUNK_SHIFT) + 1) * CHUNK, skp)
    nkb = (hi + kb - 1) // kb
    qpos = q_off + q0 + lax.broadcasted_iota(jnp.int32, (1, qb), 1)
    cend = ((qpos >> CHUNK_SHIFT) + 1) * CHUNK
    row = lax.broadcasted_iota(jnp.int32, (kb, qb), 0)
    neg_key = lax.bitcast_convert_type(jnp.full((1, 1), NEG_INF, F32), jnp.int32) ^ 0x7FFFFFFF

    def score_block(i, c):
        k0 = pl.multiple_of(i * kb, kb)
        kx = kx_ref[0, pl.ds(k0, kb), :]
        score = jnp.zeros((kb, qb), F32)
        for j in range(IDX_HEADS):
            r = _dot_nt(kx, qi_ref[0, j])
            wj = w_ref[0, j:j + 1, :] * (IDX_HEADS ** -0.5)
            score = score + wj * jnp.maximum(r, 0.0)
        score = jnp.where(score == 0.0, 0.0, score)
        score = jnp.where(row + k0 < cend, score, NEG_INF)
        bits = lax.bitcast_convert_type(score, jnp.int32)
        key_ref[pl.ds(k0, kb), :] = jnp.where(bits < 0, bits ^ 0x7FFFFFFF, bits)
        return c

    lax.fori_loop(0, nkb, score_block, 0)

    def count_ge(cand):
        def blk(i, cnt):
            k0 = pl.multiple_of(i * kb, kb)
            ge = key_ref[pl.ds(k0, kb), :] >= cand
            return cnt + jnp.sum(jnp.where(ge, 1.0, 0.0), axis=0, keepdims=True)
        return lax.fori_loop(0, nkb, blk, jnp.zeros((1, qb), F32))

    def bisect(t, thr):
        bit = jnp.left_shift(jnp.int32(1), 31 - t)
        cand = thr + bit
        return jnp.where(count_ge(cand) >= n_sel, cand, thr)

    thr = lax.fori_loop(0, 32, bisect, jnp.full((1, qb), INT_MIN, jnp.int32))
    n_ge = count_ge(thr)
    n_gt = count_ge(thr + 1)
    need = n_sel - n_gt
    excess = jnp.where((n_ge - n_gt > need) & (thr > neg_key), 1.0, 0.0)
    cut_ref[...] = jnp.full((1, qb), skp, jnp.int32)

    @pl.when(jnp.max(excess) > 0.0)
    def _():
        def count_tied_below(limit):
            def blk(i, cnt):
                k0 = pl.multiple_of(i * kb, kb)
                tied = (key_ref[pl.ds(k0, kb), :] == thr) & (row + k0 < limit)
                return cnt + jnp.sum(jnp.where(tied, 1.0, 0.0), axis=0, keepdims=True)
            return lax.fori_loop(0, nkb, blk, jnp.zeros((1, qb), F32))

        n_bits = max(1, (skp - 1).bit_length())

        def bisect_cut(t, cut):
            cand = cut + jnp.left_shift(jnp.int32(1), n_bits - 1 - t)
            return jnp.where(count_tied_below(cand) < need, cand, cut)

        cut = lax.fori_loop(0, n_bits, bisect_cut, jnp.zeros((1, qb), jnp.int32))
        cut_ref[...] = jnp.where(excess > 0.0, cut + 1, skp)

    cut = cut_ref[...]

    for h in range(N_HEADS):
        pair = h // 2
        qh = _head_mask(q_ref[0, :, pair * LANES:(pair + 1) * LANES], h)

        def body(i, carry, qh=qh, pair=pair):
            m, l, acc = carry
            k0 = pl.multiple_of(i * kb, kb)
            keys = key_ref[pl.ds(k0, kb), :]
            kpos = row + k0
            sel = ((keys > thr) | ((keys == thr) & (kpos < cut))) & (kpos < cend)
            kblk = k_ref[0, pl.ds(k0, kb), pair * LANES:(pair + 1) * LANES]
            s = jnp.where(sel, _dot_nt(kblk, qh), NEG_INF)
            m_new = jnp.maximum(m, jnp.max(s, axis=0, keepdims=True))
            alpha = jnp.exp(m - m_new)
            p = jnp.where(sel, jnp.exp(s - m_new), 0.0)
            l = alpha * l + jnp.sum(p, axis=0, keepdims=True)
            vblk = vt_ref[0, pair * LANES:(pair + 1) * LANES, pl.ds(k0, kb)]
            acc = alpha * acc + jnp.dot(vblk, p.astype(BF16), preferred_element_type=F32)
            return m_new, l, acc

        init = (jnp.full((1, qb), NEG_INF, F32), jnp.zeros((1, qb), F32),
                jnp.zeros((LANES, qb), F32))
        m, l, acc = lax.fori_loop(0, nkb, body, init)
        half = (h % 2) * HEAD_DIM
        o_ref[0, h * HEAD_DIM:(h + 1) * HEAD_DIM, :] = (acc / l)[half:half + HEAD_DIM, :]


def _dsa_attention(qi, w, kx, q, k, vt, q_off, n_sel):
    b, sq, _ = q.shape
    skp = k.shape[1]
    qb = _pick(sq, (256, 128, 64))
    kb = KEY_ALIGN
    kern = functools.partial(_dsa_kernel, qb=qb, kb=kb, q_off=q_off, skp=skp, n_sel=n_sel)
    return pl.pallas_call(
        kern,
        grid=(b, sq // qb),
        in_specs=[pl.BlockSpec((1, IDX_HEADS, qb, IDX_DIM), lambda bi, qi_: (bi, 0, qi_, 0)),
                  pl.BlockSpec((1, IDX_HEADS, qb), lambda bi, qi_: (bi, 0, qi_)),
                  pl.BlockSpec((1, skp, IDX_DIM), lambda bi, qi_: (bi, 0, 0),
                               pipeline_mode=pl.Buffered(1)),
                  pl.BlockSpec((1, qb, BRANCH_WIDTH), lambda bi, qi_: (bi, qi_, 0)),
                  pl.BlockSpec((1, skp, BRANCH_WIDTH), lambda bi, qi_: (bi, 0, 0),
                               pipeline_mode=pl.Buffered(1)),
                  pl.BlockSpec((1, BRANCH_WIDTH, skp), lambda bi, qi_: (bi, 0, 0),
                               pipeline_mode=pl.Buffered(1))],
        out_specs=pl.BlockSpec((1, BRANCH_WIDTH, qb), lambda bi, qi_: (bi, 0, qi_)),
        out_shape=jax.ShapeDtypeStruct((b, BRANCH_WIDTH, sq), F32),
        scratch_shapes=[pltpu.VMEM((skp, qb), jnp.int32), pltpu.VMEM((1, qb), jnp.int32)],
        compiler_params=_cparams(2),
        name="dsa_attention",
    )(qi, w, kx, q, k, vt)


def _layer_norm(z, g, b):
    mu = jnp.mean(z, axis=-1, keepdims=True)
    zc = z - mu
    var = jnp.mean(zc * zc, axis=-1, keepdims=True)
    return zc * lax.rsqrt(var + LN_EPS) * g + b


def _merge_kernel(x_ref, gate_ref, oa_ref, ob_ref, oc_ref, od_ref, wb_ref, wo_ref, g_ref, b_ref,
                  y_ref):
    merged = None
    for n, o_ref in enumerate((oa_ref, ob_ref, oc_ref, od_ref)):
        gate = jax.nn.sigmoid(gate_ref[:, n * D_MODEL:(n + 1) * D_MODEL])
        term = gate * jnp.dot(o_ref[...].astype(BF16), wb_ref[n], preferred_element_type=F32)
        merged = term if merged is None else merged + term
    y = jnp.dot(merged.astype(BF16), wo_ref[...], preferred_element_type=F32)
    y_ref[...] = _layer_norm(ALPHA * x_ref[...] + y, g_ref[...], b_ref[...])


def _merge_out_ln(x, proj, o_a, o_b, o_c, o_d, w_branch, w_out, g, b):
    n = x.shape[0]
    ts = _pick(n, (512, 256, 128, 64))
    row = lambda i: (i, 0)
    obs = pl.BlockSpec((ts, BRANCH_WIDTH), row)
    return pl.pallas_call(
        _merge_kernel,
        grid=(n // ts,),
        in_specs=[pl.BlockSpec((ts, D_MODEL), row),
                  pl.BlockSpec((ts, GATE_WIDTH), row),
                  obs, obs, obs, obs,
                  pl.BlockSpec((N_BRANCH, BRANCH_WIDTH, D_MODEL), lambda i: (0, 0, 0)),
                  pl.BlockSpec((D_MODEL, D_MODEL), lambda i: (0, 0)),
                  pl.BlockSpec((1, D_MODEL), lambda i: (0, 0)),
                  pl.BlockSpec((1, D_MODEL), lambda i: (0, 0))],
        out_specs=pl.BlockSpec((ts, D_MODEL), row),
        out_shape=jax.ShapeDtypeStruct((n, D_MODEL), F32),
        compiler_params=_cparams(1),
        name="merge_out_ln",
    )(x, proj, o_a, o_b, o_c, o_d, w_branch, w_out, g, b)


def _ffn_kernel(x_ref, wg_ref, wu_ref, wd_ref, g_ref, b_ref, y_ref, acc_ref):
    f = pl.program_id(1)
    xb = x_ref[...].astype(BF16)
    gate = jnp.dot(xb, wg_ref[...], preferred_element_type=F32)
    up = jnp.dot(xb, wu_ref[...], preferred_element_type=F32)
    hidden = (gate * jax.nn.sigmoid(gate) * up).astype(BF16)
    part = jnp.dot(hidden, wd_ref[...], preferred_element_type=F32)

    @pl.when(f == 0)
    def _():
        acc_ref[...] = part

    @pl.when(f > 0)
    def _():
        acc_ref[...] += part

    @pl.when(f == pl.num_programs(1) - 1)
    def _():
        y_ref[...] = _layer_norm(ALPHA * x_ref[...] + acc_ref[...], g_ref[...], b_ref[...])


def _ffn_ln(x, w_gate, w_up, w_down, g, b):
    n = x.shape[0]
    ts = _pick(n, (1024, 512, 256, 128, 64))
    tf = D_FF // 2
    return pl.pallas_call(
        _ffn_kernel,
        grid=(n // ts, D_FF // tf),
        in_specs=[pl.BlockSpec((ts, D_MODEL), lambda i, f: (i, 0)),
                  pl.BlockSpec((D_MODEL, tf), lambda i, f: (0, f)),
                  pl.BlockSpec((D_MODEL, tf), lambda i, f: (0, f)),
                  pl.BlockSpec((tf, D_MODEL), lambda i, f: (f, 0)),
                  pl.BlockSpec((1, D_MODEL), lambda i, f: (0, 0)),
                  pl.BlockSpec((1, D_MODEL), lambda i, f: (0, 0))],
        out_specs=pl.BlockSpec((ts, D_MODEL), lambda i, f: (i, 0)),
        out_shape=jax.ShapeDtypeStruct((n, D_MODEL), F32),
        scratch_shapes=[pltpu.VMEM((ts, D_MODEL), F32)],
        compiler_params=_cparams(2),
        name="ffn_ln",
    )(x, w_gate, w_up, w_down, g, b)


def _rope(x, pos, theta):
    r = x.shape[-1]
    half = r // 2
    inv = jnp.exp(jnp.arange(half, dtype=F32) * (-2.0 * math.log(theta) / r))
    ang = pos.astype(F32)[:, None] * inv[None, :]
    cos = jnp.cos(ang)[None, :, None, :]
    sin = jnp.sin(ang)[None, :, None, :]
    x1, x2 = x[..., :half], x[..., half:]
    return jnp.concatenate([x1 * cos - x2 * sin, x2 * cos + x1 * sin], axis=-1)


def _partial_rope(x, pos):
    r = x.shape[-1] // 4
    return jnp.concatenate([_rope(x[..., :r], pos, ROPE_THETA), x[..., r:]], axis=-1)


def _rms_norm(x, g):
    return x * lax.rsqrt(jnp.mean(jnp.square(x), axis=-1, keepdims=True) + LN_EPS) * g


def _pad_keys(a, axis=1):
    n = a.shape[axis]
    pad = (-n) % KEY_ALIGN
    if pad == 0:
        return a
    widths = [(0, 0)] * a.ndim
    widths[axis] = (0, pad)
    return jnp.pad(a, widths)


def _prepare_weights(w_in, mla_w_uk, mla_w_uv, w_branch, w_out, w_gate_up, w_down):
    width = GATE_WIDTH + REST_WIDTH
    pad = (-width) % 512
    w_in_p = jnp.concatenate([w_in[:, REST_WIDTH:], w_in[:, :REST_WIDTH],
                              jnp.zeros((D_MODEL, pad), w_in.dtype)], axis=1).astype(BF16)
    w_ukv = jnp.concatenate([mla_w_uk.reshape(MLA_D_C, -1), mla_w_uv.reshape(MLA_D_C, -1)],
                            axis=1).astype(BF16)
    return dict(w_in=w_in_p, w_ukv=w_ukv, w_branch=w_branch.astype(BF16),
                w_out=w_out.astype(BF16), w_gate=w_gate_up[:, :D_FF].astype(BF16),
                w_up=w_gate_up[:, D_FF:].astype(BF16), w_down=w_down.astype(BF16))


def _trunk_layer(x, q_off, past, wts, mla_kv_norm, band_rel_bias, ln1_g, ln1_b, ln2_g, ln2_b):
    bsz, s_len, _ = x.shape
    n = bsz * s_len
    pos = q_off + jnp.arange(s_len, dtype=jnp.int32)
    x2 = x.reshape(n, D_MODEL)
    proj = _matmul(x2, wts['w_in'])
    rest = proj[:, GATE_WIDTH:GATE_WIDTH + REST_WIDTH].reshape(bsz, s_len, REST_WIDTH)
    offs = np.cumsum(IN_SIZES[:-1])[:-1].tolist()
    a_q, a_ckv, a_kr, sb_qkv, bd_qkv, ds_qkv, ix_q, ix_k, ix_w = jnp.split(rest, offs, axis=-1)

    a_q = a_q.reshape(bsz, s_len, N_HEADS, MLA_D_NOPE + MLA_D_ROPE)
    q_rope = _rope(a_q[..., MLA_D_NOPE:], pos, MLA_THETA)
    mla_scale = (MLA_D_NOPE + MLA_D_ROPE) ** -0.5
    q_a = jnp.concatenate([a_q[..., :MLA_D_NOPE], q_rope,
                           jnp.zeros((bsz, s_len, N_HEADS, LANES - MLA_D_NOPE - MLA_D_ROPE), F32)],
                          axis=-1) * mla_scale
    q_a = q_a.reshape(bsz, s_len, N_HEADS * LANES).astype(BF16)
    k_rope_new = _rope(a_kr[:, :, None, :], pos, MLA_THETA)[:, :, 0]
    lat_new = jnp.concatenate([_rms_norm(a_ckv, mla_kv_norm), k_rope_new], axis=-1)

    sb = sb_qkv.reshape(bsz, s_len, 3, N_HEADS, HEAD_DIM)
    sb_kv_new = sb[:, :, 1:]
    bd = bd_qkv.reshape(bsz, s_len, 3, N_HEADS, HEAD_DIM)
    bd_kv_new = bd[:, :, 1:]
    ds = ds_qkv.reshape(bsz, s_len, 3, N_HEADS, HEAD_DIM)
    ds_kv_new = jnp.stack([_partial_rope(ds[:, :, 1], pos), ds[:, :, 2]], axis=2)
    kidx_new = _partial_rope(ix_k[:, :, None, :], pos)[:, :, 0]

    head_scale = HEAD_DIM ** -0.5
    q_b = (sb[:, :, 0].reshape(bsz, s_len, BRANCH_WIDTH) * head_scale).astype(BF16)
    q_c = (bd[:, :, 0].reshape(bsz, s_len, BRANCH_WIDTH) * head_scale).astype(BF16)
    q_d = (_partial_rope(ds[:, :, 0], pos).reshape(bsz, s_len, BRANCH_WIDTH) * head_scale).astype(BF16)
    q_ix = _partial_rope(ix_q.reshape(bsz, s_len, IDX_HEADS, IDX_DIM), pos) * IDX_DIM ** -0.5
    q_ix = jnp.transpose(q_ix, (0, 2, 1, 3)).astype(BF16)
    w_ix = jnp.transpose(ix_w, (0, 2, 1))

    if past is None:
        lat_all, sb_kv_all, ds_kv_all, kidx_all = lat_new, sb_kv_new, ds_kv_new, kidx_new
        band_kv = jnp.pad(bd_kv_new, ((0, 0), (BAND_WINDOW, 0), (0, 0), (0, 0), (0, 0)))
        band_rows = bd_kv_new[:, s_len - min(BAND_WINDOW, s_len):]
    else:
        past_lat, past_sb, past_band, past_ds, past_kidx = past
        lat_all = jnp.concatenate([past_lat, lat_new], axis=1)
        sb_kv_all = jnp.concatenate([past_sb, sb_kv_new], axis=1)
        ds_kv_all = jnp.concatenate([past_ds, ds_kv_new], axis=1)
        kidx_all = jnp.concatenate([past_kidx, kidx_new], axis=1)
        band_kv = jnp.concatenate([past_band, bd_kv_new], axis=1)
        band_rows = bd_kv_new
    s_k = lat_all.shape[1]

    ckv_all = lat_all[..., :MLA_D_C].reshape(bsz * s_k, MLA_D_C)
    kv_a = _matmul(ckv_all, wts['w_ukv']).reshape(bsz, s_k, 2, N_HEADS, HEAD_DIM)
    kr_all = jnp.broadcast_to(lat_all[:, :, None, MLA_D_C:], (bsz, s_k, N_HEADS, MLA_D_ROPE))
    k_a = jnp.concatenate([kv_a[:, :, 0], kr_all,
                           jnp.zeros((bsz, s_k, N_HEADS, LANES - MLA_D_NOPE - MLA_D_ROPE), F32)],
                          axis=-1).reshape(bsz, s_k, N_HEADS * LANES)
    k_a = _pad_keys(k_a.astype(BF16))
    v_a = _pad_keys(kv_a[:, :, 1].reshape(bsz, s_k, BRANCH_WIDTH).astype(BF16))
    o_a = _mla_attention(q_a, k_a, v_a, q_off)

    k_b = _pad_keys(sb_kv_all[:, :, 0].reshape(bsz, s_k, BRANCH_WIDTH).astype(BF16))
    v_b = _pad_keys(sb_kv_all[:, :, 1].reshape(bsz, s_k, BRANCH_WIDTH).astype(BF16))
    o_b = _sb_attention(q_b, k_b, v_b, q_off)

    s_band = band_kv.shape[1]
    k_c = band_kv[:, :, 0].reshape(bsz, s_band, BRANCH_WIDTH).astype(BF16)
    v_c = band_kv[:, :, 1].reshape(bsz, s_band, BRANCH_WIDTH).astype(BF16)
    o_c = _band_attention(q_c, k_c, v_c, band_rel_bias, q_off)

    k_d = _pad_keys(ds_kv_all[:, :, 0].reshape(bsz, s_k, BRANCH_WIDTH).astype(BF16))
    vt_d = _pad_keys(jnp.transpose(ds_kv_all[:, :, 1].reshape(bsz, s_k, BRANCH_WIDTH),
                                   (0, 2, 1)).astype(BF16), axis=2)
    kx_d = _pad_keys(kidx_all.astype(BF16))
    n_sel = min(DSA_TOPK, s_k // 4)
    o_d = jnp.transpose(_dsa_attention(q_ix, w_ix, kx_d, q_d, k_d, vt_d, q_off, n_sel), (0, 2, 1))

    flat = lambda o: o.reshape(n, BRANCH_WIDTH)
    x1 = _merge_out_ln(x2, proj, flat(o_a), flat(o_b), flat(o_c), flat(o_d),
                       wts['w_branch'], wts['w_out'], ln1_g[None], ln1_b[None])
    x_out = _ffn_ln(x1, wts['w_gate'], wts['w_up'], wts['w_down'], ln2_g[None], ln2_b[None])
    return x_out.reshape(bsz, s_len, D_MODEL), (lat_new, sb_kv_new, band_rows, ds_kv_new, kidx_new)


def kernel(x_prompt, x_sample, cache_mla_latent, cache_sb_kv, cache_band_kv, cache_dsa_kv, cache_dsa_kidx, w_in, mla_kv_norm, mla_w_uk, mla_w_uv, band_rel_bias, w_branch, w_out, ln1_g, ln1_b, w_gate_up, w_down, ln2_g, ln2_b):
    past_len = cache_mla_latent.shape[2]
    xp, xs = x_prompt, x_sample
    st_p, st_s = [], []
    for l in range(w_in.shape[0]):
        wts = _prepare_weights(w_in[l], mla_w_uk[l], mla_w_uv[l], w_branch[l], w_out[l],
                               w_gate_up[l], w_down[l])
        params = (wts, mla_kv_norm[l], band_rel_bias[l], ln1_g[l], ln1_b[l], ln2_g[l], ln2_b[l])
        xp, new_p = _trunk_layer(xp, 0, None, *params)
        past = (cache_mla_latent[l], cache_sb_kv[l], cache_band_kv[l], cache_dsa_kv[l],
                cache_dsa_kidx[l])
        xs, new_s = _trunk_layer(xs, past_len, past, *params)
        st_p.append(new_p)
        st_s.append(new_s)
    stack = lambda st, i: jnp.stack([s[i] for s in st])
    return (xp, xs) + tuple(stack(st_p, i) for i in range(5)) + tuple(stack(st_s, i) for i in range(5))
```

```python
import functools
import math

import numpy as np
import jax
import jax.numpy as jnp
from jax import lax
from jax.experimental import pallas as pl
from jax.experimental.pallas import tpu as pltpu

D_MODEL = 1024
CHUNK = 64
CHUNK_SHIFT = 6
N_BRANCH = 4
N_HEADS = 4
HEAD_DIM = 64
BRANCH_WIDTH = N_HEADS * HEAD_DIM
MLA_D_C = 128
MLA_D_NOPE = 64
MLA_D_ROPE = 32
MLA_THETA = 10000.0
ROPE_THETA = 500000.0
BAND_LEFT_CHUNKS = 8
BAND_WINDOW = BAND_LEFT_CHUNKS * CHUNK
REL_CLIP = 128
IDX_HEADS = 8
IDX_DIM = 64
DSA_TOPK = 256
D_FF = ((8 * D_MODEL // 3 + 255) // 256) * 256
DEPTH = 2
ALPHA = (2 * DEPTH) ** 0.25
NEG_INF = -1e30
LN_EPS = 1e-5
IN_SIZES = (N_HEADS * (MLA_D_NOPE + MLA_D_ROPE), MLA_D_C, MLA_D_ROPE,
            3 * BRANCH_WIDTH, 3 * BRANCH_WIDTH, 3 * BRANCH_WIDTH,
            IDX_HEADS * IDX_DIM, IDX_DIM, IDX_HEADS, N_BRANCH * D_MODEL)
REST_WIDTH = sum(IN_SIZES[:-1])
GATE_WIDTH = IN_SIZES[-1]

LANES = 128
KEY_ALIGN = 512
INT_MIN = -2 ** 31
VMEM_LIMIT = 56 * 1024 * 1024

F32 = jnp.float32
BF16 = jnp.bfloat16


def _cparams(n_axes):
    return pltpu.CompilerParams(dimension_semantics=("arbitrary",) * n_axes,
                                vmem_limit_bytes=VMEM_LIMIT)


def _pick(n, candidates):
    for c in candidates:
        if n % c == 0:
            return c
    return n


def _dot_nt(a, b):
    return lax.dot_general(a, b, (((1,), (1,)), ((), ())), preferred_element_type=F32)


def _mm_kernel(a_ref, b_ref, o_ref):
    o_ref[...] = jnp.dot(a_ref[...].astype(BF16), b_ref[...], preferred_element_type=F32)


def _matmul(a, b):
    m, k = a.shape
    n = b.shape[1]
    tm = _pick(m, (1024, 512, 256, 128, 64, 32, 16, 8))
    tn = _pick(n, (512, 384, 256, 128))
    return pl.pallas_call(
        _mm_kernel,
        grid=(m // tm, n // tn),
        in_specs=[pl.BlockSpec((tm, k), lambda i, j: (i, 0)),
                  pl.BlockSpec((k, tn), lambda i, j: (0, j))],
        out_specs=pl.BlockSpec((tm, tn), lambda i, j: (i, j)),
        out_shape=jax.ShapeDtypeStruct((m, n), F32),
        compiler_params=_cparams(2),
        name="matmul",
    )(a, b)


def _mla_kernel(q_ref, k_ref, v_ref, o_ref, *, qb, kb, q_off, skp):
    q0 = pl.program_id(1) * qb
    hi = jnp.minimum((((q_off + q0 + qb - 1) >> CHUNK_SHIFT) + 1) * CHUNK, skp)
    nkb = (hi + kb - 1) // kb
    qpos = q_off + q0 + lax.broadcasted_iota(jnp.int32, (qb, 1), 0)
    cend = ((qpos >> CHUNK_SHIFT) + 1) * CHUNK
    col = lax.broadcasted_iota(jnp.int32, (qb, kb), 1)
    lane = lax.broadcasted_iota(jnp.int32, (qb, LANES), 1)
    n_full = jnp.minimum(((((q_off + q0) >> CHUNK_SHIFT) + 1) * CHUNK) // kb, nkb)

    def body(i, carry, masked):
        k0 = pl.multiple_of(i * kb, kb)
        s = [_dot_nt(q_ref[0, :, h * LANES:(h + 1) * LANES],
                     k_ref[0, pl.ds(k0, kb), h * LANES:(h + 1) * LANES]) for h in range(N_HEADS)]
        if masked:
            vis = col + k0 < cend
            s = [jnp.where(vis, sh, NEG_INF) for sh in s]
        p, new = [], []
        for h in range(N_HEADS):
            m, l, acc = carry[3 * h:3 * h + 3]
            m_new = jnp.maximum(m, jnp.max(s[h], axis=1, keepdims=True))
            alpha = jnp.exp(m - m_new)
            ph = jnp.exp(s[h] - m_new)
            p.append(ph.astype(BF16))
            new += [m_new, alpha * l + jnp.sum(ph, axis=1, keepdims=True), alpha * acc]
        for h in range(N_HEADS):
            pair = h // 2
            vblk = v_ref[0, pl.ds(k0, kb), pair * LANES:(pair + 1) * LANES]
            new[3 * h + 2] = new[3 * h + 2] + jnp.dot(p[h], vblk, preferred_element_type=F32)
        return tuple(new)

    init = (jnp.full((qb, 1), NEG_INF, F32), jnp.zeros((qb, 1), F32),
            jnp.zeros((qb, LANES), F32)) * N_HEADS
    carry = lax.fori_loop(0, n_full, functools.partial(body, masked=False), init)
    carry = lax.fori_loop(n_full, nkb, functools.partial(body, masked=True), carry)
    outs = [carry[3 * h + 2] / carry[3 * h + 1] for h in range(N_HEADS)]
    for pair in range(N_HEADS // 2):
        o_ref[0, :, pair * LANES:(pair + 1) * LANES] = jnp.where(
            lane < HEAD_DIM, outs[2 * pair], outs[2 * pair + 1])


def _mla_attention(q, k, v, q_off):
    b, sq, _ = q.shape
    skp = k.shape[1]
    qb = _pick(sq, (256, 128, 64))
    kb = _pick(skp, (1024, KEY_ALIGN))
    kern = functools.partial(_mla_kernel, qb=qb, kb=kb, q_off=q_off, skp=skp)
    return pl.pallas_call(
        kern,
        grid=(b, sq // qb),
        in_specs=[pl.BlockSpec((1, qb, N_HEADS * LANES), lambda bi, qi: (bi, qi, 0)),
                  pl.BlockSpec((1, skp, N_HEADS * LANES), lambda bi, qi: (bi, 0, 0),
                               pipeline_mode=pl.Buffered(1)),
                  pl.BlockSpec((1, skp, BRANCH_WIDTH), lambda bi, qi: (bi, 0, 0),
                               pipeline_mode=pl.Buffered(1))],
        out_specs=pl.BlockSpec((1, qb, BRANCH_WIDTH), lambda bi, qi: (bi, qi, 0)),
        out_shape=jax.ShapeDtypeStruct((b, sq, BRANCH_WIDTH), F32),
        compiler_params=_cparams(2),
        name="mla_attention",
    )(q, k, v)


def _head_mask(x_pair, h):
    lane = lax.broadcasted_iota(jnp.int32, x_pair.shape, 1)
    keep = (lane < HEAD_DIM) if h % 2 == 0 else (lane >= HEAD_DIM)
    return jnp.where(keep, x_pair, jnp.zeros_like(x_pair))


def _sb_kernel(q_ref, k_ref, v_ref, t_ref, o_ref, *, qb, kb, sub, q_off):
    q0 = pl.program_id(1) * qb
    hi = q_off + q0 + qb - 1
    nkb = (hi + kb - 1) // kb
    qpos = q_off + q0 + lax.broadcasted_iota(jnp.int32, (qb, 1), 0)
    col = lax.broadcasted_iota(jnp.int32, (qb, sub), 1)
    lane = lax.broadcasted_iota(jnp.int32, (qb, LANES), 1)
    tri = t_ref[...]
    n_full = jnp.minimum((q_off + q0) // kb, nkb)
    qh = [_head_mask(q_ref[0, :, (h // 2) * LANES:(h // 2 + 1) * LANES], h) for h in range(N_HEADS)]

    def body(i, carry, masked):
        k0 = pl.multiple_of(i * kb, kb)
        carry = list(carry)
        units = [(j, h) for j in reversed(range(kb // sub)) for h in range(N_HEADS)]
        strict = {j: col + (k0 + j * sub) < qpos for j in range(kb // sub)} if masked else None
        z = {}
        for j, h in units:
            pair = h // 2
            kblk = k_ref[0, pl.ds(k0 + j * sub, sub), pair * LANES:(pair + 1) * LANES]
            z[j, h] = _dot_nt(qh[h], kblk)
        log_1m, suffix = {}, {}
        for u in units:
            t = -(jnp.maximum(z[u], 0.0) + jnp.log(1.0 + jnp.exp(-jnp.abs(z[u]))))
            if masked:
                t = jnp.where(strict[u[0]], t, 0.0)
            log_1m[u] = t
        for u in units:
            hi_part = log_1m[u].astype(BF16)
            lo_part = (log_1m[u] - hi_part.astype(F32)).astype(BF16)
            suffix[u] = (jnp.dot(hi_part, tri, preferred_element_type=F32)
                         + jnp.dot(lo_part, tri, preferred_element_type=F32))
        for j, h in units:
            u = (j, h)
            run, acc = carry[2 * h:2 * h + 2]
            a = jnp.exp(z[u] + log_1m[u] + suffix[u] + run)
            if masked:
                a = jnp.where(strict[j], a, 0.0)
            pair = h // 2
            vblk = v_ref[0, pl.ds(k0 + j * sub, sub), pair * LANES:(pair + 1) * LANES]
            carry[2 * h + 1] = acc + jnp.dot(a.astype(BF16), vblk, preferred_element_type=F32)
            carry[2 * h] = run + jnp.sum(log_1m[u], axis=1, keepdims=True)
        return tuple(carry)

    def reversed_body(lo, n, masked):
        return lambda i, carry: body(lo + n - 1 - i, carry, masked)

    init = (jnp.zeros((qb, 1), F32), jnp.zeros((qb, LANES), F32)) * N_HEADS
    carry = lax.fori_loop(0, nkb - n_full, reversed_body(n_full, nkb - n_full, True), init)
    carry = lax.fori_loop(0, n_full, reversed_body(0, n_full, False), carry)
    outs = [carry[2 * h + 1] for h in range(N_HEADS)]
    for pair in range(N_HEADS // 2):
        o_ref[0, :, pair * LANES:(pair + 1) * LANES] = jnp.where(
            lane < HEAD_DIM, outs[2 * pair], outs[2 * pair + 1])


def _sb_attention(q, k, v, q_off):
    b, sq, _ = q.shape
    skp = k.shape[1]
    qb = _pick(sq, (256, 128, 64))
    sub = 256
    kb = KEY_ALIGN
    tri = jnp.asarray(np.tril(np.ones((sub, sub), np.float32), -1), BF16)
    kern = functools.partial(_sb_kernel, qb=qb, kb=kb, sub=sub, q_off=q_off)
    return pl.pallas_call(
        kern,
        grid=(b, sq // qb),
        in_specs=[pl.BlockSpec((1, qb, BRANCH_WIDTH), lambda bi, qi: (bi, qi, 0)),
                  pl.BlockSpec((1, skp, BRANCH_WIDTH), lambda bi, qi: (bi, 0, 0),
                               pipeline_mode=pl.Buffered(1)),
                  pl.BlockSpec((1, skp, BRANCH_WIDTH), lambda bi, qi: (bi, 0, 0),
                               pipeline_mode=pl.Buffered(1)),
                  pl.BlockSpec((sub, sub), lambda bi, qi: (0, 0))],
        out_specs=pl.BlockSpec((1, qb, BRANCH_WIDTH), lambda bi, qi: (bi, qi, 0)),
        out_shape=jax.ShapeDtypeStruct((b, sq, BRANCH_WIDTH), F32),
        compiler_params=_cparams(2),
        name="stick_breaking_attention",
    )(q, k, v, tri)


def _band_kernel(bias_ref, q_ref, k_ref, v_ref, o_ref, tile_ref, *, qb, win, q_off):
    first = (pl.program_id(0) == 0) & (pl.program_id(1) == 0)
    row = lax.broadcasted_iota(jnp.int32, (qb, win), 0)
    col = lax.broadcasted_iota(jnp.int32, (qb, win), 1)

    @pl.when(first)
    def _():
        rel = jnp.clip(row + BAND_WINDOW - col, -REL_CLIP, REL_CLIP) + REL_CLIP
        qch = row >> CHUNK_SHIFT
        kch = col >> CHUNK_SHIFT
        in_band = (kch >= qch) & (kch <= qch + BAND_LEFT_CHUNKS)
        for h in range(N_HEADS):
            tile_ref[h] = jnp.full((qb, win), bias_ref[h, 0], F32)

        def fill(r, c):
            hit = rel == r
            for h in range(N_HEADS):
                tile_ref[h] = jnp.where(hit, bias_ref[h, r], tile_ref[h])
            return c

        lax.fori_loop(1, 2 * REL_CLIP + 1, fill, 0)
        for h in range(N_HEADS):
            tile_ref[h] = jnp.where(in_band, tile_ref[h], NEG_INF)

    q0 = pl.multiple_of(pl.program_id(1) * qb, qb)
    kpos = col + (q_off + q0 - BAND_WINDOW)
    lane = lax.broadcasted_iota(jnp.int32, (qb, LANES), 1)
    outs = []
    for h in range(N_HEADS):
        pair = h // 2
        qh = _head_mask(q_ref[0, :, pair * LANES:(pair + 1) * LANES], h)
        kwin = k_ref[0, pl.ds(q0, win), pair * LANES:(pair + 1) * LANES]
        s = _dot_nt(qh, kwin) + tile_ref[h]
        s = jnp.where(kpos >= 0, s, NEG_INF)
        m = jnp.max(s, axis=1, keepdims=True)
        p = jnp.exp(s - m)
        l = jnp.sum(p, axis=1, keepdims=True)
        vwin = v_ref[0, pl.ds(q0, win), pair * LANES:(pair + 1) * LANES]
        outs.append(jnp.dot(p.astype(BF16), vwin, preferred_element_type=F32) / l)
    for pair in range(N_HEADS // 2):
        o_ref[0, :, pair * LANES:(pair + 1) * LANES] = jnp.where(
            lane < HEAD_DIM, outs[2 * pair], outs[2 * pair + 1])


def _band_attention(q, k, v, rel_bias, q_off):
    b, sq, _ = q.shape
    sk = k.shape[1]
    qb = _pick(sq, (256, 128, 64))
    win = qb + BAND_WINDOW
    kern = functools.partial(_band_kernel, qb=qb, win=win, q_off=q_off)
    return pl.pallas_call(
        kern,
        grid=(b, sq // qb),
        in_specs=[pl.BlockSpec(memory_space=pltpu.SMEM),
                  pl.BlockSpec((1, qb, BRANCH_WIDTH), lambda bi, qi: (bi, qi, 0)),
                  pl.BlockSpec((1, sk, BRANCH_WIDTH), lambda bi, qi: (bi, 0, 0),
                               pipeline_mode=pl.Buffered(1)),
                  pl.BlockSpec((1, sk, BRANCH_WIDTH), lambda bi, qi: (bi, 0, 0),
                               pipeline_mode=pl.Buffered(1))],
        out_specs=pl.BlockSpec((1, qb, BRANCH_WIDTH), lambda bi, qi: (bi, qi, 0)),
        out_shape=jax.ShapeDtypeStruct((b, sq, BRANCH_WIDTH), F32),
        scratch_shapes=[pltpu.VMEM((N_HEADS, qb, win), F32)],
        compiler_params=_cparams(2),
        name="band_attention",
    )(rel_bias, q, k, v)


def _dsa_kernel(qi_ref, w_ref, kx_ref, q_ref, k_ref, vt_ref, o_ref, key_ref, cut_ref,
                *, qb, kb, q_off, skp, n_sel):
    q0 = pl.program_id(1) * qb
    hi = jnp.minimum((((q_off + q0 + qb - 1) >> CHUNK_SHIFT) + 1) * CHUNK, skp)
    nkb = (hi + kb - 1) // kb
    qpos = q_off + q0 + lax.broadcasted_iota(jnp.int32, (1, qb), 1)
    cend = ((qpos >> CHUNK_SHIFT) + 1) * CHUNK
    row = lax.broadcasted_iota(jnp.int32, (kb, qb), 0)
    neg_key = lax.bitcast_convert_type(jnp.full((1, 1), NEG_INF, F32), jnp.int32) ^ 0x7FFFFFFF

    def score_block(i, c):
        k0 = pl.multiple_of(i * kb, kb)
        kx = kx_ref[0, pl.ds(k0, kb), :]
        r = [_dot_nt(kx, qi_ref[0, j]) for j in range(IDX_HEADS)]
        score = jnp.zeros((kb, qb), F32)
        for j in range(IDX_HEADS):
            wj = w_ref[0, j:j + 1, :] * (IDX_HEADS ** -0.5)
            score = score + wj * jnp.maximum(r[j], 0.0)
        score = jnp.where(score == 0.0, 0.0, score)
        score = jnp.where(row + k0 < cend, score, NEG_INF)
        bits = lax.bitcast_convert_type(score, jnp.int32)
        key_ref[pl.ds(k0, kb), :] = jnp.where(bits < 0, bits ^ 0x7FFFFFFF, bits)
        return c

    lax.fori_loop(0, nkb, score_block, 0)

    slab = 64 if kb % 64 == 0 else kb

    def count_ge(cand):
        def blk(i, cnt):
            k0 = pl.multiple_of(i * kb, kb)
            for s in range(kb // slab):
                keys = key_ref[pl.ds(k0 + s * slab, slab), :]
                cnt = cnt + jnp.where(keys >= cand, 1.0, 0.0)
            return cnt
        part = lax.fori_loop(0, nkb, blk, jnp.zeros((slab, qb), F32))
        return jnp.sum(part, axis=0, keepdims=True)

    def undecided(state):
        t, _, n_ge = state
        return (t < 32) & (jnp.max(jnp.abs(n_ge - n_sel)) > 0.0)

    def bisect(state):
        t, thr, n_ge = state
        cand = thr + jnp.left_shift(jnp.int32(1), 31 - t)
        cnt = count_ge(cand)
        ok = cnt >= n_sel
        return t + 1, jnp.where(ok, cand, thr), jnp.where(ok, cnt, n_ge)

    stored = (nkb * kb).astype(F32)
    _, thr, n_ge = lax.while_loop(
        undecided, bisect,
        (jnp.int32(0), jnp.full((1, qb), INT_MIN, jnp.int32), jnp.full((1, qb), stored, F32)))
    excess = jnp.where((n_ge > n_sel) & (thr > neg_key), 1.0, 0.0)
    cut_ref[...] = jnp.full((1, qb), skp, jnp.int32)

    @pl.when(jnp.max(excess) > 0.0)
    def _():
        need = n_sel - count_ge(thr + 1)

        def count_tied_below(limit):
            def blk(i, cnt):
                k0 = pl.multiple_of(i * kb, kb)
                tied = (key_ref[pl.ds(k0, kb), :] == thr) & (row + k0 < limit)
                return cnt + jnp.sum(jnp.where(tied, 1.0, 0.0), axis=0, keepdims=True)
            return lax.fori_loop(0, nkb, blk, jnp.zeros((1, qb), F32))

        n_bits = max(1, (skp - 1).bit_length())

        def bisect_cut(t, cut):
            cand = cut + jnp.left_shift(jnp.int32(1), n_bits - 1 - t)
            return jnp.where(count_tied_below(cand) < need, cand, cut)

        cut = lax.fori_loop(0, n_bits, bisect_cut, jnp.zeros((1, qb), jnp.int32))
        cut_ref[...] = jnp.where(excess > 0.0, cut + 1, skp)

    cut = cut_ref[...]

    def mask_block(i, c):
        k0 = pl.multiple_of(i * kb, kb)
        keys = key_ref[pl.ds(k0, kb), :]
        kpos = row + k0
        sel = (keys >= thr) & ((keys != thr) | (kpos < cut)) & (kpos < cend)
        key_ref[pl.ds(k0, kb), :] = lax.bitcast_convert_type(jnp.where(sel, 0.0, NEG_INF), jnp.int32)
        return c

    lax.fori_loop(0, nkb, mask_block, 0)

    qh = [_head_mask(q_ref[0, :, (h // 2) * LANES:(h // 2 + 1) * LANES], h) for h in range(N_HEADS)]

    def body(i, carry):
        k0 = pl.multiple_of(i * kb, kb)
        bias = lax.bitcast_convert_type(key_ref[pl.ds(k0, kb), :], F32)
        s = [_dot_nt(k_ref[0, pl.ds(k0, kb), (h // 2) * LANES:(h // 2 + 1) * LANES], qh[h])
             for h in range(N_HEADS)]
        p, new = [], []
        for h in range(N_HEADS):
            m, l, acc = carry[3 * h:3 * h + 3]
            sh = s[h] + bias
            m_new = jnp.maximum(m, jnp.max(sh, axis=0, keepdims=True))
            alpha = jnp.exp(m - m_new)
            ph = jnp.exp(sh - m_new)
            p.append(ph.astype(BF16))
            new += [m_new, alpha * l + jnp.sum(ph, axis=0, keepdims=True), alpha * acc]
        for h in range(N_HEADS):
            pair = h // 2
            vblk = vt_ref[0, pair * LANES:(pair + 1) * LANES, pl.ds(k0, kb)]
            new[3 * h + 2] = new[3 * h + 2] + jnp.dot(vblk, p[h], preferred_element_type=F32)
        return tuple(new)

    init = (jnp.full((1, qb), NEG_INF, F32), jnp.zeros((1, qb), F32),
            jnp.zeros((LANES, qb), F32)) * N_HEADS
    carry = lax.fori_loop(0, nkb, body, init)
    for h in range(N_HEADS):
        half = (h % 2) * HEAD_DIM
        out = carry[3 * h + 2] / carry[3 * h + 1]
        o_ref[0, h * HEAD_DIM:(h + 1) * HEAD_DIM, :] = out[half:half + HEAD_DIM, :]


def _dsa_attention(qi, w, kx, q, k, vt, q_off, n_sel):
    b, sq, _ = q.shape
    skp = k.shape[1]
    qb = _pick(sq, (256, 128, 64))
    kb = _pick(skp, (1024, KEY_ALIGN))
    kern = functools.partial(_dsa_kernel, qb=qb, kb=kb, q_off=q_off, skp=skp, n_sel=n_sel)
    return pl.pallas_call(
        kern,
        grid=(b, sq // qb),
        in_specs=[pl.BlockSpec((1, IDX_HEADS, qb, IDX_DIM), lambda bi, qi_: (bi, 0, qi_, 0)),
                  pl.BlockSpec((1, IDX_HEADS, qb), lambda bi, qi_: (bi, 0, qi_)),
                  pl.BlockSpec((1, skp, IDX_DIM), lambda bi, qi_: (bi, 0, 0),
                               pipeline_mode=pl.Buffered(1)),
                  pl.BlockSpec((1, qb, BRANCH_WIDTH), lambda bi, qi_: (bi, qi_, 0)),
                  pl.BlockSpec((1, skp, BRANCH_WIDTH), lambda bi, qi_: (bi, 0, 0),
                               pipeline_mode=pl.Buffered(1)),
                  pl.BlockSpec((1, BRANCH_WIDTH, skp), lambda bi, qi_: (bi, 0, 0),
                               pipeline_mode=pl.Buffered(1))],
        out_specs=pl.BlockSpec((1, BRANCH_WIDTH, qb), lambda bi, qi_: (bi, 0, qi_)),
        out_shape=jax.ShapeDtypeStruct((b, BRANCH_WIDTH, sq), F32),
        scratch_shapes=[pltpu.VMEM((skp, qb), jnp.int32), pltpu.VMEM((1, qb), jnp.int32)],
        compiler_params=_cparams(2),
        name="dsa_attention",
    )(qi, w, kx, q, k, vt)


def _layer_norm(z, g, b):
    mu = jnp.mean(z, axis=-1, keepdims=True)
    zc = z - mu
    var = jnp.mean(zc * zc, axis=-1, keepdims=True)
    return zc * lax.rsqrt(var + LN_EPS) * g + b


def _merge_kernel(x_ref, gate_ref, oa_ref, ob_ref, oc_ref, od_ref, wb_ref, wo_ref, g_ref, b_ref,
                  y_ref):
    merged = None
    for n, o_ref in enumerate((oa_ref, ob_ref, oc_ref, od_ref)):
        gate = jax.nn.sigmoid(gate_ref[:, n * D_MODEL:(n + 1) * D_MODEL])
        term = gate * jnp.dot(o_ref[...].astype(BF16), wb_ref[n], preferred_element_type=F32)
        merged = term if merged is None else merged + term
    y = jnp.dot(merged.astype(BF16), wo_ref[...], preferred_element_type=F32)
    y_ref[...] = _layer_norm(ALPHA * x_ref[...] + y, g_ref[...], b_ref[...])


def _merge_out_ln(x, proj, o_a, o_b, o_c, o_d, w_branch, w_out, g, b):
    n = x.shape[0]
    ts = _pick(n, (512, 256, 128, 64))
    row = lambda i: (i, 0)
    obs = pl.BlockSpec((ts, BRANCH_WIDTH), row)
    return pl.pallas_call(
        _merge_kernel,
        grid=(n // ts,),
        in_specs=[pl.BlockSpec((ts, D_MODEL), row),
                  pl.BlockSpec((ts, GATE_WIDTH), row),
                  obs, obs, obs, obs,
                  pl.BlockSpec((N_BRANCH, BRANCH_WIDTH, D_MODEL), lambda i: (0, 0, 0)),
                  pl.BlockSpec((D_MODEL, D_MODEL), lambda i: (0, 0)),
                  pl.BlockSpec((1, D_MODEL), lambda i: (0, 0)),
                  pl.BlockSpec((1, D_MODEL), lambda i: (0, 0))],
        out_specs=pl.BlockSpec((ts, D_MODEL), row),
        out_shape=jax.ShapeDtypeStruct((n, D_MODEL), F32),
        compiler_params=_cparams(1),
        name="merge_out_ln",
    )(x, proj, o_a, o_b, o_c, o_d, w_branch, w_out, g, b)


def _ffn_kernel(x_ref, wg_ref, wu_ref, wd_ref, g_ref, b_ref, y_ref, acc_ref):
    f = pl.program_id(1)
    xb = x_ref[...].astype(BF16)
    gate = jnp.dot(xb, wg_ref[...], preferred_element_type=F32)
    up = jnp.dot(xb, wu_ref[...], preferred_element_type=F32)
    hidden = (gate * jax.nn.sigmoid(gate) * up).astype(BF16)
    part = jnp.dot(hidden, wd_ref[...], preferred_element_type=F32)

    @pl.when(f == 0)
    def _():
        acc_ref[...] = part

    @pl.when(f > 0)
    def _():
        acc_ref[...] += part

    @pl.when(f == pl.num_programs(1) - 1)
    def _():
        y_ref[...] = _layer_norm(ALPHA * x_ref[...] + acc_ref[...], g_ref[...], b_ref[...])


def _ffn_ln(x, w_gate, w_up, w_down, g, b):
    n = x.shape[0]
    ts = _pick(n, (1024, 512, 256, 128, 64))
    tf = D_FF // 2
    return pl.pallas_call(
        _ffn_kernel,
        grid=(n // ts, D_FF // tf),
        in_specs=[pl.BlockSpec((ts, D_MODEL), lambda i, f: (i, 0)),
                  pl.BlockSpec((D_MODEL, tf), lambda i, f: (0, f)),
                  pl.BlockSpec((D_MODEL, tf), lambda i, f: (0, f)),
                  pl.BlockSpec((tf, D_MODEL), lambda i, f: (f, 0)),
                  pl.BlockSpec((1, D_MODEL), lambda i, f: (0, 0)),
                  pl.BlockSpec((1, D_MODEL), lambda i, f: (0, 0))],
        out_specs=pl.BlockSpec((ts, D_MODEL), lambda i, f: (i, 0)),
        out_shape=jax.ShapeDtypeStruct((n, D_MODEL), F32),
        scratch_shapes=[pltpu.VMEM((ts, D_MODEL), F32)],
        compiler_params=_cparams(2),
        name="ffn_ln",
    )(x, w_gate, w_up, w_down, g, b)


def _rope(x, pos, theta):
    r = x.shape[-1]
    half = r // 2
    inv = jnp.exp(jnp.arange(half, dtype=F32) * (-2.0 * math.log(theta) / r))
    ang = pos.astype(F32)[:, None] * inv[None, :]
    cos = jnp.cos(ang)[None, :, None, :]
    sin = jnp.sin(ang)[None, :, None, :]
    x1, x2 = x[..., :half], x[..., half:]
    return jnp.concatenate([x1 * cos - x2 * sin, x2 * cos + x1 * sin], axis=-1)


def _partial_rope(x, pos):
    r = x.shape[-1] // 4
    return jnp.concatenate([_rope(x[..., :r], pos, ROPE_THETA), x[..., r:]], axis=-1)


def _rms_norm(x, g):
    return x * lax.rsqrt(jnp.mean(jnp.square(x), axis=-1, keepdims=True) + LN_EPS) * g


def _pad_keys(a, axis=1):
    n = a.shape[axis]
    pad = (-n) % KEY_ALIGN
    if pad == 0:
        return a
    widths = [(0, 0)] * a.ndim
    widths[axis] = (0, pad)
    return jnp.pad(a, widths)


def _prepare_weights(w_in, mla_w_uk, mla_w_uv, w_branch, w_out, w_gate_up, w_down):
    width = GATE_WIDTH + REST_WIDTH
    pad = (-width) % 512
    w_in_p = jnp.concatenate([w_in[:, REST_WIDTH:], w_in[:, :REST_WIDTH],
                              jnp.zeros((D_MODEL, pad), w_in.dtype)], axis=1).astype(BF16)
    w_ukv = jnp.concatenate([mla_w_uk.reshape(MLA_D_C, -1), mla_w_uv.reshape(MLA_D_C, -1)],
                            axis=1).astype(BF16)
    return dict(w_in=w_in_p, w_ukv=w_ukv, w_branch=w_branch.astype(BF16),
                w_out=w_out.astype(BF16), w_gate=w_gate_up[:, :D_FF].astype(BF16),
                w_up=w_gate_up[:, D_FF:].astype(BF16), w_down=w_down.astype(BF16))


def _trunk_layer(x, q_off, past, wts, mla_kv_norm, band_rel_bias, ln1_g, ln1_b, ln2_g, ln2_b):
    bsz, s_len, _ = x.shape
    n = bsz * s_len
    pos = q_off + jnp.arange(s_len, dtype=jnp.int32)
    x2 = x.reshape(n, D_MODEL)
    proj = _matmul(x2, wts['w_in'])
    rest = proj[:, GATE_WIDTH:GATE_WIDTH + REST_WIDTH].reshape(bsz, s_len, REST_WIDTH)
    offs = np.cumsum(IN_SIZES[:-1])[:-1].tolist()
    a_q, a_ckv, a_kr, sb_qkv, bd_qkv, ds_qkv, ix_q, ix_k, ix_w = jnp.split(rest, offs, axis=-1)

    a_q = a_q.reshape(bsz, s_len, N_HEADS, MLA_D_NOPE + MLA_D_ROPE)
    q_rope = _rope(a_q[..., MLA_D_NOPE:], pos, MLA_THETA)
    mla_scale = (MLA_D_NOPE + MLA_D_ROPE) ** -0.5
    q_a = jnp.concatenate([a_q[..., :MLA_D_NOPE], q_rope,
                           jnp.zeros((bsz, s_len, N_HEADS, LANES - MLA_D_NOPE - MLA_D_ROPE), F32)],
                          axis=-1) * mla_scale
    q_a = q_a.reshape(bsz, s_len, N_HEADS * LANES).astype(BF16)
    k_rope_new = _rope(a_kr[:, :, None, :], pos, MLA_THETA)[:, :, 0]
    lat_new = jnp.concatenate([_rms_norm(a_ckv, mla_kv_norm), k_rope_new], axis=-1)

    sb = sb_qkv.reshape(bsz, s_len, 3, N_HEADS, HEAD_DIM)
    sb_kv_new = sb[:, :, 1:]
    bd = bd_qkv.reshape(bsz, s_len, 3, N_HEADS, HEAD_DIM)
    bd_kv_new = bd[:, :, 1:]
    ds = ds_qkv.reshape(bsz, s_len, 3, N_HEADS, HEAD_DIM)
    ds_kv_new = jnp.stack([_partial_rope(ds[:, :, 1], pos), ds[:, :, 2]], axis=2)
    kidx_new = _partial_rope(ix_k[:, :, None, :], pos)[:, :, 0]

    head_scale = HEAD_DIM ** -0.5
    q_b = (sb[:, :, 0].reshape(bsz, s_len, BRANCH_WIDTH) * head_scale).astype(BF16)
    q_c = (bd[:, :, 0].reshape(bsz, s_len, BRANCH_WIDTH) * head_scale).astype(BF16)
    q_d = (_partial_rope(ds[:, :, 0], pos).reshape(bsz, s_len, BRANCH_WIDTH) * head_scale).astype(BF16)
    q_ix = _partial_rope(ix_q.reshape(bsz, s_len, IDX_HEADS, IDX_DIM), pos) * IDX_DIM ** -0.5
    q_ix = jnp.transpose(q_ix, (0, 2, 1, 3)).astype(BF16)
    w_ix = jnp.transpose(ix_w, (0, 2, 1))

    if past is None:
        lat_all, sb_kv_all, ds_kv_all, kidx_all = lat_new, sb_kv_new, ds_kv_new, kidx_new
        band_kv = jnp.pad(bd_kv_new, ((0, 0), (BAND_WINDOW, 0), (0, 0), (0, 0), (0, 0)))
        band_rows = bd_kv_new[:, s_len - min(BAND_WINDOW, s_len):]
    else:
        past_lat, past_sb, past_band, past_ds, past_kidx = past
        lat_all = jnp.concatenate([past_lat, lat_new], axis=1)
        sb_kv_all = jnp.concatenate([past_sb, sb_kv_new], axis=1)
        ds_kv_all = jnp.concatenate([past_ds, ds_kv_new], axis=1)
        kidx_all = jnp.concatenate([past_kidx, kidx_new], axis=1)
        band_kv = jnp.concatenate([past_band, bd_kv_new], axis=1)
        band_rows = bd_kv_new
    s_k = lat_all.shape[1]

    ckv_all = lat_all[..., :MLA_D_C].reshape(bsz * s_k, MLA_D_C)
    kv_a = _matmul(ckv_all, wts['w_ukv']).reshape(bsz, s_k, 2, N_HEADS, HEAD_DIM)
    kr_all = jnp.broadcast_to(lat_all[:, :, None, MLA_D_C:], (bsz, s_k, N_HEADS, MLA_D_ROPE))
    k_a = jnp.concatenate([kv_a[:, :, 0], kr_all,
                           jnp.zeros((bsz, s_k, N_HEADS, LANES - MLA_D_NOPE - MLA_D_ROPE), F32)],
                          axis=-1).reshape(bsz, s_k, N_HEADS * LANES)
    k_a = _pad_keys(k_a.astype(BF16))
    v_a = _pad_keys(kv_a[:, :, 1].reshape(bsz, s_k, BRANCH_WIDTH).astype(BF16))
    o_a = _mla_attention(q_a, k_a, v_a, q_off)

    k_b = _pad_keys(sb_kv_all[:, :, 0].reshape(bsz, s_k, BRANCH_WIDTH).astype(BF16))
    v_b = _pad_keys(sb_kv_all[:, :, 1].reshape(bsz, s_k, BRANCH_WIDTH).astype(BF16))
    o_b = _sb_attention(q_b, k_b, v_b, q_off)

    s_band = band_kv.shape[1]
    k_c = band_kv[:, :, 0].reshape(bsz, s_band, BRANCH_WIDTH).astype(BF16)
    v_c = band_kv[:, :, 1].reshape(bsz, s_band, BRANCH_WIDTH).astype(BF16)
    o_c = _band_attention(q_c, k_c, v_c, band_rel_bias, q_off)

    k_d = _pad_keys(ds_kv_all[:, :, 0].reshape(bsz, s_k, BRANCH_WIDTH).astype(BF16))
    vt_d = _pad_keys(jnp.transpose(ds_kv_all[:, :, 1].reshape(bsz, s_k, BRANCH_WIDTH),
                                   (0, 2, 1)).astype(BF16), axis=2)
    kx_d = _pad_keys(kidx_all.astype(BF16))
    n_sel = min(DSA_TOPK, s_k // 4)
    o_d = jnp.transpose(_dsa_attention(q_ix, w_ix, kx_d, q_d, k_d, vt_d, q_off, n_sel), (0, 2, 1))

    flat = lambda o: o.reshape(n, BRANCH_WIDTH)
    x1 = _merge_out_ln(x2, proj, flat(o_a), flat(o_b), flat(o_c), flat(o_d),
                       wts['w_branch'], wts['w_out'], ln1_g[None], ln1_b[None])
    x_out = _ffn_ln(x1, wts['w_gate'], wts['w_up'], wts['w_down'], ln2_g[None], ln2_b[None])
    return x_out.reshape(bsz, s_len, D_MODEL), (lat_new, sb_kv_new, band_rows, ds_kv_new, kidx_new)


def kernel(x_prompt, x_sample, cache_mla_latent, cache_sb_kv, cache_band_kv, cache_dsa_kv, cache_dsa_kidx, w_in, mla_kv_norm, mla_w_uk, mla_w_uv, band_rel_bias, w_branch, w_out, ln1_g, ln1_b, w_gate_up, w_down, ln2_g, ln2_b):
    past_len = cache_mla_latent.shape[2]
    xp, xs = x_prompt, x_sample
    st_p, st_s = [], []
    for l in range(w_in.shape[0]):
        wts = _prepare_weights(w_in[l], mla_w_uk[l], mla_w_uv[l], w_branch[l], w_out[l],
                               w_gate_up[l], w_down[l])
        params = (wts, mla_kv_norm[l], band_rel_bias[l], ln1_g[l], ln1_b[l], ln2_g[l], ln2_b[l])
        xp, new_p = _trunk_layer(xp, 0, None, *params)
        past = (cache_mla_latent[l], cache_sb_kv[l], cache_band_kv[l], cache_dsa_kv[l],
                cache_dsa_kidx[l])
        xs, new_s = _trunk_layer(xs, past_len, past, *params)
        st_p.append(new_p)
        st_s.append(new_s)
    stack = lambda st, i: jnp.stack([s[i] for s in st])
    return (xp, xs) + tuple(stack(st_p, i) for i in range(5)) + tuple(stack(st_s, i) for i in range(5))
```

```python
import functools
import math

import numpy as np
import jax
import jax.numpy as jnp
from jax import lax
from jax.experimental import pallas as pl
from jax.experimental.pallas import tpu as pltpu

D_MODEL = 1024
CHUNK = 64
CHUNK_SHIFT = 6
N_BRANCH = 4
N_HEADS = 4
HEAD_DIM = 64
BRANCH_WIDTH = N_HEADS * HEAD_DIM
MLA_D_C = 128
MLA_D_NOPE = 64
MLA_D_ROPE = 32
MLA_THETA = 10000.0
ROPE_THETA = 500000.0
BAND_LEFT_CHUNKS = 8
BAND_WINDOW = BAND_LEFT_CHUNKS * CHUNK
REL_CLIP = 128
IDX_HEADS = 8
IDX_DIM = 64
DSA_TOPK = 256
D_FF = ((8 * D_MODEL // 3 + 255) // 256) * 256
DEPTH = 2
ALPHA = (2 * DEPTH) ** 0.25
NEG_INF = -1e30
LN_EPS = 1e-5
IN_SIZES = (N_HEADS * (MLA_D_NOPE + MLA_D_ROPE), MLA_D_C, MLA_D_ROPE,
            3 * BRANCH_WIDTH, 3 * BRANCH_WIDTH, 3 * BRANCH_WIDTH,
            IDX_HEADS * IDX_DIM, IDX_DIM, IDX_HEADS, N_BRANCH * D_MODEL)
REST_WIDTH = sum(IN_SIZES[:-1])
GATE_WIDTH = IN_SIZES[-1]

LANES = 128
KEY_ALIGN = 512
INT_MIN = -2 ** 31
VMEM_LIMIT = 56 * 1024 * 1024

F32 = jnp.float32
BF16 = jnp.bfloat16


def _cparams(n_axes):
    return pltpu.CompilerParams(dimension_semantics=("arbitrary",) * n_axes,
                                vmem_limit_bytes=VMEM_LIMIT)


def _pick(n, candidates):
    for c in candidates:
        if n % c == 0:
            return c
    return n


def _dot_nt(a, b):
    return lax.dot_general(a, b, (((1,), (1,)), ((), ())), preferred_element_type=F32)


def _mm_kernel(a_ref, b_ref, o_ref):
    o_ref[...] = jnp.dot(a_ref[...].astype(BF16), b_ref[...], preferred_element_type=F32)


def _matmul(a, b):
    m, k = a.shape
    n = b.shape[1]
    tm = _pick(m, (1024, 512, 256, 128, 64, 32, 16, 8))
    tn = _pick(n, (512, 384, 256, 128))
    return pl.pallas_call(
        _mm_kernel,
        grid=(m // tm, n // tn),
        in_specs=[pl.BlockSpec((tm, k), lambda i, j: (i, 0)),
                  pl.BlockSpec((k, tn), lambda i, j: (0, j))],
        out_specs=pl.BlockSpec((tm, tn), lambda i, j: (i, j)),
        out_shape=jax.ShapeDtypeStruct((m, n), F32),
        compiler_params=_cparams(2),
        name="matmul",
    )(a, b)


SEG_AQ = 0
SEG_CKV = 4
SEG_KR = 5
SEG_SB = 6
SEG_BD = 12
SEG_DS = 18
SEG_IXQ = 24
SEG_IXK = 28
N_SEG = 30
IXW_LANE = 96
PROJ_WIDTH = N_SEG * LANES
MLA_ROPE_LANE = 64
LAT_WIDTH = MLA_D_C + MLA_D_ROPE


def _proj_kernel(x_ref, w_ref, wukv_ref, g_ref, ca_ref, sma_ref, spa_ref, cp_ref, smp_ref, spp_ref,
                 qa_ref, lat_ref, ka_ref, va_ref,
                 qb_ref, kvb_ref, sbkv_ref, qc_ref, kvc_ref, bdkv_ref,
                 qd_ref, kd_ref, vtd_ref, dskv_ref, qix_ref, kidx_ref, kx_ref, wix_ref):
    proj = jnp.dot(x_ref[...].astype(BF16), w_ref[...], preferred_element_type=F32)
    seg = lambda s, n=1: proj[:, s * LANES:(s + n) * LANES]

    def rope(t, c_ref, sm_ref, sp_ref, half):
        return (t * c_ref[...] + pltpu.roll(t, LANES - half, 1) * sm_ref[...]
                + pltpu.roll(t, half, 1) * sp_ref[...])

    rope_a = functools.partial(rope, c_ref=ca_ref, sm_ref=sma_ref, sp_ref=spa_ref, half=MLA_D_ROPE // 2)
    rope_p = functools.partial(rope, c_ref=cp_ref, sm_ref=smp_ref, sp_ref=spp_ref, half=HEAD_DIM // 8)
    head_scale = HEAD_DIM ** -0.5

    mla_scale = (MLA_D_NOPE + MLA_D_ROPE) ** -0.5
    for h in range(N_HEADS):
        qa_ref[:, h * LANES:(h + 1) * LANES] = (rope_a(seg(SEG_AQ + h)) * mla_scale).astype(BF16)
    ckv = seg(SEG_CKV)
    ckv = ckv * lax.rsqrt(jnp.mean(ckv * ckv, axis=-1, keepdims=True) + LN_EPS) * g_ref[...]
    kr = rope_a(seg(SEG_KR))
    lat_ref[:, :MLA_D_C] = ckv
    lat_ref[:, MLA_D_C:] = kr[:, MLA_ROPE_LANE:MLA_ROPE_LANE + MLA_D_ROPE]
    kv_a = jnp.dot(ckv.astype(BF16), wukv_ref[...], preferred_element_type=F32)
    for h in range(N_HEADS):
        ka_ref[:, h * LANES:(h + 1) * LANES] = (kv_a[:, h * LANES:(h + 1) * LANES] + kr).astype(BF16)
    va_ref[...] = kv_a[:, N_HEADS * LANES:].astype(BF16)

    for s0, q_ref, kv_ref, new_ref in ((SEG_SB, qb_ref, kvb_ref, sbkv_ref),
                                       (SEG_BD, qc_ref, kvc_ref, bdkv_ref)):
        q_ref[...] = (seg(s0, 2) * head_scale).astype(BF16)
        kv = seg(s0 + 2, 4)
        new_ref[...] = kv
        kv_ref[...] = kv.astype(BF16)

    for p in range(2):
        qd_ref[:, p * LANES:(p + 1) * LANES] = (rope_p(seg(SEG_DS + p)) * head_scale).astype(BF16)
        k_rot = rope_p(seg(SEG_DS + 2 + p))
        dskv_ref[:, p * LANES:(p + 1) * LANES] = k_rot
        kd_ref[:, p * LANES:(p + 1) * LANES] = k_rot.astype(BF16)
    v_d = seg(SEG_DS + 4, 2)
    dskv_ref[:, BRANCH_WIDTH:] = v_d
    vtd_ref[...] = v_d.T.astype(BF16)
    for p in range(IDX_HEADS // 2):
        qix_ref[:, p * LANES:(p + 1) * LANES] = (rope_p(seg(SEG_IXQ + p)) * IDX_DIM ** -0.5).astype(BF16)
    ixk = rope_p(seg(SEG_IXK))
    kidx_ref[...] = ixk[:, :IDX_DIM]
    kx_ref[...] = ixk[:, :IDX_DIM].astype(BF16)
    wix_ref[...] = ixk.T[IXW_LANE:IXW_LANE + IDX_HEADS, :]


def _rope_tables(pos, theta, width, starts):
    half = width // 2
    inv = jnp.exp(jnp.arange(half, dtype=F32) * (-2.0 * math.log(theta) / width))
    ang = pos.astype(F32)[:, None] * inv[None, :]
    cos, sin = jnp.cos(ang), jnp.sin(ang)
    n = pos.shape[0]
    c = jnp.ones((n, LANES), F32)
    sm = jnp.zeros((n, LANES), F32)
    sp = jnp.zeros((n, LANES), F32)
    for s in starts:
        c = c.at[:, s:s + half].set(cos).at[:, s + half:s + width].set(cos)
        sm = sm.at[:, s:s + half].set(-sin)
        sp = sp.at[:, s + half:s + width].set(sin)
    return c, sm, sp


def _project(x, pos, w_rest, w_ukv, kv_norm):
    n = x.shape[0]
    ts = _pick(n, (512, 256, 128, 64))
    tables = (_rope_tables(pos, MLA_THETA, MLA_D_ROPE, (MLA_ROPE_LANE,))
              + _rope_tables(pos, ROPE_THETA, HEAD_DIM // 4, (0, HEAD_DIM)))
    row = lambda i: (i, 0)
    fixed = lambda i: (0, 0)
    rows = lambda w: pl.BlockSpec((ts, w), row)
    out = lambda w, dt: jax.ShapeDtypeStruct((n, w), dt)
    specs = [
        (N_HEADS * LANES, BF16), (LAT_WIDTH, F32), (N_HEADS * LANES, BF16), (BRANCH_WIDTH, BF16),
        (BRANCH_WIDTH, BF16), (2 * BRANCH_WIDTH, BF16), (2 * BRANCH_WIDTH, F32),
        (BRANCH_WIDTH, BF16), (2 * BRANCH_WIDTH, BF16), (2 * BRANCH_WIDTH, F32),
        (BRANCH_WIDTH, BF16), (BRANCH_WIDTH, BF16), None, (2 * BRANCH_WIDTH, F32),
        (IDX_HEADS * IDX_DIM, BF16), (IDX_DIM, F32), (IDX_DIM, BF16), None]
    out_specs, out_shape = [], []
    for k, spec in enumerate(specs):
        if spec is None:
            rows_t, dt = ((BRANCH_WIDTH, BF16), (IDX_HEADS, F32))[k > 12]
            out_specs.append(pl.BlockSpec((rows_t, ts), lambda i: (0, i)))
            out_shape.append(jax.ShapeDtypeStruct((rows_t, n), dt))
        else:
            out_specs.append(rows(spec[0]))
            out_shape.append(out(*spec))
    return pl.pallas_call(
        _proj_kernel,
        grid=(n // ts,),
        in_specs=[rows(D_MODEL),
                  pl.BlockSpec((D_MODEL, PROJ_WIDTH), fixed, pipeline_mode=pl.Buffered(1)),
                  pl.BlockSpec((MLA_D_C, (N_HEADS + 2) * LANES), fixed),
                  pl.BlockSpec((1, MLA_D_C), fixed)] + [rows(LANES)] * 6,
        out_specs=out_specs,
        out_shape=out_shape,
        compiler_params=_cparams(1),
        name="project_prepare",
    )(x, w_rest, w_ukv, kv_norm, *tables)


def _mla_kernel(q_ref, k_ref, v_ref, o_ref, *, qb, kb, q_off, skp):
    q0 = pl.program_id(1) * qb
    hi = jnp.minimum((((q_off + q0 + qb - 1) >> CHUNK_SHIFT) + 1) * CHUNK, skp)
    nkb = (hi + kb - 1) // kb
    qpos = q_off + q0 + lax.broadcasted_iota(jnp.int32, (qb, 1), 0)
    cend = ((qpos >> CHUNK_SHIFT) + 1) * CHUNK
    col = lax.broadcasted_iota(jnp.int32, (qb, kb), 1)
    lane = lax.broadcasted_iota(jnp.int32, (qb, LANES), 1)
    n_full = jnp.minimum(((((q_off + q0) >> CHUNK_SHIFT) + 1) * CHUNK) // kb, nkb)

    def body(i, carry, masked):
        k0 = pl.multiple_of(i * kb, kb)
        s = [_dot_nt(q_ref[0, :, h * LANES:(h + 1) * LANES],
                     k_ref[0, pl.ds(k0, kb), h * LANES:(h + 1) * LANES]) for h in range(N_HEADS)]
        if masked:
            vis = col + k0 < cend
            s = [jnp.where(vis, sh, NEG_INF) for sh in s]
        p, new = [], []
        for h in range(N_HEADS):
            m, l, acc = carry[3 * h:3 * h + 3]
            m_new = jnp.maximum(m, jnp.max(s[h], axis=1, keepdims=True))
            alpha = jnp.exp(m - m_new)
            ph = jnp.exp(s[h] - m_new)
            p.append(ph.astype(BF16))
            new += [m_new, alpha * l + jnp.sum(ph, axis=1, keepdims=True), alpha * acc]
        for h in range(N_HEADS):
            pair = h // 2
            vblk = v_ref[0, pl.ds(k0, kb), pair * LANES:(pair + 1) * LANES]
            new[3 * h + 2] = new[3 * h + 2] + jnp.dot(p[h], vblk, preferred_element_type=F32)
        return tuple(new)

    init = (jnp.full((qb, 1), NEG_INF, F32), jnp.zeros((qb, 1), F32),
            jnp.zeros((qb, LANES), F32)) * N_HEADS
    carry = lax.fori_loop(0, n_full, functools.partial(body, masked=False), init)
    carry = lax.fori_loop(n_full, nkb, functools.partial(body, masked=True), carry)
    outs = [carry[3 * h + 2] / carry[3 * h + 1] for h in range(N_HEADS)]
    for pair in range(N_HEADS // 2):
        o_ref[0, :, pair * LANES:(pair + 1) * LANES] = jnp.where(
            lane < HEAD_DIM, outs[2 * pair], outs[2 * pair + 1])


def _mla_attention(q, k, v, q_off):
    b, sq, _ = q.shape
    skp = k.shape[1]
    qb = _pick(sq, (256, 128, 64))
    kb = _pick(skp, (1024, KEY_ALIGN))
    kern = functools.partial(_mla_kernel, qb=qb, kb=kb, q_off=q_off, skp=skp)
    return pl.pallas_call(
        kern,
        grid=(b, sq // qb),
        in_specs=[pl.BlockSpec((1, qb, N_HEADS * LANES), lambda bi, qi: (bi, qi, 0)),
                  pl.BlockSpec((1, skp, N_HEADS * LANES), lambda bi, qi: (bi, 0, 0),
                               pipeline_mode=pl.Buffered(1)),
                  pl.BlockSpec((1, skp, BRANCH_WIDTH), lambda bi, qi: (bi, 0, 0),
                               pipeline_mode=pl.Buffered(1))],
        out_specs=pl.BlockSpec((1, qb, BRANCH_WIDTH), lambda bi, qi: (bi, qi, 0)),
        out_shape=jax.ShapeDtypeStruct((b, sq, BRANCH_WIDTH), F32),
        compiler_params=_cparams(2),
        name="mla_attention",
    )(q, k, v)


def _head_mask(x_pair, h):
    lane = lax.broadcasted_iota(jnp.int32, x_pair.shape, 1)
    keep = (lane < HEAD_DIM) if h % 2 == 0 else (lane >= HEAD_DIM)
    return jnp.where(keep, x_pair, jnp.zeros_like(x_pair))


def _sb_kernel(q_ref, kv_ref, t_ref, o_ref, *, qb, kb, sub, q_off):
    q0 = pl.program_id(1) * qb
    hi = q_off + q0 + qb - 1
    nkb = (hi + kb - 1) // kb
    qpos = q_off + q0 + lax.broadcasted_iota(jnp.int32, (qb, 1), 0)
    col = lax.broadcasted_iota(jnp.int32, (qb, sub), 1)
    lane = lax.broadcasted_iota(jnp.int32, (qb, LANES), 1)
    tri = t_ref[...]
    n_full = jnp.minimum((q_off + q0) // kb, nkb)
    qh = [_head_mask(q_ref[0, :, (h // 2) * LANES:(h // 2 + 1) * LANES], h) for h in range(N_HEADS)]

    def body(i, carry, masked):
        k0 = pl.multiple_of(i * kb, kb)
        carry = list(carry)
        units = [(j, h) for j in reversed(range(kb // sub)) for h in range(N_HEADS)]
        strict = {j: col + (k0 + j * sub) < qpos for j in range(kb // sub)} if masked else None
        z = {}
        for j, h in units:
            pair = h // 2
            kblk = kv_ref[0, pl.ds(k0 + j * sub, sub), pair * LANES:(pair + 1) * LANES]
            z[j, h] = _dot_nt(qh[h], kblk)
        log_1m, suffix = {}, {}
        for u in units:
            t = -(jnp.maximum(z[u], 0.0) + jnp.log(1.0 + jnp.exp(-jnp.abs(z[u]))))
            if masked:
                t = jnp.where(strict[u[0]], t, 0.0)
            log_1m[u] = t
        for u in units:
            hi_part = log_1m[u].astype(BF16)
            lo_part = (log_1m[u] - hi_part.astype(F32)).astype(BF16)
            suffix[u] = (jnp.dot(hi_part, tri, preferred_element_type=F32)
                         + jnp.dot(lo_part, tri, preferred_element_type=F32))
        for j, h in units:
            u = (j, h)
            run, acc = carry[2 * h:2 * h + 2]
            a = jnp.exp(z[u] + log_1m[u] + suffix[u] + run)
            if masked:
                a = jnp.where(strict[j], a, 0.0)
            pair = h // 2
            vblk = kv_ref[0, pl.ds(k0 + j * sub, sub),
                          BRANCH_WIDTH + pair * LANES:BRANCH_WIDTH + (pair + 1) * LANES]
            carry[2 * h + 1] = acc + jnp.dot(a.astype(BF16), vblk, preferred_element_type=F32)
            carry[2 * h] = run + jnp.sum(log_1m[u], axis=1, keepdims=True)
        return tuple(carry)

    def reversed_body(lo, n, masked):
        return lambda i, carry: body(lo + n - 1 - i, carry, masked)

    init = (jnp.zeros((qb, 1), F32), jnp.zeros((qb, LANES), F32)) * N_HEADS
    carry = lax.fori_loop(0, nkb - n_full, reversed_body(n_full, nkb - n_full, True), init)
    carry = lax.fori_loop(0, n_full, reversed_body(0, n_full, False), carry)
    outs = [carry[2 * h + 1] for h in range(N_HEADS)]
    for pair in range(N_HEADS // 2):
        o_ref[0, :, pair * LANES:(pair + 1) * LANES] = jnp.where(
            lane < HEAD_DIM, outs[2 * pair], outs[2 * pair + 1])


def _sb_attention(q, kv, q_off):
    b, sq, _ = q.shape
    skp = kv.shape[1]
    qb = _pick(sq, (256, 128, 64))
    sub = 256
    kb = KEY_ALIGN
    tri = jnp.asarray(np.tril(np.ones((sub, sub), np.float32), -1), BF16)
    kern = functools.partial(_sb_kernel, qb=qb, kb=kb, sub=sub, q_off=q_off)
    return pl.pallas_call(
        kern,
        grid=(b, sq // qb),
        in_specs=[pl.BlockSpec((1, qb, BRANCH_WIDTH), lambda bi, qi: (bi, qi, 0)),
                  pl.BlockSpec((1, skp, 2 * BRANCH_WIDTH), lambda bi, qi: (bi, 0, 0),
                               pipeline_mode=pl.Buffered(1)),
                  pl.BlockSpec((sub, sub), lambda bi, qi: (0, 0))],
        out_specs=pl.BlockSpec((1, qb, BRANCH_WIDTH), lambda bi, qi: (bi, qi, 0)),
        out_shape=jax.ShapeDtypeStruct((b, sq, BRANCH_WIDTH), F32),
        compiler_params=_cparams(2),
        name="stick_breaking_attention",
    )(q, kv, tri)


def _band_kernel(bias_ref, q_ref, kv_ref, o_ref, tile_ref, *, qb, win, q_off):
    first = (pl.program_id(0) == 0) & (pl.program_id(1) == 0)
    row = lax.broadcasted_iota(jnp.int32, (qb, win), 0)
    col = lax.broadcasted_iota(jnp.int32, (qb, win), 1)

    @pl.when(first)
    def _():
        rel = jnp.clip(row + BAND_WINDOW - col, -REL_CLIP, REL_CLIP) + REL_CLIP
        qch = row >> CHUNK_SHIFT
        kch = col >> CHUNK_SHIFT
        in_band = (kch >= qch) & (kch <= qch + BAND_LEFT_CHUNKS)
        for h in range(N_HEADS):
            tile_ref[h] = jnp.full((qb, win), bias_ref[h, 0], F32)

        def fill(r, c):
            hit = rel == r
            for h in range(N_HEADS):
                tile_ref[h] = jnp.where(hit, bias_ref[h, r], tile_ref[h])
            return c

        lax.fori_loop(1, 2 * REL_CLIP + 1, fill, 0)
        for h in range(N_HEADS):
            tile_ref[h] = jnp.where(in_band, tile_ref[h], NEG_INF)

    q0 = pl.multiple_of(pl.program_id(1) * qb, qb)
    kpos = col + (q_off + q0 - BAND_WINDOW)
    lane = lax.broadcasted_iota(jnp.int32, (qb, LANES), 1)
    outs = []
    for h in range(N_HEADS):
        pair = h // 2
        qh = _head_mask(q_ref[0, :, pair * LANES:(pair + 1) * LANES], h)
        kwin = kv_ref[0, pl.ds(q0, win), pair * LANES:(pair + 1) * LANES]
        s = _dot_nt(qh, kwin) + tile_ref[h]
        s = jnp.where(kpos >= 0, s, NEG_INF)
        m = jnp.max(s, axis=1, keepdims=True)
        p = jnp.exp(s - m)
        l = jnp.sum(p, axis=1, keepdims=True)
        vwin = kv_ref[0, pl.ds(q0, win),
                      BRANCH_WIDTH + pair * LANES:BRANCH_WIDTH + (pair + 1) * LANES]
        outs.append(jnp.dot(p.astype(BF16), vwin, preferred_element_type=F32) / l)
    for pair in range(N_HEADS // 2):
        o_ref[0, :, pair * LANES:(pair + 1) * LANES] = jnp.where(
            lane < HEAD_DIM, outs[2 * pair], outs[2 * pair + 1])


def _band_attention(q, kv, rel_bias, q_off):
    b, sq, _ = q.shape
    sk = kv.shape[1]
    qb = _pick(sq, (256, 128, 64))
    win = qb + BAND_WINDOW
    kern = functools.partial(_band_kernel, qb=qb, win=win, q_off=q_off)
    return pl.pallas_call(
        kern,
        grid=(b, sq // qb),
        in_specs=[pl.BlockSpec(memory_space=pltpu.SMEM),
                  pl.BlockSpec((1, qb, BRANCH_WIDTH), lambda bi, qi: (bi, qi, 0)),
                  pl.BlockSpec((1, sk, 2 * BRANCH_WIDTH), lambda bi, qi: (bi, 0, 0),
                               pipeline_mode=pl.Buffered(1))],
        out_specs=pl.BlockSpec((1, qb, BRANCH_WIDTH), lambda bi, qi: (bi, qi, 0)),
        out_shape=jax.ShapeDtypeStruct((b, sq, BRANCH_WIDTH), F32),
        scratch_shapes=[pltpu.VMEM((N_HEADS, qb, win), F32)],
        compiler_params=_cparams(2),
        name="band_attention",
    )(rel_bias, q, kv)


def _dsa_kernel(qi_ref, w_ref, kx_ref, q_ref, k_ref, vt_ref, o_ref, key_ref, cut_ref,
                *, qb, kb, q_off, skp, n_sel):
    q0 = pl.program_id(1) * qb
    hi = jnp.minimum((((q_off + q0 + qb - 1) >> CHUNK_SHIFT) + 1) * CHUNK, skp)
    nkb = (hi + kb - 1) // kb
    qpos = q_off + q0 + lax.broadcasted_iota(jnp.int32, (1, qb), 1)
    cend = ((qpos >> CHUNK_SHIFT) + 1) * CHUNK
    row = lax.broadcasted_iota(jnp.int32, (kb, qb), 0)
    neg_key = lax.bitcast_convert_type(jnp.full((1, 1), NEG_INF, F32), jnp.int32) ^ 0x7FFFFFFF

    def score_block(i, c):
        k0 = pl.multiple_of(i * kb, kb)
        kx = kx_ref[0, pl.ds(k0, kb), :]
        r = [_dot_nt(kx, qi_ref[0, :, j * IDX_DIM:(j + 1) * IDX_DIM]) for j in range(IDX_HEADS)]
        score = jnp.zeros((kb, qb), F32)
        for j in range(IDX_HEADS):
            wj = w_ref[0, j:j + 1, :] * (IDX_HEADS ** -0.5)
            score = score + wj * jnp.maximum(r[j], 0.0)
        score = jnp.where(score == 0.0, 0.0, score)
        score = jnp.where(row + k0 < cend, score, NEG_INF)
        bits = lax.bitcast_convert_type(score, jnp.int32)
        key_ref[pl.ds(k0, kb), :] = jnp.where(bits < 0, bits ^ 0x7FFFFFFF, bits)
        return c

    lax.fori_loop(0, nkb, score_block, 0)

    slab = 64 if kb % 64 == 0 else kb

    def count_ge(cand):
        def blk(i, cnt):
            k0 = pl.multiple_of(i * kb, kb)
            for s in range(kb // slab):
                keys = key_ref[pl.ds(k0 + s * slab, slab), :]
                cnt = cnt + jnp.where(keys >= cand, 1.0, 0.0)
            return cnt
        part = lax.fori_loop(0, nkb, blk, jnp.zeros((slab, qb), F32))
        return jnp.sum(part, axis=0, keepdims=True)

    def undecided(state):
        t, _, n_ge = state
        return (t < 32) & (jnp.max(jnp.abs(n_ge - n_sel)) > 0.0)

    def bisect(state):
        t, thr, n_ge = state
        cand = thr + jnp.left_shift(jnp.int32(1), 31 - t)
        cnt = count_ge(cand)
        ok = cnt >= n_sel
        return t + 1, jnp.where(ok, cand, thr), jnp.where(ok, cnt, n_ge)

    stored = (nkb * kb).astype(F32)
    _, thr, n_ge = lax.while_loop(
        undecided, bisect,
        (jnp.int32(0), jnp.full((1, qb), INT_MIN, jnp.int32), jnp.full((1, qb), stored, F32)))
    excess = jnp.where((n_ge > n_sel) & (thr > neg_key), 1.0, 0.0)
    cut_ref[...] = jnp.full((1, qb), skp, jnp.int32)

    @pl.when(jnp.max(excess) > 0.0)
    def _():
        need = n_sel - count_ge(thr + 1)

        def count_tied_below(limit):
            def blk(i, cnt):
                k0 = pl.multiple_of(i * kb, kb)
                tied = (key_ref[pl.ds(k0, kb), :] == thr) & (row + k0 < limit)
                return cnt + jnp.sum(jnp.where(tied, 1.0, 0.0), axis=0, keepdims=True)
            return lax.fori_loop(0, nkb, blk, jnp.zeros((1, qb), F32))

        n_bits = max(1, (skp - 1).bit_length())

        def bisect_cut(t, cut):
            cand = cut + jnp.left_shift(jnp.int32(1), n_bits - 1 - t)
            return jnp.where(count_tied_below(cand) < need, cand, cut)

        cut = lax.fori_loop(0, n_bits, bisect_cut, jnp.zeros((1, qb), jnp.int32))
        cut_ref[...] = jnp.where(excess > 0.0, cut + 1, skp)

    cut = cut_ref[...]

    def mask_block(i, c):
        k0 = pl.multiple_of(i * kb, kb)
        keys = key_ref[pl.ds(k0, kb), :]
        kpos = row + k0
        sel = (keys >= thr) & ((keys != thr) | (kpos < cut)) & (kpos < cend)
        key_ref[pl.ds(k0, kb), :] = lax.bitcast_convert_type(jnp.where(sel, 0.0, NEG_INF), jnp.int32)
        return c

    lax.fori_loop(0, nkb, mask_block, 0)

    qh = [_head_mask(q_ref[0, :, (h // 2) * LANES:(h // 2 + 1) * LANES], h) for h in range(N_HEADS)]

    def body(i, carry):
        k0 = pl.multiple_of(i * kb, kb)
        bias = lax.bitcast_convert_type(key_ref[pl.ds(k0, kb), :], F32)
        s = [_dot_nt(k_ref[0, pl.ds(k0, kb), (h // 2) * LANES:(h // 2 + 1) * LANES], qh[h])
             for h in range(N_HEADS)]
        p, new = [], []
        for h in range(N_HEADS):
            m, l, acc = carry[3 * h:3 * h + 3]
            sh = s[h] + bias
            m_new = jnp.maximum(m, jnp.max(sh, axis=0, keepdims=True))
            alpha = jnp.exp(m - m_new)
            ph = jnp.exp(sh - m_new)
            p.append(ph.astype(BF16))
            new += [m_new, alpha * l + jnp.sum(ph, axis=0, keepdims=True), alpha * acc]
        for h in range(N_HEADS):
            pair = h // 2
            vblk = vt_ref[0, pair * LANES:(pair + 1) * LANES, pl.ds(k0, kb)]
            new[3 * h + 2] = new[3 * h + 2] + jnp.dot(vblk, p[h], preferred_element_type=F32)
        return tuple(new)

    init = (jnp.full((1, qb), NEG_INF, F32), jnp.zeros((1, qb), F32),
            jnp.zeros((LANES, qb), F32)) * N_HEADS
    carry = lax.fori_loop(0, nkb, body, init)
    for h in range(N_HEADS):
        half = (h % 2) * HEAD_DIM
        out = carry[3 * h + 2] / carry[3 * h + 1]
        o_ref[0, h * HEAD_DIM:(h + 1) * HEAD_DIM, :] = out[half:half + HEAD_DIM, :]


def _dsa_attention(qi, w, kx, q, k, vt, q_off, n_sel):
    b, sq, _ = q.shape
    skp = k.shape[1]
    qb = _pick(sq, (256, 128, 64))
    kb = _pick(skp, (1024, KEY_ALIGN))
    kern = functools.partial(_dsa_kernel, qb=qb, kb=kb, q_off=q_off, skp=skp, n_sel=n_sel)
    return pl.pallas_call(
        kern,
        grid=(b, sq // qb),
        in_specs=[pl.BlockSpec((1, qb, IDX_HEADS * IDX_DIM), lambda bi, qi_: (bi, qi_, 0)),
                  pl.BlockSpec((1, IDX_HEADS, qb), lambda bi, qi_: (bi, 0, qi_)),
                  pl.BlockSpec((1, skp, IDX_DIM), lambda bi, qi_: (bi, 0, 0),
                               pipeline_mode=pl.Buffered(1)),
                  pl.BlockSpec((1, qb, BRANCH_WIDTH), lambda bi, qi_: (bi, qi_, 0)),
                  pl.BlockSpec((1, skp, BRANCH_WIDTH), lambda bi, qi_: (bi, 0, 0),
                               pipeline_mode=pl.Buffered(1)),
                  pl.BlockSpec((1, BRANCH_WIDTH, skp), lambda bi, qi_: (bi, 0, 0),
                               pipeline_mode=pl.Buffered(1))],
        out_specs=pl.BlockSpec((1, BRANCH_WIDTH, qb), lambda bi, qi_: (bi, 0, qi_)),
        out_shape=jax.ShapeDtypeStruct((b, BRANCH_WIDTH, sq), F32),
        scratch_shapes=[pltpu.VMEM((skp, qb), jnp.int32), pltpu.VMEM((1, qb), jnp.int32)],
        compiler_params=_cparams(2),
        name="dsa_attention",
    )(qi, w, kx, q, k, vt)


def _layer_norm(z, g, b):
    mu = jnp.mean(z, axis=-1, keepdims=True)
    zc = z - mu
    var = jnp.mean(zc * zc, axis=-1, keepdims=True)
    return zc * lax.rsqrt(var + LN_EPS) * g + b


def _merge_kernel(x_ref, wgate_ref, oa_ref, ob_ref, oc_ref, od_ref, wb_ref, wo_ref, g_ref, b_ref,
                  y_ref):
    xb = x_ref[...].astype(BF16)
    logits = [jnp.dot(xb, wgate_ref[:, n * D_MODEL:(n + 1) * D_MODEL], preferred_element_type=F32)
              for n in range(N_BRANCH)]
    branch = [jnp.dot(o_ref[...].astype(BF16), wb_ref[n], preferred_element_type=F32)
              for n, o_ref in enumerate((oa_ref, ob_ref, oc_ref, od_ref))]
    merged = jax.nn.sigmoid(logits[0]) * branch[0]
    for n in range(1, N_BRANCH):
        merged = merged + jax.nn.sigmoid(logits[n]) * branch[n]
    y = jnp.dot(merged.astype(BF16), wo_ref[...], preferred_element_type=F32)
    y_ref[...] = _layer_norm(ALPHA * x_ref[...] + y, g_ref[...], b_ref[...])


def _merge_out_ln(x, w_gate, o_a, o_b, o_c, o_d, w_branch, w_out, g, b):
    n = x.shape[0]
    ts = _pick(n, (512, 256, 128, 64))
    row = lambda i: (i, 0)
    obs = pl.BlockSpec((ts, BRANCH_WIDTH), row)
    return pl.pallas_call(
        _merge_kernel,
        grid=(n // ts,),
        in_specs=[pl.BlockSpec((ts, D_MODEL), row),
                  pl.BlockSpec((D_MODEL, GATE_WIDTH), lambda i: (0, 0), pipeline_mode=pl.Buffered(1)),
                  obs, obs, obs, obs,
                  pl.BlockSpec((N_BRANCH, BRANCH_WIDTH, D_MODEL), lambda i: (0, 0, 0)),
                  pl.BlockSpec((D_MODEL, D_MODEL), lambda i: (0, 0)),
                  pl.BlockSpec((1, D_MODEL), lambda i: (0, 0)),
                  pl.BlockSpec((1, D_MODEL), lambda i: (0, 0))],
        out_specs=pl.BlockSpec((ts, D_MODEL), row),
        out_shape=jax.ShapeDtypeStruct((n, D_MODEL), F32),
        compiler_params=_cparams(1),
        name="merge_out_ln",
    )(x, w_gate, o_a, o_b, o_c, o_d, w_branch, w_out, g, b)


def _ffn_kernel(x_ref, wg_ref, wu_ref, wd_ref, g_ref, b_ref, y_ref, acc_ref):
    f = pl.program_id(1)
    xb = x_ref[...].astype(BF16)
    gate = jnp.dot(xb, wg_ref[...], preferred_element_type=F32)
    up = jnp.dot(xb, wu_ref[...], preferred_element_type=F32)
    hidden = (gate * jax.nn.sigmoid(gate) * up).astype(BF16)
    part = jnp.dot(hidden, wd_ref[...], preferred_element_type=F32)

    @pl.when(f == 0)
    def _():
        acc_ref[...] = part

    @pl.when(f > 0)
    def _():
        acc_ref[...] += part

    @pl.when(f == pl.num_programs(1) - 1)
    def _():
        y_ref[...] = _layer_norm(ALPHA * x_ref[...] + acc_ref[...], g_ref[...], b_ref[...])


def _ffn_ln(x, w_gate, w_up, w_down, g, b):
    n = x.shape[0]
    ts = _pick(n, (1024, 512, 256, 128, 64))
    tf = D_FF // 2
    return pl.pallas_call(
        _ffn_kernel,
        grid=(n // ts, D_FF // tf),
        in_specs=[pl.BlockSpec((ts, D_MODEL), lambda i, f: (i, 0)),
                  pl.BlockSpec((D_MODEL, tf), lambda i, f: (0, f)),
                  pl.BlockSpec((D_MODEL, tf), lambda i, f: (0, f)),
                  pl.BlockSpec((tf, D_MODEL), lambda i, f: (f, 0)),
                  pl.BlockSpec((1, D_MODEL), lambda i, f: (0, 0)),
                  pl.BlockSpec((1, D_MODEL), lambda i, f: (0, 0))],
        out_specs=pl.BlockSpec((ts, D_MODEL), lambda i, f: (i, 0)),
        out_shape=jax.ShapeDtypeStruct((n, D_MODEL), F32),
        scratch_shapes=[pltpu.VMEM((ts, D_MODEL), F32)],
        compiler_params=_cparams(2),
        name="ffn_ln",
    )(x, w_gate, w_up, w_down, g, b)


def _pad_keys(a, axis=1):
    n = a.shape[axis]
    pad = (-n) % KEY_ALIGN
    if pad == 0:
        return a
    widths = [(0, 0)] * a.ndim
    widths[axis] = (0, pad)
    return jnp.pad(a, widths)


def _prepare_weights(w_in, mla_w_uk, mla_w_uv, w_branch, w_out, w_gate_up, w_down):
    w_rest = jnp.concatenate([w_in, jnp.zeros((D_MODEL, 1), w_in.dtype)], axis=1)
    w_rest = jnp.take(w_rest, _projection_columns(), axis=1).astype(BF16)
    w_uk = jnp.pad(mla_w_uk, ((0, 0), (0, 0), (0, LANES - MLA_D_NOPE))).reshape(MLA_D_C, -1)
    w_ukv = jnp.concatenate([w_uk, mla_w_uv.reshape(MLA_D_C, -1)], axis=1).astype(BF16)
    return dict(w_rest=w_rest, w_gates=w_in[:, REST_WIDTH:].astype(BF16), w_ukv=w_ukv,
                w_branch=w_branch.astype(BF16),
                w_out=w_out.astype(BF16), w_gate=w_gate_up[:, :D_FF].astype(BF16),
                w_up=w_gate_up[:, D_FF:].astype(BF16), w_down=w_down.astype(BF16))


def _projection_columns():
    zero_col = REST_WIDTH + GATE_WIDTH
    src = np.full(PROJ_WIDTH, zero_col, np.int32)
    o_aq, o_ckv, o_kr, o_sb, o_bd, o_ds, o_ixq, o_ixk, o_ixw = np.concatenate(
        [[0], np.cumsum(IN_SIZES[:-1])])[:9]
    hd = MLA_D_NOPE + MLA_D_ROPE

    def put(lane0, src0, n):
        src[lane0:lane0 + n] = src0 + np.arange(n)

    for h in range(N_HEADS):
        put((SEG_AQ + h) * LANES, o_aq + h * hd, hd)
    put(SEG_CKV * LANES, o_ckv, MLA_D_C)
    put(SEG_KR * LANES + MLA_ROPE_LANE, o_kr, MLA_D_ROPE)
    put(SEG_SB * LANES, o_sb, 3 * BRANCH_WIDTH)
    put(SEG_BD * LANES, o_bd, 3 * BRANCH_WIDTH)
    put(SEG_DS * LANES, o_ds, 3 * BRANCH_WIDTH)
    put(SEG_IXQ * LANES, o_ixq, IDX_HEADS * IDX_DIM)
    put(SEG_IXK * LANES, o_ixk, IDX_DIM)
    put(SEG_IXK * LANES + IXW_LANE, o_ixw, IDX_HEADS)
    return src


def _trunk_layer(x, q_off, past, wts, mla_kv_norm, band_rel_bias, ln1_g, ln1_b, ln2_g, ln2_b):
    bsz, s_len, _ = x.shape
    n = bsz * s_len
    pos = jnp.tile(q_off + jnp.arange(s_len, dtype=jnp.int32), bsz)
    x2 = x.reshape(n, D_MODEL)
    (q_a, lat_new, k_a, v_a, q_b, kv_b, sb_kv_new, q_c, kv_c, bd_kv_new,
     q_d, k_d, vt_d, ds_kv_new, q_ix, kidx_new, kx_d, w_ix) = _project(
        x2, pos, wts['w_rest'], wts['w_ukv'], mla_kv_norm[None])
    seq = lambda a: a.reshape(bsz, s_len, a.shape[-1])
    seq_t = lambda a: jnp.transpose(a.reshape(a.shape[0], bsz, s_len), (1, 0, 2))
    q_a, k_a, v_a, q_b, kv_b, q_c, kv_c, q_d, k_d, q_ix, kx_d = map(
        seq, (q_a, k_a, v_a, q_b, kv_b, q_c, kv_c, q_d, k_d, q_ix, kx_d))
    vt_d, w_ix = seq_t(vt_d), seq_t(w_ix)
    kv_state = lambda a: a.reshape(bsz, s_len, 2, N_HEADS, HEAD_DIM)
    lat_new, kidx_new = seq(lat_new), seq(kidx_new)
    sb_kv_new, bd_kv_new, ds_kv_new = kv_state(sb_kv_new), kv_state(bd_kv_new), kv_state(ds_kv_new)

    if past is None:
        kv_c = jnp.pad(kv_c, ((0, 0), (BAND_WINDOW, 0), (0, 0)))
        band_rows = bd_kv_new[:, s_len - min(BAND_WINDOW, s_len):]
        s_k = s_len
    else:
        past_lat, past_sb, past_band, past_ds, past_kidx = past
        p_len = past_lat.shape[1]
        s_k = p_len + s_len
        rows_bf16 = lambda a: a.reshape(bsz, a.shape[1], -1).astype(BF16)
        kv_past = _matmul(past_lat[..., :MLA_D_C].reshape(bsz * p_len, MLA_D_C), wts['w_ukv'])
        kv_past = kv_past.reshape(bsz, p_len, -1)
        kr_past = jnp.pad(past_lat[..., MLA_D_C:],
                          ((0, 0), (0, 0), (MLA_ROPE_LANE, LANES - MLA_ROPE_LANE - MLA_D_ROPE)))
        k_past = kv_past[..., :N_HEADS * LANES].reshape(bsz, p_len, N_HEADS, LANES) + kr_past[:, :, None]
        k_a = jnp.concatenate([rows_bf16(k_past), k_a], axis=1)
        v_a = jnp.concatenate([kv_past[..., N_HEADS * LANES:].astype(BF16), v_a], axis=1)
        kv_b = jnp.concatenate([rows_bf16(past_sb), kv_b], axis=1)
        kv_c = jnp.concatenate([rows_bf16(past_band), kv_c], axis=1)
        past_ds = past_ds.reshape(bsz, p_len, 2 * BRANCH_WIDTH)
        k_d = jnp.concatenate([past_ds[..., :BRANCH_WIDTH].astype(BF16), k_d], axis=1)
        vt_d = jnp.concatenate(
            [jnp.transpose(past_ds[..., BRANCH_WIDTH:], (0, 2, 1)).astype(BF16), vt_d], axis=2)
        kx_d = jnp.concatenate([past_kidx.astype(BF16), kx_d], axis=1)
        band_rows = bd_kv_new

    o_a = _mla_attention(q_a, _pad_keys(k_a), _pad_keys(v_a), q_off)
    o_b = _sb_attention(q_b, _pad_keys(kv_b), q_off)
    o_c = _band_attention(q_c, kv_c, band_rel_bias, q_off)
    n_sel = min(DSA_TOPK, s_k // 4)
    o_d = _dsa_attention(q_ix, w_ix, _pad_keys(kx_d), q_d, _pad_keys(k_d), _pad_keys(vt_d, axis=2),
                         q_off, n_sel)
    o_d = jnp.transpose(o_d, (0, 2, 1))

    flat = lambda o: o.reshape(n, BRANCH_WIDTH)
    x1 = _merge_out_ln(x2, wts['w_gates'], flat(o_a), flat(o_b), flat(o_c), flat(o_d),
                       wts['w_branch'], wts['w_out'], ln1_g[None], ln1_b[None])
    x_out = _ffn_ln(x1, wts['w_gate'], wts['w_up'], wts['w_down'], ln2_g[None], ln2_b[None])
    return x_out.reshape(bsz, s_len, D_MODEL), (lat_new, sb_kv_new, band_rows, ds_kv_new, kidx_new)


def kernel(x_prompt, x_sample, cache_mla_latent, cache_sb_kv, cache_band_kv, cache_dsa_kv, cache_dsa_kidx, w_in, mla_kv_norm, mla_w_uk, mla_w_uv, band_rel_bias, w_branch, w_out, ln1_g, ln1_b, w_gate_up, w_down, ln2_g, ln2_b):
    past_len = cache_mla_latent.shape[2]
    xp, xs = x_prompt, x_sample
    st_p, st_s = [], []
    for l in range(w_in.shape[0]):
        wts = _prepare_weights(w_in[l], mla_w_uk[l], mla_w_uv[l], w_branch[l], w_out[l],
                               w_gate_up[l], w_down[l])
        params = (wts, mla_kv_norm[l], band_rel_bias[l], ln1_g[l], ln1_b[l], ln2_g[l], ln2_b[l])
        xp, new_p = _trunk_layer(xp, 0, None, *params)
        past = (cache_mla_latent[l], cache_sb_kv[l], cache_band_kv[l], cache_dsa_kv[l],
                cache_dsa_kidx[l])
        xs, new_s = _trunk_layer(xs, past_len, past, *params)
        st_p.append(new_p)
        st_s.append(new_s)
    stack = lambda st, i: jnp.stack([s[i] for s in st])
    return (xp, xs) + tuple(stack(st_p, i) for i in range(5)) + tuple(stack(st_s, i) for i in range(5))
```

```python
import functools
import math

import numpy as np
import jax
import jax.numpy as jnp
from jax import lax
from jax.experimental import pallas as pl
from jax.experimental.pallas import tpu as pltpu

D_MODEL = 1024
CHUNK = 64
CHUNK_SHIFT = 6
N_BRANCH = 4
N_HEADS = 4
HEAD_DIM = 64
BRANCH_WIDTH = N_HEADS * HEAD_DIM
MLA_D_C = 128
MLA_D_NOPE = 64
MLA_D_ROPE = 32
MLA_THETA = 10000.0
ROPE_THETA = 500000.0
BAND_LEFT_CHUNKS = 8
BAND_WINDOW = BAND_LEFT_CHUNKS * CHUNK
REL_CLIP = 128
IDX_HEADS = 8
IDX_DIM = 64
DSA_TOPK = 256
D_FF = ((8 * D_MODEL // 3 + 255) // 256) * 256
DEPTH = 2
ALPHA = (2 * DEPTH) ** 0.25
NEG_INF = -1e30
LOG2_E = math.log2(math.e)
LN_EPS = 1e-5
IN_SIZES = (N_HEADS * (MLA_D_NOPE + MLA_D_ROPE), MLA_D_C, MLA_D_ROPE,
            3 * BRANCH_WIDTH, 3 * BRANCH_WIDTH, 3 * BRANCH_WIDTH,
            IDX_HEADS * IDX_DIM, IDX_DIM, IDX_HEADS, N_BRANCH * D_MODEL)
REST_WIDTH = sum(IN_SIZES[:-1])
GATE_WIDTH = IN_SIZES[-1]

LANES = 128
KEY_ALIGN = 512
INT_MIN = -2 ** 31
VMEM_LIMIT = 56 * 1024 * 1024

F32 = jnp.float32
BF16 = jnp.bfloat16


def _cparams(n_axes):
    return pltpu.CompilerParams(dimension_semantics=("arbitrary",) * n_axes,
                                vmem_limit_bytes=VMEM_LIMIT)


def _pick(n, candidates):
    for c in candidates:
        if n % c == 0:
            return c
    return n


def _ones_upper_half(rows):
    lane = lax.broadcasted_iota(jnp.int32, (rows, N_HEADS * LANES), 1)
    return jnp.where((lane & (LANES - 1)) >= HEAD_DIM, 1.0, 0.0)


def _dot_nt(a, b):
    return lax.dot_general(a, b, (((1,), (1,)), ((), ())), preferred_element_type=F32)


def _mm_kernel(a_ref, b_ref, o_ref):
    o_ref[...] = jnp.dot(a_ref[...].astype(BF16), b_ref[...], preferred_element_type=F32)


def _matmul(a, b):
    m, k = a.shape
    n = b.shape[1]
    tm = _pick(m, (1024, 512, 256, 128, 64, 32, 16, 8))
    tn = _pick(n, (512, 384, 256, 128))
    return pl.pallas_call(
        _mm_kernel,
        grid=(m // tm, n // tn),
        in_specs=[pl.BlockSpec((tm, k), lambda i, j: (i, 0)),
                  pl.BlockSpec((k, tn), lambda i, j: (0, j))],
        out_specs=pl.BlockSpec((tm, tn), lambda i, j: (i, j)),
        out_shape=jax.ShapeDtypeStruct((m, n), F32),
        compiler_params=_cparams(2),
        name="matmul",
    )(a, b)


SEG_AQ = 0
SEG_CKV = 4
SEG_KR = 5
SEG_SB = 6
SEG_BD = 12
SEG_DS = 18
SEG_IXQ = 24
SEG_IXK = 28
N_SEG = 30
IXW_LANE = 96
PROJ_WIDTH = N_SEG * LANES
MLA_ROPE_LANE = 64
LAT_WIDTH = MLA_D_C + MLA_D_ROPE


def _proj_kernel(x_ref, w_ref, wukv_ref, g_ref, ca_ref, sma_ref, spa_ref, cp_ref, smp_ref, spp_ref,
                 qa_ref, lat_ref, ka_ref, va_ref,
                 qb_ref, kvb_ref, sbkv_ref, qc_ref, kvc_ref, bdkv_ref,
                 qd_ref, kd_ref, vtd_ref, dskv_ref, qix_ref, kidx_ref, kx_ref, wix_ref):
    proj = jnp.dot(x_ref[...].astype(BF16), w_ref[...], preferred_element_type=F32)
    seg = lambda s, n=1: proj[:, s * LANES:(s + n) * LANES]

    def rope(t, c_ref, sm_ref, sp_ref, half):
        return (t * c_ref[...] + pltpu.roll(t, LANES - half, 1) * sm_ref[...]
                + pltpu.roll(t, half, 1) * sp_ref[...])

    rope_a = functools.partial(rope, c_ref=ca_ref, sm_ref=sma_ref, sp_ref=spa_ref, half=MLA_D_ROPE // 2)
    rope_p = functools.partial(rope, c_ref=cp_ref, sm_ref=smp_ref, sp_ref=spp_ref, half=HEAD_DIM // 8)
    head_scale = HEAD_DIM ** -0.5 * LOG2_E

    mla_scale = (MLA_D_NOPE + MLA_D_ROPE) ** -0.5 * LOG2_E
    for h in range(N_HEADS):
        qa_ref[:, h * LANES:(h + 1) * LANES] = (rope_a(seg(SEG_AQ + h)) * mla_scale).astype(BF16)
    ckv = seg(SEG_CKV)
    ckv = ckv * lax.rsqrt(jnp.mean(ckv * ckv, axis=-1, keepdims=True) + LN_EPS) * g_ref[...]
    kr = rope_a(seg(SEG_KR))
    lat_ref[:, :MLA_D_C] = ckv
    lat_ref[:, MLA_D_C:] = kr[:, MLA_ROPE_LANE:MLA_ROPE_LANE + MLA_D_ROPE]
    kv_a = jnp.dot(ckv.astype(BF16), wukv_ref[...], preferred_element_type=F32)
    for h in range(N_HEADS):
        ka_ref[:, h * LANES:(h + 1) * LANES] = (kv_a[:, h * LANES:(h + 1) * LANES] + kr).astype(BF16)
    va_ref[...] = (kv_a[:, N_HEADS * LANES:] + _ones_upper_half(kv_a.shape[0])).astype(BF16)

    for s0, q_ref, kv_ref, new_ref in ((SEG_SB, qb_ref, kvb_ref, sbkv_ref),
                                       (SEG_BD, qc_ref, kvc_ref, bdkv_ref)):
        q_ref[...] = (seg(s0, 2) * head_scale).astype(BF16)
        kv = seg(s0 + 2, 4)
        new_ref[...] = kv
        kv_ref[...] = kv.astype(BF16)

    for p in range(2):
        qd_ref[:, p * LANES:(p + 1) * LANES] = (rope_p(seg(SEG_DS + p)) * head_scale).astype(BF16)
        k_rot = rope_p(seg(SEG_DS + 2 + p))
        dskv_ref[:, p * LANES:(p + 1) * LANES] = k_rot
        kd_ref[:, p * LANES:(p + 1) * LANES] = k_rot.astype(BF16)
    v_d = seg(SEG_DS + 4, 2)
    dskv_ref[:, BRANCH_WIDTH:] = v_d
    vt = v_d.T.astype(BF16)
    for h in range(N_HEADS):
        vtd_ref[h * LANES:h * LANES + HEAD_DIM, :] = vt[h * HEAD_DIM:(h + 1) * HEAD_DIM, :]
        vtd_ref[h * LANES + HEAD_DIM:(h + 1) * LANES, :] = jnp.ones((LANES - HEAD_DIM, vt.shape[1]), BF16)
    for p in range(IDX_HEADS // 2):
        qix_ref[:, p * LANES:(p + 1) * LANES] = (rope_p(seg(SEG_IXQ + p)) * IDX_DIM ** -0.5).astype(BF16)
    ixk = rope_p(seg(SEG_IXK))
    kidx_ref[...] = ixk[:, :IDX_DIM]
    kx_ref[...] = ixk[:, :IDX_DIM].astype(BF16)
    wix_ref[...] = ixk.T[IXW_LANE:IXW_LANE + IDX_HEADS, :]


def _rope_tables(pos, theta, width, starts):
    half = width // 2
    inv = jnp.exp(jnp.arange(half, dtype=F32) * (-2.0 * math.log(theta) / width))
    freq_idx = np.zeros(LANES, np.int32)
    x1 = np.zeros(LANES, np.float32)
    x2 = np.zeros(LANES, np.float32)
    for s in starts:
        freq_idx[s:s + width] = np.tile(np.arange(half), 2)
        x1[s:s + half] = 1.0
        x2[s + half:s + width] = 1.0
    inv_lane = inv[freq_idx] * jnp.asarray(x1 + x2)
    ang = pos.astype(F32)[:, None] * inv_lane[None, :]
    sin = jnp.sin(ang)
    return jnp.cos(ang), -sin * x1[None, :], sin * x2[None, :]


def _project(x, pos, w_rest, w_ukv, kv_norm):
    n = x.shape[0]
    ts = _pick(n, (512, 256, 128, 64))
    tables = (_rope_tables(pos, MLA_THETA, MLA_D_ROPE, (MLA_ROPE_LANE,))
              + _rope_tables(pos, ROPE_THETA, HEAD_DIM // 4, (0, HEAD_DIM)))
    row = lambda i: (i, 0)
    fixed = lambda i: (0, 0)
    rows = lambda w: pl.BlockSpec((ts, w), row)
    out = lambda w, dt: jax.ShapeDtypeStruct((n, w), dt)
    specs = [
        (N_HEADS * LANES, BF16), (LAT_WIDTH, F32), (N_HEADS * LANES, BF16), (N_HEADS * LANES, BF16),
        (BRANCH_WIDTH, BF16), (2 * BRANCH_WIDTH, BF16), (2 * BRANCH_WIDTH, F32),
        (BRANCH_WIDTH, BF16), (2 * BRANCH_WIDTH, BF16), (2 * BRANCH_WIDTH, F32),
        (BRANCH_WIDTH, BF16), (BRANCH_WIDTH, BF16), None, (2 * BRANCH_WIDTH, F32),
        (IDX_HEADS * IDX_DIM, BF16), (IDX_DIM, F32), (IDX_DIM, BF16), None]
    out_specs, out_shape = [], []
    for k, spec in enumerate(specs):
        if spec is None:
            rows_t, dt = ((N_HEADS * LANES, BF16), (IDX_HEADS, F32))[k > 12]
            out_specs.append(pl.BlockSpec((rows_t, ts), lambda i: (0, i)))
            out_shape.append(jax.ShapeDtypeStruct((rows_t, n), dt))
        else:
            out_specs.append(rows(spec[0]))
            out_shape.append(out(*spec))
    return pl.pallas_call(
        _proj_kernel,
        grid=(n // ts,),
        in_specs=[rows(D_MODEL),
                  pl.BlockSpec((D_MODEL, PROJ_WIDTH), fixed, pipeline_mode=pl.Buffered(1)),
                  pl.BlockSpec((MLA_D_C, 2 * N_HEADS * LANES), fixed),
                  pl.BlockSpec((1, MLA_D_C), fixed)] + [rows(LANES)] * 6,
        out_specs=out_specs,
        out_shape=out_shape,
        compiler_params=_cparams(1),
        name="project_prepare",
    )(x, w_rest, w_ukv, kv_norm, *tables)


def _mla_kernel(q_ref, k_ref, v_ref, o_ref, *, qb, kb, q_off, skp):
    q0 = pl.program_id(1) * qb
    hi = jnp.minimum((((q_off + q0 + qb - 1) >> CHUNK_SHIFT) + 1) * CHUNK, skp)
    nkb = (hi + kb - 1) // kb
    qpos = q_off + q0 + lax.broadcasted_iota(jnp.int32, (qb, 1), 0)
    cend = ((qpos >> CHUNK_SHIFT) + 1) * CHUNK
    col = lax.broadcasted_iota(jnp.int32, (qb, kb), 1)
    lane = lax.broadcasted_iota(jnp.int32, (qb, LANES), 1)
    n_full = jnp.minimum(((((q_off + q0) >> CHUNK_SHIFT) + 1) * CHUNK) // kb, nkb)

    def body(i, carry, masked):
        k0 = pl.multiple_of(i * kb, kb)
        s = [_dot_nt(q_ref[0, :, h * LANES:(h + 1) * LANES],
                     k_ref[0, pl.ds(k0, kb), h * LANES:(h + 1) * LANES]) for h in range(N_HEADS)]
        if masked:
            vis = col + k0 < cend
            s = [jnp.where(vis, sh, NEG_INF) for sh in s]
        p, new = [], []
        for h in range(N_HEADS):
            m, acc = carry[2 * h:2 * h + 2]
            m_new = jnp.maximum(m, jnp.max(s[h], axis=1, keepdims=True))
            p.append(jnp.exp2((s[h] - m_new).astype(BF16)))
            new += [m_new, jnp.exp2(m - m_new) * acc]
        for h in range(N_HEADS):
            vblk = v_ref[0, pl.ds(k0, kb), h * LANES:(h + 1) * LANES]
            new[2 * h + 1] = new[2 * h + 1] + jnp.dot(p[h], vblk, preferred_element_type=F32)
        return tuple(new)

    init = (jnp.full((qb, 1), NEG_INF, F32), jnp.zeros((qb, LANES), F32)) * N_HEADS
    carry = lax.fori_loop(0, n_full, functools.partial(body, masked=False), init)
    carry = lax.fori_loop(n_full, nkb, functools.partial(body, masked=True), carry)
    outs = [carry[2 * h + 1] / pltpu.roll(carry[2 * h + 1], HEAD_DIM, 1) for h in range(N_HEADS)]
    for pair in range(N_HEADS // 2):
        o_ref[0, :, pair * LANES:(pair + 1) * LANES] = jnp.where(
            lane < HEAD_DIM, outs[2 * pair], pltpu.roll(outs[2 * pair + 1], HEAD_DIM, 1))


def _mla_attention(q, k, v, q_off):
    b, sq, _ = q.shape
    skp = k.shape[1]
    qb = _pick(sq, (256, 128, 64))
    kb = _pick(skp, (1024, KEY_ALIGN))
    kern = functools.partial(_mla_kernel, qb=qb, kb=kb, q_off=q_off, skp=skp)
    return pl.pallas_call(
        kern,
        grid=(b, sq // qb),
        in_specs=[pl.BlockSpec((1, qb, N_HEADS * LANES), lambda bi, qi: (bi, qi, 0)),
                  pl.BlockSpec((1, skp, N_HEADS * LANES), lambda bi, qi: (bi, 0, 0),
                               pipeline_mode=pl.Buffered(1)),
                  pl.BlockSpec((1, skp, N_HEADS * LANES), lambda bi, qi: (bi, 0, 0),
                               pipeline_mode=pl.Buffered(1))],
        out_specs=pl.BlockSpec((1, qb, BRANCH_WIDTH), lambda bi, qi: (bi, qi, 0)),
        out_shape=jax.ShapeDtypeStruct((b, sq, BRANCH_WIDTH), F32),
        compiler_params=_cparams(2),
        name="mla_attention",
    )(q, k, v)


def _head_mask(x_pair, h):
    lane = lax.broadcasted_iota(jnp.int32, x_pair.shape, 1)
    keep = (lane < HEAD_DIM) if h % 2 == 0 else (lane >= HEAD_DIM)
    return jnp.where(keep, x_pair, jnp.zeros_like(x_pair))


def _sb_kernel(q_ref, kv_ref, t_ref, o_ref, *, qb, kb, sub, q_off):
    q0 = pl.program_id(1) * qb
    hi = q_off + q0 + qb - 1
    nkb = (hi + kb - 1) // kb
    qpos = q_off + q0 + lax.broadcasted_iota(jnp.int32, (qb, 1), 0)
    col = lax.broadcasted_iota(jnp.int32, (qb, sub), 1)
    lane = lax.broadcasted_iota(jnp.int32, (qb, LANES), 1)
    tri = t_ref[...]
    n_full = jnp.minimum((q_off + q0) // kb, nkb)
    qh = [_head_mask(q_ref[0, :, (h // 2) * LANES:(h // 2 + 1) * LANES], h) for h in range(N_HEADS)]

    def body(i, carry, masked):
        k0 = pl.multiple_of(i * kb, kb)
        carry = list(carry)
        units = [(j, h) for j in reversed(range(kb // sub)) for h in range(N_HEADS)]
        strict = {j: col + (k0 + j * sub) < qpos for j in range(kb // sub)} if masked else None
        z = {}
        for j, h in units:
            pair = h // 2
            kblk = kv_ref[0, pl.ds(k0 + j * sub, sub), pair * LANES:(pair + 1) * LANES]
            z[j, h] = _dot_nt(qh[h], kblk)
        log_1m, suffix = {}, {}
        for u in units:
            nz = -z[u]
            t = jnp.minimum(nz, 0.0) - jnp.log2(1.0 + jnp.exp2(jnp.minimum(z[u], nz)))
            if masked:
                t = jnp.where(strict[u[0]], t, 0.0)
            log_1m[u] = t
        for u in units:
            hi_part = log_1m[u].astype(BF16)
            lo_part = (log_1m[u] - hi_part.astype(F32)).astype(BF16)
            suffix[u] = (jnp.dot(hi_part, tri, preferred_element_type=F32)
                         + jnp.dot(lo_part, tri, preferred_element_type=F32))
        for j, h in units:
            u = (j, h)
            run, acc = carry[2 * h:2 * h + 2]
            expo = z[u] + log_1m[u] + suffix[u] + run
            if masked:
                expo = jnp.where(strict[j], expo, NEG_INF)
            a = jnp.exp2(expo.astype(BF16))
            pair = h // 2
            vblk = kv_ref[0, pl.ds(k0 + j * sub, sub),
                          BRANCH_WIDTH + pair * LANES:BRANCH_WIDTH + (pair + 1) * LANES]
            carry[2 * h + 1] = acc + jnp.dot(a, vblk, preferred_element_type=F32)
            carry[2 * h] = run + jnp.sum(log_1m[u], axis=1, keepdims=True)
        return tuple(carry)

    def reversed_body(lo, n, masked):
        return lambda i, carry: body(lo + n - 1 - i, carry, masked)

    init = (jnp.zeros((qb, 1), F32), jnp.zeros((qb, LANES), F32)) * N_HEADS
    carry = lax.fori_loop(0, nkb - n_full, reversed_body(n_full, nkb - n_full, True), init)
    carry = lax.fori_loop(0, n_full, reversed_body(0, n_full, False), carry)
    outs = [carry[2 * h + 1] for h in range(N_HEADS)]
    for pair in range(N_HEADS // 2):
        o_ref[0, :, pair * LANES:(pair + 1) * LANES] = jnp.where(
            lane < HEAD_DIM, outs[2 * pair], outs[2 * pair + 1])


def _sb_attention(q, kv, q_off):
    b, sq, _ = q.shape
    skp = kv.shape[1]
    qb = _pick(sq, (256, 128, 64))
    sub = 256
    kb = _pick(skp, (1024, KEY_ALIGN))
    tri =jnp.asarray(np.tril(np.ones((sub, sub), np.float32), -1), BF16)
    kern = functools.partial(_sb_kernel, qb=qb, kb=kb, sub=sub, q_off=q_off)
    return pl.pallas_call(
        kern,
        grid=(b, sq // qb),
        in_specs=[pl.BlockSpec((1, qb, BRANCH_WIDTH), lambda bi, qi: (bi, qi, 0)),
                  pl.BlockSpec((1, skp, 2 * BRANCH_WIDTH), lambda bi, qi: (bi, 0, 0),
                               pipeline_mode=pl.Buffered(1)),
                  pl.BlockSpec((sub, sub), lambda bi, qi: (0, 0))],
        out_specs=pl.BlockSpec((1, qb, BRANCH_WIDTH), lambda bi, qi: (bi, qi, 0)),
        out_shape=jax.ShapeDtypeStruct((b, sq, BRANCH_WIDTH), F32),
        compiler_params=_cparams(2),
        name="stick_breaking_attention",
    )(q, kv, tri)


def _band_kernel(bias_ref, q_ref, kv_ref, o_ref, tile_ref, *, qb, win, q_off):
    first = (pl.program_id(0) == 0) & (pl.program_id(1) == 0)
    row = lax.broadcasted_iota(jnp.int32, (qb, win), 0)
    col = lax.broadcasted_iota(jnp.int32, (qb, win), 1)

    @pl.when(first)
    def _():
        rel = jnp.clip(row + BAND_WINDOW - col, -REL_CLIP, REL_CLIP) + REL_CLIP
        qch = row >> CHUNK_SHIFT
        kch = col >> CHUNK_SHIFT
        in_band = (kch >= qch) & (kch <= qch + BAND_LEFT_CHUNKS)
        for h in range(N_HEADS):
            tile_ref[h] = jnp.full((qb, win), bias_ref[h, 0], F32)

        def fill(r, c):
            hit = rel == r
            for h in range(N_HEADS):
                tile_ref[h] = jnp.where(hit, bias_ref[h, r], tile_ref[h])
            return c

        lax.fori_loop(1, 2 * REL_CLIP + 1, fill, 0)
        for h in range(N_HEADS):
            tile_ref[h] = jnp.where(in_band, tile_ref[h] * LOG2_E, NEG_INF)

    q0 = pl.multiple_of(pl.program_id(1) * qb, qb)
    kpos = col + (q_off + q0 - BAND_WINDOW)
    lane = lax.broadcasted_iota(jnp.int32, (qb, LANES), 1)
    outs = []
    for h in range(N_HEADS):
        pair = h // 2
        qh = _head_mask(q_ref[0, :, pair * LANES:(pair + 1) * LANES], h)
        kwin = kv_ref[0, pl.ds(q0, win), pair * LANES:(pair + 1) * LANES]
        s = _dot_nt(qh, kwin) + tile_ref[h]
        s = jnp.where(kpos >= 0, s, NEG_INF)
        m = jnp.max(s, axis=1, keepdims=True)
        p = jnp.exp2(s - m)
        l = jnp.sum(p, axis=1, keepdims=True)
        vwin = kv_ref[0, pl.ds(q0, win),
                      BRANCH_WIDTH + pair * LANES:BRANCH_WIDTH + (pair + 1) * LANES]
        outs.append(jnp.dot(p.astype(BF16), vwin, preferred_element_type=F32) / l)
    for pair in range(N_HEADS // 2):
        o_ref[0, :, pair * LANES:(pair + 1) * LANES] = jnp.where(
            lane < HEAD_DIM, outs[2 * pair], outs[2 * pair + 1])


def _band_attention(q, kv, rel_bias, q_off):
    b, sq, _ = q.shape
    sk = kv.shape[1]
    qb = _pick(sq, (256, 128, 64))
    win = qb + BAND_WINDOW
    kern = functools.partial(_band_kernel, qb=qb, win=win, q_off=q_off)
    return pl.pallas_call(
        kern,
        grid=(b, sq // qb),
        in_specs=[pl.BlockSpec(memory_space=pltpu.SMEM),
                  pl.BlockSpec((1, qb, BRANCH_WIDTH), lambda bi, qi: (bi, qi, 0)),
                  pl.BlockSpec((1, sk, 2 * BRANCH_WIDTH), lambda bi, qi: (bi, 0, 0),
                               pipeline_mode=pl.Buffered(1))],
        out_specs=pl.BlockSpec((1, qb, BRANCH_WIDTH), lambda bi, qi: (bi, qi, 0)),
        out_shape=jax.ShapeDtypeStruct((b, sq, BRANCH_WIDTH), F32),
        scratch_shapes=[pltpu.VMEM((N_HEADS, qb, win), F32)],
        compiler_params=_cparams(2),
        name="band_attention",
    )(rel_bias, q, kv)


def _dsa_kernel(qi_ref, w_ref, kx_ref, q_ref, k_ref, vt_ref, o_ref, key_ref, cut_ref,
                *, qb, kb, q_off, skp, n_sel):
    q0 = pl.program_id(1) * qb
    hi = jnp.minimum((((q_off + q0 + qb - 1) >> CHUNK_SHIFT) + 1) * CHUNK, skp)
    nkb = (hi + kb - 1) // kb
    qpos = q_off + q0 + lax.broadcasted_iota(jnp.int32, (1, qb), 1)
    cend = ((qpos >> CHUNK_SHIFT) + 1) * CHUNK
    row = lax.broadcasted_iota(jnp.int32, (kb, qb), 0)
    neg_key = lax.bitcast_convert_type(jnp.full((1, 1), NEG_INF, F32), jnp.int32) ^ 0x7FFFFFFF

    def score_block(i, c):
        k0 = pl.multiple_of(i * kb, kb)
        kx = kx_ref[0, pl.ds(k0, kb), :]
        r = [_dot_nt(kx, qi_ref[0, :, j * IDX_DIM:(j + 1) * IDX_DIM]) for j in range(IDX_HEADS)]
        score = jnp.zeros((kb, qb), F32)
        for j in range(IDX_HEADS):
            wj = w_ref[0, j:j + 1, :] * (IDX_HEADS ** -0.5)
            score = score + wj * jnp.maximum(r[j], 0.0)
        score = jnp.where(score == 0.0, 0.0, score)
        score = jnp.where(row + k0 < cend, score, NEG_INF)
        bits = lax.bitcast_convert_type(score, jnp.int32)
        key_ref[pl.ds(k0, kb), :] = jnp.where(bits < 0, bits ^ 0x7FFFFFFF, bits)
        return c

    lax.fori_loop(0, nkb, score_block, 0)

    slab = 64 if kb % 64 == 0 else kb

    def count_ge(cand):
        def blk(i, cnt):
            k0 = pl.multiple_of(i * kb, kb)
            for s in range(kb // slab):
                keys = key_ref[pl.ds(k0 + s * slab, slab), :]
                cnt = cnt + jnp.where(keys >= cand, 1.0, 0.0)
            return cnt
        part = lax.fori_loop(0, nkb, blk, jnp.zeros((slab, qb), F32))
        return jnp.sum(part, axis=0, keepdims=True)

    def undecided(state):
        t, _, n_ge = state
        return (t < 32) & (jnp.max(jnp.abs(n_ge - n_sel)) > 0.0)

    def bisect(state):
        t, thr, n_ge = state
        cand = thr + jnp.left_shift(jnp.int32(1), 31 - t)
        cnt = count_ge(cand)
        ok = cnt >= n_sel
        return t + 1, jnp.where(ok, cand, thr), jnp.where(ok, cnt, n_ge)

    stored = (nkb * kb).astype(F32)
    _, thr, n_ge = lax.while_loop(
        undecided, bisect,
        (jnp.int32(0), jnp.full((1, qb), INT_MIN, jnp.int32), jnp.full((1, qb), stored, F32)))
    excess = jnp.where((n_ge > n_sel) & (thr > neg_key), 1.0, 0.0)
    cut_ref[...] = jnp.full((1, qb), skp, jnp.int32)

    @pl.when(jnp.max(excess) > 0.0)
    def _():
        need = n_sel - count_ge(thr + 1)

        def count_tied_below(limit):
            def blk(i, cnt):
                k0 = pl.multiple_of(i * kb, kb)
                tied = (key_ref[pl.ds(k0, kb), :] == thr) & (row + k0 < limit)
                return cnt + jnp.sum(jnp.where(tied, 1.0, 0.0), axis=0, keepdims=True)
            return lax.fori_loop(0, nkb, blk, jnp.zeros((1, qb), F32))

        n_bits = max(1, (skp - 1).bit_length())

        def bisect_cut(t, cut):
            cand = cut + jnp.left_shift(jnp.int32(1), n_bits - 1 - t)
            return jnp.where(count_tied_below(cand) < need, cand, cut)

        cut = lax.fori_loop(0, n_bits, bisect_cut, jnp.zeros((1, qb), jnp.int32))
        cut_ref[...] = jnp.where(excess > 0.0, cut + 1, skp)

    cut = cut_ref[...]

    def mask_block(i, c):
        k0 = pl.multiple_of(i * kb, kb)
        keys = key_ref[pl.ds(k0, kb), :]
        kpos = row + k0
        sel = (keys >= thr) & ((keys != thr) | (kpos < cut)) & (kpos < cend)
        key_ref[pl.ds(k0, kb), :] = lax.bitcast_convert_type(jnp.where(sel, 0.0, NEG_INF), jnp.int32)
        return c

    lax.fori_loop(0, nkb, mask_block, 0)

    qh = [_head_mask(q_ref[0, :, (h // 2) * LANES:(h // 2 + 1) * LANES], h) for h in range(N_HEADS)]

    def body(i, carry):
        k0 = pl.multiple_of(i * kb, kb)
        bias = lax.bitcast_convert_type(key_ref[pl.ds(k0, kb), :], F32)
        s = [_dot_nt(k_ref[0, pl.ds(k0, kb), (h // 2) * LANES:(h // 2 + 1) * LANES], qh[h])
             for h in range(N_HEADS)]
        p, new = [], []
        for h in range(N_HEADS):
            m, acc = carry[2 * h:2 * h + 2]
            sh = s[h] + bias
            m_new = jnp.maximum(m, jnp.max(sh, axis=0, keepdims=True))
            p.append(jnp.exp2((sh - m_new).astype(BF16)))
            new += [m_new, jnp.exp2(m - m_new) * acc]
        for h in range(N_HEADS):
            vblk = vt_ref[0, h * LANES:(h + 1) * LANES, pl.ds(k0, kb)]
            new[2 * h + 1] = new[2 * h + 1] + jnp.dot(vblk, p[h], preferred_element_type=F32)
        return tuple(new)

    init = (jnp.full((1, qb), NEG_INF, F32), jnp.zeros((LANES, qb), F32)) * N_HEADS
    carry = lax.fori_loop(0, nkb, body, init)
    for h in range(N_HEADS):
        acc = carry[2 * h + 1]
        o_ref[0, h * HEAD_DIM:(h + 1) * HEAD_DIM, :] = acc[:HEAD_DIM, :] / acc[HEAD_DIM:HEAD_DIM + 1, :]


def _dsa_attention(qi, w, kx, q, k, vt, q_off, n_sel):
    b, sq, _ = q.shape
    skp = k.shape[1]
    qb = _pick(sq, (256, 128, 64))
    kb = _pick(skp, (1024, KEY_ALIGN))
    kern = functools.partial(_dsa_kernel, qb=qb, kb=kb, q_off=q_off, skp=skp, n_sel=n_sel)
    return pl.pallas_call(
        kern,
        grid=(b, sq // qb),
        in_specs=[pl.BlockSpec((1, qb, IDX_HEADS * IDX_DIM), lambda bi, qi_: (bi, qi_, 0)),
                  pl.BlockSpec((1, IDX_HEADS, qb), lambda bi, qi_: (bi, 0, qi_)),
                  pl.BlockSpec((1, skp, IDX_DIM), lambda bi, qi_: (bi, 0, 0),
                               pipeline_mode=pl.Buffered(1)),
                  pl.BlockSpec((1, qb, BRANCH_WIDTH), lambda bi, qi_: (bi, qi_, 0)),
                  pl.BlockSpec((1, skp, BRANCH_WIDTH), lambda bi, qi_: (bi, 0, 0),
                               pipeline_mode=pl.Buffered(1)),
                  pl.BlockSpec((1, N_HEADS * LANES, skp), lambda bi, qi_: (bi, 0, 0),
                               pipeline_mode=pl.Buffered(1))],
        out_specs=pl.BlockSpec((1, BRANCH_WIDTH, qb), lambda bi, qi_: (bi, 0, qi_)),
        out_shape=jax.ShapeDtypeStruct((b, BRANCH_WIDTH, sq), F32),
        scratch_shapes=[pltpu.VMEM((skp, qb), jnp.int32), pltpu.VMEM((1, qb), jnp.int32)],
        compiler_params=_cparams(2),
        name="dsa_attention",
    )(qi, w, kx, q, k, vt)


def _layer_norm(z, g, b):
    mu = jnp.mean(z, axis=-1, keepdims=True)
    zc = z - mu
    var = jnp.mean(zc * zc, axis=-1, keepdims=True)
    return zc * lax.rsqrt(var + LN_EPS) * g + b


def _merge_kernel(x_ref, wgate_ref, oa_ref, ob_ref, oc_ref, od_ref, wb_ref, wo_ref, g_ref, b_ref,
                  y_ref):
    xb = x_ref[...].astype(BF16)
    logits = [jnp.dot(xb, wgate_ref[:, n * D_MODEL:(n + 1) * D_MODEL], preferred_element_type=F32)
              for n in range(N_BRANCH)]
    branch = [jnp.dot(o_ref[...].astype(BF16), wb_ref[n], preferred_element_type=F32)
              for n, o_ref in enumerate((oa_ref, ob_ref, oc_ref, od_ref))]
    merged = jax.nn.sigmoid(logits[0]) * branch[0]
    for n in range(1, N_BRANCH):
        merged = merged + jax.nn.sigmoid(logits[n]) * branch[n]
    y = jnp.dot(merged.astype(BF16), wo_ref[...], preferred_element_type=F32)
    y_ref[...] = _layer_norm(ALPHA * x_ref[...] + y, g_ref[...], b_ref[...])


def _merge_out_ln(x, w_gate, o_a, o_b, o_c, o_d, w_branch, w_out, g, b):
    n = x.shape[0]
    ts = _pick(n, (512, 256, 128, 64))
    row = lambda i: (i, 0)
    obs = pl.BlockSpec((ts, BRANCH_WIDTH), row)
    return pl.pallas_call(
        _merge_kernel,
        grid=(n // ts,),
        in_specs=[pl.BlockSpec((ts, D_MODEL), row),
                  pl.BlockSpec((D_MODEL, GATE_WIDTH), lambda i: (0, 0), pipeline_mode=pl.Buffered(1)),
                  obs, obs, obs, obs,
                  pl.BlockSpec((N_BRANCH, BRANCH_WIDTH, D_MODEL), lambda i: (0, 0, 0)),
                  pl.BlockSpec((D_MODEL, D_MODEL), lambda i: (0, 0)),
                  pl.BlockSpec((1, D_MODEL), lambda i: (0, 0)),
                  pl.BlockSpec((1, D_MODEL), lambda i: (0, 0))],
        out_specs=pl.BlockSpec((ts, D_MODEL), row),
        out_shape=jax.ShapeDtypeStruct((n, D_MODEL), F32),
        compiler_params=_cparams(1),
        name="merge_out_ln",
    )(x, w_gate, o_a, o_b, o_c, o_d, w_branch, w_out, g, b)


def _ffn_kernel(x_ref, wg_ref, wu_ref, wd_ref, g_ref, b_ref, y_ref, acc_ref):
    f = pl.program_id(1)
    xb = x_ref[...].astype(BF16)
    gate = jnp.dot(xb, wg_ref[...], preferred_element_type=F32)
    up = jnp.dot(xb, wu_ref[...], preferred_element_type=F32)
    hidden = (gate * jax.nn.sigmoid(gate) * up).astype(BF16)
    part = jnp.dot(hidden, wd_ref[...], preferred_element_type=F32)

    @pl.when(f == 0)
    def _():
        acc_ref[...] = part

    @pl.when(f > 0)
    def _():
        acc_ref[...] += part

    @pl.when(f == pl.num_programs(1) - 1)
    def _():
        y_ref[...] = _layer_norm(ALPHA * x_ref[...] + acc_ref[...], g_ref[...], b_ref[...])


def _ffn_ln(x, w_gate, w_up, w_down, g, b):
    n = x.shape[0]
    ts = _pick(n, (1024, 512, 256, 128, 64))
    tf = D_FF // 2
    return pl.pallas_call(
        _ffn_kernel,
        grid=(n // ts, D_FF // tf),
        in_specs=[pl.BlockSpec((ts, D_MODEL), lambda i, f: (i, 0)),
                  pl.BlockSpec((D_MODEL, tf), lambda i, f: (0, f)),
                  pl.BlockSpec((D_MODEL, tf), lambda i, f: (0, f)),
                  pl.BlockSpec((tf, D_MODEL), lambda i, f: (f, 0)),
                  pl.BlockSpec((1, D_MODEL), lambda i, f: (0, 0)),
                  pl.BlockSpec((1, D_MODEL), lambda i, f: (0, 0))],
        out_specs=pl.BlockSpec((ts, D_MODEL), lambda i, f: (i, 0)),
        out_shape=jax.ShapeDtypeStruct((n, D_MODEL), F32),
        scratch_shapes=[pltpu.VMEM((ts, D_MODEL), F32)],
        compiler_params=_cparams(2),
        name="ffn_ln",
    )(x, w_gate, w_up, w_down, g, b)


def _pad_keys(a, axis=1):
    n = a.shape[axis]
    pad = (-n) % KEY_ALIGN
    if pad == 0:
        return a
    widths = [(0, 0)] * a.ndim
    widths[axis] = (0, pad)
    return jnp.pad(a, widths)


def _prepare_weights(w_in, mla_w_uk, mla_w_uv, w_branch, w_out, w_gate_up, w_down):
    w_rest = jnp.concatenate([w_in, jnp.zeros((D_MODEL, 1), w_in.dtype)], axis=1)
    w_rest = jnp.take(w_rest, _projection_columns(), axis=1).astype(BF16)
    w_uk = jnp.pad(mla_w_uk, ((0, 0), (0, 0), (0, LANES - MLA_D_NOPE))).reshape(MLA_D_C, -1)
    w_uv = jnp.pad(mla_w_uv, ((0, 0), (0, 0), (0, LANES - HEAD_DIM))).reshape(MLA_D_C, -1)
    w_ukv = jnp.concatenate([w_uk, w_uv], axis=1).astype(BF16)
    return dict(w_rest=w_rest, w_gates=w_in[:, REST_WIDTH:].astype(BF16), w_ukv=w_ukv,
                w_branch=w_branch.astype(BF16),
                w_out=w_out.astype(BF16), w_gate=w_gate_up[:, :D_FF].astype(BF16),
                w_up=w_gate_up[:, D_FF:].astype(BF16), w_down=w_down.astype(BF16))


def _projection_columns():
    zero_col = REST_WIDTH + GATE_WIDTH
    src = np.full(PROJ_WIDTH, zero_col, np.int32)
    o_aq, o_ckv, o_kr, o_sb, o_bd, o_ds, o_ixq, o_ixk, o_ixw = np.concatenate(
        [[0], np.cumsum(IN_SIZES[:-1])])[:9]
    hd = MLA_D_NOPE + MLA_D_ROPE

    def put(lane0, src0, n):
        src[lane0:lane0 + n] = src0 + np.arange(n)

    for h in range(N_HEADS):
        put((SEG_AQ + h) * LANES, o_aq + h * hd, hd)
    put(SEG_CKV * LANES, o_ckv, MLA_D_C)
    put(SEG_KR * LANES + MLA_ROPE_LANE, o_kr, MLA_D_ROPE)
    put(SEG_SB * LANES, o_sb, 3 * BRANCH_WIDTH)
    put(SEG_BD * LANES, o_bd, 3 * BRANCH_WIDTH)
    put(SEG_DS * LANES, o_ds, 3 * BRANCH_WIDTH)
    put(SEG_IXQ * LANES, o_ixq, IDX_HEADS * IDX_DIM)
    put(SEG_IXK * LANES, o_ixk, IDX_DIM)
    put(SEG_IXK * LANES + IXW_LANE, o_ixw, IDX_HEADS)
    return src


def _trunk_layer(x, q_off, past, wts, mla_kv_norm, band_rel_bias, ln1_g, ln1_b, ln2_g, ln2_b):
    bsz, s_len, _ = x.shape
    n = bsz * s_len
    pos = jnp.tile(q_off + jnp.arange(s_len, dtype=jnp.int32), bsz)
    x2 = x.reshape(n, D_MODEL)
    (q_a, lat_new, k_a, v_a, q_b, kv_b, sb_kv_new, q_c, kv_c, bd_kv_new,
     q_d, k_d, vt_d, ds_kv_new, q_ix, kidx_new, kx_d, w_ix) = _project(
        x2, pos, wts['w_rest'], wts['w_ukv'], mla_kv_norm[None])
    seq = lambda a: a.reshape(bsz, s_len, a.shape[-1])
    seq_t = lambda a: jnp.transpose(a.reshape(a.shape[0], bsz, s_len), (1, 0, 2))
    q_a, k_a, v_a, q_b, kv_b, q_c, kv_c, q_d, k_d, q_ix, kx_d = map(
        seq, (q_a, k_a, v_a, q_b, kv_b, q_c, kv_c, q_d, k_d, q_ix, kx_d))
    vt_d, w_ix = seq_t(vt_d), seq_t(w_ix)
    kv_state = lambda a: a.reshape(bsz, s_len, 2, N_HEADS, HEAD_DIM)
    lat_new, kidx_new = seq(lat_new), seq(kidx_new)
    sb_kv_new, bd_kv_new, ds_kv_new = kv_state(sb_kv_new), kv_state(bd_kv_new), kv_state(ds_kv_new)

    if past is None:
        kv_c = jnp.pad(kv_c, ((0, 0), (BAND_WINDOW, 0), (0, 0)))
        band_rows = bd_kv_new[:, s_len - min(BAND_WINDOW, s_len):]
        s_k = s_len
    else:
        past_lat, past_sb, past_band, past_ds, past_kidx = past
        p_len = past_lat.shape[1]
        s_k = p_len + s_len
        rows_bf16 = lambda a: a.reshape(bsz, a.shape[1], -1).astype(BF16)
        kv_past = _matmul(past_lat[..., :MLA_D_C].reshape(bsz * p_len, MLA_D_C), wts['w_ukv'])
        kv_past = kv_past.reshape(bsz, p_len, -1)
        kr_past = jnp.pad(past_lat[..., MLA_D_C:],
                          ((0, 0), (0, 0), (MLA_ROPE_LANE, LANES - MLA_ROPE_LANE - MLA_D_ROPE)))
        k_past = kv_past[..., :N_HEADS * LANES].reshape(bsz, p_len, N_HEADS, LANES) + kr_past[:, :, None]
        k_a = jnp.concatenate([rows_bf16(k_past), k_a], axis=1)
        v_past = kv_past[..., N_HEADS * LANES:] + _ones_upper_half(p_len)
        v_a = jnp.concatenate([v_past.astype(BF16), v_a], axis=1)
        kv_b = jnp.concatenate([rows_bf16(past_sb), kv_b], axis=1)
        kv_c = jnp.concatenate([rows_bf16(past_band), kv_c], axis=1)
        past_ds = past_ds.reshape(bsz, p_len, 2 * BRANCH_WIDTH)
        k_d = jnp.concatenate([past_ds[..., :BRANCH_WIDTH].astype(BF16), k_d], axis=1)
        vt_past = jnp.transpose(past_ds[..., BRANCH_WIDTH:], (0, 2, 1)).astype(BF16)
        vt_past = jnp.concatenate([vt_past.reshape(bsz, N_HEADS, HEAD_DIM, p_len),
                                   jnp.ones((bsz, N_HEADS, LANES - HEAD_DIM, p_len), BF16)], axis=2)
        vt_d = jnp.concatenate([vt_past.reshape(bsz, N_HEADS * LANES, p_len), vt_d], axis=2)
        kx_d = jnp.concatenate([past_kidx.astype(BF16), kx_d], axis=1)
        band_rows = bd_kv_new

    o_a = _mla_attention(q_a, _pad_keys(k_a), _pad_keys(v_a), q_off)
    o_b = _sb_attention(q_b, _pad_keys(kv_b), q_off)
    o_c = _band_attention(q_c, kv_c, band_rel_bias, q_off)
    n_sel = min(DSA_TOPK, s_k // 4)
    o_d = _dsa_attention(q_ix, w_ix, _pad_keys(kx_d), q_d, _pad_keys(k_d), _pad_keys(vt_d, axis=2),
                         q_off, n_sel)
    o_d = jnp.transpose(o_d, (0, 2, 1))

    flat = lambda o: o.reshape(n, BRANCH_WIDTH)
    x1 = _merge_out_ln(x2, wts['w_gates'], flat(o_a), flat(o_b), flat(o_c), flat(o_d),
                       wts['w_branch'], wts['w_out'], ln1_g[None], ln1_b[None])
    x_out = _ffn_ln(x1, wts['w_gate'], wts['w_up'], wts['w_down'], ln2_g[None], ln2_b[None])
    return x_out.reshape(bsz, s_len, D_MODEL), (lat_new, sb_kv_new, band_rows, ds_kv_new, kidx_new)


def kernel(x_prompt, x_sample, cache_mla_latent, cache_sb_kv, cache_band_kv, cache_dsa_kv, cache_dsa_kidx, w_in, mla_kv_norm, mla_w_uk, mla_w_uv, band_rel_bias, w_branch, w_out, ln1_g, ln1_b, w_gate_up, w_down, ln2_g, ln2_b):
    past_len = cache_mla_latent.shape[2]
    xp, xs = x_prompt, x_sample
    st_p, st_s = [], []
    for l in range(w_in.shape[0]):
        wts = _prepare_weights(w_in[l], mla_w_uk[l], mla_w_uv[l], w_branch[l], w_out[l],
                               w_gate_up[l], w_down[l])
        params = (wts, mla_kv_norm[l], band_rel_bias[l], ln1_g[l], ln1_b[l], ln2_g[l], ln2_b[l])
        xp, new_p = _trunk_layer(xp, 0, None, *params)
        past = (cache_mla_latent[l], cache_sb_kv[l], cache_band_kv[l], cache_dsa_kv[l],
                cache_dsa_kidx[l])
        xs, new_s = _trunk_layer(xs, past_len, past, *params)
        st_p.append(new_p)
        st_s.append(new_s)
    stack = lambda st, i: jnp.stack([s[i] for s in st])
    return (xp, xs) + tuple(stack(st_p, i) for i in range(5)) + tuple(stack(st_s, i) for i in range(5))
```

```python
import functools
import math

import numpy as np
import jax
import jax.numpy as jnp
from jax import lax
from jax.experimental import pallas as pl
from jax.experimental.pallas import tpu as pltpu

D_MODEL = 1024
CHUNK = 64
CHUNK_SHIFT = 6
N_BRANCH = 4
N_HEADS = 4
HEAD_DIM = 64
BRANCH_WIDTH = N_HEADS * HEAD_DIM
MLA_D_C = 128
MLA_D_NOPE = 64
MLA_D_ROPE = 32
MLA_THETA = 10000.0
ROPE_THETA = 500000.0
BAND_LEFT_CHUNKS = 8
BAND_WINDOW = BAND_LEFT_CHUNKS * CHUNK
REL_CLIP = 128
IDX_HEADS = 8
IDX_DIM = 64
DSA_TOPK = 256
D_FF = ((8 * D_MODEL // 3 + 255) // 256) * 256
DEPTH = 2
ALPHA = (2 * DEPTH) ** 0.25
NEG_INF = -1e30
LOG2_E = math.log2(math.e)
LN_EPS = 1e-5
IN_SIZES = (N_HEADS * (MLA_D_NOPE + MLA_D_ROPE), MLA_D_C, MLA_D_ROPE,
            3 * BRANCH_WIDTH, 3 * BRANCH_WIDTH, 3 * BRANCH_WIDTH,
            IDX_HEADS * IDX_DIM, IDX_DIM, IDX_HEADS, N_BRANCH * D_MODEL)
REST_WIDTH = sum(IN_SIZES[:-1])
GATE_WIDTH = IN_SIZES[-1]

LANES = 128
KEY_ALIGN = 512
INT_MIN = -2 ** 31
VMEM_LIMIT = 56 * 1024 * 1024

F32 = jnp.float32
BF16 = jnp.bfloat16


def _cparams(n_axes):
    return pltpu.CompilerParams(dimension_semantics=("arbitrary",) * n_axes,
                                vmem_limit_bytes=VMEM_LIMIT)


def _pick(n, candidates):
    for c in candidates:
        if n % c == 0:
            return c
    return n


def _ones_upper_half(rows):
    lane = lax.broadcasted_iota(jnp.int32, (rows, N_HEADS * LANES), 1)
    return jnp.where((lane & (LANES - 1)) >= HEAD_DIM, 1.0, 0.0)


def _dot_nt(a, b):
    return lax.dot_general(a, b, (((1,), (1,)), ((), ())), preferred_element_type=F32)


def _mm_kernel(a_ref, b_ref, o_ref):
    o_ref[...] = jnp.dot(a_ref[...].astype(BF16), b_ref[...], preferred_element_type=F32)


def _matmul(a, b):
    m, k = a.shape
    n = b.shape[1]
    tm = _pick(m, (1024, 512, 256, 128, 64, 32, 16, 8))
    tn = _pick(n, (512, 384, 256, 128))
    return pl.pallas_call(
        _mm_kernel,
        grid=(m // tm, n // tn),
        in_specs=[pl.BlockSpec((tm, k), lambda i, j: (i, 0)),
                  pl.BlockSpec((k, tn), lambda i, j: (0, j))],
        out_specs=pl.BlockSpec((tm, tn), lambda i, j: (i, j)),
        out_shape=jax.ShapeDtypeStruct((m, n), F32),
        compiler_params=_cparams(2),
        name="matmul",
    )(a, b)


SEG_AQ = 0
SEG_CKV = 4
SEG_KR = 5
SEG_SB = 6
SEG_BD = 12
SEG_DS = 18
SEG_IXQ = 24
SEG_IXK = 28
N_SEG = 30
IXW_LANE = 96
PROJ_WIDTH = N_SEG * LANES
MLA_ROPE_LANE = 64
LAT_WIDTH = MLA_D_C + MLA_D_ROPE


def _proj_kernel(x_ref, w_ref, wukv_ref, g_ref, ca_ref, sma_ref, spa_ref, cp_ref, smp_ref, spp_ref,
                 qa_ref, lat_ref, ka_ref, va_ref,
                 qb_ref, kvb_ref, sbkv_ref, qc_ref, kvc_ref, bdkv_ref,
                 qd_ref, kd_ref, vtd_ref, dskv_ref, qix_ref, kidx_ref, kx_ref, wix_ref):
    proj = jnp.dot(x_ref[...].astype(BF16), w_ref[...], preferred_element_type=F32)
    seg = lambda s, n=1: proj[:, s * LANES:(s + n) * LANES]

    def rope(t, c_ref, sm_ref, sp_ref, half):
        return (t * c_ref[...] + pltpu.roll(t, LANES - half, 1) * sm_ref[...]
                + pltpu.roll(t, half, 1) * sp_ref[...])

    rope_a = functools.partial(rope, c_ref=ca_ref, sm_ref=sma_ref, sp_ref=spa_ref, half=MLA_D_ROPE // 2)
    rope_p = functools.partial(rope, c_ref=cp_ref, sm_ref=smp_ref, sp_ref=spp_ref, half=HEAD_DIM // 8)
    head_scale = HEAD_DIM ** -0.5 * LOG2_E

    mla_scale = (MLA_D_NOPE + MLA_D_ROPE) ** -0.5 * LOG2_E
    for h in range(N_HEADS):
        qa_ref[:, h * LANES:(h + 1) * LANES] = (rope_a(seg(SEG_AQ + h)) * mla_scale).astype(BF16)
    ckv = seg(SEG_CKV)
    ckv = ckv * lax.rsqrt(jnp.mean(ckv * ckv, axis=-1, keepdims=True) + LN_EPS) * g_ref[...]
    kr = rope_a(seg(SEG_KR))
    lat_ref[:, :MLA_D_C] = ckv
    lat_ref[:, MLA_D_C:] = kr[:, MLA_ROPE_LANE:MLA_ROPE_LANE + MLA_D_ROPE]
    kv_a = jnp.dot(ckv.astype(BF16), wukv_ref[...], preferred_element_type=F32)
    for h in range(N_HEADS):
        ka_ref[:, h * LANES:(h + 1) * LANES] = (kv_a[:, h * LANES:(h + 1) * LANES] + kr).astype(BF16)
    va_ref[...] = (kv_a[:, N_HEADS * LANES:] + _ones_upper_half(kv_a.shape[0])).astype(BF16)

    for s0, q_ref, kv_ref, new_ref in ((SEG_SB, qb_ref, kvb_ref, sbkv_ref),
                                       (SEG_BD, qc_ref, kvc_ref, bdkv_ref)):
        q_ref[...] = (seg(s0, 2) * head_scale).astype(BF16)
        kv = seg(s0 + 2, 4)
        new_ref[...] = kv
        kv_ref[...] = kv.astype(BF16)

    for p in range(2):
        qd_ref[:, p * LANES:(p + 1) * LANES] = (rope_p(seg(SEG_DS + p)) * head_scale).astype(BF16)
        k_rot = rope_p(seg(SEG_DS + 2 + p))
        dskv_ref[:, p * LANES:(p + 1) * LANES] = k_rot
        kd_ref[:, p * LANES:(p + 1) * LANES] = k_rot.astype(BF16)
    v_d = seg(SEG_DS + 4, 2)
    dskv_ref[:, BRANCH_WIDTH:] = v_d
    vtd_ref[...] = v_d.T.astype(BF16)
    for p in range(IDX_HEADS // 2):
        qix_ref[:, p * LANES:(p + 1) * LANES] = (rope_p(seg(SEG_IXQ + p)) * IDX_DIM ** -0.5).astype(BF16)
    ixk = rope_p(seg(SEG_IXK))
    kidx_ref[...] = ixk[:, :IDX_DIM]
    kx_ref[...] = ixk[:, :IDX_DIM].astype(BF16)
    wix_ref[...] = ixk.T[IXW_LANE:IXW_LANE + IDX_HEADS, :]


def _rope_tables(pos, theta, width, starts):
    half = width // 2
    inv = jnp.exp(jnp.arange(half, dtype=F32) * (-2.0 * math.log(theta) / width))
    freq_idx = np.zeros(LANES, np.int32)
    x1 = np.zeros(LANES, np.float32)
    x2 = np.zeros(LANES, np.float32)
    for s in starts:
        freq_idx[s:s + width] = np.tile(np.arange(half), 2)
        x1[s:s + half] = 1.0
        x2[s + half:s + width] = 1.0
    inv_lane = inv[freq_idx] * jnp.asarray(x1 + x2)
    ang = pos.astype(F32)[:, None] * inv_lane[None, :]
    sin = jnp.sin(ang)
    return jnp.cos(ang), -sin * x1[None, :], sin * x2[None, :]


def _project(x, pos, w_rest, w_ukv, kv_norm):
    n = x.shape[0]
    ts = _pick(n, (512, 256, 128, 64))
    tables = (_rope_tables(pos, MLA_THETA, MLA_D_ROPE, (MLA_ROPE_LANE,))
              + _rope_tables(pos, ROPE_THETA, HEAD_DIM // 4, (0, HEAD_DIM)))
    row = lambda i: (i, 0)
    fixed = lambda i: (0, 0)
    rows = lambda w: pl.BlockSpec((ts, w), row)
    out = lambda w, dt: jax.ShapeDtypeStruct((n, w), dt)
    specs = [
        (N_HEADS * LANES, BF16), (LAT_WIDTH, F32), (N_HEADS * LANES, BF16), (N_HEADS * LANES, BF16),
        (BRANCH_WIDTH, BF16), (2 * BRANCH_WIDTH, BF16), (2 * BRANCH_WIDTH, F32),
        (BRANCH_WIDTH, BF16), (2 * BRANCH_WIDTH, BF16), (2 * BRANCH_WIDTH, F32),
        (BRANCH_WIDTH, BF16), (BRANCH_WIDTH, BF16), None, (2 * BRANCH_WIDTH, F32),
        (IDX_HEADS * IDX_DIM, BF16), (IDX_DIM, F32), (IDX_DIM, BF16), None]
    out_specs, out_shape = [], []
    for k, spec in enumerate(specs):
        if spec is None:
            rows_t, dt = ((BRANCH_WIDTH, BF16), (IDX_HEADS, F32))[k > 12]
            out_specs.append(pl.BlockSpec((rows_t, ts), lambda i: (0, i)))
            out_shape.append(jax.ShapeDtypeStruct((rows_t, n), dt))
        else:
            out_specs.append(rows(spec[0]))
            out_shape.append(out(*spec))
    return pl.pallas_call(
        _proj_kernel,
        grid=(n // ts,),
        in_specs=[rows(D_MODEL),
                  pl.BlockSpec((D_MODEL, PROJ_WIDTH), fixed, pipeline_mode=pl.Buffered(1)),
                  pl.BlockSpec((MLA_D_C, 2 * N_HEADS * LANES), fixed),
                  pl.BlockSpec((1, MLA_D_C), fixed)] + [rows(LANES)] * 6,
        out_specs=out_specs,
        out_shape=out_shape,
        compiler_params=_cparams(1),
        name="project_prepare",
    )(x, w_rest, w_ukv, kv_norm, *tables)


def _mla_kernel(q_ref, k_ref, v_ref, o_ref, *, qb, kb, q_off, skp):
    q0 = pl.program_id(1) * qb
    hi = jnp.minimum((((q_off + q0 + qb - 1) >> CHUNK_SHIFT) + 1) * CHUNK, skp)
    nkb = (hi + kb - 1) // kb
    qpos = q_off + q0 + lax.broadcasted_iota(jnp.int32, (qb, 1), 0)
    cend = ((qpos >> CHUNK_SHIFT) + 1) * CHUNK
    col = lax.broadcasted_iota(jnp.int32, (qb, kb), 1)
    lane = lax.broadcasted_iota(jnp.int32, (qb, LANES), 1)
    n_full = jnp.minimum(((((q_off + q0) >> CHUNK_SHIFT) + 1) * CHUNK) // kb, nkb)

    def body(i, carry, masked):
        k0 = pl.multiple_of(i * kb, kb)
        s = [_dot_nt(q_ref[0, :, h * LANES:(h + 1) * LANES],
                     k_ref[0, pl.ds(k0, kb), h * LANES:(h + 1) * LANES]) for h in range(N_HEADS)]
        if masked:
            vis = col + k0 < cend
            s = [jnp.where(vis, sh, NEG_INF) for sh in s]
        p, new = [], []
        for h in range(N_HEADS):
            m, acc = carry[2 * h:2 * h + 2]
            m_new = jnp.maximum(m, jnp.max(s[h], axis=1, keepdims=True))
            p.append(jnp.exp2((s[h] - m_new).astype(BF16)))
            new += [m_new, jnp.exp2(m - m_new) * acc]
        for h in range(N_HEADS):
            vblk = v_ref[0, pl.ds(k0, kb), h * LANES:(h + 1) * LANES]
            new[2 * h + 1] = new[2 * h + 1] + jnp.dot(p[h], vblk, preferred_element_type=F32)
        return tuple(new)

    init = (jnp.full((qb, 1), NEG_INF, F32), jnp.zeros((qb, LANES), F32)) * N_HEADS
    carry = lax.fori_loop(0, n_full, functools.partial(body, masked=False), init)
    carry = lax.fori_loop(n_full, nkb, functools.partial(body, masked=True), carry)
    outs = [carry[2 * h + 1] / pltpu.roll(carry[2 * h + 1], HEAD_DIM, 1) for h in range(N_HEADS)]
    for pair in range(N_HEADS // 2):
        o_ref[0, :, pair * LANES:(pair + 1) * LANES] = jnp.where(
            lane < HEAD_DIM, outs[2 * pair], pltpu.roll(outs[2 * pair + 1], HEAD_DIM, 1))


def _mla_attention(q, k, v, q_off):
    b, sq, _ = q.shape
    skp = k.shape[1]
    qb = _pick(sq, (256, 128, 64))
    kb = _pick(skp, (1024, KEY_ALIGN))
    kern = functools.partial(_mla_kernel, qb=qb, kb=kb, q_off=q_off, skp=skp)
    return pl.pallas_call(
        kern,
        grid=(b, sq // qb),
        in_specs=[pl.BlockSpec((1, qb, N_HEADS * LANES), lambda bi, qi: (bi, qi, 0)),
                  pl.BlockSpec((1, skp, N_HEADS * LANES), lambda bi, qi: (bi, 0, 0),
                               pipeline_mode=pl.Buffered(1)),
                  pl.BlockSpec((1, skp, N_HEADS * LANES), lambda bi, qi: (bi, 0, 0),
                               pipeline_mode=pl.Buffered(1))],
        out_specs=pl.BlockSpec((1, qb, BRANCH_WIDTH), lambda bi, qi: (bi, qi, 0)),
        out_shape=jax.ShapeDtypeStruct((b, sq, BRANCH_WIDTH), F32),
        compiler_params=_cparams(2),
        name="mla_attention",
    )(q, k, v)


def _head_mask(x_pair, h):
    lane = lax.broadcasted_iota(jnp.int32, x_pair.shape, 1)
    keep = (lane < HEAD_DIM) if h % 2 == 0 else (lane >= HEAD_DIM)
    return jnp.where(keep, x_pair, jnp.zeros_like(x_pair))


def _sb_kernel(q_ref, kv_ref, t_ref, o_ref, *, qb, kb, sub, q_off):
    q0 = pl.program_id(1) * qb
    hi = q_off + q0 + qb - 1
    nkb = (hi + kb - 1) // kb
    qpos = q_off + q0 + lax.broadcasted_iota(jnp.int32, (qb, 1), 0)
    col = lax.broadcasted_iota(jnp.int32, (qb, sub), 1)
    lane = lax.broadcasted_iota(jnp.int32, (qb, LANES), 1)
    tri = t_ref[...]
    n_full = jnp.minimum((q_off + q0) // kb, nkb)
    qh = [_head_mask(q_ref[0, :, (h // 2) * LANES:(h // 2 + 1) * LANES], h) for h in range(N_HEADS)]

    def body(i, carry, masked):
        k0 = pl.multiple_of(i * kb, kb)
        carry = list(carry)
        units = [(j, h) for j in reversed(range(kb // sub)) for h in range(N_HEADS)]
        strict = {j: col + (k0 + j * sub) < qpos for j in range(kb // sub)} if masked else None
        z = {}
        for j, h in units:
            pair = h // 2
            kblk = kv_ref[0, pl.ds(k0 + j * sub, sub), pair * LANES:(pair + 1) * LANES]
            z[j, h] = _dot_nt(qh[h], kblk)
        log_1m, suffix = {}, {}
        for u in units:
            nz = -z[u]
            t = jnp.minimum(nz, 0.0) - jnp.log2(1.0 + jnp.exp2(jnp.minimum(z[u], nz)))
            if masked:
                t = jnp.where(strict[u[0]], t, 0.0)
            log_1m[u] = t
        for u in units:
            hi_part = log_1m[u].astype(BF16)
            lo_part = (log_1m[u] - hi_part.astype(F32)).astype(BF16)
            suffix[u] = (jnp.dot(hi_part, tri, preferred_element_type=F32)
                         + jnp.dot(lo_part, tri, preferred_element_type=F32))
        for j, h in units:
            u = (j, h)
            run, acc = carry[2 * h:2 * h + 2]
            expo = z[u] + log_1m[u] + suffix[u] + run
            if masked:
                expo = jnp.where(strict[j], expo, NEG_INF)
            a = jnp.exp2(expo.astype(BF16))
            pair = h // 2
            vblk = kv_ref[0, pl.ds(k0 + j * sub, sub),
                          BRANCH_WIDTH + pair * LANES:BRANCH_WIDTH + (pair + 1) * LANES]
            carry[2 * h + 1] = acc + jnp.dot(a, vblk, preferred_element_type=F32)
            carry[2 * h] = run + jnp.sum(log_1m[u], axis=1, keepdims=True)
        return tuple(carry)

    def reversed_body(lo, n, masked):
        return lambda i, carry: body(lo + n - 1 - i, carry, masked)

    init = (jnp.zeros((qb, 1), F32), jnp.zeros((qb, LANES), F32)) * N_HEADS
    carry = lax.fori_loop(0, nkb - n_full, reversed_body(n_full, nkb - n_full, True), init)
    carry = lax.fori_loop(0, n_full, reversed_body(0, n_full, False), carry)
    outs = [carry[2 * h + 1] for h in range(N_HEADS)]
    for pair in range(N_HEADS // 2):
        o_ref[0, :, pair * LANES:(pair + 1) * LANES] = jnp.where(
            lane < HEAD_DIM, outs[2 * pair], outs[2 * pair + 1])


def _sb_attention(q, kv, q_off):
    b, sq, _ = q.shape
    skp = kv.shape[1]
    qb = _pick(sq, (256, 128, 64))
    sub = 256
    kb = _pick(skp, (1024, KEY_ALIGN))
    tri =jnp.asarray(np.tril(np.ones((sub, sub), np.float32), -1), BF16)
    kern = functools.partial(_sb_kernel, qb=qb, kb=kb, sub=sub, q_off=q_off)
    return pl.pallas_call(
        kern,
        grid=(b, sq // qb),
        in_specs=[pl.BlockSpec((1, qb, BRANCH_WIDTH), lambda bi, qi: (bi, qi, 0)),
                  pl.BlockSpec((1, skp, 2 * BRANCH_WIDTH), lambda bi, qi: (bi, 0, 0),
                               pipeline_mode=pl.Buffered(1)),
                  pl.BlockSpec((sub, sub), lambda bi, qi: (0, 0))],
        out_specs=pl.BlockSpec((1, qb, BRANCH_WIDTH), lambda bi, qi: (bi, qi, 0)),
        out_shape=jax.ShapeDtypeStruct((b, sq, BRANCH_WIDTH), F32),
        compiler_params=_cparams(2),
        name="stick_breaking_attention",
    )(q, kv, tri)


def _band_kernel(bias_ref, q_ref, kv_ref, o_ref, tile_ref, *, qb, win, q_off):
    first = (pl.program_id(0) == 0) & (pl.program_id(1) == 0)
    row = lax.broadcasted_iota(jnp.int32, (qb, win), 0)
    col = lax.broadcasted_iota(jnp.int32, (qb, win), 1)

    @pl.when(first)
    def _():
        rel = jnp.clip(row + BAND_WINDOW - col, -REL_CLIP, REL_CLIP) + REL_CLIP
        qch = row >> CHUNK_SHIFT
        kch = col >> CHUNK_SHIFT
        in_band = (kch >= qch) & (kch <= qch + BAND_LEFT_CHUNKS)
        for h in range(N_HEADS):
            tile_ref[h] = jnp.full((qb, win), bias_ref[h, 0], F32)

        def fill(r, c):
            hit = rel == r
            for h in range(N_HEADS):
                tile_ref[h] = jnp.where(hit, bias_ref[h, r], tile_ref[h])
            return c

        lax.fori_loop(1, 2 * REL_CLIP + 1, fill, 0)
        for h in range(N_HEADS):
            tile_ref[h] = jnp.where(in_band, tile_ref[h] * LOG2_E, NEG_INF)

    q0 = pl.multiple_of(pl.program_id(1) * qb, qb)
    kpos = col + (q_off + q0 - BAND_WINDOW)
    lane = lax.broadcasted_iota(jnp.int32, (qb, LANES), 1)
    outs = []
    for h in range(N_HEADS):
        pair = h // 2
        qh = _head_mask(q_ref[0, :, pair * LANES:(pair + 1) * LANES], h)
        kwin = kv_ref[0, pl.ds(q0, win), pair * LANES:(pair + 1) * LANES]
        s = _dot_nt(qh, kwin) + tile_ref[h]
        s = jnp.where(kpos >= 0, s, NEG_INF)
        m = jnp.max(s, axis=1, keepdims=True)
        p = jnp.exp2(s - m)
        l = jnp.sum(p, axis=1, keepdims=True)
        vwin = kv_ref[0, pl.ds(q0, win),
                      BRANCH_WIDTH + pair * LANES:BRANCH_WIDTH + (pair + 1) * LANES]
        outs.append(jnp.dot(p.astype(BF16), vwin, preferred_element_type=F32) / l)
    for pair in range(N_HEADS // 2):
        o_ref[0, :, pair * LANES:(pair + 1) * LANES] = jnp.where(
            lane < HEAD_DIM, outs[2 * pair], outs[2 * pair + 1])


def _band_attention(q, kv, rel_bias, q_off):
    b, sq, _ = q.shape
    sk = kv.shape[1]
    qb = _pick(sq, (256, 128, 64))
    win = qb + BAND_WINDOW
    kern = functools.partial(_band_kernel, qb=qb, win=win, q_off=q_off)
    return pl.pallas_call(
        kern,
        grid=(b, sq // qb),
        in_specs=[pl.BlockSpec(memory_space=pltpu.SMEM),
                  pl.BlockSpec((1, qb, BRANCH_WIDTH), lambda bi, qi: (bi, qi, 0)),
                  pl.BlockSpec((1, sk, 2 * BRANCH_WIDTH), lambda bi, qi: (bi, 0, 0),
                               pipeline_mode=pl.Buffered(1))],
        out_specs=pl.BlockSpec((1, qb, BRANCH_WIDTH), lambda bi, qi: (bi, qi, 0)),
        out_shape=jax.ShapeDtypeStruct((b, sq, BRANCH_WIDTH), F32),
        scratch_shapes=[pltpu.VMEM((N_HEADS, qb, win), F32)],
        compiler_params=_cparams(2),
        name="band_attention",
    )(rel_bias, q, kv)


def _dsa_kernel(qi_ref, w_ref, kx_ref, q_ref, k_ref, vt_ref, o_ref, key_ref, key16_ref, cut_ref,
                *, qb, kb, q_off, skp, n_sel):
    q0 = pl.program_id(1) * qb
    hi = jnp.minimum((((q_off + q0 + qb - 1) >> CHUNK_SHIFT) + 1) * CHUNK, skp)
    nkb = (hi + kb - 1) // kb
    qpos = q_off + q0 + lax.broadcasted_iota(jnp.int32, (1, qb), 1)
    cend = ((qpos >> CHUNK_SHIFT) + 1) * CHUNK
    row = lax.broadcasted_iota(jnp.int32, (kb, qb), 0)
    neg_key = lax.bitcast_convert_type(jnp.full((1, 1), NEG_INF, F32), jnp.int32) ^ 0x7FFFFFFF

    def score_block(i, c):
        k0 = pl.multiple_of(i * kb, kb)
        kx = kx_ref[0, pl.ds(k0, kb), :]
        r = [_dot_nt(kx, qi_ref[0, :, j * IDX_DIM:(j + 1) * IDX_DIM]) for j in range(IDX_HEADS)]
        score = jnp.zeros((kb, qb), F32)
        for j in range(IDX_HEADS):
            wj = w_ref[0, j:j + 1, :] * (IDX_HEADS ** -0.5)
            score = score + wj * jnp.maximum(r[j], 0.0)
        score = jnp.where(score == 0.0, 0.0, score)
        score = jnp.where(row + k0 < cend, score, NEG_INF)
        bits = lax.bitcast_convert_type(score, jnp.int32)
        keys = jnp.where(bits < 0, bits ^ 0x7FFFFFFF, bits)
        key_ref[pl.ds(k0, kb), :] = keys
        key16_ref[pl.ds(k0, kb), :] = (keys >> 16).astype(jnp.int16)
        return c

    lax.fori_loop(0, nkb, score_block, 0)

    slab = 64 if kb % 64 == 0 else kb

    def count_ge(cand):
        def blk(i, cnt):
            k0 = pl.multiple_of(i * kb, kb)
            for s in range(kb // slab):
                keys = key_ref[pl.ds(k0 + s * slab, slab), :]
                cnt = cnt + jnp.where(keys >= cand, 1.0, 0.0)
            return cnt
        part = lax.fori_loop(0, nkb, blk, jnp.zeros((slab, qb), F32))
        return jnp.sum(part, axis=0, keepdims=True)

    slab16 = 128 if kb % 128 == 0 else kb

    def count16_ge(cand16):
        def blk(i, cnt):
            k0 = pl.multiple_of(i * kb, kb)
            for s in range(kb // slab16):
                k16 = key16_ref[pl.ds(k0 + s * slab16, slab16), :]
                cnt = cnt + jnp.where(k16 >= cand16, jnp.int16(1), jnp.int16(0))
            return cnt
        part = lax.fori_loop(0, nkb, blk, jnp.zeros((slab16, qb), jnp.int16))
        return jnp.sum(part.astype(F32), axis=0, keepdims=True)

    def undecided(t_end):
        def cond(state):
            t, _, n_ge = state
            return (t < t_end) & (jnp.max(jnp.abs(n_ge - n_sel)) > 0.0)
        return cond

    def bisect(to_16bit):
        def step(state):
            t, thr, n_ge = state
            cand = thr + jnp.left_shift(jnp.int32(1), 31 - t)
            cnt = count16_ge(to_16bit(cand))
            ok = cnt >= n_sel
            return t + 1, jnp.where(ok, cand, thr), jnp.where(ok, cnt, n_ge)
        return step

    stored = (nkb * kb).astype(F32)
    state = lax.while_loop(
        undecided(16), bisect(lambda cand: (cand >> 16).astype(jnp.int16)),
        (jnp.int32(0), jnp.full((1, qb), INT_MIN, jnp.int32), jnp.full((1, qb), stored, F32)))

    @pl.when(undecided(32)(state))
    def _():
        thr_hi = state[1] >> 16

        def low_halves(i, c):
            k0 = pl.multiple_of(i * kb, kb)
            keys = key_ref[pl.ds(k0, kb), :]
            hi16 = keys >> 16
            low = jnp.where(hi16 == thr_hi, (keys & 0xFFFF) - 0x8000,
                            jnp.where(hi16 > thr_hi, 0x7FFF, -0x8000))
            key16_ref[pl.ds(k0, kb), :] = low.astype(jnp.int16)
            return c

        lax.fori_loop(0, nkb, low_halves, 0)

    _, thr, n_ge = lax.while_loop(
        undecided(32), bisect(lambda cand: ((cand & 0xFFFF) - 0x8000).astype(jnp.int16)), state)
    excess = jnp.where((n_ge > n_sel) & (thr > neg_key), 1.0, 0.0)
    cut_ref[...] = jnp.full((1, qb), skp, jnp.int32)

    @pl.when(jnp.max(excess) > 0.0)
    def _():
        need = n_sel - count_ge(thr + 1)

        def count_tied_below(limit):
            def blk(i, cnt):
                k0 = pl.multiple_of(i * kb, kb)
                tied = (key_ref[pl.ds(k0, kb), :] == thr) & (row + k0 < limit)
                return cnt + jnp.sum(jnp.where(tied, 1.0, 0.0), axis=0, keepdims=True)
            return lax.fori_loop(0, nkb, blk, jnp.zeros((1, qb), F32))

        n_bits = max(1, (skp - 1).bit_length())

        def bisect_cut(t, cut):
            cand = cut + jnp.left_shift(jnp.int32(1), n_bits - 1 - t)
            return jnp.where(count_tied_below(cand) < need, cand, cut)

        cut = lax.fori_loop(0, n_bits, bisect_cut, jnp.zeros((1, qb), jnp.int32))
        cut_ref[...] = jnp.where(excess > 0.0, cut + 1, skp)

    cut = cut_ref[...]

    def mask_block(i, c):
        k0 = pl.multiple_of(i * kb, kb)
        keys = key_ref[pl.ds(k0, kb), :]
        kpos = row + k0
        sel = (keys >= thr) & ((keys != thr) | (kpos < cut)) & (kpos < cend)
        key_ref[pl.ds(k0, kb), :] = lax.bitcast_convert_type(jnp.where(sel, 0.0, NEG_INF), jnp.int32)
        return c

    lax.fori_loop(0, nkb, mask_block, 0)

    qh = [_head_mask(q_ref[0, :, (h // 2) * LANES:(h // 2 + 1) * LANES], h) for h in range(N_HEADS)]

    def body(i, carry):
        k0 = pl.multiple_of(i * kb, kb)
        bias = lax.bitcast_convert_type(key_ref[pl.ds(k0, kb), :], F32)
        s = [_dot_nt(k_ref[0, pl.ds(k0, kb), (h // 2) * LANES:(h // 2 + 1) * LANES], qh[h])
             for h in range(N_HEADS)]
        p, new = [], []
        for h in range(N_HEADS):
            m, acc = carry[2 * h:2 * h + 2]
            sh = s[h] + bias
            m_new = jnp.maximum(m, jnp.max(sh, axis=0, keepdims=True))
            p.append(jnp.exp2((sh - m_new).astype(BF16)))
            new += [m_new, jnp.exp2(m - m_new) * acc]
        for h in range(N_HEADS):
            vblk = jnp.concatenate([vt_ref[0, h * HEAD_DIM:(h + 1) * HEAD_DIM, pl.ds(k0, kb)],
                                    jnp.ones((LANES - HEAD_DIM, kb), BF16)], axis=0)
            new[2 * h + 1] = new[2 * h + 1] + jnp.dot(vblk, p[h], preferred_element_type=F32)
        return tuple(new)

    init = (jnp.full((1, qb), NEG_INF, F32), jnp.zeros((LANES, qb), F32)) * N_HEADS
    carry = lax.fori_loop(0, nkb, body, init)
    for h in range(N_HEADS):
        acc = carry[2 * h + 1]
        o_ref[0, h * HEAD_DIM:(h + 1) * HEAD_DIM, :] = acc[:HEAD_DIM, :] / acc[HEAD_DIM:HEAD_DIM + 1, :]


def _dsa_attention(qi, w, kx, q, k, vt, q_off, n_sel):
    b, sq, _ = q.shape
    skp = k.shape[1]
    qb = _pick(sq, (256, 128, 64))
    kb = _pick(skp, (1024, KEY_ALIGN))
    kern = functools.partial(_dsa_kernel, qb=qb, kb=kb, q_off=q_off, skp=skp, n_sel=n_sel)
    return pl.pallas_call(
        kern,
        grid=(b, sq // qb),
        in_specs=[pl.BlockSpec((1, qb, IDX_HEADS * IDX_DIM), lambda bi, qi_: (bi, qi_, 0)),
                  pl.BlockSpec((1, IDX_HEADS, qb), lambda bi, qi_: (bi, 0, qi_)),
                  pl.BlockSpec((1, skp, IDX_DIM), lambda bi, qi_: (bi, 0, 0),
                               pipeline_mode=pl.Buffered(1)),
                  pl.BlockSpec((1, qb, BRANCH_WIDTH), lambda bi, qi_: (bi, qi_, 0)),
                  pl.BlockSpec((1, skp, BRANCH_WIDTH), lambda bi, qi_: (bi, 0, 0),
                               pipeline_mode=pl.Buffered(1)),
                  pl.BlockSpec((1, BRANCH_WIDTH, skp), lambda bi, qi_: (bi, 0, 0),
                               pipeline_mode=pl.Buffered(1))],
        out_specs=pl.BlockSpec((1, BRANCH_WIDTH, qb), lambda bi, qi_: (bi, 0, qi_)),
        out_shape=jax.ShapeDtypeStruct((b, BRANCH_WIDTH, sq), F32),
        scratch_shapes=[pltpu.VMEM((skp, qb), jnp.int32), pltpu.VMEM((skp, qb), jnp.int16),
                        pltpu.VMEM((1, qb), jnp.int32)],
        compiler_params=_cparams(2),
        name="dsa_attention",
    )(qi, w, kx, q, k, vt)


def _layer_norm(z, g, b):
    mu = jnp.mean(z, axis=-1, keepdims=True)
    zc = z - mu
    var = jnp.mean(zc * zc, axis=-1, keepdims=True)
    return zc * lax.rsqrt(var + LN_EPS) * g + b


def _merge_kernel(x_ref, wgate_ref, oa_ref, ob_ref, oc_ref, od_ref, wb_ref, wo_ref, g_ref, b_ref,
                  y_ref):
    xb = x_ref[...].astype(BF16)
    logits = [jnp.dot(xb, wgate_ref[:, n * D_MODEL:(n + 1) * D_MODEL], preferred_element_type=F32)
              for n in range(N_BRANCH)]
    branch = [jnp.dot(o_ref[...].astype(BF16), wb_ref[n], preferred_element_type=F32)
              for n, o_ref in enumerate((oa_ref, ob_ref, oc_ref, od_ref))]
    merged = jax.nn.sigmoid(logits[0]) * branch[0]
    for n in range(1, N_BRANCH):
        merged = merged + jax.nn.sigmoid(logits[n]) * branch[n]
    y = jnp.dot(merged.astype(BF16), wo_ref[...], preferred_element_type=F32)
    y_ref[...] = _layer_norm(ALPHA * x_ref[...] + y, g_ref[...], b_ref[...])


def _merge_out_ln(x, w_gate, o_a, o_b, o_c, o_d, w_branch, w_out, g, b):
    n = x.shape[0]
    ts = _pick(n, (512, 256, 128, 64))
    row = lambda i: (i, 0)
    obs = pl.BlockSpec((ts, BRANCH_WIDTH), row)
    return pl.pallas_call(
        _merge_kernel,
        grid=(n // ts,),
        in_specs=[pl.BlockSpec((ts, D_MODEL), row),
                  pl.BlockSpec((D_MODEL, GATE_WIDTH), lambda i: (0, 0), pipeline_mode=pl.Buffered(1)),
                  obs, obs, obs, obs,
                  pl.BlockSpec((N_BRANCH, BRANCH_WIDTH, D_MODEL), lambda i: (0, 0, 0)),
                  pl.BlockSpec((D_MODEL, D_MODEL), lambda i: (0, 0)),
                  pl.BlockSpec((1, D_MODEL), lambda i: (0, 0)),
                  pl.BlockSpec((1, D_MODEL), lambda i: (0, 0))],
        out_specs=pl.BlockSpec((ts, D_MODEL), row),
        out_shape=jax.ShapeDtypeStruct((n, D_MODEL), F32),
        compiler_params=_cparams(1),
        name="merge_out_ln",
    )(x, w_gate, o_a, o_b, o_c, o_d, w_branch, w_out, g, b)


def _ffn_kernel(x_ref, wg_ref, wu_ref, wd_ref, g_ref, b_ref, y_ref, acc_ref):
    f = pl.program_id(1)
    xb = x_ref[...].astype(BF16)
    gate = jnp.dot(xb, wg_ref[...], preferred_element_type=F32)
    up = jnp.dot(xb, wu_ref[...], preferred_element_type=F32)
    hidden = (gate * jax.nn.sigmoid(gate) * up).astype(BF16)
    part = jnp.dot(hidden, wd_ref[...], preferred_element_type=F32)

    @pl.when(f == 0)
    def _():
        acc_ref[...] = part

    @pl.when(f > 0)
    def _():
        acc_ref[...] += part

    @pl.when(f == pl.num_programs(1) - 1)
    def _():
        y_ref[...] = _layer_norm(ALPHA * x_ref[...] + acc_ref[...], g_ref[...], b_ref[...])


def _ffn_ln(x, w_gate, w_up, w_down, g, b):
    n = x.shape[0]
    ts = _pick(n, (1024, 512, 256, 128, 64))
    tf = D_FF // 2
    return pl.pallas_call(
        _ffn_kernel,
        grid=(n // ts, D_FF // tf),
        in_specs=[pl.BlockSpec((ts, D_MODEL), lambda i, f: (i, 0)),
                  pl.BlockSpec((D_MODEL, tf), lambda i, f: (0, f)),
                  pl.BlockSpec((D_MODEL, tf), lambda i, f: (0, f)),
                  pl.BlockSpec((tf, D_MODEL), lambda i, f: (f, 0)),
                  pl.BlockSpec((1, D_MODEL), lambda i, f: (0, 0)),
                  pl.BlockSpec((1, D_MODEL), lambda i, f: (0, 0))],
        out_specs=pl.BlockSpec((ts, D_MODEL), lambda i, f: (i, 0)),
        out_shape=jax.ShapeDtypeStruct((n, D_MODEL), F32),
        scratch_shapes=[pltpu.VMEM((ts, D_MODEL), F32)],
        compiler_params=_cparams(2),
        name="ffn_ln",
    )(x, w_gate, w_up, w_down, g, b)


def _pad_keys(a, axis=1):
    n = a.shape[axis]
    pad = (-n) % KEY_ALIGN
    if pad == 0:
        return a
    widths = [(0, 0)] * a.ndim
    widths[axis] = (0, pad)
    return jnp.pad(a, widths)


def _prepare_weights(w_in, mla_w_uk, mla_w_uv, w_branch, w_out, w_gate_up, w_down):
    w_rest = jnp.concatenate([w_in, jnp.zeros((D_MODEL, 1), w_in.dtype)], axis=1)
    w_rest = jnp.take(w_rest, _projection_columns(), axis=1).astype(BF16)
    w_uk = jnp.pad(mla_w_uk, ((0, 0), (0, 0), (0, LANES - MLA_D_NOPE))).reshape(MLA_D_C, -1)
    w_uv = jnp.pad(mla_w_uv, ((0, 0), (0, 0), (0, LANES - HEAD_DIM))).reshape(MLA_D_C, -1)
    w_ukv = jnp.concatenate([w_uk, w_uv], axis=1).astype(BF16)
    return dict(w_rest=w_rest, w_gates=w_in[:, REST_WIDTH:].astype(BF16), w_ukv=w_ukv,
                w_branch=w_branch.astype(BF16),
                w_out=w_out.astype(BF16), w_gate=w_gate_up[:, :D_FF].astype(BF16),
                w_up=w_gate_up[:, D_FF:].astype(BF16), w_down=w_down.astype(BF16))


def _projection_columns():
    zero_col = REST_WIDTH + GATE_WIDTH
    src = np.full(PROJ_WIDTH, zero_col, np.int32)
    o_aq, o_ckv, o_kr, o_sb, o_bd, o_ds, o_ixq, o_ixk, o_ixw = np.concatenate(
        [[0], np.cumsum(IN_SIZES[:-1])])[:9]
    hd = MLA_D_NOPE + MLA_D_ROPE

    def put(lane0, src0, n):
        src[lane0:lane0 + n] = src0 + np.arange(n)

    for h in range(N_HEADS):
        put((SEG_AQ + h) * LANES, o_aq + h * hd, hd)
    put(SEG_CKV * LANES, o_ckv, MLA_D_C)
    put(SEG_KR * LANES + MLA_ROPE_LANE, o_kr, MLA_D_ROPE)
    put(SEG_SB * LANES, o_sb, 3 * BRANCH_WIDTH)
    put(SEG_BD * LANES, o_bd, 3 * BRANCH_WIDTH)
    put(SEG_DS * LANES, o_ds, 3 * BRANCH_WIDTH)
    put(SEG_IXQ * LANES, o_ixq, IDX_HEADS * IDX_DIM)
    put(SEG_IXK * LANES, o_ixk, IDX_DIM)
    put(SEG_IXK * LANES + IXW_LANE, o_ixw, IDX_HEADS)
    return src


def _trunk_layer(x, q_off, past, wts, mla_kv_norm, band_rel_bias, ln1_g, ln1_b, ln2_g, ln2_b):
    bsz, s_len, _ = x.shape
    n = bsz * s_len
    pos = jnp.tile(q_off + jnp.arange(s_len, dtype=jnp.int32), bsz)
    x2 = x.reshape(n, D_MODEL)
    (q_a, lat_new, k_a, v_a, q_b, kv_b, sb_kv_new, q_c, kv_c, bd_kv_new,
     q_d, k_d, vt_d, ds_kv_new, q_ix, kidx_new, kx_d, w_ix) = _project(
        x2, pos, wts['w_rest'], wts['w_ukv'], mla_kv_norm[None])
    seq = lambda a: a.reshape(bsz, s_len, a.shape[-1])
    seq_t = lambda a: jnp.transpose(a.reshape(a.shape[0], bsz, s_len), (1, 0, 2))
    q_a, k_a, v_a, q_b, kv_b, q_c, kv_c, q_d, k_d, q_ix, kx_d = map(
        seq, (q_a, k_a, v_a, q_b, kv_b, q_c, kv_c, q_d, k_d, q_ix, kx_d))
    vt_d, w_ix = seq_t(vt_d), seq_t(w_ix)
    kv_state = lambda a: a.reshape(bsz, s_len, 2, N_HEADS, HEAD_DIM)
    lat_new, kidx_new = seq(lat_new), seq(kidx_new)
    sb_kv_new, bd_kv_new, ds_kv_new = kv_state(sb_kv_new), kv_state(bd_kv_new), kv_state(ds_kv_new)

    if past is None:
        kv_c = jnp.pad(kv_c, ((0, 0), (BAND_WINDOW, 0), (0, 0)))
        band_rows = bd_kv_new[:, s_len - min(BAND_WINDOW, s_len):]
        s_k = s_len
    else:
        past_lat, past_sb, past_band, past_ds, past_kidx = past
        p_len = past_lat.shape[1]
        s_k = p_len + s_len
        rows_bf16 = lambda a: a.reshape(bsz, a.shape[1], -1).astype(BF16)
        kv_past = _matmul(past_lat[..., :MLA_D_C].reshape(bsz * p_len, MLA_D_C), wts['w_ukv'])
        kv_past = kv_past.reshape(bsz, p_len, -1)
        kr_past = jnp.pad(past_lat[..., MLA_D_C:],
                          ((0, 0), (0, 0), (MLA_ROPE_LANE, LANES - MLA_ROPE_LANE - MLA_D_ROPE)))
        k_past = kv_past[..., :N_HEADS * LANES].reshape(bsz, p_len, N_HEADS, LANES) + kr_past[:, :, None]
        k_a = jnp.concatenate([rows_bf16(k_past), k_a], axis=1)
        v_past = kv_past[..., N_HEADS * LANES:] + _ones_upper_half(p_len)
        v_a = jnp.concatenate([v_past.astype(BF16), v_a], axis=1)
        kv_b = jnp.concatenate([rows_bf16(past_sb), kv_b], axis=1)
        kv_c = jnp.concatenate([rows_bf16(past_band), kv_c], axis=1)
        past_ds = past_ds.reshape(bsz, p_len, 2 * BRANCH_WIDTH)
        k_d = jnp.concatenate([past_ds[..., :BRANCH_WIDTH].astype(BF16), k_d], axis=1)
        vt_d = jnp.concatenate(
            [jnp.transpose(past_ds[..., BRANCH_WIDTH:], (0, 2, 1)).astype(BF16), vt_d], axis=2)
        kx_d = jnp.concatenate([past_kidx.astype(BF16), kx_d], axis=1)
        band_rows = bd_kv_new

    o_a = _mla_attention(q_a, _pad_keys(k_a), _pad_keys(v_a), q_off)
    o_b = _sb_attention(q_b, _pad_keys(kv_b), q_off)
    o_c = _band_attention(q_c, kv_c, band_rel_bias, q_off)
    n_sel = min(DSA_TOPK, s_k // 4)
    o_d = _dsa_attention(q_ix, w_ix, _pad_keys(kx_d), q_d, _pad_keys(k_d), _pad_keys(vt_d, axis=2),
                         q_off, n_sel)
    o_d = jnp.transpose(o_d, (0, 2, 1))

    flat = lambda o: o.reshape(n, BRANCH_WIDTH)
    x1 = _merge_out_ln(x2, wts['w_gates'], flat(o_a), flat(o_b), flat(o_c), flat(o_d),
                       wts['w_branch'], wts['w_out'], ln1_g[None], ln1_b[None])
    x_out = _ffn_ln(x1, wts['w_gate'], wts['w_up'], wts['w_down'], ln2_g[None], ln2_b[None])
    return x_out.reshape(bsz, s_len, D_MODEL), (lat_new, sb_kv_new, band_rows, ds_kv_new, kidx_new)


def kernel(x_prompt, x_sample, cache_mla_latent, cache_sb_kv, cache_band_kv, cache_dsa_kv, cache_dsa_kidx, w_in, mla_kv_norm, mla_w_uk, mla_w_uv, band_rel_bias, w_branch, w_out, ln1_g, ln1_b, w_gate_up, w_down, ln2_g, ln2_b):
    past_len = cache_mla_latent.shape[2]
    xp, xs = x_prompt, x_sample
    st_p, st_s = [], []
    for l in range(w_in.shape[0]):
        wts = _prepare_weights(w_in[l], mla_w_uk[l], mla_w_uv[l], w_branch[l], w_out[l],
                               w_gate_up[l], w_down[l])
        params = (wts, mla_kv_norm[l], band_rel_bias[l], ln1_g[l], ln1_b[l], ln2_g[l], ln2_b[l])
        xp, new_p = _trunk_layer(xp, 0, None, *params)
        past = (cache_mla_latent[l], cache_sb_kv[l], cache_band_kv[l], cache_dsa_kv[l],
                cache_dsa_kidx[l])
        xs, new_s = _trunk_layer(xs, past_len, past, *params)
        st_p.append(new_p)
        st_s.append(new_s)
    stack = lambda st, i: jnp.stack([s[i] for s in st])
    return (xp, xs) + tuple(stack(st_p, i) for i in range(5)) + tuple(stack(st_s, i) for i in range(5))
```

```python
import functools
import math

import numpy as np
import jax
import jax.numpy as jnp
from jax import lax
from jax.experimental import pallas as pl
from jax.experimental.pallas import tpu as pltpu

D_MODEL = 1024
CHUNK = 64
CHUNK_SHIFT = 6
N_BRANCH = 4
N_HEADS = 4
HEAD_DIM = 64
BRANCH_WIDTH = N_HEADS * HEAD_DIM
MLA_D_C = 128
MLA_D_NOPE = 64
MLA_D_ROPE = 32
MLA_THETA = 10000.0
ROPE_THETA = 500000.0
BAND_LEFT_CHUNKS = 8
BAND_WINDOW = BAND_LEFT_CHUNKS * CHUNK
REL_CLIP = 128
IDX_HEADS = 8
IDX_DIM = 64
DSA_TOPK = 256
D_FF = ((8 * D_MODEL // 3 + 255) // 256) * 256
DEPTH = 2
ALPHA = (2 * DEPTH) ** 0.25
NEG_INF = -1e30
LOG2_E = math.log2(math.e)
SB_RUN_FLOOR = -150.0
LN_EPS = 1e-5
IN_SIZES = (N_HEADS * (MLA_D_NOPE + MLA_D_ROPE), MLA_D_C, MLA_D_ROPE,
            3 * BRANCH_WIDTH, 3 * BRANCH_WIDTH, 3 * BRANCH_WIDTH,
            IDX_HEADS * IDX_DIM, IDX_DIM, IDX_HEADS, N_BRANCH * D_MODEL)
REST_WIDTH = sum(IN_SIZES[:-1])
GATE_WIDTH = IN_SIZES[-1]

LANES = 128
KEY_ALIGN = 512
INT_MIN = -2 ** 31
VMEM_LIMIT = 56 * 1024 * 1024

F32 = jnp.float32
BF16 = jnp.bfloat16


def _cparams(n_axes):
    return pltpu.CompilerParams(dimension_semantics=("arbitrary",) * n_axes,
                                vmem_limit_bytes=VMEM_LIMIT)


def _pick(n, candidates):
    for c in candidates:
        if n % c == 0:
            return c
    return n


def _ones_upper_half(rows):
    lane = lax.broadcasted_iota(jnp.int32, (rows, N_HEADS * LANES), 1)
    return jnp.where((lane & (LANES - 1)) >= HEAD_DIM, 1.0, 0.0)


def _dot_nt(a, b):
    return lax.dot_general(a, b, (((1,), (1,)), ((), ())), preferred_element_type=F32)


def _mm_kernel(a_ref, b_ref, o_ref):
    o_ref[...] = jnp.dot(a_ref[...].astype(BF16), b_ref[...], preferred_element_type=F32)


def _matmul(a, b):
    m, k = a.shape
    n = b.shape[1]
    tm = _pick(m, (1024, 512, 256, 128, 64, 32, 16, 8))
    tn = _pick(n, (512, 384, 256, 128))
    return pl.pallas_call(
        _mm_kernel,
        grid=(m // tm, n // tn),
        in_specs=[pl.BlockSpec((tm, k), lambda i, j: (i, 0)),
                  pl.BlockSpec((k, tn), lambda i, j: (0, j))],
        out_specs=pl.BlockSpec((tm, tn), lambda i, j: (i, j)),
        out_shape=jax.ShapeDtypeStruct((m, n), F32),
        compiler_params=_cparams(2),
        name="matmul",
    )(a, b)


SEG_AQ = 0
SEG_CKV = 4
SEG_KR = 5
SEG_SB = 6
SEG_BD = 12
SEG_DS = 18
SEG_IXQ = 24
SEG_IXK = 28
N_SEG = 30
IXW_LANE = 96
PROJ_WIDTH = N_SEG * LANES
MLA_ROPE_LANE = 64
LAT_WIDTH = MLA_D_C + MLA_D_ROPE


def _proj_kernel(x_ref, w_ref, wukv_ref, g_ref, ca_ref, sma_ref, spa_ref, cp_ref, smp_ref, spp_ref,
                 qa_ref, lat_ref, ka_ref, va_ref,
                 qb_ref, kvb_ref, sbkv_ref, qc_ref, kvc_ref, bdkv_ref,
                 qd_ref, kd_ref, vtd_ref, dskv_ref, qix_ref, kidx_ref, kx_ref, wix_ref):
    proj = jnp.dot(x_ref[...].astype(BF16), w_ref[...], preferred_element_type=F32)
    seg = lambda s, n=1: proj[:, s * LANES:(s + n) * LANES]

    def rope(t, c_ref, sm_ref, sp_ref, half):
        return (t * c_ref[...] + pltpu.roll(t, LANES - half, 1) * sm_ref[...]
                + pltpu.roll(t, half, 1) * sp_ref[...])

    rope_a = functools.partial(rope, c_ref=ca_ref, sm_ref=sma_ref, sp_ref=spa_ref, half=MLA_D_ROPE // 2)
    rope_p = functools.partial(rope, c_ref=cp_ref, sm_ref=smp_ref, sp_ref=spp_ref, half=HEAD_DIM // 8)
    head_scale = HEAD_DIM ** -0.5 * LOG2_E

    mla_scale = (MLA_D_NOPE + MLA_D_ROPE) ** -0.5 * LOG2_E
    for h in range(N_HEADS):
        qa_ref[:, h * LANES:(h + 1) * LANES] = (rope_a(seg(SEG_AQ + h)) * mla_scale).astype(BF16)
    ckv = seg(SEG_CKV)
    ckv = ckv * lax.rsqrt(jnp.mean(ckv * ckv, axis=-1, keepdims=True) + LN_EPS) * g_ref[...]
    kr = rope_a(seg(SEG_KR))
    lat_ref[:, :MLA_D_C] = ckv
    lat_ref[:, MLA_D_C:] = kr[:, MLA_ROPE_LANE:MLA_ROPE_LANE + MLA_D_ROPE]
    kv_a = jnp.dot(ckv.astype(BF16), wukv_ref[...], preferred_element_type=F32)
    for h in range(N_HEADS):
        ka_ref[:, h * LANES:(h + 1) * LANES] = (kv_a[:, h * LANES:(h + 1) * LANES] + kr).astype(BF16)
    va_ref[...] = (kv_a[:, N_HEADS * LANES:] + _ones_upper_half(kv_a.shape[0])).astype(BF16)

    for s0, q_ref, kv_ref, new_ref in ((SEG_SB, qb_ref, kvb_ref, sbkv_ref),
                                       (SEG_BD, qc_ref, kvc_ref, bdkv_ref)):
        q_ref[...] = (seg(s0, 2) * head_scale).astype(BF16)
        kv = seg(s0 + 2, 4)
        new_ref[...] = kv
        kv_ref[...] = kv.astype(BF16)

    for p in range(2):
        qd_ref[:, p * LANES:(p + 1) * LANES] = (rope_p(seg(SEG_DS + p)) * head_scale).astype(BF16)
        k_rot = rope_p(seg(SEG_DS + 2 + p))
        dskv_ref[:, p * LANES:(p + 1) * LANES] = k_rot
        kd_ref[:, p * LANES:(p + 1) * LANES] = k_rot.astype(BF16)
    v_d = seg(SEG_DS + 4, 2)
    dskv_ref[:, BRANCH_WIDTH:] = v_d
    vtd_ref[...] = v_d.T.astype(BF16)
    for p in range(IDX_HEADS // 2):
        qix_ref[:, p * LANES:(p + 1) * LANES] = (rope_p(seg(SEG_IXQ + p)) * IDX_DIM ** -0.5).astype(BF16)
    ixk = rope_p(seg(SEG_IXK))
    kidx_ref[...] = ixk[:, :IDX_DIM]
    kx_ref[...] = ixk[:, :IDX_DIM].astype(BF16)
    wix_ref[...] = ixk.T[IXW_LANE:IXW_LANE + IDX_HEADS, :]


def _rope_tables(pos, theta, width, starts):
    half = width // 2
    inv = jnp.exp(jnp.arange(half, dtype=F32) * (-2.0 * math.log(theta) / width))
    freq_idx = np.zeros(LANES, np.int32)
    x1 = np.zeros(LANES, np.float32)
    x2 = np.zeros(LANES, np.float32)
    for s in starts:
        freq_idx[s:s + width] = np.tile(np.arange(half), 2)
        x1[s:s + half] = 1.0
        x2[s + half:s + width] = 1.0
    inv_lane = inv[freq_idx] * jnp.asarray(x1 + x2)
    ang = pos.astype(F32)[:, None] * inv_lane[None, :]
    sin = jnp.sin(ang)
    return jnp.cos(ang), -sin * x1[None, :], sin * x2[None, :]


def _project(x, pos, w_rest, w_ukv, kv_norm):
    n = x.shape[0]
    ts = _pick(n, (512, 256, 128, 64))
    tables = (_rope_tables(pos, MLA_THETA, MLA_D_ROPE, (MLA_ROPE_LANE,))
              + _rope_tables(pos, ROPE_THETA, HEAD_DIM // 4, (0, HEAD_DIM)))
    row = lambda i: (i, 0)
    fixed = lambda i: (0, 0)
    rows = lambda w: pl.BlockSpec((ts, w), row)
    out = lambda w, dt: jax.ShapeDtypeStruct((n, w), dt)
    specs = [
        (N_HEADS * LANES, BF16), (LAT_WIDTH, F32), (N_HEADS * LANES, BF16), (N_HEADS * LANES, BF16),
        (BRANCH_WIDTH, BF16), (2 * BRANCH_WIDTH, BF16), (2 * BRANCH_WIDTH, F32),
        (BRANCH_WIDTH, BF16), (2 * BRANCH_WIDTH, BF16), (2 * BRANCH_WIDTH, F32),
        (BRANCH_WIDTH, BF16), (BRANCH_WIDTH, BF16), None, (2 * BRANCH_WIDTH, F32),
        (IDX_HEADS * IDX_DIM, BF16), (IDX_DIM, F32), (IDX_DIM, BF16), None]
    out_specs, out_shape = [], []
    for k, spec in enumerate(specs):
        if spec is None:
            rows_t, dt = ((BRANCH_WIDTH, BF16), (IDX_HEADS, F32))[k > 12]
            out_specs.append(pl.BlockSpec((rows_t, ts), lambda i: (0, i)))
            out_shape.append(jax.ShapeDtypeStruct((rows_t, n), dt))
        else:
            out_specs.append(rows(spec[0]))
            out_shape.append(out(*spec))
    return pl.pallas_call(
        _proj_kernel,
        grid=(n // ts,),
        in_specs=[rows(D_MODEL),
                  pl.BlockSpec((D_MODEL, PROJ_WIDTH), fixed, pipeline_mode=pl.Buffered(1)),
                  pl.BlockSpec((MLA_D_C, 2 * N_HEADS * LANES), fixed),
                  pl.BlockSpec((1, MLA_D_C), fixed)] + [rows(LANES)] * 6,
        out_specs=out_specs,
        out_shape=out_shape,
        compiler_params=_cparams(1),
        name="project_prepare",
    )(x, w_rest, w_ukv, kv_norm, *tables)


def _mla_kernel(q_ref, k_ref, v_ref, o_ref, *, qb, kb, q_off, skp):
    q0 = pl.program_id(1) * qb
    hi = jnp.minimum((((q_off + q0 + qb - 1) >> CHUNK_SHIFT) + 1) * CHUNK, skp)
    nkb = (hi + kb - 1) // kb
    qpos = q_off + q0 + lax.broadcasted_iota(jnp.int32, (qb, 1), 0)
    cend = ((qpos >> CHUNK_SHIFT) + 1) * CHUNK
    col = lax.broadcasted_iota(jnp.int32, (qb, kb), 1)
    lane = lax.broadcasted_iota(jnp.int32, (qb, LANES), 1)
    n_full = jnp.minimum(((((q_off + q0) >> CHUNK_SHIFT) + 1) * CHUNK) // kb, nkb)

    def body(i, carry, masked):
        k0 = pl.multiple_of(i * kb, kb)
        s = [_dot_nt(q_ref[0, :, h * LANES:(h + 1) * LANES],
                     k_ref[0, pl.ds(k0, kb), h * LANES:(h + 1) * LANES]) for h in range(N_HEADS)]
        if masked:
            vis = col + k0 < cend
            s = [jnp.where(vis, sh, NEG_INF) for sh in s]
        p, new = [], []
        for h in range(N_HEADS):
            m, acc = carry[2 * h:2 * h + 2]
            m_new = jnp.maximum(m, jnp.max(s[h], axis=1, keepdims=True))
            p.append(jnp.exp2((s[h] - m_new).astype(BF16)))
            new += [m_new, jnp.exp2(m - m_new) * acc]
        for h in range(N_HEADS):
            vblk = v_ref[0, pl.ds(k0, kb), h * LANES:(h + 1) * LANES]
            new[2 * h + 1] = new[2 * h + 1] + jnp.dot(p[h], vblk, preferred_element_type=F32)
        return tuple(new)

    init = (jnp.full((qb, 1), NEG_INF, F32), jnp.zeros((qb, LANES), F32)) * N_HEADS
    carry = lax.fori_loop(0, n_full, functools.partial(body, masked=False), init)
    carry = lax.fori_loop(n_full, nkb, functools.partial(body, masked=True), carry)
    outs = [carry[2 * h + 1] / pltpu.roll(carry[2 * h + 1], HEAD_DIM, 1) for h in range(N_HEADS)]
    for pair in range(N_HEADS // 2):
        o_ref[0, :, pair * LANES:(pair + 1) * LANES] = jnp.where(
            lane < HEAD_DIM, outs[2 * pair], pltpu.roll(outs[2 * pair + 1], HEAD_DIM, 1))


def _mla_attention(q, k, v, q_off):
    b, sq, _ = q.shape
    skp = k.shape[1]
    qb = _pick(sq, (256, 128, 64))
    kb = _pick(skp, (1024, KEY_ALIGN))
    kern = functools.partial(_mla_kernel, qb=qb, kb=kb, q_off=q_off, skp=skp)
    return pl.pallas_call(
        kern,
        grid=(b, sq // qb),
        in_specs=[pl.BlockSpec((1, qb, N_HEADS * LANES), lambda bi, qi: (bi, qi, 0)),
                  pl.BlockSpec((1, skp, N_HEADS * LANES), lambda bi, qi: (bi, 0, 0),
                               pipeline_mode=pl.Buffered(1)),
                  pl.BlockSpec((1, skp, N_HEADS * LANES), lambda bi, qi: (bi, 0, 0),
                               pipeline_mode=pl.Buffered(1))],
        out_specs=pl.BlockSpec((1, qb, BRANCH_WIDTH), lambda bi, qi: (bi, qi, 0)),
        out_shape=jax.ShapeDtypeStruct((b, sq, BRANCH_WIDTH), F32),
        compiler_params=_cparams(2),
        name="mla_attention",
    )(q, k, v)


def _head_mask(x_pair, h):
    lane = lax.broadcasted_iota(jnp.int32, x_pair.shape, 1)
    keep = (lane < HEAD_DIM) if h % 2 == 0 else (lane >= HEAD_DIM)
    return jnp.where(keep, x_pair, jnp.zeros_like(x_pair))


def _sb_kernel(q_ref, kv_ref, t_ref, o_ref, *, qb, kb, sub, q_off):
    q0 = pl.program_id(1) * qb
    hi = q_off + q0 + qb - 1
    nkb = (hi + kb - 1) // kb
    qpos = q_off + q0 + lax.broadcasted_iota(jnp.int32, (qb, 1), 0)
    col = lax.broadcasted_iota(jnp.int32, (qb, sub), 1)
    lane = lax.broadcasted_iota(jnp.int32, (qb, LANES), 1)
    tri = t_ref[...]
    n_full = jnp.minimum((q_off + q0) // kb, nkb)
    qh = [_head_mask(q_ref[0, :, (h // 2) * LANES:(h // 2 + 1) * LANES], h) for h in range(N_HEADS)]

    def body(i, carry, masked):
        k0 = pl.multiple_of(i * kb, kb)
        carry = list(carry)
        units = [(j, h) for j in reversed(range(kb // sub)) for h in range(N_HEADS)]
        strict = {j: col + (k0 + j * sub) < qpos for j in range(kb // sub)} if masked else None
        z = {}
        for j, h in units:
            pair = h // 2
            kblk = kv_ref[0, pl.ds(k0 + j * sub, sub), pair * LANES:(pair + 1) * LANES]
            z[j, h] = _dot_nt(qh[h], kblk)
        log_1m, suffix = {}, {}
        for u in units:
            nz = -z[u]
            t = jnp.minimum(nz, 0.0) - jnp.log2(1.0 + jnp.exp2(jnp.minimum(z[u], nz)))
            if masked:
                t = jnp.where(strict[u[0]], t, 0.0)
            log_1m[u] = t
        for u in units:
            hi_part = log_1m[u].astype(BF16)
            lo_part = (log_1m[u] - hi_part.astype(F32)).astype(BF16)
            suffix[u] = (jnp.dot(hi_part, tri, preferred_element_type=F32)
                         + jnp.dot(lo_part, tri, preferred_element_type=F32))
        for j, h in units:
            u = (j, h)
            run, acc = carry[2 * h:2 * h + 2]
            expo = z[u] + log_1m[u] + suffix[u] + run
            if masked:
                expo = jnp.where(strict[j], expo, NEG_INF)
            a = jnp.exp2(expo).astype(BF16)
            pair = h // 2
            vblk = kv_ref[0, pl.ds(k0 + j * sub, sub),
                          BRANCH_WIDTH + pair * LANES:BRANCH_WIDTH + (pair + 1) * LANES]
            carry[2 * h + 1] = acc + jnp.dot(a, vblk, preferred_element_type=F32)
            carry[2 * h] = run + jnp.sum(log_1m[u], axis=1, keepdims=True)
        return tuple(carry)

    def live(state):
        i, carry = state[0], state[1:]
        top = carry[0]
        for h in range(1, N_HEADS):
            top = jnp.maximum(top, carry[2 * h])
        return (i >= 0) & (jnp.max(top) >= SB_RUN_FLOOR)

    def step(masked):
        return lambda state: (state[0] - 1,) + body(state[0], state[1:], masked)

    init = (jnp.zeros((qb, 1), F32), jnp.zeros((qb, LANES), F32)) * N_HEADS
    carry = lax.fori_loop(0, nkb - n_full, lambda i, c: body(nkb - 1 - i, c, True), init)
    state = lax.while_loop(live, step(False), (n_full - 1,) + tuple(carry))
    carry = state[1:]
    outs = [carry[2 * h + 1] for h in range(N_HEADS)]
    for pair in range(N_HEADS // 2):
        o_ref[0, :, pair * LANES:(pair + 1) * LANES] = jnp.where(
            lane < HEAD_DIM, outs[2 * pair], outs[2 * pair + 1])


def _sb_attention(q, kv, q_off):
    b, sq, _ = q.shape
    skp = kv.shape[1]
    qb = _pick(sq, (256, 128, 64))
    sub = 256
    kb = KEY_ALIGN
    tri =jnp.asarray(np.tril(np.ones((sub, sub), np.float32), -1), BF16)
    kern = functools.partial(_sb_kernel, qb=qb, kb=kb, sub=sub, q_off=q_off)
    return pl.pallas_call(
        kern,
        grid=(b, sq // qb),
        in_specs=[pl.BlockSpec((1, qb, BRANCH_WIDTH), lambda bi, qi: (bi, qi, 0)),
                  pl.BlockSpec((1, skp, 2 * BRANCH_WIDTH), lambda bi, qi: (bi, 0, 0),
                               pipeline_mode=pl.Buffered(1)),
                  pl.BlockSpec((sub, sub), lambda bi, qi: (0, 0))],
        out_specs=pl.BlockSpec((1, qb, BRANCH_WIDTH), lambda bi, qi: (bi, qi, 0)),
        out_shape=jax.ShapeDtypeStruct((b, sq, BRANCH_WIDTH), F32),
        compiler_params=_cparams(2),
        name="stick_breaking_attention",
    )(q, kv, tri)


def _band_kernel(bias_ref, q_ref, kv_ref, o_ref, tile_ref, *, qb, win, q_off):
    first = (pl.program_id(0) == 0) & (pl.program_id(1) == 0)
    row = lax.broadcasted_iota(jnp.int32, (qb, win), 0)
    col = lax.broadcasted_iota(jnp.int32, (qb, win), 1)

    @pl.when(first)
    def _():
        rel = jnp.clip(row + BAND_WINDOW - col, -REL_CLIP, REL_CLIP) + REL_CLIP
        qch = row >> CHUNK_SHIFT
        kch = col >> CHUNK_SHIFT
        in_band = (kch >= qch) & (kch <= qch + BAND_LEFT_CHUNKS)
        for h in range(N_HEADS):
            tile_ref[h] = jnp.full((qb, win), bias_ref[h, 0], F32)

        def fill(r, c):
            hit = rel == r
            for h in range(N_HEADS):
                tile_ref[h] = jnp.where(hit, bias_ref[h, r], tile_ref[h])
            return c

        lax.fori_loop(1, 2 * REL_CLIP + 1, fill, 0)
        for h in range(N_HEADS):
            tile_ref[h] = jnp.where(in_band, tile_ref[h] * LOG2_E, NEG_INF)

    q0 = pl.multiple_of(pl.program_id(1) * qb, qb)
    kpos = col + (q_off + q0 - BAND_WINDOW)
    lane = lax.broadcasted_iota(jnp.int32, (qb, LANES), 1)
    outs = []
    for h in range(N_HEADS):
        pair = h // 2
        qh = _head_mask(q_ref[0, :, pair * LANES:(pair + 1) * LANES], h)
        kwin = kv_ref[0, pl.ds(q0, win), pair * LANES:(pair + 1) * LANES]
        s = _dot_nt(qh, kwin) + tile_ref[h]
        s = jnp.where(kpos >= 0, s, NEG_INF)
        m = jnp.max(s, axis=1, keepdims=True)
        p = jnp.exp2(s - m)
        l = jnp.sum(p, axis=1, keepdims=True)
        vwin = kv_ref[0, pl.ds(q0, win),
                      BRANCH_WIDTH + pair * LANES:BRANCH_WIDTH + (pair + 1) * LANES]
        outs.append(jnp.dot(p.astype(BF16), vwin, preferred_element_type=F32) / l)
    for pair in range(N_HEADS // 2):
        o_ref[0, :, pair * LANES:(pair + 1) * LANES] = jnp.where(
            lane < HEAD_DIM, outs[2 * pair], outs[2 * pair + 1])


def _band_attention(q, kv, rel_bias, q_off):
    b, sq, _ = q.shape
    sk = kv.shape[1]
    qb = _pick(sq, (256, 128, 64))
    win = qb + BAND_WINDOW
    kern = functools.partial(_band_kernel, qb=qb, win=win, q_off=q_off)
    return pl.pallas_call(
        kern,
        grid=(b, sq // qb),
        in_specs=[pl.BlockSpec(memory_space=pltpu.SMEM),
                  pl.BlockSpec((1, qb, BRANCH_WIDTH), lambda bi, qi: (bi, qi, 0)),
                  pl.BlockSpec((1, sk, 2 * BRANCH_WIDTH), lambda bi, qi: (bi, 0, 0),
                               pipeline_mode=pl.Buffered(1))],
        out_specs=pl.BlockSpec((1, qb, BRANCH_WIDTH), lambda bi, qi: (bi, qi, 0)),
        out_shape=jax.ShapeDtypeStruct((b, sq, BRANCH_WIDTH), F32),
        scratch_shapes=[pltpu.VMEM((N_HEADS, qb, win), F32)],
        compiler_params=_cparams(2),
        name="band_attention",
    )(rel_bias, q, kv)


def _dsa_kernel(qi_ref, w_ref, kx_ref, q_ref, k_ref, vt_ref, o_ref, key_ref, key16_ref, cut_ref,
                *, qb, kb, q_off, skp, n_sel):
    q0 = pl.program_id(1) * qb
    hi = jnp.minimum((((q_off + q0 + qb - 1) >> CHUNK_SHIFT) + 1) * CHUNK, skp)
    nkb = (hi + kb - 1) // kb
    qpos = q_off + q0 + lax.broadcasted_iota(jnp.int32, (1, qb), 1)
    cend = ((qpos >> CHUNK_SHIFT) + 1) * CHUNK
    row = lax.broadcasted_iota(jnp.int32, (kb, qb), 0)
    float_key = lambda bits: jnp.where(bits < 0, INT_MIN - bits, bits)
    neg_key = float_key(lax.bitcast_convert_type(jnp.full((1, 1), NEG_INF, F32), jnp.int32))
    n_full = jnp.minimum(((((q_off + q0) >> CHUNK_SHIFT) + 1) * CHUNK) // kb, nkb)

    def score_block(i, c, masked):
        k0 = pl.multiple_of(i * kb, kb)
        kx = kx_ref[0, pl.ds(k0, kb), :]
        r = [_dot_nt(kx, qi_ref[0, :, j * IDX_DIM:(j + 1) * IDX_DIM]) for j in range(IDX_HEADS)]
        score = jnp.zeros((kb, qb), F32)
        for j in range(IDX_HEADS):
            wj = w_ref[0, j:j + 1, :] * (IDX_HEADS ** -0.5)
            score = score + wj * jnp.maximum(r[j], 0.0)
        if masked:
            score = jnp.where(row + k0 < cend, score, NEG_INF)
        keys = float_key(lax.bitcast_convert_type(score, jnp.int32))
        key_ref[pl.ds(k0, kb), :] = keys
        key16_ref[pl.ds(k0, kb), :] = (keys >> 16).astype(jnp.int16)
        return c

    lax.fori_loop(0, n_full, functools.partial(score_block, masked=False), 0)
    lax.fori_loop(n_full, nkb, functools.partial(score_block, masked=True), 0)

    slab = 64 if kb % 64 == 0 else kb

    def count_ge(cand):
        def blk(i, cnt):
            k0 = pl.multiple_of(i * kb, kb)
            for s in range(kb // slab):
                keys = key_ref[pl.ds(k0 + s * slab, slab), :]
                cnt = cnt + jnp.where(keys >= cand, 1.0, 0.0)
            return cnt
        part = lax.fori_loop(0, nkb, blk, jnp.zeros((slab, qb), F32))
        return jnp.sum(part, axis=0, keepdims=True)

    slab16 = 128 if kb % 128 == 0 else kb

    def count16_ge(cand16):
        def blk(i, cnt):
            k0 = pl.multiple_of(i * kb, kb)
            for s in range(kb // slab16):
                k16 = key16_ref[pl.ds(k0 + s * slab16, slab16), :]
                cnt = cnt + jnp.where(k16 >= cand16, jnp.int16(1), jnp.int16(0))
            return cnt
        part = lax.fori_loop(0, nkb, blk, jnp.zeros((slab16, qb), jnp.int16))
        return jnp.sum(part.astype(F32), axis=0, keepdims=True)

    def undecided(t_end):
        def cond(state):
            t, _, n_ge = state
            return (t < t_end) & (jnp.max(jnp.abs(n_ge - n_sel)) > 0.0)
        return cond

    def bisect(to_16bit):
        def step(state):
            t, thr, n_ge = state
            cand = thr + jnp.left_shift(jnp.int32(1), 31 - t)
            cnt = count16_ge(to_16bit(cand))
            ok = cnt >= n_sel
            return t + 1, jnp.where(ok, cand, thr), jnp.where(ok, cnt, n_ge)
        return step

    stored = (nkb * kb).astype(F32)
    state = lax.while_loop(
        undecided(16), bisect(lambda cand: (cand >> 16).astype(jnp.int16)),
        (jnp.int32(0), jnp.full((1, qb), INT_MIN, jnp.int32), jnp.full((1, qb), stored, F32)))

    @pl.when(undecided(32)(state))
    def _():
        thr_hi = state[1] >> 16

        def low_halves(i, c):
            k0 = pl.multiple_of(i * kb, kb)
            keys = key_ref[pl.ds(k0, kb), :]
            hi16 = keys >> 16
            low = jnp.where(hi16 == thr_hi, (keys & 0xFFFF) - 0x8000,
                            jnp.where(hi16 > thr_hi, 0x7FFF, -0x8000))
            key16_ref[pl.ds(k0, kb), :] = low.astype(jnp.int16)
            return c

        lax.fori_loop(0, nkb, low_halves, 0)

    _, thr, n_ge = lax.while_loop(
        undecided(32), bisect(lambda cand: ((cand & 0xFFFF) - 0x8000).astype(jnp.int16)), state)
    excess = jnp.where((n_ge > n_sel) & (thr > neg_key), 1.0, 0.0)
    cut_ref[...] = jnp.full((1, qb), skp, jnp.int32)

    @pl.when(jnp.max(excess) > 0.0)
    def _():
        need = n_sel - count_ge(thr + 1)

        def count_tied_below(limit):
            def blk(i, cnt):
                k0 = pl.multiple_of(i * kb, kb)
                tied = (key_ref[pl.ds(k0, kb), :] == thr) & (row + k0 < limit)
                return cnt + jnp.sum(jnp.where(tied, 1.0, 0.0), axis=0, keepdims=True)
            return lax.fori_loop(0, nkb, blk, jnp.zeros((1, qb), F32))

        n_bits = max(1, (skp - 1).bit_length())

        def bisect_cut(t, cut):
            cand = cut + jnp.left_shift(jnp.int32(1), n_bits - 1 - t)
            return jnp.where(count_tied_below(cand) < need, cand, cut)

        cut = lax.fori_loop(0, n_bits, bisect_cut, jnp.zeros((1, qb), jnp.int32))
        cut_ref[...] = jnp.where(excess > 0.0, cut + 1, skp)

    cut = cut_ref[...]

    def mask_block(i, c):
        k0 = pl.multiple_of(i * kb, kb)
        keys = key_ref[pl.ds(k0, kb), :]
        kpos = row + k0
        sel = (keys >= thr) & ((keys != thr) | (kpos < cut)) & (kpos < cend)
        key_ref[pl.ds(k0, kb), :] = lax.bitcast_convert_type(jnp.where(sel, 0.0, NEG_INF), jnp.int32)
        return c

    lax.fori_loop(0, nkb, mask_block, 0)

    qh = [_head_mask(q_ref[0, :, (h // 2) * LANES:(h // 2 + 1) * LANES], h) for h in range(N_HEADS)]

    def body(i, carry):
        k0 = pl.multiple_of(i * kb, kb)
        bias = lax.bitcast_convert_type(key_ref[pl.ds(k0, kb), :], F32)
        s = [_dot_nt(k_ref[0, pl.ds(k0, kb), (h // 2) * LANES:(h // 2 + 1) * LANES], qh[h])
             for h in range(N_HEADS)]
        p, new = [], []
        for h in range(N_HEADS):
            m, acc = carry[2 * h:2 * h + 2]
            sh = s[h] + bias
            m_new = jnp.maximum(m, jnp.max(sh, axis=0, keepdims=True))
            p.append(jnp.exp2((sh - m_new).astype(BF16)))
            new += [m_new, jnp.exp2(m - m_new) * acc]
        for h in range(N_HEADS):
            vblk = jnp.concatenate([vt_ref[0, h * HEAD_DIM:(h + 1) * HEAD_DIM, pl.ds(k0, kb)],
                                    jnp.ones((LANES - HEAD_DIM, kb), BF16)], axis=0)
            new[2 * h + 1] = new[2 * h + 1] + jnp.dot(vblk, p[h], preferred_element_type=F32)
        return tuple(new)

    init = (jnp.full((1, qb), NEG_INF, F32), jnp.zeros((LANES, qb), F32)) * N_HEADS
    carry = lax.fori_loop(0, nkb, body, init)
    for h in range(N_HEADS):
        acc = carry[2 * h + 1]
        o_ref[0, h * HEAD_DIM:(h + 1) * HEAD_DIM, :] = acc[:HEAD_DIM, :] / acc[HEAD_DIM:HEAD_DIM + 1, :]


def _dsa_attention(qi, w, kx, q, k, vt, q_off, n_sel):
    b, sq, _ = q.shape
    skp = k.shape[1]
    qb = _pick(sq, (256, 128, 64))
    kb = _pick(skp, (1024, KEY_ALIGN))
    kern = functools.partial(_dsa_kernel, qb=qb, kb=kb, q_off=q_off, skp=skp, n_sel=n_sel)
    return pl.pallas_call(
        kern,
        grid=(b, sq // qb),
        in_specs=[pl.BlockSpec((1, qb, IDX_HEADS * IDX_DIM), lambda bi, qi_: (bi, qi_, 0)),
                  pl.BlockSpec((1, IDX_HEADS, qb), lambda bi, qi_: (bi, 0, qi_)),
                  pl.BlockSpec((1, skp, IDX_DIM), lambda bi, qi_: (bi, 0, 0),
                               pipeline_mode=pl.Buffered(1)),
                  pl.BlockSpec((1, qb, BRANCH_WIDTH), lambda bi, qi_: (bi, qi_, 0)),
                  pl.BlockSpec((1, skp, BRANCH_WIDTH), lambda bi, qi_: (bi, 0, 0),
                               pipeline_mode=pl.Buffered(1)),
                  pl.BlockSpec((1, BRANCH_WIDTH, skp), lambda bi, qi_: (bi, 0, 0),
                               pipeline_mode=pl.Buffered(1))],
        out_specs=pl.BlockSpec((1, BRANCH_WIDTH, qb), lambda bi, qi_: (bi, 0, qi_)),
        out_shape=jax.ShapeDtypeStruct((b, BRANCH_WIDTH, sq), F32),
        scratch_shapes=[pltpu.VMEM((skp, qb), jnp.int32), pltpu.VMEM((skp, qb), jnp.int16),
                        pltpu.VMEM((1, qb), jnp.int32)],
        compiler_params=_cparams(2),
        name="dsa_attention",
    )(qi, w, kx, q, k, vt)


def _layer_norm(z, g, b):
    mu = jnp.mean(z, axis=-1, keepdims=True)
    zc = z - mu
    var = jnp.mean(zc * zc, axis=-1, keepdims=True)
    return zc * lax.rsqrt(var + LN_EPS) * g + b


def _merge_kernel(x_ref, wgate_ref, oa_ref, ob_ref, oc_ref, od_ref, wb_ref, wo_ref, g_ref, b_ref,
                  y_ref):
    xb = x_ref[...].astype(BF16)
    logits = [jnp.dot(xb, wgate_ref[:, n * D_MODEL:(n + 1) * D_MODEL], preferred_element_type=F32)
              for n in range(N_BRANCH)]
    branch = [jnp.dot(o_ref[...].astype(BF16), wb_ref[n], preferred_element_type=F32)
              for n, o_ref in enumerate((oa_ref, ob_ref, oc_ref, od_ref))]
    merged = jax.nn.sigmoid(logits[0]) * branch[0]
    for n in range(1, N_BRANCH):
        merged = merged + jax.nn.sigmoid(logits[n]) * branch[n]
    y = jnp.dot(merged.astype(BF16), wo_ref[...], preferred_element_type=F32)
    y_ref[...] = _layer_norm(ALPHA * x_ref[...] + y, g_ref[...], b_ref[...])


def _merge_out_ln(x, w_gate, o_a, o_b, o_c, o_d, w_branch, w_out, g, b):
    n = x.shape[0]
    ts = _pick(n, (512, 256, 128, 64))
    row = lambda i: (i, 0)
    obs = pl.BlockSpec((ts, BRANCH_WIDTH), row)
    return pl.pallas_call(
        _merge_kernel,
        grid=(n // ts,),
        in_specs=[pl.BlockSpec((ts, D_MODEL), row),
                  pl.BlockSpec((D_MODEL, GATE_WIDTH), lambda i: (0, 0), pipeline_mode=pl.Buffered(1)),
                  obs, obs, obs, obs,
                  pl.BlockSpec((N_BRANCH, BRANCH_WIDTH, D_MODEL), lambda i: (0, 0, 0)),
                  pl.BlockSpec((D_MODEL, D_MODEL), lambda i: (0, 0)),
                  pl.BlockSpec((1, D_MODEL), lambda i: (0, 0)),
                  pl.BlockSpec((1, D_MODEL), lambda i: (0, 0))],
        out_specs=pl.BlockSpec((ts, D_MODEL), row),
        out_shape=jax.ShapeDtypeStruct((n, D_MODEL), F32),
        compiler_params=_cparams(1),
        name="merge_out_ln",
    )(x, w_gate, o_a, o_b, o_c, o_d, w_branch, w_out, g, b)


def _ffn_kernel(x_ref, wg_ref, wu_ref, wd_ref, g_ref, b_ref, y_ref, acc_ref):
    f = pl.program_id(1)
    xb = x_ref[...].astype(BF16)
    gate = jnp.dot(xb, wg_ref[...], preferred_element_type=F32)
    up = jnp.dot(xb, wu_ref[...], preferred_element_type=F32)
    hidden = (gate * jax.nn.sigmoid(gate) * up).astype(BF16)
    part = jnp.dot(hidden, wd_ref[...], preferred_element_type=F32)

    @pl.when(f == 0)
    def _():
        acc_ref[...] = part

    @pl.when(f > 0)
    def _():
        acc_ref[...] += part

    @pl.when(f == pl.num_programs(1) - 1)
    def _():
        y_ref[...] = _layer_norm(ALPHA * x_ref[...] + acc_ref[...], g_ref[...], b_ref[...])


def _ffn_ln(x, w_gate, w_up, w_down, g, b):
    n = x.shape[0]
    ts = _pick(n, (1024, 512, 256, 128, 64))
    tf = D_FF // 2
    return pl.pallas_call(
        _ffn_kernel,
        grid=(n // ts, D_FF // tf),
        in_specs=[pl.BlockSpec((ts, D_MODEL), lambda i, f: (i, 0)),
                  pl.BlockSpec((D_MODEL, tf), lambda i, f: (0, f)),
                  pl.BlockSpec((D_MODEL, tf), lambda i, f: (0, f)),
                  pl.BlockSpec((tf, D_MODEL), lambda i, f: (f, 0)),
                  pl.BlockSpec((1, D_MODEL), lambda i, f: (0, 0)),
                  pl.BlockSpec((1, D_MODEL), lambda i, f: (0, 0))],
        out_specs=pl.BlockSpec((ts, D_MODEL), lambda i, f: (i, 0)),
        out_shape=jax.ShapeDtypeStruct((n, D_MODEL), F32),
        scratch_shapes=[pltpu.VMEM((ts, D_MODEL), F32)],
        compiler_params=_cparams(2),
        name="ffn_ln",
    )(x, w_gate, w_up, w_down, g, b)


def _pad_keys(a, axis=1):
    n = a.shape[axis]
    pad = (-n) % KEY_ALIGN
    if pad == 0:
        return a
    widths = [(0, 0)] * a.ndim
    widths[axis] = (0, pad)
    return jnp.pad(a, widths)


def _prepare_weights(w_in, mla_w_uk, mla_w_uv, w_branch, w_out, w_gate_up, w_down):
    w_rest = jnp.concatenate([w_in, jnp.zeros((D_MODEL, 1), w_in.dtype)], axis=1)
    w_rest = jnp.take(w_rest, _projection_columns(), axis=1).astype(BF16)
    w_uk = jnp.pad(mla_w_uk, ((0, 0), (0, 0), (0, LANES - MLA_D_NOPE))).reshape(MLA_D_C, -1)
    w_uv = jnp.pad(mla_w_uv, ((0, 0), (0, 0), (0, LANES - HEAD_DIM))).reshape(MLA_D_C, -1)
    w_ukv = jnp.concatenate([w_uk, w_uv], axis=1).astype(BF16)
    return dict(w_rest=w_rest, w_gates=w_in[:, REST_WIDTH:].astype(BF16), w_ukv=w_ukv,
                w_branch=w_branch.astype(BF16),
                w_out=w_out.astype(BF16), w_gate=w_gate_up[:, :D_FF].astype(BF16),
                w_up=w_gate_up[:, D_FF:].astype(BF16), w_down=w_down.astype(BF16))


def _projection_columns():
    zero_col = REST_WIDTH + GATE_WIDTH
    src = np.full(PROJ_WIDTH, zero_col, np.int32)
    o_aq, o_ckv, o_kr, o_sb, o_bd, o_ds, o_ixq, o_ixk, o_ixw = np.concatenate(
        [[0], np.cumsum(IN_SIZES[:-1])])[:9]
    hd = MLA_D_NOPE + MLA_D_ROPE

    def put(lane0, src0, n):
        src[lane0:lane0 + n] = src0 + np.arange(n)

    for h in range(N_HEADS):
        put((SEG_AQ + h) * LANES, o_aq + h * hd, hd)
    put(SEG_CKV * LANES, o_ckv, MLA_D_C)
    put(SEG_KR * LANES + MLA_ROPE_LANE, o_kr, MLA_D_ROPE)
    put(SEG_SB * LANES, o_sb, 3 * BRANCH_WIDTH)
    put(SEG_BD * LANES, o_bd, 3 * BRANCH_WIDTH)
    put(SEG_DS * LANES, o_ds, 3 * BRANCH_WIDTH)
    put(SEG_IXQ * LANES, o_ixq, IDX_HEADS * IDX_DIM)
    put(SEG_IXK * LANES, o_ixk, IDX_DIM)
    put(SEG_IXK * LANES + IXW_LANE, o_ixw, IDX_HEADS)
    return src


def _trunk_layer(x, q_off, past, wts, mla_kv_norm, band_rel_bias, ln1_g, ln1_b, ln2_g, ln2_b):
    bsz, s_len, _ = x.shape
    n = bsz * s_len
    pos = jnp.tile(q_off + jnp.arange(s_len, dtype=jnp.int32), bsz)
    x2 = x.reshape(n, D_MODEL)
    (q_a, lat_new, k_a, v_a, q_b, kv_b, sb_kv_new, q_c, kv_c, bd_kv_new,
     q_d, k_d, vt_d, ds_kv_new, q_ix, kidx_new, kx_d, w_ix) = _project(
        x2, pos, wts['w_rest'], wts['w_ukv'], mla_kv_norm[None])
    seq = lambda a: a.reshape(bsz, s_len, a.shape[-1])
    seq_t = lambda a: jnp.transpose(a.reshape(a.shape[0], bsz, s_len), (1, 0, 2))
    q_a, k_a, v_a, q_b, kv_b, q_c, kv_c, q_d, k_d, q_ix, kx_d = map(
        seq, (q_a, k_a, v_a, q_b, kv_b, q_c, kv_c, q_d, k_d, q_ix, kx_d))
    vt_d, w_ix = seq_t(vt_d), seq_t(w_ix)
    kv_state = lambda a: a.reshape(bsz, s_len, 2, N_HEADS, HEAD_DIM)
    lat_new, kidx_new = seq(lat_new), seq(kidx_new)
    sb_kv_new, bd_kv_new, ds_kv_new = kv_state(sb_kv_new), kv_state(bd_kv_new), kv_state(ds_kv_new)

    if past is None:
        kv_c = jnp.pad(kv_c, ((0, 0), (BAND_WINDOW, 0), (0, 0)))
        band_rows = bd_kv_new[:, s_len - min(BAND_WINDOW, s_len):]
        s_k = s_len
    else:
        past_lat, past_sb, past_band, past_ds, past_kidx = past
        p_len = past_lat.shape[1]
        s_k = p_len + s_len
        rows_bf16 = lambda a: a.reshape(bsz, a.shape[1], -1).astype(BF16)
        kv_past = _matmul(past_lat[..., :MLA_D_C].reshape(bsz * p_len, MLA_D_C), wts['w_ukv'])
        kv_past = kv_past.reshape(bsz, p_len, -1)
        kr_past = jnp.pad(past_lat[..., MLA_D_C:],
                          ((0, 0), (0, 0), (MLA_ROPE_LANE, LANES - MLA_ROPE_LANE - MLA_D_ROPE)))
        k_past = kv_past[..., :N_HEADS * LANES].reshape(bsz, p_len, N_HEADS, LANES) + kr_past[:, :, None]
        k_a = jnp.concatenate([rows_bf16(k_past), k_a], axis=1)
        v_past = kv_past[..., N_HEADS * LANES:] + _ones_upper_half(p_len)
        v_a = jnp.concatenate([v_past.astype(BF16), v_a], axis=1)
        kv_b = jnp.concatenate([rows_bf16(past_sb), kv_b], axis=1)
        kv_c = jnp.concatenate([rows_bf16(past_band), kv_c], axis=1)
        past_ds = past_ds.reshape(bsz, p_len, 2 * BRANCH_WIDTH)
        k_d = jnp.concatenate([past_ds[..., :BRANCH_WIDTH].astype(BF16), k_d], axis=1)
        vt_d = jnp.concatenate(
            [jnp.transpose(past_ds[..., BRANCH_WIDTH:], (0, 2, 1)).astype(BF16), vt_d], axis=2)
        kx_d = jnp.concatenate([past_kidx.astype(BF16), kx_d], axis=1)
        band_rows = bd_kv_new

    o_a = _mla_attention(q_a, _pad_keys(k_a), _pad_keys(v_a), q_off)
    o_b = _sb_attention(q_b, _pad_keys(kv_b), q_off)
    o_c = _band_attention(q_c, kv_c, band_rel_bias, q_off)
    n_sel = min(DSA_TOPK, s_k // 4)
    o_d = _dsa_attention(q_ix, w_ix, _pad_keys(kx_d), q_d, _pad_keys(k_d), _pad_keys(vt_d, axis=2),
                         q_off, n_sel)
    o_d = jnp.transpose(o_d, (0, 2, 1))

    flat = lambda o: o.reshape(n, BRANCH_WIDTH)
    x1 = _merge_out_ln(x2, wts['w_gates'], flat(o_a), flat(o_b), flat(o_c), flat(o_d),
                       wts['w_branch'], wts['w_out'], ln1_g[None], ln1_b[None])
    x_out = _ffn_ln(x1, wts['w_gate'], wts['w_up'], wts['w_down'], ln2_g[None], ln2_b[None])
    return x_out.reshape(bsz, s_len, D_MODEL), (lat_new, sb_kv_new, band_rows, ds_kv_new, kidx_new)


def kernel(x_prompt, x_sample, cache_mla_latent, cache_sb_kv, cache_band_kv, cache_dsa_kv, cache_dsa_kidx, w_in, mla_kv_norm, mla_w_uk, mla_w_uv, band_rel_bias, w_branch, w_out, ln1_g, ln1_b, w_gate_up, w_down, ln2_g, ln2_b):
    past_len = cache_mla_latent.shape[2]
    xp, xs = x_prompt, x_sample
    st_p, st_s = [], []
    for l in range(w_in.shape[0]):
        wts = _prepare_weights(w_in[l], mla_w_uk[l], mla_w_uv[l], w_branch[l], w_out[l],
                               w_gate_up[l], w_down[l])
        params = (wts, mla_kv_norm[l], band_rel_bias[l], ln1_g[l], ln1_b[l], ln2_g[l], ln2_b[l])
        xp, new_p = _trunk_layer(xp, 0, None, *params)
        past = (cache_mla_latent[l], cache_sb_kv[l], cache_band_kv[l], cache_dsa_kv[l],
                cache_dsa_kidx[l])
        xs, new_s = _trunk_layer(xs, past_len, past, *params)
        st_p.append(new_p)
        st_s.append(new_s)
    stack = lambda st, i: jnp.stack([s[i] for s in st])
    return (xp, xs) + tuple(stack(st_p, i) for i in range(5)) + tuple(stack(st_s, i) for i in range(5))
```

```python
import functools
import math

import numpy as np
import jax
import jax.numpy as jnp
from jax import lax
from jax.experimental import pallas as pl
from jax.experimental.pallas import tpu as pltpu

D_MODEL = 1024
CHUNK = 64
CHUNK_SHIFT = 6
N_BRANCH = 4
N_HEADS = 4
HEAD_DIM = 64
BRANCH_WIDTH = N_HEADS * HEAD_DIM
MLA_D_C = 128
MLA_D_NOPE = 64
MLA_D_ROPE = 32
MLA_THETA = 10000.0
ROPE_THETA = 500000.0
BAND_LEFT_CHUNKS = 8
BAND_WINDOW = BAND_LEFT_CHUNKS * CHUNK
REL_CLIP = 128
IDX_HEADS = 8
IDX_DIM = 64
DSA_TOPK = 256
D_FF = ((8 * D_MODEL // 3 + 255) // 256) * 256
DEPTH = 2
ALPHA = (2 * DEPTH) ** 0.25
NEG_INF = -1e30
LOG2_E = math.log2(math.e)
SB_RUN_FLOOR = -150.0
LN_EPS = 1e-5
IN_SIZES = (N_HEADS * (MLA_D_NOPE + MLA_D_ROPE), MLA_D_C, MLA_D_ROPE,
            3 * BRANCH_WIDTH, 3 * BRANCH_WIDTH, 3 * BRANCH_WIDTH,
            IDX_HEADS * IDX_DIM, IDX_DIM, IDX_HEADS, N_BRANCH * D_MODEL)
REST_WIDTH = sum(IN_SIZES[:-1])
GATE_WIDTH = IN_SIZES[-1]

LANES = 128
KEY_ALIGN = 512
INT_MIN = -2 ** 31
VMEM_LIMIT = 56 * 1024 * 1024

F32 = jnp.float32
BF16 = jnp.bfloat16


def _cparams(n_axes):
    return pltpu.CompilerParams(dimension_semantics=("arbitrary",) * n_axes,
                                vmem_limit_bytes=VMEM_LIMIT)


def _pick(n, candidates):
    for c in candidates:
        if n % c == 0:
            return c
    return n


def _ones_upper_half(rows):
    lane = lax.broadcasted_iota(jnp.int32, (rows, N_HEADS * LANES), 1)
    return jnp.where((lane & (LANES - 1)) >= HEAD_DIM, 1.0, 0.0)


def _dot_nt(a, b):
    return lax.dot_general(a, b, (((1,), (1,)), ((), ())), preferred_element_type=F32)


def _mm_kernel(a_ref, b_ref, o_ref):
    o_ref[...] = jnp.dot(a_ref[...].astype(BF16), b_ref[...], preferred_element_type=F32)


def _matmul(a, b):
    m, k = a.shape
    n = b.shape[1]
    tm = _pick(m, (1024, 512, 256, 128, 64, 32, 16, 8))
    tn = _pick(n, (512, 384, 256, 128))
    return pl.pallas_call(
        _mm_kernel,
        grid=(m // tm, n // tn),
        in_specs=[pl.BlockSpec((tm, k), lambda i, j: (i, 0)),
                  pl.BlockSpec((k, tn), lambda i, j: (0, j))],
        out_specs=pl.BlockSpec((tm, tn), lambda i, j: (i, j)),
        out_shape=jax.ShapeDtypeStruct((m, n), F32),
        compiler_params=_cparams(2),
        name="matmul",
    )(a, b)


SEG_AQ = 0
SEG_CKV = 4
SEG_KR = 5
SEG_SB = 6
SEG_BD = 12
SEG_DS = 18
SEG_IXQ = 24
SEG_IXK = 28
N_SEG = 30
IXW_LANE = 96
PROJ_WIDTH = N_SEG * LANES
MLA_ROPE_LANE = 64
LAT_WIDTH = MLA_D_C + MLA_D_ROPE


def _proj_kernel(x_ref, w_ref, wukv_ref, g_ref, ca_ref, sma_ref, spa_ref, cp_ref, smp_ref, spp_ref,
                 qa_ref, lat_ref, ka_ref, va_ref,
                 qb_ref, kvb_ref, sbkv_ref, qc_ref, kvc_ref, bdkv_ref,
                 qd_ref, kd_ref, vtd_ref, dskv_ref, qix_ref, kidx_ref, kx_ref, wix_ref):
    proj = jnp.dot(x_ref[...].astype(BF16), w_ref[...], preferred_element_type=F32)
    seg = lambda s, n=1: proj[:, s * LANES:(s + n) * LANES]

    def rope(t, c_ref, sm_ref, sp_ref, half):
        return (t * c_ref[...] + pltpu.roll(t, LANES - half, 1) * sm_ref[...]
                + pltpu.roll(t, half, 1) * sp_ref[...])

    rope_a = functools.partial(rope, c_ref=ca_ref, sm_ref=sma_ref, sp_ref=spa_ref, half=MLA_D_ROPE // 2)
    rope_p = functools.partial(rope, c_ref=cp_ref, sm_ref=smp_ref, sp_ref=spp_ref, half=HEAD_DIM // 8)
    head_scale = HEAD_DIM ** -0.5 * LOG2_E

    mla_scale = (MLA_D_NOPE + MLA_D_ROPE) ** -0.5 * LOG2_E
    for h in range(N_HEADS):
        qa_ref[:, h * LANES:(h + 1) * LANES] = (rope_a(seg(SEG_AQ + h)) * mla_scale).astype(BF16)
    ckv = seg(SEG_CKV)
    ckv = ckv * lax.rsqrt(jnp.mean(ckv * ckv, axis=-1, keepdims=True) + LN_EPS) * g_ref[...]
    kr = rope_a(seg(SEG_KR))
    lat_ref[:, :MLA_D_C] = ckv
    lat_ref[:, MLA_D_C:] = kr[:, MLA_ROPE_LANE:MLA_ROPE_LANE + MLA_D_ROPE]
    kv_a = jnp.dot(ckv.astype(BF16), wukv_ref[...], preferred_element_type=F32)
    for h in range(N_HEADS):
        ka_ref[:, h * LANES:(h + 1) * LANES] = (kv_a[:, h * LANES:(h + 1) * LANES] + kr).astype(BF16)
    va_ref[...] = (kv_a[:, N_HEADS * LANES:] + _ones_upper_half(kv_a.shape[0])).astype(BF16)

    for s0, q_ref, kv_ref, new_ref in ((SEG_SB, qb_ref, kvb_ref, sbkv_ref),
                                       (SEG_BD, qc_ref, kvc_ref, bdkv_ref)):
        q_ref[...] = (seg(s0, 2) * head_scale).astype(BF16)
        kv = seg(s0 + 2, 4)
        new_ref[...] = kv
        kv_ref[...] = kv.astype(BF16)

    for p in range(2):
        qd_ref[:, p * LANES:(p + 1) * LANES] = (rope_p(seg(SEG_DS + p)) * head_scale).astype(BF16)
        k_rot = rope_p(seg(SEG_DS + 2 + p))
        dskv_ref[:, p * LANES:(p + 1) * LANES] = k_rot
        kd_ref[:, p * LANES:(p + 1) * LANES] = k_rot.astype(BF16)
    v_d = seg(SEG_DS + 4, 2)
    dskv_ref[:, BRANCH_WIDTH:] = v_d
    vtd_ref[...] = v_d.T.astype(BF16)
    for p in range(IDX_HEADS // 2):
        qix_ref[:, p * LANES:(p + 1) * LANES] = (rope_p(seg(SEG_IXQ + p)) * IDX_DIM ** -0.5).astype(BF16)
    ixk = rope_p(seg(SEG_IXK))
    kidx_ref[...] = ixk[:, :IDX_DIM]
    kx_ref[...] = ixk[:, :IDX_DIM].astype(BF16)
    wix_ref[...] = ixk.T[IXW_LANE:IXW_LANE + IDX_HEADS, :]


def _rope_tables(pos, theta, width, starts):
    half = width // 2
    inv = jnp.exp(jnp.arange(half, dtype=F32) * (-2.0 * math.log(theta) / width))
    freq_idx = np.zeros(LANES, np.int32)
    x1 = np.zeros(LANES, np.float32)
    x2 = np.zeros(LANES, np.float32)
    for s in starts:
        freq_idx[s:s + width] = np.tile(np.arange(half), 2)
        x1[s:s + half] = 1.0
        x2[s + half:s + width] = 1.0
    inv_lane = inv[freq_idx] * jnp.asarray(x1 + x2)
    ang = pos.astype(F32)[:, None] * inv_lane[None, :]
    sin = jnp.sin(ang)
    return jnp.cos(ang), -sin * x1[None, :], sin * x2[None, :]


def _project(x, pos, w_rest, w_ukv, kv_norm):
    n = x.shape[0]
    ts = _pick(n, (512, 256, 128, 64))
    tables = (_rope_tables(pos, MLA_THETA, MLA_D_ROPE, (MLA_ROPE_LANE,))
              + _rope_tables(pos, ROPE_THETA, HEAD_DIM // 4, (0, HEAD_DIM)))
    row = lambda i: (i, 0)
    fixed = lambda i: (0, 0)
    rows = lambda w: pl.BlockSpec((ts, w), row)
    out = lambda w, dt: jax.ShapeDtypeStruct((n, w), dt)
    specs = [
        (N_HEADS * LANES, BF16), (LAT_WIDTH, F32), (N_HEADS * LANES, BF16), (N_HEADS * LANES, BF16),
        (BRANCH_WIDTH, BF16), (2 * BRANCH_WIDTH, BF16), (2 * BRANCH_WIDTH, F32),
        (BRANCH_WIDTH, BF16), (2 * BRANCH_WIDTH, BF16), (2 * BRANCH_WIDTH, F32),
        (BRANCH_WIDTH, BF16), (BRANCH_WIDTH, BF16), None, (2 * BRANCH_WIDTH, F32),
        (IDX_HEADS * IDX_DIM, BF16), (IDX_DIM, F32), (IDX_DIM, BF16), None]
    out_specs, out_shape = [], []
    for k, spec in enumerate(specs):
        if spec is None:
            rows_t, dt = ((BRANCH_WIDTH, BF16), (IDX_HEADS, F32))[k > 12]
            out_specs.append(pl.BlockSpec((rows_t, ts), lambda i: (0, i)))
            out_shape.append(jax.ShapeDtypeStruct((rows_t, n), dt))
        else:
            out_specs.append(rows(spec[0]))
            out_shape.append(out(*spec))
    return pl.pallas_call(
        _proj_kernel,
        grid=(n // ts,),
        in_specs=[rows(D_MODEL),
                  pl.BlockSpec((D_MODEL, PROJ_WIDTH), fixed, pipeline_mode=pl.Buffered(1)),
                  pl.BlockSpec((MLA_D_C, 2 * N_HEADS * LANES), fixed),
                  pl.BlockSpec((1, MLA_D_C), fixed)] + [rows(LANES)] * 6,
        out_specs=out_specs,
        out_shape=out_shape,
        compiler_params=_cparams(1),
        name="project_prepare",
    )(x, w_rest, w_ukv, kv_norm, *tables)


def _mla_kernel(q_ref, k_ref, v_ref, o_ref, *, qb, kb, q_off, skp):
    q0 = pl.program_id(1) * qb
    hi = jnp.minimum((((q_off + q0 + qb - 1) >> CHUNK_SHIFT) + 1) * CHUNK, skp)
    nkb = (hi + kb - 1) // kb
    qpos = q_off + q0 + lax.broadcasted_iota(jnp.int32, (qb, 1), 0)
    cend = ((qpos >> CHUNK_SHIFT) + 1) * CHUNK
    col = lax.broadcasted_iota(jnp.int32, (qb, kb), 1)
    lane = lax.broadcasted_iota(jnp.int32, (qb, LANES), 1)
    n_full = jnp.minimum(((((q_off + q0) >> CHUNK_SHIFT) + 1) * CHUNK) // kb, nkb)

    def body(i, carry, masked):
        k0 = pl.multiple_of(i * kb, kb)
        s = [_dot_nt(q_ref[0, :, h * LANES:(h + 1) * LANES],
                     k_ref[0, pl.ds(k0, kb), h * LANES:(h + 1) * LANES]) for h in range(N_HEADS)]
        if masked:
            vis = col + k0 < cend
            s = [jnp.where(vis, sh, NEG_INF) for sh in s]
        p, new = [], []
        for h in range(N_HEADS):
            m, acc = carry[2 * h:2 * h + 2]
            m_new = jnp.maximum(m, jnp.max(s[h], axis=1, keepdims=True))
            p.append(jnp.exp2((s[h] - m_new).astype(BF16)))
            new += [m_new, jnp.exp2(m - m_new) * acc]
        for h in range(N_HEADS):
            vblk = v_ref[0, pl.ds(k0, kb), h * LANES:(h + 1) * LANES]
            new[2 * h + 1] = new[2 * h + 1] + jnp.dot(p[h], vblk, preferred_element_type=F32)
        return tuple(new)

    init = (jnp.full((qb, 1), NEG_INF, F32), jnp.zeros((qb, LANES), F32)) * N_HEADS
    carry = lax.fori_loop(0, n_full, functools.partial(body, masked=False), init)
    carry = lax.fori_loop(n_full, nkb, functools.partial(body, masked=True), carry)
    outs = [carry[2 * h + 1] / pltpu.roll(carry[2 * h + 1], HEAD_DIM, 1) for h in range(N_HEADS)]
    for pair in range(N_HEADS // 2):
        o_ref[0, :, pair * LANES:(pair + 1) * LANES] = jnp.where(
            lane < HEAD_DIM, outs[2 * pair], pltpu.roll(outs[2 * pair + 1], HEAD_DIM, 1))


def _mla_attention(q, k, v, q_off):
    b, sq, _ = q.shape
    skp = k.shape[1]
    qb = _pick(sq, (256, 128, 64))
    kb = _pick(skp, (1024, KEY_ALIGN))
    kern = functools.partial(_mla_kernel, qb=qb, kb=kb, q_off=q_off, skp=skp)
    return pl.pallas_call(
        kern,
        grid=(b, sq // qb),
        in_specs=[pl.BlockSpec((1, qb, N_HEADS * LANES), lambda bi, qi: (bi, qi, 0)),
                  pl.BlockSpec((1, skp, N_HEADS * LANES), lambda bi, qi: (bi, 0, 0),
                               pipeline_mode=pl.Buffered(1)),
                  pl.BlockSpec((1, skp, N_HEADS * LANES), lambda bi, qi: (bi, 0, 0),
                               pipeline_mode=pl.Buffered(1))],
        out_specs=pl.BlockSpec((1, qb, BRANCH_WIDTH), lambda bi, qi: (bi, qi, 0)),
        out_shape=jax.ShapeDtypeStruct((b, sq, BRANCH_WIDTH), F32),
        compiler_params=_cparams(2),
        name="mla_attention",
    )(q, k, v)


def _head_mask(x_pair, h):
    lane = lax.broadcasted_iota(jnp.int32, x_pair.shape, 1)
    keep = (lane < HEAD_DIM) if h % 2 == 0 else (lane >= HEAD_DIM)
    return jnp.where(keep, x_pair, jnp.zeros_like(x_pair))


def _sb_kernel(q_ref, kv_ref, t_ref, o_ref, *, qb, kb, sub, q_off):
    q0 = pl.program_id(1) * qb
    hi = q_off + q0 + qb - 1
    nkb = (hi + kb - 1) // kb
    qpos = q_off + q0 + lax.broadcasted_iota(jnp.int32, (qb, 1), 0)
    col = lax.broadcasted_iota(jnp.int32, (qb, sub), 1)
    lane = lax.broadcasted_iota(jnp.int32, (qb, LANES), 1)
    tri = t_ref[...]
    n_full = jnp.minimum((q_off + q0) // kb, nkb)
    qh = [_head_mask(q_ref[0, :, (h // 2) * LANES:(h // 2 + 1) * LANES], h) for h in range(N_HEADS)]

    def body(i, carry, masked):
        k0 = pl.multiple_of(i * kb, kb)
        carry = list(carry)
        units = [(j, h) for j in reversed(range(kb // sub)) for h in range(N_HEADS)]
        strict = {j: col + (k0 + j * sub) < qpos for j in range(kb // sub)} if masked else None
        z = {}
        for j, h in units:
            pair = h // 2
            kblk = kv_ref[0, pl.ds(k0 + j * sub, sub), pair * LANES:(pair + 1) * LANES]
            z[j, h] = _dot_nt(qh[h], kblk)
        log_1m, suffix = {}, {}
        for u in units:
            nz = -z[u]
            t = jnp.minimum(nz, 0.0) - jnp.log2(1.0 + jnp.exp2(jnp.minimum(z[u], nz)))
            if masked:
                t = jnp.where(strict[u[0]], t, 0.0)
            log_1m[u] = t
        for u in units:
            hi_part = log_1m[u].astype(BF16)
            lo_part = (log_1m[u] - hi_part.astype(F32)).astype(BF16)
            suffix[u] = (jnp.dot(hi_part, tri, preferred_element_type=F32)
                         + jnp.dot(lo_part, tri, preferred_element_type=F32))
        for j, h in units:
            u = (j, h)
            run, acc = carry[2 * h:2 * h + 2]
            expo = z[u] + log_1m[u] + suffix[u] + run
            if masked:
                expo = jnp.where(strict[j], expo, NEG_INF)
            a = jnp.exp2(expo).astype(BF16)
            pair = h // 2
            vblk = kv_ref[0, pl.ds(k0 + j * sub, sub),
                          BRANCH_WIDTH + pair * LANES:BRANCH_WIDTH + (pair + 1) * LANES]
            carry[2 * h + 1] = acc + jnp.dot(a, vblk, preferred_element_type=F32)
            carry[2 * h] = run + jnp.sum(log_1m[u], axis=1, keepdims=True)
        return tuple(carry)

    def live(state):
        i, carry = state[0], state[1:]
        top = carry[0]
        for h in range(1, N_HEADS):
            top = jnp.maximum(top, carry[2 * h])
        return (i >= 0) & (jnp.max(top) >= SB_RUN_FLOOR)

    def step(masked):
        return lambda state: (state[0] - 1,) + body(state[0], state[1:], masked)

    init = (jnp.zeros((qb, 1), F32), jnp.zeros((qb, LANES), F32)) * N_HEADS
    carry = lax.fori_loop(0, nkb - n_full, lambda i, c: body(nkb - 1 - i, c, True), init)
    state = lax.while_loop(live, step(False), (n_full - 1,) + tuple(carry))
    carry = state[1:]
    outs = [carry[2 * h + 1] for h in range(N_HEADS)]
    for pair in range(N_HEADS // 2):
        o_ref[0, :, pair * LANES:(pair + 1) * LANES] = jnp.where(
            lane < HEAD_DIM, outs[2 * pair], outs[2 * pair + 1])


def _sb_attention(q, kv, q_off):
    b, sq, _ = q.shape
    skp = kv.shape[1]
    qb = _pick(sq, (256, 128, 64))
    sub = 256
    kb = KEY_ALIGN
    tri =jnp.asarray(np.tril(np.ones((sub, sub), np.float32), -1), BF16)
    kern = functools.partial(_sb_kernel, qb=qb, kb=kb, sub=sub, q_off=q_off)
    return pl.pallas_call(
        kern,
        grid=(b, sq // qb),
        in_specs=[pl.BlockSpec((1, qb, BRANCH_WIDTH), lambda bi, qi: (bi, qi, 0)),
                  pl.BlockSpec((1, skp, 2 * BRANCH_WIDTH), lambda bi, qi: (bi, 0, 0),
                               pipeline_mode=pl.Buffered(1)),
                  pl.BlockSpec((sub, sub), lambda bi, qi: (0, 0))],
        out_specs=pl.BlockSpec((1, qb, BRANCH_WIDTH), lambda bi, qi: (bi, qi, 0)),
        out_shape=jax.ShapeDtypeStruct((b, sq, BRANCH_WIDTH), F32),
        compiler_params=_cparams(2),
        name="stick_breaking_attention",
    )(q, kv, tri)


def _band_kernel(bias_ref, q_ref, kv_ref, o_ref, tile_ref, *, qb, win, q_off):
    first = (pl.program_id(0) == 0) & (pl.program_id(1) == 0)
    row = lax.broadcasted_iota(jnp.int32, (qb, win), 0)
    col = lax.broadcasted_iota(jnp.int32, (qb, win), 1)

    @pl.when(first)
    def _():
        rel = jnp.clip(row + BAND_WINDOW - col, -REL_CLIP, REL_CLIP) + REL_CLIP
        qch = row >> CHUNK_SHIFT
        kch = col >> CHUNK_SHIFT
        in_band = (kch >= qch) & (kch <= qch + BAND_LEFT_CHUNKS)
        for h in range(N_HEADS):
            tile_ref[h] = jnp.full((qb, win), bias_ref[h, 0], F32)

        def fill(r, c):
            hit = rel == r
            for h in range(N_HEADS):
                tile_ref[h] = jnp.where(hit, bias_ref[h, r], tile_ref[h])
            return c

        lax.fori_loop(1, 2 * REL_CLIP + 1, fill, 0)
        for h in range(N_HEADS):
            tile_ref[h] = jnp.where(in_band, tile_ref[h] * LOG2_E, NEG_INF)

    q0 = pl.multiple_of(pl.program_id(1) * qb, qb)
    kpos = col + (q_off + q0 - BAND_WINDOW)
    lane = lax.broadcasted_iota(jnp.int32, (qb, LANES), 1)
    outs = []
    for h in range(N_HEADS):
        pair = h // 2
        qh = _head_mask(q_ref[0, :, pair * LANES:(pair + 1) * LANES], h)
        kwin = kv_ref[0, pl.ds(q0, win), pair * LANES:(pair + 1) * LANES]
        s = _dot_nt(qh, kwin) + tile_ref[h]
        s = jnp.where(kpos >= 0, s, NEG_INF)
        m = jnp.max(s, axis=1, keepdims=True)
        p = jnp.exp2(s - m)
        l = jnp.sum(p, axis=1, keepdims=True)
        vwin = kv_ref[0, pl.ds(q0, win),
                      BRANCH_WIDTH + pair * LANES:BRANCH_WIDTH + (pair + 1) * LANES]
        outs.append(jnp.dot(p.astype(BF16), vwin, preferred_element_type=F32) / l)
    for pair in range(N_HEADS // 2):
        o_ref[0, :, pair * LANES:(pair + 1) * LANES] = jnp.where(
            lane < HEAD_DIM, outs[2 * pair], outs[2 * pair + 1])


def _band_attention(q, kv, rel_bias, q_off):
    b, sq, _ = q.shape
    sk = kv.shape[1]
    qb = _pick(sq, (256, 128, 64))
    win = qb + BAND_WINDOW
    kern = functools.partial(_band_kernel, qb=qb, win=win, q_off=q_off)
    return pl.pallas_call(
        kern,
        grid=(b, sq // qb),
        in_specs=[pl.BlockSpec(memory_space=pltpu.SMEM),
                  pl.BlockSpec((1, qb, BRANCH_WIDTH), lambda bi, qi: (bi, qi, 0)),
                  pl.BlockSpec((1, sk, 2 * BRANCH_WIDTH), lambda bi, qi: (bi, 0, 0),
                               pipeline_mode=pl.Buffered(1))],
        out_specs=pl.BlockSpec((1, qb, BRANCH_WIDTH), lambda bi, qi: (bi, qi, 0)),
        out_shape=jax.ShapeDtypeStruct((b, sq, BRANCH_WIDTH), F32),
        scratch_shapes=[pltpu.VMEM((N_HEADS, qb, win), F32)],
        compiler_params=_cparams(2),
        name="band_attention",
    )(rel_bias, q, kv)


def _dsa_kernel(qi_ref, w_ref, kx_ref, q_ref, k_ref, vt_ref, o_ref, key_ref, key16_ref, cut_ref,
                sa_ref, sb_ref,
                *, qb, kb, q_off, skp, n_sel):
    q0 = pl.program_id(1) * qb
    hi = jnp.minimum((((q_off + q0 + qb - 1) >> CHUNK_SHIFT) + 1) * CHUNK, skp)
    nkb = (hi + kb - 1) // kb
    qpos = q_off + q0 + lax.broadcasted_iota(jnp.int32, (1, qb), 1)
    cend = ((qpos >> CHUNK_SHIFT) + 1) * CHUNK
    row = lax.broadcasted_iota(jnp.int32, (kb, qb), 0)
    float_key = lambda bits: jnp.where(bits < 0, INT_MIN - bits, bits)
    neg_key = float_key(lax.bitcast_convert_type(jnp.full((1, 1), NEG_INF, F32), jnp.int32))
    n_full = jnp.minimum(((((q_off + q0) >> CHUNK_SHIFT) + 1) * CHUNK) // kb, nkb)

    def score_block(i, c, masked):
        k0 = pl.multiple_of(i * kb, kb)
        kx = kx_ref[0, pl.ds(k0, kb), :]
        r = [_dot_nt(kx, qi_ref[0, :, j * IDX_DIM:(j + 1) * IDX_DIM]) for j in range(IDX_HEADS)]
        score = jnp.zeros((kb, qb), F32)
        for j in range(IDX_HEADS):
            wj = w_ref[0, j:j + 1, :] * (IDX_HEADS ** -0.5)
            score = score + wj * jnp.maximum(r[j], 0.0)
        if masked:
            score = jnp.where(row + k0 < cend, score, NEG_INF)
        keys = float_key(lax.bitcast_convert_type(score, jnp.int32))
        key_ref[pl.ds(k0, kb), :] = keys
        key16_ref[pl.ds(k0, kb), :] = (keys >> 16).astype(jnp.int16)
        return c

    lax.fori_loop(0, n_full, functools.partial(score_block, masked=False), 0)
    lax.fori_loop(n_full, nkb, functools.partial(score_block, masked=True), 0)

    slab = 64 if kb % 64 == 0 else kb

    def count_ge(cand):
        def blk(i, cnt):
            k0 = pl.multiple_of(i * kb, kb)
            for s in range(kb // slab):
                keys = key_ref[pl.ds(k0 + s * slab, slab), :]
                cnt = cnt + jnp.where(keys >= cand, 1.0, 0.0)
            return cnt
        part = lax.fori_loop(0, nkb, blk, jnp.zeros((slab, qb), F32))
        return jnp.sum(part, axis=0, keepdims=True)

    slab16 = 128 if kb % 128 == 0 else kb

    def count16_ge(cand16):
        def blk(i, cnt):
            k0 = pl.multiple_of(i * kb, kb)
            for s in range(kb // slab16):
                k16 = key16_ref[pl.ds(k0 + s * slab16, slab16), :]
                cnt = cnt + jnp.where(k16 >= cand16, jnp.int16(1), jnp.int16(0))
            return cnt
        part = lax.fori_loop(0, nkb, blk, jnp.zeros((slab16, qb), jnp.int16))
        return jnp.sum(part.astype(F32), axis=0, keepdims=True)

    def undecided(t_end):
        def cond(state):
            t, _, n_ge = state
            return (t < t_end) & (jnp.max(jnp.abs(n_ge - n_sel)) > 0.0)
        return cond

    def bisect(to_16bit):
        def step(state):
            t, thr, n_ge = state
            cand = thr + jnp.left_shift(jnp.int32(1), 31 - t)
            cnt = count16_ge(to_16bit(cand))
            ok = cnt >= n_sel
            return t + 1, jnp.where(ok, cand, thr), jnp.where(ok, cnt, n_ge)
        return step

    stored = (nkb * kb).astype(F32)
    state = lax.while_loop(
        undecided(16), bisect(lambda cand: (cand >> 16).astype(jnp.int16)),
        (jnp.int32(0), jnp.full((1, qb), INT_MIN, jnp.int32), jnp.full((1, qb), stored, F32)))

    @pl.when(undecided(32)(state))
    def _():
        thr_hi = state[1] >> 16

        def low_halves(i, c):
            k0 = pl.multiple_of(i * kb, kb)
            keys = key_ref[pl.ds(k0, kb), :]
            hi16 = keys >> 16
            low = jnp.where(hi16 == thr_hi, (keys & 0xFFFF) - 0x8000,
                            jnp.where(hi16 > thr_hi, 0x7FFF, -0x8000))
            key16_ref[pl.ds(k0, kb), :] = low.astype(jnp.int16)
            return c

        lax.fori_loop(0, nkb, low_halves, 0)

    _, thr, n_ge = lax.while_loop(
        undecided(32), bisect(lambda cand: ((cand & 0xFFFF) - 0x8000).astype(jnp.int16)), state)
    excess = jnp.where((n_ge > n_sel) & (thr > neg_key), 1.0, 0.0)
    cut_ref[...] = jnp.full((1, qb), skp, jnp.int32)

    @pl.when(jnp.max(excess) > 0.0)
    def _():
        need = n_sel - count_ge(thr + 1)

        def count_tied_below(limit):
            def blk(i, cnt):
                k0 = pl.multiple_of(i * kb, kb)
                tied = (key_ref[pl.ds(k0, kb), :] == thr) & (row + k0 < limit)
                return cnt + jnp.sum(jnp.where(tied, 1.0, 0.0), axis=0, keepdims=True)
            return lax.fori_loop(0, nkb, blk, jnp.zeros((1, qb), F32))

        n_bits = max(1, (skp - 1).bit_length())

        def bisect_cut(t, cut):
            cand = cut + jnp.left_shift(jnp.int32(1), n_bits - 1 - t)
            return jnp.where(count_tied_below(cand) < need, cand, cut)

        cut = lax.fori_loop(0, n_bits, bisect_cut, jnp.zeros((1, qb), jnp.int32))
        cut_ref[...] = jnp.where(excess > 0.0, cut + 1, skp)

    cut = cut_ref[...]

    qh = [_head_mask(q_ref[0, :, (h // 2) * LANES:(h // 2 + 1) * LANES], h) for h in range(N_HEADS)]

    kb3 = kb // 2
    row3 = lax.broadcasted_iota(jnp.int32, (kb3, qb), 0)

    def scores(i):
        k0 = pl.multiple_of(i * kb3, kb3)
        return tuple(_dot_nt(k_ref[0, pl.ds(k0, kb3), (h // 2) * LANES:(h // 2 + 1) * LANES], qh[h])
                     for h in range(N_HEADS))

    def attend(i, s, carry):
        k0 = pl.multiple_of(i * kb3, kb3)
        keys = key_ref[pl.ds(k0, kb3), :]
        kpos = row3 + k0
        sel = (keys >= thr) & ((keys != thr) | (kpos < cut)) & (kpos < cend)
        bias = jnp.where(sel, 0.0, NEG_INF)
        p, new = [], []
        for h in range(N_HEADS):
            m, acc = carry[2 * h:2 * h + 2]
            sh = s[h] + bias
            m_new = jnp.maximum(m, jnp.max(sh, axis=0, keepdims=True))
            p.append(jnp.exp2((sh - m_new).astype(BF16)))
            new += [m_new, jnp.exp2(m - m_new) * acc]
        for h in range(N_HEADS):
            vblk = jnp.concatenate([vt_ref[0, h * HEAD_DIM:(h + 1) * HEAD_DIM, pl.ds(k0, kb3)],
                                    jnp.ones((LANES - HEAD_DIM, kb3), BF16)], axis=0)
            new[2 * h + 1] = new[2 * h + 1] + jnp.dot(vblk, p[h], preferred_element_type=F32)
        return tuple(new)

    def put_scores(slot_ref, i):
        for h, sh in enumerate(scores(i)):
            slot_ref[h] = sh

    def attend_from(slot_ref, i, carry):
        return attend(i, tuple(slot_ref[h] for h in range(N_HEADS)), carry)

    def body(j, carry):
        put_scores(sb_ref, 2 * j + 1)
        carry = attend_from(sa_ref, 2 * j, carry)
        put_scores(sa_ref, 2 * j + 2)
        return attend_from(sb_ref, 2 * j + 1, carry)

    init = (jnp.full((1, qb), NEG_INF, F32), jnp.zeros((LANES, qb), F32)) * N_HEADS
    n_pairs = nkb * (kb // (2 * kb3))
    put_scores(sa_ref, 0)
    carry = lax.fori_loop(0, n_pairs - 1, body, init)
    last = 2 * (n_pairs - 1)
    put_scores(sb_ref, last + 1)
    carry = attend_from(sa_ref, last, carry)
    carry = attend_from(sb_ref, last + 1, carry)
    for h in range(N_HEADS):
        acc = carry[2 * h + 1]
        o_ref[0, h * HEAD_DIM:(h + 1) * HEAD_DIM, :] = acc[:HEAD_DIM, :] / acc[HEAD_DIM:HEAD_DIM + 1, :]


def _dsa_attention(qi, w, kx, q, k, vt, q_off, n_sel):
    b, sq, _ = q.shape
    skp = k.shape[1]
    qb = _pick(sq, (256, 128, 64))
    kb = _pick(skp, (1024, KEY_ALIGN))
    kern = functools.partial(_dsa_kernel, qb=qb, kb=kb, q_off=q_off, skp=skp, n_sel=n_sel)
    return pl.pallas_call(
        kern,
        grid=(b, sq // qb),
        in_specs=[pl.BlockSpec((1, qb, IDX_HEADS * IDX_DIM), lambda bi, qi_: (bi, qi_, 0)),
                  pl.BlockSpec((1, IDX_HEADS, qb), lambda bi, qi_: (bi, 0, qi_)),
                  pl.BlockSpec((1, skp, IDX_DIM), lambda bi, qi_: (bi, 0, 0),
                               pipeline_mode=pl.Buffered(1)),
                  pl.BlockSpec((1, qb, BRANCH_WIDTH), lambda bi, qi_: (bi, qi_, 0)),
                  pl.BlockSpec((1, skp, BRANCH_WIDTH), lambda bi, qi_: (bi, 0, 0),
                               pipeline_mode=pl.Buffered(1)),
                  pl.BlockSpec((1, BRANCH_WIDTH, skp), lambda bi, qi_: (bi, 0, 0),
                               pipeline_mode=pl.Buffered(1))],
        out_specs=pl.BlockSpec((1, BRANCH_WIDTH, qb), lambda bi, qi_: (bi, 0, qi_)),
        out_shape=jax.ShapeDtypeStruct((b, BRANCH_WIDTH, sq), F32),
        scratch_shapes=[pltpu.VMEM((skp, qb), jnp.int32), pltpu.VMEM((skp, qb), jnp.int16),
                        pltpu.VMEM((1, qb), jnp.int32),
                        pltpu.VMEM((N_HEADS, kb // 2, qb), F32), pltpu.VMEM((N_HEADS, kb // 2, qb), F32)],
        compiler_params=_cparams(2),
        name="dsa_attention",
    )(qi, w, kx, q, k, vt)


def _layer_norm(z, g, b):
    mu = jnp.mean(z, axis=-1, keepdims=True)
    zc = z - mu
    var = jnp.mean(zc * zc, axis=-1, keepdims=True)
    return zc * lax.rsqrt(var + LN_EPS) * g + b


def _merge_kernel(x_ref, wgate_ref, oa_ref, ob_ref, oc_ref, od_ref, wb_ref, wo_ref, g_ref, b_ref,
                  y_ref):
    xb = x_ref[...].astype(BF16)
    logits = [jnp.dot(xb, wgate_ref[:, n * D_MODEL:(n + 1) * D_MODEL], preferred_element_type=F32)
              for n in range(N_BRANCH)]
    branch = [jnp.dot(o_ref[...].astype(BF16), wb_ref[n], preferred_element_type=F32)
              for n, o_ref in enumerate((oa_ref, ob_ref, oc_ref, od_ref))]
    merged = jax.nn.sigmoid(logits[0]) * branch[0]
    for n in range(1, N_BRANCH):
        merged = merged + jax.nn.sigmoid(logits[n]) * branch[n]
    y = jnp.dot(merged.astype(BF16), wo_ref[...], preferred_element_type=F32)
    y_ref[...] = _layer_norm(ALPHA * x_ref[...] + y, g_ref[...], b_ref[...])


def _merge_out_ln(x, w_gate, o_a, o_b, o_c, o_d, w_branch, w_out, g, b):
    n = x.shape[0]
    ts = _pick(n, (512, 256, 128, 64))
    row = lambda i: (i, 0)
    obs = pl.BlockSpec((ts, BRANCH_WIDTH), row)
    return pl.pallas_call(
        _merge_kernel,
        grid=(n // ts,),
        in_specs=[pl.BlockSpec((ts, D_MODEL), row),
                  pl.BlockSpec((D_MODEL, GATE_WIDTH), lambda i: (0, 0), pipeline_mode=pl.Buffered(1)),
                  obs, obs, obs, obs,
                  pl.BlockSpec((N_BRANCH, BRANCH_WIDTH, D_MODEL), lambda i: (0, 0, 0)),
                  pl.BlockSpec((D_MODEL, D_MODEL), lambda i: (0, 0)),
                  pl.BlockSpec((1, D_MODEL), lambda i: (0, 0)),
                  pl.BlockSpec((1, D_MODEL), lambda i: (0, 0))],
        out_specs=pl.BlockSpec((ts, D_MODEL), row),
        out_shape=jax.ShapeDtypeStruct((n, D_MODEL), F32),
        compiler_params=_cparams(1),
        name="merge_out_ln",
    )(x, w_gate, o_a, o_b, o_c, o_d, w_branch, w_out, g, b)


def _ffn_kernel(x_ref, wg_ref, wu_ref, wd_ref, g_ref, b_ref, y_ref, acc_ref):
    f = pl.program_id(1)
    xb = x_ref[...].astype(BF16)
    gate = jnp.dot(xb, wg_ref[...], preferred_element_type=F32)
    up = jnp.dot(xb, wu_ref[...], preferred_element_type=F32)
    hidden = (gate * jax.nn.sigmoid(gate) * up).astype(BF16)
    part = jnp.dot(hidden, wd_ref[...], preferred_element_type=F32)

    @pl.when(f == 0)
    def _():
        acc_ref[...] = part

    @pl.when(f > 0)
    def _():
        acc_ref[...] += part

    @pl.when(f == pl.num_programs(1) - 1)
    def _():
        y_ref[...] = _layer_norm(ALPHA * x_ref[...] + acc_ref[...], g_ref[...], b_ref[...])


def _ffn_ln(x, w_gate, w_up, w_down, g, b):
    n = x.shape[0]
    ts = _pick(n, (1024, 512, 256, 128, 64))
    tf = D_FF // 2
    return pl.pallas_call(
        _ffn_kernel,
        grid=(n // ts, D_FF // tf),
        in_specs=[pl.BlockSpec((ts, D_MODEL), lambda i, f: (i, 0)),
                  pl.BlockSpec((D_MODEL, tf), lambda i, f: (0, f)),
                  pl.BlockSpec((D_MODEL, tf), lambda i, f: (0, f)),
                  pl.BlockSpec((tf, D_MODEL), lambda i, f: (f, 0)),
                  pl.BlockSpec((1, D_MODEL), lambda i, f: (0, 0)),
                  pl.BlockSpec((1, D_MODEL), lambda i, f: (0, 0))],
        out_specs=pl.BlockSpec((ts, D_MODEL), lambda i, f: (i, 0)),
        out_shape=jax.ShapeDtypeStruct((n, D_MODEL), F32),
        scratch_shapes=[pltpu.VMEM((ts, D_MODEL), F32)],
        compiler_params=_cparams(2),
        name="ffn_ln",
    )(x, w_gate, w_up, w_down, g, b)


def _pad_keys(a, axis=1):
    n = a.shape[axis]
    pad = (-n) % KEY_ALIGN
    if pad == 0:
        return a
    widths = [(0, 0)] * a.ndim
    widths[axis] = (0, pad)
    return jnp.pad(a, widths)


def _prepare_weights(w_in, mla_w_uk, mla_w_uv, w_branch, w_out, w_gate_up, w_down):
    w_rest = jnp.concatenate([w_in, jnp.zeros((D_MODEL, 1), w_in.dtype)], axis=1)
    w_rest = jnp.take(w_rest, _projection_columns(), axis=1).astype(BF16)
    w_uk = jnp.pad(mla_w_uk, ((0, 0), (0, 0), (0, LANES - MLA_D_NOPE))).reshape(MLA_D_C, -1)
    w_uv = jnp.pad(mla_w_uv, ((0, 0), (0, 0), (0, LANES - HEAD_DIM))).reshape(MLA_D_C, -1)
    w_ukv = jnp.concatenate([w_uk, w_uv], axis=1).astype(BF16)
    return dict(w_rest=w_rest, w_gates=w_in[:, REST_WIDTH:].astype(BF16), w_ukv=w_ukv,
                w_branch=w_branch.astype(BF16),
                w_out=w_out.astype(BF16), w_gate=w_gate_up[:, :D_FF].astype(BF16),
                w_up=w_gate_up[:, D_FF:].astype(BF16), w_down=w_down.astype(BF16))


def _projection_columns():
    zero_col = REST_WIDTH + GATE_WIDTH
    src = np.full(PROJ_WIDTH, zero_col, np.int32)
    o_aq, o_ckv, o_kr, o_sb, o_bd, o_ds, o_ixq, o_ixk, o_ixw = np.concatenate(
        [[0], np.cumsum(IN_SIZES[:-1])])[:9]
    hd = MLA_D_NOPE + MLA_D_ROPE

    def put(lane0, src0, n):
        src[lane0:lane0 + n] = src0 + np.arange(n)

    for h in range(N_HEADS):
        put((SEG_AQ + h) * LANES, o_aq + h * hd, hd)
    put(SEG_CKV * LANES, o_ckv, MLA_D_C)
    put(SEG_KR * LANES + MLA_ROPE_LANE, o_kr, MLA_D_ROPE)
    put(SEG_SB * LANES, o_sb, 3 * BRANCH_WIDTH)
    put(SEG_BD * LANES, o_bd, 3 * BRANCH_WIDTH)
    put(SEG_DS * LANES, o_ds, 3 * BRANCH_WIDTH)
    put(SEG_IXQ * LANES, o_ixq, IDX_HEADS * IDX_DIM)
    put(SEG_IXK * LANES, o_ixk, IDX_DIM)
    put(SEG_IXK * LANES + IXW_LANE, o_ixw, IDX_HEADS)
    return src


def _trunk_layer(x, q_off, past, wts, mla_kv_norm, band_rel_bias, ln1_g, ln1_b, ln2_g, ln2_b):
    bsz, s_len, _ = x.shape
    n = bsz * s_len
    pos = jnp.tile(q_off + jnp.arange(s_len, dtype=jnp.int32), bsz)
    x2 = x.reshape(n, D_MODEL)
    (q_a, lat_new, k_a, v_a, q_b, kv_b, sb_kv_new, q_c, kv_c, bd_kv_new,
     q_d, k_d, vt_d, ds_kv_new, q_ix, kidx_new, kx_d, w_ix) = _project(
        x2, pos, wts['w_rest'], wts['w_ukv'], mla_kv_norm[None])
    seq = lambda a: a.reshape(bsz, s_len, a.shape[-1])
    seq_t = lambda a: jnp.transpose(a.reshape(a.shape[0], bsz, s_len), (1, 0, 2))
    q_a, k_a, v_a, q_b, kv_b, q_c, kv_c, q_d, k_d, q_ix, kx_d = map(
        seq, (q_a, k_a, v_a, q_b, kv_b, q_c, kv_c, q_d, k_d, q_ix, kx_d))
    vt_d, w_ix = seq_t(vt_d), seq_t(w_ix)
    kv_state = lambda a: a.reshape(bsz, s_len, 2, N_HEADS, HEAD_DIM)
    lat_new, kidx_new = seq(lat_new), seq(kidx_new)
    sb_kv_new, bd_kv_new, ds_kv_new = kv_state(sb_kv_new), kv_state(bd_kv_new), kv_state(ds_kv_new)

    if past is None:
        kv_c = jnp.pad(kv_c, ((0, 0), (BAND_WINDOW, 0), (0, 0)))
        band_rows = bd_kv_new[:, s_len - min(BAND_WINDOW, s_len):]
        s_k = s_len
    else:
        past_lat, past_sb, past_band, past_ds, past_kidx = past
        p_len = past_lat.shape[1]
        s_k = p_len + s_len
        rows_bf16 = lambda a: a.reshape(bsz, a.shape[1], -1).astype(BF16)
        kv_past = _matmul(past_lat[..., :MLA_D_C].reshape(bsz * p_len, MLA_D_C), wts['w_ukv'])
        kv_past = kv_past.reshape(bsz, p_len, -1)
        kr_past = jnp.pad(past_lat[..., MLA_D_C:],
                          ((0, 0), (0, 0), (MLA_ROPE_LANE, LANES - MLA_ROPE_LANE - MLA_D_ROPE)))
        k_past = kv_past[..., :N_HEADS * LANES].reshape(bsz, p_len, N_HEADS, LANES) + kr_past[:, :, None]
        k_a = jnp.concatenate([rows_bf16(k_past), k_a], axis=1)
        v_past = kv_past[..., N_HEADS * LANES:] + _ones_upper_half(p_len)
        v_a = jnp.concatenate([v_past.astype(BF16), v_a], axis=1)
        kv_b = jnp.concatenate([rows_bf16(past_sb), kv_b], axis=1)
        kv_c = jnp.concatenate([rows_bf16(past_band), kv_c], axis=1)
        past_ds = past_ds.reshape(bsz, p_len, 2 * BRANCH_WIDTH)
        k_d = jnp.concatenate([past_ds[..., :BRANCH_WIDTH].astype(BF16), k_d], axis=1)
        vt_d = jnp.concatenate(
            [jnp.transpose(past_ds[..., BRANCH_WIDTH:], (0, 2, 1)).astype(BF16), vt_d], axis=2)
        kx_d = jnp.concatenate([past_kidx.astype(BF16), kx_d], axis=1)
        band_rows = bd_kv_new

    o_a = _mla_attention(q_a, _pad_keys(k_a), _pad_keys(v_a), q_off)
    o_b = _sb_attention(q_b, _pad_keys(kv_b), q_off)
    o_c = _band_attention(q_c, kv_c, band_rel_bias, q_off)
    n_sel = min(DSA_TOPK, s_k // 4)
    o_d = _dsa_attention(q_ix, w_ix, _pad_keys(kx_d), q_d, _pad_keys(k_d), _pad_keys(vt_d, axis=2),
                         q_off, n_sel)
    o_d = jnp.transpose(o_d, (0, 2, 1))

    flat = lambda o: o.reshape(n, BRANCH_WIDTH)
    x1 = _merge_out_ln(x2, wts['w_gates'], flat(o_a), flat(o_b), flat(o_c), flat(o_d),
                       wts['w_branch'], wts['w_out'], ln1_g[None], ln1_b[None])
    x_out = _ffn_ln(x1, wts['w_gate'], wts['w_up'], wts['w_down'], ln2_g[None], ln2_b[None])
    return x_out.reshape(bsz, s_len, D_MODEL), (lat_new, sb_kv_new, band_rows, ds_kv_new, kidx_new)


def kernel(x_prompt, x_sample, cache_mla_latent, cache_sb_kv, cache_band_kv, cache_dsa_kv, cache_dsa_kidx, w_in, mla_kv_norm, mla_w_uk, mla_w_uv, band_rel_bias, w_branch, w_out, ln1_g, ln1_b, w_gate_up, w_down, ln2_g, ln2_b):
    past_len = cache_mla_latent.shape[2]
    xp, xs = x_prompt, x_sample
    st_p, st_s = [], []
    for l in range(w_in.shape[0]):
        wts = _prepare_weights(w_in[l], mla_w_uk[l], mla_w_uv[l], w_branch[l], w_out[l],
                               w_gate_up[l], w_down[l])
        params = (wts, mla_kv_norm[l], band_rel_bias[l], ln1_g[l], ln1_b[l], ln2_g[l], ln2_b[l])
        xp, new_p = _trunk_layer(xp, 0, None, *params)
        past = (cache_mla_latent[l], cache_sb_kv[l], cache_band_kv[l], cache_dsa_kv[l],
                cache_dsa_kidx[l])
        xs, new_s = _trunk_layer(xs, past_len, past, *params)
        st_p.append(new_p)
        st_s.append(new_s)
    stack = lambda st, i: jnp.stack([s[i] for s in st])
    return (xp, xs) + tuple(stack(st_p, i) for i in range(5)) + tuple(stack(st_s, i) for i in range(5))
```

```python
import functools
import math

import numpy as np
import jax
import jax.numpy as jnp
from jax import lax
from jax.experimental import pallas as pl
from jax.experimental.pallas import tpu as pltpu

D_MODEL = 1024
CHUNK = 64
CHUNK_SHIFT = 6
N_BRANCH = 4
N_HEADS = 4
HEAD_DIM = 64
BRANCH_WIDTH = N_HEADS * HEAD_DIM
MLA_D_C = 128
MLA_D_NOPE = 64
MLA_D_ROPE = 32
MLA_THETA = 10000.0
ROPE_THETA = 500000.0
BAND_LEFT_CHUNKS = 8
BAND_WINDOW = BAND_LEFT_CHUNKS * CHUNK
REL_CLIP = 128
IDX_HEADS = 8
IDX_DIM = 64
DSA_TOPK = 256
D_FF = ((8 * D_MODEL // 3 + 255) // 256) * 256
DEPTH = 2
ALPHA = (2 * DEPTH) ** 0.25
NEG_INF = -1e30
LOG2_E = math.log2(math.e)
SB_RUN_FLOOR = -150.0
LN_EPS = 1e-5
IN_SIZES = (N_HEADS * (MLA_D_NOPE + MLA_D_ROPE), MLA_D_C, MLA_D_ROPE,
            3 * BRANCH_WIDTH, 3 * BRANCH_WIDTH, 3 * BRANCH_WIDTH,
            IDX_HEADS * IDX_DIM, IDX_DIM, IDX_HEADS, N_BRANCH * D_MODEL)
REST_WIDTH = sum(IN_SIZES[:-1])
GATE_WIDTH = IN_SIZES[-1]

LANES = 128
KEY_ALIGN = 512
INT_MIN = -2 ** 31
VMEM_LIMIT = 56 * 1024 * 1024

F32 = jnp.float32
BF16 = jnp.bfloat16


def _cparams(n_axes):
    return pltpu.CompilerParams(dimension_semantics=("arbitrary",) * n_axes,
                                vmem_limit_bytes=VMEM_LIMIT)


def _pick(n, candidates):
    for c in candidates:
        if n % c == 0:
            return c
    return n


def _ones_upper_half(rows):
    lane = lax.broadcasted_iota(jnp.int32, (rows, N_HEADS * LANES), 1)
    return jnp.where((lane & (LANES - 1)) >= HEAD_DIM, 1.0, 0.0)


def _dot_nt(a, b):
    return lax.dot_general(a, b, (((1,), (1,)), ((), ())), preferred_element_type=F32)


def _mm_kernel(a_ref, b_ref, o_ref):
    o_ref[...] = jnp.dot(a_ref[...].astype(BF16), b_ref[...], preferred_element_type=F32)


def _matmul(a, b):
    m, k = a.shape
    n = b.shape[1]
    tm = _pick(m, (1024, 512, 256, 128, 64, 32, 16, 8))
    tn = _pick(n, (512, 384, 256, 128))
    return pl.pallas_call(
        _mm_kernel,
        grid=(m // tm, n // tn),
        in_specs=[pl.BlockSpec((tm, k), lambda i, j: (i, 0)),
                  pl.BlockSpec((k, tn), lambda i, j: (0, j))],
        out_specs=pl.BlockSpec((tm, tn), lambda i, j: (i, j)),
        out_shape=jax.ShapeDtypeStruct((m, n), F32),
        compiler_params=_cparams(2),
        name="matmul",
    )(a, b)


SEG_AQ = 0
SEG_CKV = 4
SEG_KR = 5
SEG_SB = 6
SEG_BD = 12
SEG_DS = 18
SEG_IXQ = 24
SEG_IXK = 28
N_SEG = 30
IXW_LANE = 96
PROJ_WIDTH = N_SEG * LANES
MLA_ROPE_LANE = 64
LAT_WIDTH = MLA_D_C + MLA_D_ROPE


def _proj_kernel(x_ref, w_ref, wukv_ref, g_ref, ca_ref, sma_ref, spa_ref, cp_ref, smp_ref, spp_ref,
                 qa_ref, lat_ref, ka_ref, va_ref,
                 qb_ref, kvb_ref, sbkv_ref, qc_ref, kvc_ref, bdkv_ref,
                 qd_ref, kd_ref, vtd_ref, dskv_ref, qix_ref, kidx_ref, kx_ref, wix_ref):
    proj = jnp.dot(x_ref[...].astype(BF16), w_ref[...], preferred_element_type=F32)
    seg = lambda s, n=1: proj[:, s * LANES:(s + n) * LANES]

    def rope(t, c_ref, sm_ref, sp_ref, half):
        return (t * c_ref[...] + pltpu.roll(t, LANES - half, 1) * sm_ref[...]
                + pltpu.roll(t, half, 1) * sp_ref[...])

    rope_a = functools.partial(rope, c_ref=ca_ref, sm_ref=sma_ref, sp_ref=spa_ref, half=MLA_D_ROPE // 2)
    rope_p = functools.partial(rope, c_ref=cp_ref, sm_ref=smp_ref, sp_ref=spp_ref, half=HEAD_DIM // 8)
    head_scale = HEAD_DIM ** -0.5 * LOG2_E

    mla_scale = (MLA_D_NOPE + MLA_D_ROPE) ** -0.5 * LOG2_E
    for h in range(N_HEADS):
        qa_ref[:, h * LANES:(h + 1) * LANES] = (rope_a(seg(SEG_AQ + h)) * mla_scale).astype(BF16)
    ckv = seg(SEG_CKV)
    ckv = ckv * lax.rsqrt(jnp.mean(ckv * ckv, axis=-1, keepdims=True) + LN_EPS) * g_ref[...]
    kr = rope_a(seg(SEG_KR))
    lat_ref[:, :MLA_D_C] = ckv
    lat_ref[:, MLA_D_C:] = kr[:, MLA_ROPE_LANE:MLA_ROPE_LANE + MLA_D_ROPE]
    kv_a = jnp.dot(ckv.astype(BF16), wukv_ref[...], preferred_element_type=F32)
    for h in range(N_HEADS):
        ka_ref[:, h * LANES:(h + 1) * LANES] = (kv_a[:, h * LANES:(h + 1) * LANES] + kr).astype(BF16)
    va_ref[...] = (kv_a[:, N_HEADS * LANES:] + _ones_upper_half(kv_a.shape[0])).astype(BF16)

    for s0, q_ref, kv_ref, new_ref in ((SEG_SB, qb_ref, kvb_ref, sbkv_ref),
                                       (SEG_BD, qc_ref, kvc_ref, bdkv_ref)):
        q_ref[...] = (seg(s0, 2) * head_scale).astype(BF16)
        kv = seg(s0 + 2, 4)
        new_ref[...] = kv
        kv_ref[...] = kv.astype(BF16)

    for p in range(2):
        qd_ref[:, p * LANES:(p + 1) * LANES] = (rope_p(seg(SEG_DS + p)) * head_scale).astype(BF16)
        k_rot = rope_p(seg(SEG_DS + 2 + p))
        dskv_ref[:, p * LANES:(p + 1) * LANES] = k_rot
        kd_ref[:, p * LANES:(p + 1) * LANES] = k_rot.astype(BF16)
    v_d = seg(SEG_DS + 4, 2)
    dskv_ref[:, BRANCH_WIDTH:] = v_d
    vtd_ref[...] = v_d.T.astype(BF16)
    for p in range(IDX_HEADS // 2):
        qix_ref[:, p * LANES:(p + 1) * LANES] = (rope_p(seg(SEG_IXQ + p)) * IDX_DIM ** -0.5).astype(BF16)
    ixk = rope_p(seg(SEG_IXK))
    kidx_ref[...] = ixk[:, :IDX_DIM]
    kx_ref[...] = ixk[:, :IDX_DIM].astype(BF16)
    wix_ref[...] = ixk.T[IXW_LANE:IXW_LANE + IDX_HEADS, :]


def _rope_tables(pos, theta, width, starts):
    half = width // 2
    inv = jnp.exp(jnp.arange(half, dtype=F32) * (-2.0 * math.log(theta) / width))
    ang = pos.astype(F32)[:, None] * inv[None, :]
    cos, sin = jnp.cos(ang), jnp.sin(ang)
    n = pos.shape[0]
    fill = lambda v, w: jnp.full((n, w), v, F32)
    c, sm, sp, lane = [], [], [], 0
    for s in sorted(starts):
        c += [fill(1.0, s - lane), cos, cos]
        sm += [fill(0.0, s - lane), -sin, fill(0.0, half)]
        sp += [fill(0.0, s - lane + half), sin]
        lane = s + width
    cat = lambda parts, v: jnp.concatenate(parts + [fill(v, LANES - lane)], axis=1)
    return cat(c, 1.0), cat(sm, 0.0), cat(sp, 0.0)


def _project(x, pos, w_rest, w_ukv, kv_norm):
    n = x.shape[0]
    ts = _pick(n, (512, 256, 128, 64))
    tables = (_rope_tables(pos, MLA_THETA, MLA_D_ROPE, (MLA_ROPE_LANE,))
              + _rope_tables(pos, ROPE_THETA, HEAD_DIM // 4, (0, HEAD_DIM)))
    row = lambda i: (i, 0)
    fixed = lambda i: (0, 0)
    rows = lambda w: pl.BlockSpec((ts, w), row)
    out = lambda w, dt: jax.ShapeDtypeStruct((n, w), dt)
    specs = [
        (N_HEADS * LANES, BF16), (LAT_WIDTH, F32), (N_HEADS * LANES, BF16), (N_HEADS * LANES, BF16),
        (BRANCH_WIDTH, BF16), (2 * BRANCH_WIDTH, BF16), (2 * BRANCH_WIDTH, F32),
        (BRANCH_WIDTH, BF16), (2 * BRANCH_WIDTH, BF16), (2 * BRANCH_WIDTH, F32),
        (BRANCH_WIDTH, BF16), (BRANCH_WIDTH, BF16), None, (2 * BRANCH_WIDTH, F32),
        (IDX_HEADS * IDX_DIM, BF16), (IDX_DIM, F32), (IDX_DIM, BF16), None]
    out_specs, out_shape = [], []
    for k, spec in enumerate(specs):
        if spec is None:
            rows_t, dt = ((BRANCH_WIDTH, BF16), (IDX_HEADS, F32))[k > 12]
            out_specs.append(pl.BlockSpec((rows_t, ts), lambda i: (0, i)))
            out_shape.append(jax.ShapeDtypeStruct((rows_t, n), dt))
        else:
            out_specs.append(rows(spec[0]))
            out_shape.append(out(*spec))
    return pl.pallas_call(
        _proj_kernel,
        grid=(n // ts,),
        in_specs=[rows(D_MODEL),
                  pl.BlockSpec((D_MODEL, PROJ_WIDTH), fixed, pipeline_mode=pl.Buffered(1)),
                  pl.BlockSpec((MLA_D_C, 2 * N_HEADS * LANES), fixed),
                  pl.BlockSpec((1, MLA_D_C), fixed)] + [rows(LANES)] * 6,
        out_specs=out_specs,
        out_shape=out_shape,
        compiler_params=_cparams(1),
        name="project_prepare",
    )(x, w_rest, w_ukv, kv_norm, *tables)


def _mla_kernel(q_ref, k_ref, v_ref, o_ref, *, qb, kb, q_off, skp):
    q0 = pl.program_id(1) * qb
    hi = jnp.minimum((((q_off + q0 + qb - 1) >> CHUNK_SHIFT) + 1) * CHUNK, skp)
    nkb = (hi + kb - 1) // kb
    qpos = q_off + q0 + lax.broadcasted_iota(jnp.int32, (qb, 1), 0)
    cend = ((qpos >> CHUNK_SHIFT) + 1) * CHUNK
    col = lax.broadcasted_iota(jnp.int32, (qb, kb), 1)
    lane = lax.broadcasted_iota(jnp.int32, (qb, LANES), 1)
    n_full = jnp.minimum(((((q_off + q0) >> CHUNK_SHIFT) + 1) * CHUNK) // kb, nkb)

    def body(i, carry, masked):
        k0 = pl.multiple_of(i * kb, kb)
        s = [_dot_nt(q_ref[0, :, h * LANES:(h + 1) * LANES],
                     k_ref[0, pl.ds(k0, kb), h * LANES:(h + 1) * LANES]) for h in range(N_HEADS)]
        if masked:
            vis = col + k0 < cend
            s = [jnp.where(vis, sh, NEG_INF) for sh in s]
        p, new = [], []
        for h in range(N_HEADS):
            m, acc = carry[2 * h:2 * h + 2]
            m_new = jnp.maximum(m, jnp.max(s[h], axis=1, keepdims=True))
            p.append(jnp.exp2((s[h] - m_new).astype(BF16)))
            new += [m_new, jnp.exp2(m - m_new) * acc]
        for h in range(N_HEADS):
            vblk = v_ref[0, pl.ds(k0, kb), h * LANES:(h + 1) * LANES]
            new[2 * h + 1] = new[2 * h + 1] + jnp.dot(p[h], vblk, preferred_element_type=F32)
        return tuple(new)

    init = (jnp.full((qb, 1), NEG_INF, F32), jnp.zeros((qb, LANES), F32)) * N_HEADS
    carry = lax.fori_loop(0, n_full, functools.partial(body, masked=False), init)
    carry = lax.fori_loop(n_full, nkb, functools.partial(body, masked=True), carry)
    outs = [carry[2 * h + 1] / pltpu.roll(carry[2 * h + 1], HEAD_DIM, 1) for h in range(N_HEADS)]
    for pair in range(N_HEADS // 2):
        o_ref[0, :, pair * LANES:(pair + 1) * LANES] = jnp.where(
            lane < HEAD_DIM, outs[2 * pair], pltpu.roll(outs[2 * pair + 1], HEAD_DIM, 1))


def _mla_attention(q, k, v, q_off):
    b, sq, _ = q.shape
    skp = k.shape[1]
    qb = _pick(sq, (256, 128, 64))
    kb = _pick(skp, (1024, KEY_ALIGN))
    kern = functools.partial(_mla_kernel, qb=qb, kb=kb, q_off=q_off, skp=skp)
    return pl.pallas_call(
        kern,
        grid=(b, sq // qb),
        in_specs=[pl.BlockSpec((1, qb, N_HEADS * LANES), lambda bi, qi: (bi, qi, 0)),
                  pl.BlockSpec((1, skp, N_HEADS * LANES), lambda bi, qi: (bi, 0, 0),
                               pipeline_mode=pl.Buffered(1)),
                  pl.BlockSpec((1, skp, N_HEADS * LANES), lambda bi, qi: (bi, 0, 0),
                               pipeline_mode=pl.Buffered(1))],
        out_specs=pl.BlockSpec((1, qb, BRANCH_WIDTH), lambda bi, qi: (bi, qi, 0)),
        out_shape=jax.ShapeDtypeStruct((b, sq, BRANCH_WIDTH), F32),
        compiler_params=_cparams(2),
        name="mla_attention",
    )(q, k, v)


def _head_mask(x_pair, h):
    lane = lax.broadcasted_iota(jnp.int32, x_pair.shape, 1)
    keep = (lane < HEAD_DIM) if h % 2 == 0 else (lane >= HEAD_DIM)
    return jnp.where(keep, x_pair, jnp.zeros_like(x_pair))


def _sb_kernel(q_ref, kv_ref, t_ref, o_ref, *, qb, kb, sub, q_off):
    q0 = pl.program_id(1) * qb
    hi = q_off + q0 + qb - 1
    nkb = (hi + kb - 1) // kb
    qpos = q_off + q0 + lax.broadcasted_iota(jnp.int32, (qb, 1), 0)
    col = lax.broadcasted_iota(jnp.int32, (qb, sub), 1)
    lane = lax.broadcasted_iota(jnp.int32, (qb, LANES), 1)
    tri = t_ref[...]
    n_full = jnp.minimum((q_off + q0) // kb, nkb)
    qh = [_head_mask(q_ref[0, :, (h // 2) * LANES:(h // 2 + 1) * LANES], h) for h in range(N_HEADS)]

    def body(i, carry, masked):
        k0 = pl.multiple_of(i * kb, kb)
        carry = list(carry)
        units = [(j, h) for j in reversed(range(kb // sub)) for h in range(N_HEADS)]
        strict = {j: col + (k0 + j * sub) < qpos for j in range(kb // sub)} if masked else None
        z = {}
        for j, h in units:
            pair = h // 2
            kblk = kv_ref[0, pl.ds(k0 + j * sub, sub), pair * LANES:(pair + 1) * LANES]
            z[j, h] = _dot_nt(qh[h], kblk)
        log_1m, suffix = {}, {}
        for u in units:
            nz = -z[u]
            t = jnp.minimum(nz, 0.0) - jnp.log2(1.0 + jnp.exp2(jnp.minimum(z[u], nz)))
            if masked:
                t = jnp.where(strict[u[0]], t, 0.0)
            log_1m[u] = t
        for u in units:
            hi_part = log_1m[u].astype(BF16)
            lo_part = (log_1m[u] - hi_part.astype(F32)).astype(BF16)
            suffix[u] = (jnp.dot(hi_part, tri, preferred_element_type=F32)
                         + jnp.dot(lo_part, tri, preferred_element_type=F32))
        for j, h in units:
            u = (j, h)
            run, acc = carry[2 * h:2 * h + 2]
            expo = z[u] + log_1m[u] + suffix[u] + run
            if masked:
                expo = jnp.where(strict[j], expo, NEG_INF)
            a = jnp.exp2(expo).astype(BF16)
            pair = h // 2
            vblk = kv_ref[0, pl.ds(k0 + j * sub, sub),
                          BRANCH_WIDTH + pair * LANES:BRANCH_WIDTH + (pair + 1) * LANES]
            carry[2 * h + 1] = acc + jnp.dot(a, vblk, preferred_element_type=F32)
            carry[2 * h] = run + jnp.sum(log_1m[u], axis=1, keepdims=True)
        return tuple(carry)

    def live(state):
        i, carry = state[0], state[1:]
        top = carry[0]
        for h in range(1, N_HEADS):
            top = jnp.maximum(top, carry[2 * h])
        return (i >= 0) & (jnp.max(top) >= SB_RUN_FLOOR)

    def step(masked):
        return lambda state: (state[0] - 1,) + body(state[0], state[1:], masked)

    init = (jnp.zeros((qb, 1), F32), jnp.zeros((qb, LANES), F32)) * N_HEADS
    carry = lax.fori_loop(0, nkb - n_full, lambda i, c: body(nkb - 1 - i, c, True), init)
    state = lax.while_loop(live, step(False), (n_full - 1,) + tuple(carry))
    carry = state[1:]
    outs = [carry[2 * h + 1] for h in range(N_HEADS)]
    for pair in range(N_HEADS // 2):
        o_ref[0, :, pair * LANES:(pair + 1) * LANES] = jnp.where(
            lane < HEAD_DIM, outs[2 * pair], outs[2 * pair + 1])


def _sb_attention(q, kv, q_off):
    b, sq, _ = q.shape
    skp = kv.shape[1]
    qb = _pick(sq, (256, 128, 64))
    sub = 256
    kb = KEY_ALIGN
    tri =jnp.asarray(np.tril(np.ones((sub, sub), np.float32), -1), BF16)
    kern = functools.partial(_sb_kernel, qb=qb, kb=kb, sub=sub, q_off=q_off)
    return pl.pallas_call(
        kern,
        grid=(b, sq // qb),
        in_specs=[pl.BlockSpec((1, qb, BRANCH_WIDTH), lambda bi, qi: (bi, qi, 0)),
                  pl.BlockSpec((1, skp, 2 * BRANCH_WIDTH), lambda bi, qi: (bi, 0, 0),
                               pipeline_mode=pl.Buffered(1)),
                  pl.BlockSpec((sub, sub), lambda bi, qi: (0, 0))],
        out_specs=pl.BlockSpec((1, qb, BRANCH_WIDTH), lambda bi, qi: (bi, qi, 0)),
        out_shape=jax.ShapeDtypeStruct((b, sq, BRANCH_WIDTH), F32),
        compiler_params=_cparams(2),
        name="stick_breaking_attention",
    )(q, kv, tri)


def _band_kernel(bias_ref, q_ref, kv_ref, o_ref, tile_ref, *, qb, win, q_off):
    first = (pl.program_id(0) == 0) & (pl.program_id(1) == 0)
    row = lax.broadcasted_iota(jnp.int32, (qb, win), 0)
    col = lax.broadcasted_iota(jnp.int32, (qb, win), 1)

    @pl.when(first)
    def _():
        width = win + qb
        c = lax.broadcasted_iota(jnp.int32, (8, width), 1)
        rel = jnp.where(c < win, jnp.clip(BAND_WINDOW - c, -REL_CLIP, REL_CLIP) + REL_CLIP, 2 * REL_CLIP)

        def fill(r, rows):
            hit = rel == r
            return tuple(jnp.where(hit, bias_ref[h, r], rows[h]) for h in range(N_HEADS))

        rows = lax.fori_loop(0, 2 * REL_CLIP + 1, fill, (jnp.zeros((8, width), F32),) * N_HEADS)
        qch = row >> CHUNK_SHIFT
        kch = col >> CHUNK_SHIFT
        in_band = (kch >= qch) & (kch <= qch + BAND_LEFT_CHUNKS)
        for h in range(N_HEADS):
            table = jnp.broadcast_to(rows[h][:1], (qb, width))
            skewed = pltpu.roll(table, 0, 1, stride=1, stride_axis=0)[:, :win]
            tile_ref[h] = jnp.where(in_band, skewed * LOG2_E, NEG_INF)

    q0 = pl.multiple_of(pl.program_id(1) * qb, qb)
    kpos = col + (q_off + q0 - BAND_WINDOW)
    lane = lax.broadcasted_iota(jnp.int32, (qb, LANES), 1)
    outs = []
    for h in range(N_HEADS):
        pair = h // 2
        qh = _head_mask(q_ref[0, :, pair * LANES:(pair + 1) * LANES], h)
        kwin = kv_ref[0, pl.ds(q0, win), pair * LANES:(pair + 1) * LANES]
        s = _dot_nt(qh, kwin) + tile_ref[h]
        s = jnp.where(kpos >= 0, s, NEG_INF)
        m = jnp.max(s, axis=1, keepdims=True)
        p = jnp.exp2(s - m)
        l = jnp.sum(p, axis=1, keepdims=True)
        vwin = kv_ref[0, pl.ds(q0, win),
                      BRANCH_WIDTH + pair * LANES:BRANCH_WIDTH + (pair + 1) * LANES]
        outs.append(jnp.dot(p.astype(BF16), vwin, preferred_element_type=F32) / l)
    for pair in range(N_HEADS // 2):
        o_ref[0, :, pair * LANES:(pair + 1) * LANES] = jnp.where(
            lane < HEAD_DIM, outs[2 * pair], outs[2 * pair + 1])


def _band_attention(q, kv, rel_bias, q_off):
    b, sq, _ = q.shape
    sk = kv.shape[1]
    qb = _pick(sq, (256, 128, 64))
    win = qb + BAND_WINDOW
    kern = functools.partial(_band_kernel, qb=qb, win=win, q_off=q_off)
    return pl.pallas_call(
        kern,
        grid=(b, sq // qb),
        in_specs=[pl.BlockSpec(memory_space=pltpu.SMEM),
                  pl.BlockSpec((1, qb, BRANCH_WIDTH), lambda bi, qi: (bi, qi, 0)),
                  pl.BlockSpec((1, sk, 2 * BRANCH_WIDTH), lambda bi, qi: (bi, 0, 0),
                               pipeline_mode=pl.Buffered(1))],
        out_specs=pl.BlockSpec((1, qb, BRANCH_WIDTH), lambda bi, qi: (bi, qi, 0)),
        out_shape=jax.ShapeDtypeStruct((b, sq, BRANCH_WIDTH), F32),
        scratch_shapes=[pltpu.VMEM((N_HEADS, qb, win), F32)],
        compiler_params=_cparams(2),
        name="band_attention",
    )(rel_bias, q, kv)


def _dsa_kernel(qi_ref, w_ref, kx_ref, q_ref, k_ref, vt_ref, o_ref, key_ref, key16_ref, cut_ref,
                sa_ref, sb_ref,
                *, qb, kb, q_off, skp, n_sel):
    q0 = pl.program_id(1) * qb
    hi = jnp.minimum((((q_off + q0 + qb - 1) >> CHUNK_SHIFT) + 1) * CHUNK, skp)
    nkb = (hi + kb - 1) // kb
    qpos = q_off + q0 + lax.broadcasted_iota(jnp.int32, (1, qb), 1)
    cend = ((qpos >> CHUNK_SHIFT) + 1) * CHUNK
    row = lax.broadcasted_iota(jnp.int32, (kb, qb), 0)
    float_key = lambda bits: jnp.where(bits < 0, INT_MIN - bits, bits)
    neg_key = float_key(lax.bitcast_convert_type(jnp.full((1, 1), NEG_INF, F32), jnp.int32))
    n_full = jnp.minimum(((((q_off + q0) >> CHUNK_SHIFT) + 1) * CHUNK) // kb, nkb)

    def score_block(i, c, masked):
        k0 = pl.multiple_of(i * kb, kb)
        kx = kx_ref[0, pl.ds(k0, kb), :]
        r = [_dot_nt(kx, qi_ref[0, :, j * IDX_DIM:(j + 1) * IDX_DIM]) for j in range(IDX_HEADS)]
        score = jnp.zeros((kb, qb), F32)
        for j in range(IDX_HEADS):
            wj = w_ref[0, j:j + 1, :] * (IDX_HEADS ** -0.5)
            score = score + wj * jnp.maximum(r[j], 0.0)
        if masked:
            score = jnp.where(row + k0 < cend, score, NEG_INF)
        keys = float_key(lax.bitcast_convert_type(score, jnp.int32))
        key_ref[pl.ds(k0, kb), :] = keys
        key16_ref[pl.ds(k0, kb), :] = (keys >> 16).astype(jnp.int16)
        return c

    lax.fori_loop(0, n_full, functools.partial(score_block, masked=False), 0)
    lax.fori_loop(n_full, nkb, functools.partial(score_block, masked=True), 0)

    slab = 64 if kb % 64 == 0 else kb

    def count_ge(cand):
        def blk(i, cnt):
            k0 = pl.multiple_of(i * kb, kb)
            for s in range(kb // slab):
                keys = key_ref[pl.ds(k0 + s * slab, slab), :]
                cnt = cnt + jnp.where(keys >= cand, 1.0, 0.0)
            return cnt
        part = lax.fori_loop(0, nkb, blk, jnp.zeros((slab, qb), F32))
        return jnp.sum(part, axis=0, keepdims=True)

    slab16 = 128 if kb % 128 == 0 else kb

    def count16_ge(cand16):
        def blk(i, cnt):
            k0 = pl.multiple_of(i * kb, kb)
            for s in range(kb // slab16):
                k16 = key16_ref[pl.ds(k0 + s * slab16, slab16), :]
                cnt = cnt + jnp.where(k16 >= cand16, jnp.int16(1), jnp.int16(0))
            return cnt
        part = lax.fori_loop(0, nkb, blk, jnp.zeros((slab16, qb), jnp.int16))
        return jnp.sum(part.astype(F32), axis=0, keepdims=True)

    def undecided(t_end):
        def cond(state):
            t, _, n_ge = state
            return (t < t_end) & (jnp.max(jnp.abs(n_ge - n_sel)) > 0.0)
        return cond

    def bisect(to_16bit):
        def step(state):
            t, thr, n_ge = state
            cand = thr + jnp.left_shift(jnp.int32(1), 31 - t)
            cnt = count16_ge(to_16bit(cand))
            ok = cnt >= n_sel
            return t + 1, jnp.where(ok, cand, thr), jnp.where(ok, cnt, n_ge)
        return step

    stored = (nkb * kb).astype(F32)
    state = lax.while_loop(
        undecided(16), bisect(lambda cand: (cand >> 16).astype(jnp.int16)),
        (jnp.int32(0), jnp.full((1, qb), INT_MIN, jnp.int32), jnp.full((1, qb), stored, F32)))

    @pl.when(undecided(32)(state))
    def _():
        thr_hi = state[1] >> 16

        def low_halves(i, c):
            k0 = pl.multiple_of(i * kb, kb)
            keys = key_ref[pl.ds(k0, kb), :]
            hi16 = keys >> 16
            low = jnp.where(hi16 == thr_hi, (keys & 0xFFFF) - 0x8000,
                            jnp.where(hi16 > thr_hi, 0x7FFF, -0x8000))
            key16_ref[pl.ds(k0, kb), :] = low.astype(jnp.int16)
            return c

        lax.fori_loop(0, nkb, low_halves, 0)

    _, thr, n_ge = lax.while_loop(
        undecided(32), bisect(lambda cand: ((cand & 0xFFFF) - 0x8000).astype(jnp.int16)), state)
    excess = jnp.where((n_ge > n_sel) & (thr > neg_key), 1.0, 0.0)
    cut_ref[...] = jnp.full((1, qb), skp, jnp.int32)

    @pl.when(jnp.max(excess) > 0.0)
    def _():
        need = n_sel - count_ge(thr + 1)

        def count_tied_below(limit):
            def blk(i, cnt):
                k0 = pl.multiple_of(i * kb, kb)
                tied = (key_ref[pl.ds(k0, kb), :] == thr) & (row + k0 < limit)
                return cnt + jnp.sum(jnp.where(tied, 1.0, 0.0), axis=0, keepdims=True)
            return lax.fori_loop(0, nkb, blk, jnp.zeros((1, qb), F32))

        n_bits = max(1, (skp - 1).bit_length())

        def bisect_cut(t, cut):
            cand = cut + jnp.left_shift(jnp.int32(1), n_bits - 1 - t)
            return jnp.where(count_tied_below(cand) < need, cand, cut)

        cut = lax.fori_loop(0, n_bits, bisect_cut, jnp.zeros((1, qb), jnp.int32))
        cut_ref[...] = jnp.where(excess > 0.0, cut + 1, skp)

    cut = cut_ref[...]

    qh = [_head_mask(q_ref[0, :, (h // 2) * LANES:(h // 2 + 1) * LANES], h) for h in range(N_HEADS)]

    kb3 = kb // 2
    row3 = lax.broadcasted_iota(jnp.int32, (kb3, qb), 0)

    def scores(i):
        k0 = pl.multiple_of(i * kb3, kb3)
        return tuple(_dot_nt(k_ref[0, pl.ds(k0, kb3), (h // 2) * LANES:(h // 2 + 1) * LANES], qh[h])
                     for h in range(N_HEADS))

    def attend(i, s, carry):
        k0 = pl.multiple_of(i * kb3, kb3)
        keys = key_ref[pl.ds(k0, kb3), :]
        kpos = row3 + k0
        sel = (keys >= thr) & ((keys != thr) | (kpos < cut)) & (kpos < cend)
        bias = jnp.where(sel, 0.0, NEG_INF)
        p, new = [], []
        for h in range(N_HEADS):
            m, acc = carry[2 * h:2 * h + 2]
            sh = s[h] + bias
            m_new = jnp.maximum(m, jnp.max(sh, axis=0, keepdims=True))
            p.append(jnp.exp2((sh - m_new).astype(BF16)))
            new += [m_new, jnp.exp2(m - m_new) * acc]
        for h in range(N_HEADS):
            vblk = jnp.concatenate([vt_ref[0, h * HEAD_DIM:(h + 1) * HEAD_DIM, pl.ds(k0, kb3)],
                                    jnp.ones((LANES - HEAD_DIM, kb3), BF16)], axis=0)
            new[2 * h + 1] = new[2 * h + 1] + jnp.dot(vblk, p[h], preferred_element_type=F32)
        return tuple(new)

    def put_scores(slot_ref, i):
        for h, sh in enumerate(scores(i)):
            slot_ref[h] = sh

    def attend_from(slot_ref, i, carry):
        return attend(i, tuple(slot_ref[h] for h in range(N_HEADS)), carry)

    def body(j, carry):
        put_scores(sb_ref, 2 * j + 1)
        carry = attend_from(sa_ref, 2 * j, carry)
        put_scores(sa_ref, 2 * j + 2)
        return attend_from(sb_ref, 2 * j + 1, carry)

    init = (jnp.full((1, qb), NEG_INF, F32), jnp.zeros((LANES, qb), F32)) * N_HEADS
    n_pairs = nkb * (kb // (2 * kb3))
    put_scores(sa_ref, 0)
    carry = lax.fori_loop(0, n_pairs - 1, body, init)
    last = 2 * (n_pairs - 1)
    put_scores(sb_ref, last + 1)
    carry = attend_from(sa_ref, last, carry)
    carry = attend_from(sb_ref, last + 1, carry)
    for h in range(N_HEADS):
        acc = carry[2 * h + 1]
        o_ref[0, h * HEAD_DIM:(h + 1) * HEAD_DIM, :] = acc[:HEAD_DIM, :] / acc[HEAD_DIM:HEAD_DIM + 1, :]


def _dsa_attention(qi, w, kx, q, k, vt, q_off, n_sel):
    b, sq, _ = q.shape
    skp = k.shape[1]
    qb = _pick(sq, (256, 128, 64))
    kb = _pick(skp, (1024, KEY_ALIGN))
    kern = functools.partial(_dsa_kernel, qb=qb, kb=kb, q_off=q_off, skp=skp, n_sel=n_sel)
    return pl.pallas_call(
        kern,
        grid=(b, sq // qb),
        in_specs=[pl.BlockSpec((1, qb, IDX_HEADS * IDX_DIM), lambda bi, qi_: (bi, qi_, 0)),
                  pl.BlockSpec((1, IDX_HEADS, qb), lambda bi, qi_: (bi, 0, qi_)),
                  pl.BlockSpec((1, skp, IDX_DIM), lambda bi, qi_: (bi, 0, 0),
                               pipeline_mode=pl.Buffered(1)),
                  pl.BlockSpec((1, qb, BRANCH_WIDTH), lambda bi, qi_: (bi, qi_, 0)),
                  pl.BlockSpec((1, skp, BRANCH_WIDTH), lambda bi, qi_: (bi, 0, 0),
                               pipeline_mode=pl.Buffered(1)),
                  pl.BlockSpec((1, BRANCH_WIDTH, skp), lambda bi, qi_: (bi, 0, 0),
                               pipeline_mode=pl.Buffered(1))],
        out_specs=pl.BlockSpec((1, BRANCH_WIDTH, qb), lambda bi, qi_: (bi, 0, qi_)),
        out_shape=jax.ShapeDtypeStruct((b, BRANCH_WIDTH, sq), F32),
        scratch_shapes=[pltpu.VMEM((skp, qb), jnp.int32), pltpu.VMEM((skp, qb), jnp.int16),
                        pltpu.VMEM((1, qb), jnp.int32),
                        pltpu.VMEM((N_HEADS, kb // 2, qb), F32), pltpu.VMEM((N_HEADS, kb // 2, qb), F32)],
        compiler_params=_cparams(2),
        name="dsa_attention",
    )(qi, w, kx, q, k, vt)


def _layer_norm(z, g, b):
    mu = jnp.mean(z, axis=-1, keepdims=True)
    zc = z - mu
    var = jnp.mean(zc * zc, axis=-1, keepdims=True)
    return zc * lax.rsqrt(var + LN_EPS) * g + b


def _merge_kernel(x_ref, wgate_ref, oa_ref, ob_ref, oc_ref, od_ref, wb_ref, wo_ref, g_ref, b_ref,
                  y_ref):
    xb = x_ref[...].astype(BF16)
    logits = [jnp.dot(xb, wgate_ref[:, n * D_MODEL:(n + 1) * D_MODEL], preferred_element_type=F32)
              for n in range(N_BRANCH)]
    branch = [jnp.dot(o_ref[...].astype(BF16), wb_ref[n], preferred_element_type=F32)
              for n, o_ref in enumerate((oa_ref, ob_ref, oc_ref, od_ref))]
    merged = jax.nn.sigmoid(logits[0]) * branch[0]
    for n in range(1, N_BRANCH):
        merged = merged + jax.nn.sigmoid(logits[n]) * branch[n]
    y = jnp.dot(merged.astype(BF16), wo_ref[...], preferred_element_type=F32)
    y_ref[...] = _layer_norm(ALPHA * x_ref[...] + y, g_ref[...], b_ref[...])


def _merge_out_ln(x, w_gate, o_a, o_b, o_c, o_d, w_branch, w_out, g, b):
    n = x.shape[0]
    ts = _pick(n, (512, 256, 128, 64))
    row = lambda i: (i, 0)
    obs = pl.BlockSpec((ts, BRANCH_WIDTH), row)
    return pl.pallas_call(
        _merge_kernel,
        grid=(n // ts,),
        in_specs=[pl.BlockSpec((ts, D_MODEL), row),
                  pl.BlockSpec((D_MODEL, GATE_WIDTH), lambda i: (0, 0), pipeline_mode=pl.Buffered(1)),
                  obs, obs, obs, obs,
                  pl.BlockSpec((N_BRANCH, BRANCH_WIDTH, D_MODEL), lambda i: (0, 0, 0)),
                  pl.BlockSpec((D_MODEL, D_MODEL), lambda i: (0, 0)),
                  pl.BlockSpec((1, D_MODEL), lambda i: (0, 0)),
                  pl.BlockSpec((1, D_MODEL), lambda i: (0, 0))],
        out_specs=pl.BlockSpec((ts, D_MODEL), row),
        out_shape=jax.ShapeDtypeStruct((n, D_MODEL), F32),
        compiler_params=_cparams(1),
        name="merge_out_ln",
    )(x, w_gate, o_a, o_b, o_c, o_d, w_branch, w_out, g, b)


def _ffn_kernel(x_ref, wg_ref, wu_ref, wd_ref, g_ref, b_ref, y_ref, acc_ref):
    f = pl.program_id(1)
    xb = x_ref[...].astype(BF16)
    gate = jnp.dot(xb, wg_ref[...], preferred_element_type=F32)
    up = jnp.dot(xb, wu_ref[...], preferred_element_type=F32)
    hidden = (gate * jax.nn.sigmoid(gate) * up).astype(BF16)
    part = jnp.dot(hidden, wd_ref[...], preferred_element_type=F32)

    @pl.when(f == 0)
    def _():
        acc_ref[...] = part

    @pl.when(f > 0)
    def _():
        acc_ref[...] += part

    @pl.when(f == pl.num_programs(1) - 1)
    def _():
        y_ref[...] = _layer_norm(ALPHA * x_ref[...] + acc_ref[...], g_ref[...], b_ref[...])


def _ffn_ln(x, w_gate, w_up, w_down, g, b):
    n = x.shape[0]
    ts = _pick(n, (1024, 512, 256, 128, 64))
    tf = D_FF // 2
    return pl.pallas_call(
        _ffn_kernel,
        grid=(n // ts, D_FF // tf),
        in_specs=[pl.BlockSpec((ts, D_MODEL), lambda i, f: (i, 0)),
                  pl.BlockSpec((D_MODEL, tf), lambda i, f: (0, f)),
                  pl.BlockSpec((D_MODEL, tf), lambda i, f: (0, f)),
                  pl.BlockSpec((tf, D_MODEL), lambda i, f: (f, 0)),
                  pl.BlockSpec((1, D_MODEL), lambda i, f: (0, 0)),
                  pl.BlockSpec((1, D_MODEL), lambda i, f: (0, 0))],
        out_specs=pl.BlockSpec((ts, D_MODEL), lambda i, f: (i, 0)),
        out_shape=jax.ShapeDtypeStruct((n, D_MODEL), F32),
        scratch_shapes=[pltpu.VMEM((ts, D_MODEL), F32)],
        compiler_params=_cparams(2),
        name="ffn_ln",
    )(x, w_gate, w_up, w_down, g, b)


def _pad_keys(a, axis=1):
    n = a.shape[axis]
    pad = (-n) % KEY_ALIGN
    if pad == 0:
        return a
    widths = [(0, 0)] * a.ndim
    widths[axis] = (0, pad)
    return jnp.pad(a, widths)


def _prepare_weights(w_in, mla_w_uk, mla_w_uv, w_branch, w_out, w_gate_up, w_down):
    w_rest = jnp.concatenate([w_in, jnp.zeros((D_MODEL, 1), w_in.dtype)], axis=1)
    w_rest = jnp.take(w_rest, _projection_columns(), axis=1).astype(BF16)
    w_uk = jnp.pad(mla_w_uk, ((0, 0), (0, 0), (0, LANES - MLA_D_NOPE))).reshape(MLA_D_C, -1)
    w_uv = jnp.pad(mla_w_uv, ((0, 0), (0, 0), (0, LANES - HEAD_DIM))).reshape(MLA_D_C, -1)
    w_ukv = jnp.concatenate([w_uk, w_uv], axis=1).astype(BF16)
    return dict(w_rest=w_rest, w_gates=w_in[:, REST_WIDTH:].astype(BF16), w_ukv=w_ukv,
                w_branch=w_branch.astype(BF16),
                w_out=w_out.astype(BF16), w_gate=w_gate_up[:, :D_FF].astype(BF16),
                w_up=w_gate_up[:, D_FF:].astype(BF16), w_down=w_down.astype(BF16))


def _projection_columns():
    zero_col = REST_WIDTH + GATE_WIDTH
    src = np.full(PROJ_WIDTH, zero_col, np.int32)
    o_aq, o_ckv, o_kr, o_sb, o_bd, o_ds, o_ixq, o_ixk, o_ixw = np.concatenate(
        [[0], np.cumsum(IN_SIZES[:-1])])[:9]
    hd = MLA_D_NOPE + MLA_D_ROPE

    def put(lane0, src0, n):
        src[lane0:lane0 + n] = src0 + np.arange(n)

    for h in range(N_HEADS):
        put((SEG_AQ + h) * LANES, o_aq + h * hd, hd)
    put(SEG_CKV * LANES, o_ckv, MLA_D_C)
    put(SEG_KR * LANES + MLA_ROPE_LANE, o_kr, MLA_D_ROPE)
    put(SEG_SB * LANES, o_sb, 3 * BRANCH_WIDTH)
    put(SEG_BD * LANES, o_bd, 3 * BRANCH_WIDTH)
    put(SEG_DS * LANES, o_ds, 3 * BRANCH_WIDTH)
    put(SEG_IXQ * LANES, o_ixq, IDX_HEADS * IDX_DIM)
    put(SEG_IXK * LANES, o_ixk, IDX_DIM)
    put(SEG_IXK * LANES + IXW_LANE, o_ixw, IDX_HEADS)
    return src


def _trunk_layer(x, q_off, past, wts, mla_kv_norm, band_rel_bias, ln1_g, ln1_b, ln2_g, ln2_b):
    bsz, s_len, _ = x.shape
    n = bsz * s_len
    pos = jnp.tile(q_off + jnp.arange(s_len, dtype=jnp.int32), bsz)
    x2 = x.reshape(n, D_MODEL)
    (q_a, lat_new, k_a, v_a, q_b, kv_b, sb_kv_new, q_c, kv_c, bd_kv_new,
     q_d, k_d, vt_d, ds_kv_new, q_ix, kidx_new, kx_d, w_ix) = _project(
        x2, pos, wts['w_rest'], wts['w_ukv'], mla_kv_norm[None])
    seq = lambda a: a.reshape(bsz, s_len, a.shape[-1])
    seq_t = lambda a: jnp.transpose(a.reshape(a.shape[0], bsz, s_len), (1, 0, 2))
    q_a, k_a, v_a, q_b, kv_b, q_c, kv_c, q_d, k_d, q_ix, kx_d = map(
        seq, (q_a, k_a, v_a, q_b, kv_b, q_c, kv_c, q_d, k_d, q_ix, kx_d))
    vt_d, w_ix = seq_t(vt_d), seq_t(w_ix)
    kv_state = lambda a: a.reshape(bsz, s_len, 2, N_HEADS, HEAD_DIM)
    lat_new, kidx_new = seq(lat_new), seq(kidx_new)
    sb_kv_new, bd_kv_new, ds_kv_new = kv_state(sb_kv_new), kv_state(bd_kv_new), kv_state(ds_kv_new)

    if past is None:
        kv_c = jnp.pad(kv_c, ((0, 0), (BAND_WINDOW, 0), (0, 0)))
        band_rows = bd_kv_new[:, s_len - min(BAND_WINDOW, s_len):]
        s_k = s_len
    else:
        past_lat, past_sb, past_band, past_ds, past_kidx = past
        p_len = past_lat.shape[1]
        s_k = p_len + s_len
        rows_bf16 = lambda a: a.reshape(bsz, a.shape[1], -1).astype(BF16)
        kv_past = _matmul(past_lat[..., :MLA_D_C].reshape(bsz * p_len, MLA_D_C), wts['w_ukv'])
        kv_past = kv_past.reshape(bsz, p_len, -1)
        kr_past = jnp.pad(past_lat[..., MLA_D_C:],
                          ((0, 0), (0, 0), (MLA_ROPE_LANE, LANES - MLA_ROPE_LANE - MLA_D_ROPE)))
        k_past = kv_past[..., :N_HEADS * LANES].reshape(bsz, p_len, N_HEADS, LANES) + kr_past[:, :, None]
        k_a = jnp.concatenate([rows_bf16(k_past), k_a], axis=1)
        v_past = kv_past[..., N_HEADS * LANES:] + _ones_upper_half(p_len)
        v_a = jnp.concatenate([v_past.astype(BF16), v_a], axis=1)
        kv_b = jnp.concatenate([rows_bf16(past_sb), kv_b], axis=1)
        kv_c = jnp.concatenate([rows_bf16(past_band), kv_c], axis=1)
        past_ds = past_ds.reshape(bsz, p_len, 2 * BRANCH_WIDTH)
        k_d = jnp.concatenate([past_ds[..., :BRANCH_WIDTH].astype(BF16), k_d], axis=1)
        vt_d = jnp.concatenate(
            [jnp.transpose(past_ds[..., BRANCH_WIDTH:], (0, 2, 1)).astype(BF16), vt_d], axis=2)
        kx_d = jnp.concatenate([past_kidx.astype(BF16), kx_d], axis=1)
        band_rows = bd_kv_new

    o_a = _mla_attention(q_a, _pad_keys(k_a), _pad_keys(v_a), q_off)
    o_b = _sb_attention(q_b, _pad_keys(kv_b), q_off)
    o_c = _band_attention(q_c, kv_c, band_rel_bias, q_off)
    n_sel = min(DSA_TOPK, s_k // 4)
    o_d = _dsa_attention(q_ix, w_ix, _pad_keys(kx_d), q_d, _pad_keys(k_d), _pad_keys(vt_d, axis=2),
                         q_off, n_sel)
    o_d = jnp.transpose(o_d, (0, 2, 1))

    flat = lambda o: o.reshape(n, BRANCH_WIDTH)
    x1 = _merge_out_ln(x2, wts['w_gates'], flat(o_a), flat(o_b), flat(o_c), flat(o_d),
                       wts['w_branch'], wts['w_out'], ln1_g[None], ln1_b[None])
    x_out = _ffn_ln(x1, wts['w_gate'], wts['w_up'], wts['w_down'], ln2_g[None], ln2_b[None])
    return x_out.reshape(bsz, s_len, D_MODEL), (lat_new, sb_kv_new, band_rows, ds_kv_new, kidx_new)


def kernel(x_prompt, x_sample, cache_mla_latent, cache_sb_kv, cache_band_kv, cache_dsa_kv, cache_dsa_kidx, w_in, mla_kv_norm, mla_w_uk, mla_w_uv, band_rel_bias, w_branch, w_out, ln1_g, ln1_b, w_gate_up, w_down, ln2_g, ln2_b):
    past_len = cache_mla_latent.shape[2]
    xp, xs = x_prompt, x_sample
    st_p, st_s = [], []
    for l in range(w_in.shape[0]):
        wts = _prepare_weights(w_in[l], mla_w_uk[l], mla_w_uv[l], w_branch[l], w_out[l],
                               w_gate_up[l], w_down[l])
        params = (wts, mla_kv_norm[l], band_rel_bias[l], ln1_g[l], ln1_b[l], ln2_g[l], ln2_b[l])
        xp, new_p = _trunk_layer(xp, 0, None, *params)
        past = (cache_mla_latent[l], cache_sb_kv[l], cache_band_kv[l], cache_dsa_kv[l],
                cache_dsa_kidx[l])
        xs, new_s = _trunk_layer(xs, past_len, past, *params)
        st_p.append(new_p)
        st_s.append(new_s)
    stack = lambda st, i: jnp.stack([s[i] for s in st])
    return (xp, xs) + tuple(stack(st_p, i) for i in range(5)) + tuple(stack(st_s, i) for i in range(5))
```

```python
import functools
import math

import numpy as np
import jax
import jax.numpy as jnp
from jax import lax
from jax.experimental import pallas as pl
from jax.experimental.pallas import tpu as pltpu

D_MODEL = 1024
CHUNK = 64
CHUNK_SHIFT = 6
N_BRANCH = 4
N_HEADS = 4
HEAD_DIM = 64
BRANCH_WIDTH = N_HEADS * HEAD_DIM
MLA_D_C = 128
MLA_D_NOPE = 64
MLA_D_ROPE = 32
MLA_THETA = 10000.0
ROPE_THETA = 500000.0
BAND_LEFT_CHUNKS = 8
BAND_WINDOW = BAND_LEFT_CHUNKS * CHUNK
REL_CLIP = 128
IDX_HEADS = 8
IDX_DIM = 64
DSA_TOPK = 256
D_FF = ((8 * D_MODEL // 3 + 255) // 256) * 256
DEPTH = 2
ALPHA = (2 * DEPTH) ** 0.25
NEG_INF = -1e30
LOG2_E = math.log2(math.e)
SB_RUN_FLOOR = -150.0
LN_EPS = 1e-5
IN_SIZES = (N_HEADS * (MLA_D_NOPE + MLA_D_ROPE), MLA_D_C, MLA_D_ROPE,
            3 * BRANCH_WIDTH, 3 * BRANCH_WIDTH, 3 * BRANCH_WIDTH,
            IDX_HEADS * IDX_DIM, IDX_DIM, IDX_HEADS, N_BRANCH * D_MODEL)
REST_WIDTH = sum(IN_SIZES[:-1])
GATE_WIDTH = IN_SIZES[-1]

LANES = 128
KEY_ALIGN = 512
INT_MIN = -2 ** 31
VMEM_LIMIT = 56 * 1024 * 1024

F32 = jnp.float32
BF16 = jnp.bfloat16


def _cparams(n_axes):
    return pltpu.CompilerParams(dimension_semantics=("arbitrary",) * n_axes,
                                vmem_limit_bytes=VMEM_LIMIT)


def _pick(n, candidates):
    for c in candidates:
        if n % c == 0:
            return c
    return n


def _ones_upper_half(rows):
    lane = lax.broadcasted_iota(jnp.int32, (rows, N_HEADS * LANES), 1)
    return jnp.where((lane & (LANES - 1)) >= HEAD_DIM, 1.0, 0.0)


def _dot_nt(a, b):
    return lax.dot_general(a, b, (((1,), (1,)), ((), ())), preferred_element_type=F32)


def _mm_kernel(a_ref, b_ref, o_ref):
    o_ref[...] = jnp.dot(a_ref[...].astype(BF16), b_ref[...], preferred_element_type=F32)


def _matmul(a, b):
    m, k = a.shape
    n = b.shape[1]
    tm = _pick(m, (1024, 512, 256, 128, 64, 32, 16, 8))
    tn = _pick(n, (512, 384, 256, 128))
    return pl.pallas_call(
        _mm_kernel,
        grid=(m // tm, n // tn),
        in_specs=[pl.BlockSpec((tm, k), lambda i, j: (i, 0)),
                  pl.BlockSpec((k, tn), lambda i, j: (0, j))],
        out_specs=pl.BlockSpec((tm, tn), lambda i, j: (i, j)),
        out_shape=jax.ShapeDtypeStruct((m, n), F32),
        compiler_params=_cparams(2),
        name="matmul",
    )(a, b)


SEG_AQ = 0
SEG_CKV = 4
SEG_KR = 5
SEG_SB = 6
SEG_BD = 12
SEG_DS = 18
SEG_IXQ = 24
SEG_IXK = 28
N_SEG = 30
IXW_LANE = 96
PROJ_WIDTH = N_SEG * LANES
MLA_ROPE_LANE = 64
LAT_WIDTH = MLA_D_C + MLA_D_ROPE


def _proj_kernel(x_ref, w_ref, wukv_ref, g_ref, ca_ref, sma_ref, spa_ref, cp_ref, smp_ref, spp_ref,
                 qa_ref, lat_ref, ka_ref, va_ref,
                 qb_ref, kvb_ref, sbkv_ref, qc_ref, kvc_ref, bdkv_ref,
                 qd_ref, kd_ref, vtd_ref, dskv_ref, qix_ref, kidx_ref, kx_ref, wix_ref):
    proj = jnp.dot(x_ref[...].astype(BF16), w_ref[...], preferred_element_type=F32)
    seg = lambda s, n=1: proj[:, s * LANES:(s + n) * LANES]

    def rope(t, c_ref, sm_ref, sp_ref, half):
        return (t * c_ref[...] + pltpu.roll(t, LANES - half, 1) * sm_ref[...]
                + pltpu.roll(t, half, 1) * sp_ref[...])

    rope_a = functools.partial(rope, c_ref=ca_ref, sm_ref=sma_ref, sp_ref=spa_ref, half=MLA_D_ROPE // 2)
    rope_p = functools.partial(rope, c_ref=cp_ref, sm_ref=smp_ref, sp_ref=spp_ref, half=HEAD_DIM // 8)
    head_scale = HEAD_DIM ** -0.5 * LOG2_E

    mla_scale = (MLA_D_NOPE + MLA_D_ROPE) ** -0.5 * LOG2_E
    for h in range(N_HEADS):
        qa_ref[:, h * LANES:(h + 1) * LANES] = (rope_a(seg(SEG_AQ + h)) * mla_scale).astype(BF16)
    ckv = seg(SEG_CKV)
    ckv = ckv * lax.rsqrt(jnp.mean(ckv * ckv, axis=-1, keepdims=True) + LN_EPS) * g_ref[...]
    kr = rope_a(seg(SEG_KR))
    lat_ref[:, :MLA_D_C] = ckv
    lat_ref[:, MLA_D_C:] = kr[:, MLA_ROPE_LANE:MLA_ROPE_LANE + MLA_D_ROPE]
    kv_a = jnp.dot(ckv.astype(BF16), wukv_ref[...], preferred_element_type=F32)
    for h in range(N_HEADS):
        ka_ref[:, h * LANES:(h + 1) * LANES] = (kv_a[:, h * LANES:(h + 1) * LANES] + kr).astype(BF16)
    va_ref[...] = (kv_a[:, N_HEADS * LANES:] + _ones_upper_half(kv_a.shape[0])).astype(BF16)

    for s0, q_ref, kv_ref, new_ref in ((SEG_SB, qb_ref, kvb_ref, sbkv_ref),
                                       (SEG_BD, qc_ref, kvc_ref, bdkv_ref)):
        q_ref[...] = (seg(s0, 2) * head_scale).astype(BF16)
        kv = seg(s0 + 2, 4)
        new_ref[...] = kv
        kv_ref[...] = kv.astype(BF16)

    for p in range(2):
        qd_ref[:, p * LANES:(p + 1) * LANES] = (rope_p(seg(SEG_DS + p)) * head_scale).astype(BF16)
        k_rot = rope_p(seg(SEG_DS + 2 + p))
        dskv_ref[:, p * LANES:(p + 1) * LANES] = k_rot
        kd_ref[:, p * LANES:(p + 1) * LANES] = k_rot.astype(BF16)
    v_d = seg(SEG_DS + 4, 2)
    dskv_ref[:, BRANCH_WIDTH:] = v_d
    vtd_ref[...] = v_d.T.astype(BF16)
    for p in range(IDX_HEADS // 2):
        qix_ref[:, p * LANES:(p + 1) * LANES] = (rope_p(seg(SEG_IXQ + p)) * IDX_DIM ** -0.5).astype(BF16)
    ixk = rope_p(seg(SEG_IXK))
    kidx_ref[...] = ixk[:, :IDX_DIM]
    kx_ref[...] = ixk[:, :IDX_DIM].astype(BF16)
    wix_ref[...] = ixk.T[IXW_LANE:IXW_LANE + IDX_HEADS, :]


def _rope_tables(pos, theta, width, starts):
    half = width // 2
    inv = jnp.exp(jnp.arange(half, dtype=F32) * (-2.0 * math.log(theta) / width))
    ang = pos.astype(F32)[:, None] * inv[None, :]
    cos, sin = jnp.cos(ang), jnp.sin(ang)
    n = pos.shape[0]
    fill = lambda v, w: jnp.full((n, w), v, F32)
    c, sm, sp, lane = [], [], [], 0
    for s in sorted(starts):
        c += [fill(1.0, s - lane), cos, cos]
        sm += [fill(0.0, s - lane), -sin, fill(0.0, half)]
        sp += [fill(0.0, s - lane + half), sin]
        lane = s + width
    cat = lambda parts, v: jnp.concatenate(parts + [fill(v, LANES - lane)], axis=1)
    return cat(c, 1.0), cat(sm, 0.0), cat(sp, 0.0)


def _project(x, pos, w_rest, w_ukv, kv_norm):
    n = x.shape[0]
    ts = _pick(n, (512, 256, 128, 64))
    tables = (_rope_tables(pos, MLA_THETA, MLA_D_ROPE, (MLA_ROPE_LANE,))
              + _rope_tables(pos, ROPE_THETA, HEAD_DIM // 4, (0, HEAD_DIM)))
    row = lambda i: (i, 0)
    fixed = lambda i: (0, 0)
    rows = lambda w: pl.BlockSpec((ts, w), row)
    out = lambda w, dt: jax.ShapeDtypeStruct((n, w), dt)
    specs = [
        (N_HEADS * LANES, BF16), (LAT_WIDTH, F32), (N_HEADS * LANES, BF16), (N_HEADS * LANES, BF16),
        (BRANCH_WIDTH, BF16), (2 * BRANCH_WIDTH, BF16), (2 * BRANCH_WIDTH, F32),
        (BRANCH_WIDTH, BF16), (2 * BRANCH_WIDTH, BF16), (2 * BRANCH_WIDTH, F32),
        (BRANCH_WIDTH, BF16), (BRANCH_WIDTH, BF16), None, (2 * BRANCH_WIDTH, F32),
        (IDX_HEADS * IDX_DIM, BF16), (IDX_DIM, F32), (IDX_DIM, BF16), None]
    out_specs, out_shape = [], []
    for k, spec in enumerate(specs):
        if spec is None:
            rows_t, dt = ((BRANCH_WIDTH, BF16), (IDX_HEADS, F32))[k > 12]
            out_specs.append(pl.BlockSpec((rows_t, ts), lambda i: (0, i)))
            out_shape.append(jax.ShapeDtypeStruct((rows_t, n), dt))
        else:
            out_specs.append(rows(spec[0]))
            out_shape.append(out(*spec))
    return pl.pallas_call(
        _proj_kernel,
        grid=(n // ts,),
        in_specs=[rows(D_MODEL),
                  pl.BlockSpec((D_MODEL, PROJ_WIDTH), fixed, pipeline_mode=pl.Buffered(1)),
                  pl.BlockSpec((MLA_D_C, 2 * N_HEADS * LANES), fixed),
                  pl.BlockSpec((1, MLA_D_C), fixed)] + [rows(LANES)] * 6,
        out_specs=out_specs,
        out_shape=out_shape,
        compiler_params=_cparams(1),
        name="project_prepare",
    )(x, w_rest, w_ukv, kv_norm, *tables)


def _mla_kernel(q_ref, k_ref, v_ref, o_ref, *, qb, kb, q_off, skp):
    q0 = pl.program_id(1) * qb
    hi = jnp.minimum((((q_off + q0 + qb - 1) >> CHUNK_SHIFT) + 1) * CHUNK, skp)
    nkb = (hi + kb - 1) // kb
    qpos = q_off + q0 + lax.broadcasted_iota(jnp.int32, (qb, 1), 0)
    cend = ((qpos >> CHUNK_SHIFT) + 1) * CHUNK
    col = lax.broadcasted_iota(jnp.int32, (qb, kb), 1)
    lane = lax.broadcasted_iota(jnp.int32, (qb, LANES), 1)
    n_full = jnp.minimum(((((q_off + q0) >> CHUNK_SHIFT) + 1) * CHUNK) // kb, nkb)

    def body(i, carry, masked):
        k0 = pl.multiple_of(i * kb, kb)
        s = [_dot_nt(q_ref[0, :, h * LANES:(h + 1) * LANES],
                     k_ref[0, pl.ds(k0, kb), h * LANES:(h + 1) * LANES]) for h in range(N_HEADS)]
        if masked:
            vis = col + k0 < cend
            s = [jnp.where(vis, sh, NEG_INF) for sh in s]
        p, new = [], []
        for h in range(N_HEADS):
            m, acc = carry[2 * h:2 * h + 2]
            m_new = jnp.maximum(m, jnp.max(s[h], axis=1, keepdims=True))
            p.append(jnp.exp2((s[h] - m_new).astype(BF16)))
            new += [m_new, jnp.exp2(m - m_new) * acc]
        for h in range(N_HEADS):
            vblk = v_ref[0, pl.ds(k0, kb), h * LANES:(h + 1) * LANES]
            new[2 * h + 1] = new[2 * h + 1] + jnp.dot(p[h], vblk, preferred_element_type=F32)
        return tuple(new)

    init = (jnp.full((qb, 1), NEG_INF, F32), jnp.zeros((qb, LANES), F32)) * N_HEADS
    carry = lax.fori_loop(0, n_full, functools.partial(body, masked=False), init)
    carry = lax.fori_loop(n_full, nkb, functools.partial(body, masked=True), carry)
    outs = [carry[2 * h + 1] / pltpu.roll(carry[2 * h + 1], HEAD_DIM, 1) for h in range(N_HEADS)]
    for pair in range(N_HEADS // 2):
        o_ref[0, :, pair * LANES:(pair + 1) * LANES] = jnp.where(
            lane < HEAD_DIM, outs[2 * pair], pltpu.roll(outs[2 * pair + 1], HEAD_DIM, 1))


def _mla_attention(q, k, v, q_off):
    b, sq, _ = q.shape
    skp = k.shape[1]
    qb = _pick(sq, (512, 256, 128, 64))
    kb = _pick(skp, (1024, KEY_ALIGN))
    kern = functools.partial(_mla_kernel, qb=qb, kb=kb, q_off=q_off, skp=skp)
    return pl.pallas_call(
        kern,
        grid=(b, sq // qb),
        in_specs=[pl.BlockSpec((1, qb, N_HEADS * LANES), lambda bi, qi: (bi, qi, 0)),
                  pl.BlockSpec((1, skp, N_HEADS * LANES), lambda bi, qi: (bi, 0, 0),
                               pipeline_mode=pl.Buffered(1)),
                  pl.BlockSpec((1, skp, N_HEADS * LANES), lambda bi, qi: (bi, 0, 0),
                               pipeline_mode=pl.Buffered(1))],
        out_specs=pl.BlockSpec((1, qb, BRANCH_WIDTH), lambda bi, qi: (bi, qi, 0)),
        out_shape=jax.ShapeDtypeStruct((b, sq, BRANCH_WIDTH), F32),
        compiler_params=_cparams(2),
        name="mla_attention",
    )(q, k, v)


def _head_mask(x_pair, h):
    lane = lax.broadcasted_iota(jnp.int32, x_pair.shape, 1)
    keep = (lane < HEAD_DIM) if h % 2 == 0 else (lane >= HEAD_DIM)
    return jnp.where(keep, x_pair, jnp.zeros_like(x_pair))


def _sb_kernel(q_ref, kv_ref, t_ref, o_ref, *, qb, kb, sub, q_off):
    q0 = pl.program_id(1) * qb
    hi = q_off + q0 + qb - 1
    nkb = (hi + kb - 1) // kb
    qpos = q_off + q0 + lax.broadcasted_iota(jnp.int32, (qb, 1), 0)
    col = lax.broadcasted_iota(jnp.int32, (qb, sub), 1)
    lane = lax.broadcasted_iota(jnp.int32, (qb, LANES), 1)
    tri = t_ref[...]
    n_full = jnp.minimum((q_off + q0) // kb, nkb)
    qh = [_head_mask(q_ref[0, :, (h // 2) * LANES:(h // 2 + 1) * LANES], h) for h in range(N_HEADS)]

    def body(i, carry, masked):
        k0 = pl.multiple_of(i * kb, kb)
        carry = list(carry)
        units = [(j, h) for j in reversed(range(kb // sub)) for h in range(N_HEADS)]
        strict = {j: col + (k0 + j * sub) < qpos for j in range(kb // sub)} if masked else None
        z = {}
        for j, h in units:
            pair = h // 2
            kblk = kv_ref[0, pl.ds(k0 + j * sub, sub), pair * LANES:(pair + 1) * LANES]
            z[j, h] = _dot_nt(qh[h], kblk)
        log_1m, suffix = {}, {}
        for u in units:
            nz = -z[u]
            t = jnp.minimum(nz, 0.0) - jnp.log2(1.0 + jnp.exp2(jnp.minimum(z[u], nz)))
            if masked:
                t = jnp.where(strict[u[0]], t, 0.0)
            log_1m[u] = t
        for u in units:
            hi_part = log_1m[u].astype(BF16)
            lo_part = (log_1m[u] - hi_part.astype(F32)).astype(BF16)
            suffix[u] = (jnp.dot(hi_part, tri, preferred_element_type=F32)
                         + jnp.dot(lo_part, tri, preferred_element_type=F32))
        for j, h in units:
            u = (j, h)
            run, acc = carry[2 * h:2 * h + 2]
            expo = z[u] + log_1m[u] + suffix[u] + run
            if masked:
                expo = jnp.where(strict[j], expo, NEG_INF)
            a = jnp.exp2(expo).astype(BF16)
            pair = h // 2
            vblk = kv_ref[0, pl.ds(k0 + j * sub, sub),
                          BRANCH_WIDTH + pair * LANES:BRANCH_WIDTH + (pair + 1) * LANES]
            carry[2 * h + 1] = acc + jnp.dot(a, vblk, preferred_element_type=F32)
            carry[2 * h] = run + jnp.sum(log_1m[u], axis=1, keepdims=True)
        return tuple(carry)

    def live(state):
        i, carry = state[0], state[1:]
        top = carry[0]
        for h in range(1, N_HEADS):
            top = jnp.maximum(top, carry[2 * h])
        return (i >= 0) & (jnp.max(top) >= SB_RUN_FLOOR)

    def step(masked):
        return lambda state: (state[0] - 1,) + body(state[0], state[1:], masked)

    init = (jnp.zeros((qb, 1), F32), jnp.zeros((qb, LANES), F32)) * N_HEADS
    carry = lax.fori_loop(0, nkb - n_full, lambda i, c: body(nkb - 1 - i, c, True), init)
    state = lax.while_loop(live, step(False), (n_full - 1,) + tuple(carry))
    carry = state[1:]
    outs = [carry[2 * h + 1] for h in range(N_HEADS)]
    for pair in range(N_HEADS // 2):
        o_ref[0, :, pair * LANES:(pair + 1) * LANES] = jnp.where(
            lane < HEAD_DIM, outs[2 * pair], outs[2 * pair + 1])


def _sb_attention(q, kv, q_off):
    b, sq, _ = q.shape
    skp = kv.shape[1]
    qb = _pick(sq, (256, 128, 64))
    sub = 256
    kb = KEY_ALIGN
    tri =jnp.asarray(np.tril(np.ones((sub, sub), np.float32), -1), BF16)
    kern = functools.partial(_sb_kernel, qb=qb, kb=kb, sub=sub, q_off=q_off)
    return pl.pallas_call(
        kern,
        grid=(b, sq // qb),
        in_specs=[pl.BlockSpec((1, qb, BRANCH_WIDTH), lambda bi, qi: (bi, qi, 0)),
                  pl.BlockSpec((1, skp, 2 * BRANCH_WIDTH), lambda bi, qi: (bi, 0, 0),
                               pipeline_mode=pl.Buffered(1)),
                  pl.BlockSpec((sub, sub), lambda bi, qi: (0, 0))],
        out_specs=pl.BlockSpec((1, qb, BRANCH_WIDTH), lambda bi, qi: (bi, qi, 0)),
        out_shape=jax.ShapeDtypeStruct((b, sq, BRANCH_WIDTH), F32),
        compiler_params=_cparams(2),
        name="stick_breaking_attention",
    )(q, kv, tri)


def _band_kernel(bias_ref, q_ref, kv_ref, o_ref, tile_ref, *, qb, win, q_off):
    first = (pl.program_id(0) == 0) & (pl.program_id(1) == 0)
    row = lax.broadcasted_iota(jnp.int32, (qb, win), 0)
    col = lax.broadcasted_iota(jnp.int32, (qb, win), 1)

    @pl.when(first)
    def _():
        width = win + qb
        c = lax.broadcasted_iota(jnp.int32, (8, width), 1)
        rel = jnp.where(c < win, jnp.clip(BAND_WINDOW - c, -REL_CLIP, REL_CLIP) + REL_CLIP, 2 * REL_CLIP)

        def fill(r, rows):
            hit = rel == r
            return tuple(jnp.where(hit, bias_ref[h, r], rows[h]) for h in range(N_HEADS))

        rows = lax.fori_loop(0, 2 * REL_CLIP + 1, fill, (jnp.zeros((8, width), F32),) * N_HEADS)
        qch = row >> CHUNK_SHIFT
        kch = col >> CHUNK_SHIFT
        in_band = (kch >= qch) & (kch <= qch + BAND_LEFT_CHUNKS)
        for h in range(N_HEADS):
            table = jnp.broadcast_to(rows[h][:1], (qb, width))
            skewed = pltpu.roll(table, 0, 1, stride=1, stride_axis=0)[:, :win]
            tile_ref[h] = jnp.where(in_band, skewed * LOG2_E, NEG_INF)

    q0 = pl.multiple_of(pl.program_id(1) * qb, qb)
    kpos = col + (q_off + q0 - BAND_WINDOW)
    lane = lax.broadcasted_iota(jnp.int32, (qb, LANES), 1)
    outs = []
    for h in range(N_HEADS):
        pair = h // 2
        qh = _head_mask(q_ref[0, :, pair * LANES:(pair + 1) * LANES], h)
        kwin = kv_ref[0, pl.ds(q0, win), pair * LANES:(pair + 1) * LANES]
        s = _dot_nt(qh, kwin) + tile_ref[h]
        s = jnp.where(kpos >= 0, s, NEG_INF)
        m = jnp.max(s, axis=1, keepdims=True)
        p = jnp.exp2(s - m)
        l = jnp.sum(p, axis=1, keepdims=True)
        vwin = kv_ref[0, pl.ds(q0, win),
                      BRANCH_WIDTH + pair * LANES:BRANCH_WIDTH + (pair + 1) * LANES]
        outs.append(jnp.dot(p.astype(BF16), vwin, preferred_element_type=F32) / l)
    for pair in range(N_HEADS // 2):
        o_ref[0, :, pair * LANES:(pair + 1) * LANES] = jnp.where(
            lane < HEAD_DIM, outs[2 * pair], outs[2 * pair + 1])


def _band_attention(q, kv, rel_bias, q_off):
    b, sq, _ = q.shape
    sk = kv.shape[1]
    qb = _pick(sq, (256, 128, 64))
    win = qb + BAND_WINDOW
    kern = functools.partial(_band_kernel, qb=qb, win=win, q_off=q_off)
    return pl.pallas_call(
        kern,
        grid=(b, sq // qb),
        in_specs=[pl.BlockSpec(memory_space=pltpu.SMEM),
                  pl.BlockSpec((1, qb, BRANCH_WIDTH), lambda bi, qi: (bi, qi, 0)),
                  pl.BlockSpec((1, sk, 2 * BRANCH_WIDTH), lambda bi, qi: (bi, 0, 0),
                               pipeline_mode=pl.Buffered(1))],
        out_specs=pl.BlockSpec((1, qb, BRANCH_WIDTH), lambda bi, qi: (bi, qi, 0)),
        out_shape=jax.ShapeDtypeStruct((b, sq, BRANCH_WIDTH), F32),
        scratch_shapes=[pltpu.VMEM((N_HEADS, qb, win), F32)],
        compiler_params=_cparams(2),
        name="band_attention",
    )(rel_bias, q, kv)


def _dsa_kernel(qi_ref, w_ref, kx_ref, q_ref, k_ref, vt_ref, o_ref, key_ref, key16_ref, cut_ref,
                sa_ref, sb_ref,
                *, qb, kb, q_off, skp, n_sel):
    q0 = pl.program_id(1) * qb
    hi = jnp.minimum((((q_off + q0 + qb - 1) >> CHUNK_SHIFT) + 1) * CHUNK, skp)
    nkb = (hi + kb - 1) // kb
    qpos = q_off + q0 + lax.broadcasted_iota(jnp.int32, (1, qb), 1)
    cend = ((qpos >> CHUNK_SHIFT) + 1) * CHUNK
    row = lax.broadcasted_iota(jnp.int32, (kb, qb), 0)
    float_key = lambda bits: jnp.where(bits < 0, INT_MIN - bits, bits)
    neg_key = float_key(lax.bitcast_convert_type(jnp.full((1, 1), NEG_INF, F32), jnp.int32))
    n_full = jnp.minimum(((((q_off + q0) >> CHUNK_SHIFT) + 1) * CHUNK) // kb, nkb)

    def score_block(i, c, masked):
        k0 = pl.multiple_of(i * kb, kb)
        kx = kx_ref[0, pl.ds(k0, kb), :]
        r = [_dot_nt(kx, qi_ref[0, :, j * IDX_DIM:(j + 1) * IDX_DIM]) for j in range(IDX_HEADS)]
        score = jnp.zeros((kb, qb), F32)
        for j in range(IDX_HEADS):
            wj = w_ref[0, j:j + 1, :] * (IDX_HEADS ** -0.5)
            score = score + wj * jnp.maximum(r[j], 0.0)
        if masked:
            score = jnp.where(row + k0 < cend, score, NEG_INF)
        keys = float_key(lax.bitcast_convert_type(score, jnp.int32))
        key_ref[pl.ds(k0, kb), :] = keys
        key16_ref[pl.ds(k0, kb), :] = (keys >> 16).astype(jnp.int16)
        return c

    lax.fori_loop(0, n_full, functools.partial(score_block, masked=False), 0)
    lax.fori_loop(n_full, nkb, functools.partial(score_block, masked=True), 0)

    slab = 64 if kb % 64 == 0 else kb

    def count_ge(cand):
        def blk(i, cnt):
            k0 = pl.multiple_of(i * kb, kb)
            for s in range(kb // slab):
                keys = key_ref[pl.ds(k0 + s * slab, slab), :]
                cnt = cnt + jnp.where(keys >= cand, 1.0, 0.0)
            return cnt
        part = lax.fori_loop(0, nkb, blk, jnp.zeros((slab, qb), F32))
        return jnp.sum(part, axis=0, keepdims=True)

    slab16 = 128 if kb % 128 == 0 else kb

    def count16_ge(cand16):
        def blk(i, cnt):
            k0 = pl.multiple_of(i * kb, kb)
            for s in range(kb // slab16):
                k16 = key16_ref[pl.ds(k0 + s * slab16, slab16), :]
                cnt = cnt + jnp.where(k16 >= cand16, jnp.int16(1), jnp.int16(0))
            return cnt
        part = lax.fori_loop(0, nkb, blk, jnp.zeros((slab16, qb), jnp.int16))
        return jnp.sum(part.astype(F32), axis=0, keepdims=True)

    def undecided(t_end):
        def cond(state):
            t, _, n_ge = state
            return (t < t_end) & (jnp.max(jnp.abs(n_ge - n_sel)) > 0.0)
        return cond

    def bisect(to_16bit):
        def step(state):
            t, thr, n_ge = state
            cand = thr + jnp.left_shift(jnp.int32(1), 31 - t)
            cnt = count16_ge(to_16bit(cand))
            ok = cnt >= n_sel
            return t + 1, jnp.where(ok, cand, thr), jnp.where(ok, cnt, n_ge)
        return step

    stored = (nkb * kb).astype(F32)
    high_step = bisect(lambda cand: (cand >> 16).astype(jnp.int16))
    state = lax.fori_loop(
        0, 16, lambda _, st: high_step(st),
        (jnp.int32(0), jnp.full((1, qb), INT_MIN, jnp.int32), jnp.full((1, qb), stored, F32)))

    @pl.when(undecided(32)(state))
    def _():
        thr_hi = state[1] >> 16

        def low_halves(i, c):
            k0 = pl.multiple_of(i * kb, kb)
            keys = key_ref[pl.ds(k0, kb), :]
            hi16 = keys >> 16
            low = jnp.where(hi16 == thr_hi, (keys & 0xFFFF) - 0x8000,
                            jnp.where(hi16 > thr_hi, 0x7FFF, -0x8000))
            key16_ref[pl.ds(k0, kb), :] = low.astype(jnp.int16)
            return c

        lax.fori_loop(0, nkb, low_halves, 0)

    low_step = bisect(lambda cand: ((cand & 0xFFFF) - 0x8000).astype(jnp.int16))
    _, thr, n_ge = lax.while_loop(
        undecided(32), lambda st: low_step(low_step(low_step(low_step(st)))), state)
    excess = jnp.where((n_ge > n_sel) & (thr > neg_key), 1.0, 0.0)
    cut_ref[...] = jnp.full((1, qb), skp, jnp.int32)

    @pl.when(jnp.max(excess) > 0.0)
    def _():
        need = n_sel - count_ge(thr + 1)

        def count_tied_below(limit):
            def blk(i, cnt):
                k0 = pl.multiple_of(i * kb, kb)
                tied = (key_ref[pl.ds(k0, kb), :] == thr) & (row + k0 < limit)
                return cnt + jnp.sum(jnp.where(tied, 1.0, 0.0), axis=0, keepdims=True)
            return lax.fori_loop(0, nkb, blk, jnp.zeros((1, qb), F32))

        n_bits = max(1, (skp - 1).bit_length())

        def bisect_cut(t, cut):
            cand = cut + jnp.left_shift(jnp.int32(1), n_bits - 1 - t)
            return jnp.where(count_tied_below(cand) < need, cand, cut)

        cut = lax.fori_loop(0, n_bits, bisect_cut, jnp.zeros((1, qb), jnp.int32))
        cut_ref[...] = jnp.where(excess > 0.0, cut + 1, skp)

    cut = cut_ref[...]

    qh = [_head_mask(q_ref[0, :, (h // 2) * LANES:(h // 2 + 1) * LANES], h) for h in range(N_HEADS)]

    kb3 = kb // 2
    row3 = lax.broadcasted_iota(jnp.int32, (kb3, qb), 0)

    def scores(i):
        k0 = pl.multiple_of(i * kb3, kb3)
        return tuple(_dot_nt(k_ref[0, pl.ds(k0, kb3), (h // 2) * LANES:(h // 2 + 1) * LANES], qh[h])
                     for h in range(N_HEADS))

    def attend(i, s, carry):
        k0 = pl.multiple_of(i * kb3, kb3)
        keys = key_ref[pl.ds(k0, kb3), :]
        kpos = row3 + k0
        sel = (keys >= thr) & ((keys != thr) | (kpos < cut)) & (kpos < cend)
        bias = jnp.where(sel, 0.0, NEG_INF)
        p, new = [], []
        for h in range(N_HEADS):
            m, acc = carry[2 * h:2 * h + 2]
            sh = s[h] + bias
            m_new = jnp.maximum(m, jnp.max(sh, axis=0, keepdims=True))
            p.append(jnp.exp2((sh - m_new).astype(BF16)))
            new += [m_new, jnp.exp2(m - m_new) * acc]
        for h in range(N_HEADS):
            vblk = jnp.concatenate([vt_ref[0, h * HEAD_DIM:(h + 1) * HEAD_DIM, pl.ds(k0, kb3)],
                                    jnp.ones((LANES - HEAD_DIM, kb3), BF16)], axis=0)
            new[2 * h + 1] = new[2 * h + 1] + jnp.dot(vblk, p[h], preferred_element_type=F32)
        return tuple(new)

    def put_scores(slot_ref, i):
        for h, sh in enumerate(scores(i)):
            slot_ref[h] = sh

    def attend_from(slot_ref, i, carry):
        return attend(i, tuple(slot_ref[h] for h in range(N_HEADS)), carry)

    def body(j, carry):
        put_scores(sb_ref, 2 * j + 1)
        carry = attend_from(sa_ref, 2 * j, carry)
        put_scores(sa_ref, 2 * j + 2)
        return attend_from(sb_ref, 2 * j + 1, carry)

    init = (jnp.full((1, qb), NEG_INF, F32), jnp.zeros((LANES, qb), F32)) * N_HEADS
    n_pairs = nkb * (kb // (2 * kb3))
    put_scores(sa_ref, 0)
    carry = lax.fori_loop(0, n_pairs - 1, body, init)
    last = 2 * (n_pairs - 1)
    put_scores(sb_ref, last + 1)
    carry = attend_from(sa_ref, last, carry)
    carry = attend_from(sb_ref, last + 1, carry)
    for h in range(N_HEADS):
        acc = carry[2 * h + 1]
        o_ref[0, h * HEAD_DIM:(h + 1) * HEAD_DIM, :] = acc[:HEAD_DIM, :] / acc[HEAD_DIM:HEAD_DIM + 1, :]


def _dsa_attention(qi, w, kx, q, k, vt, q_off, n_sel):
    b, sq, _ = q.shape
    skp = k.shape[1]
    qb = _pick(sq, (256, 128, 64))
    kb = _pick(skp, (1024, KEY_ALIGN))
    kern = functools.partial(_dsa_kernel, qb=qb, kb=kb, q_off=q_off, skp=skp, n_sel=n_sel)
    return pl.pallas_call(
        kern,
        grid=(b, sq // qb),
        in_specs=[pl.BlockSpec((1, qb, IDX_HEADS * IDX_DIM), lambda bi, qi_: (bi, qi_, 0)),
                  pl.BlockSpec((1, IDX_HEADS, qb), lambda bi, qi_: (bi, 0, qi_)),
                  pl.BlockSpec((1, skp, IDX_DIM), lambda bi, qi_: (bi, 0, 0),
                               pipeline_mode=pl.Buffered(1)),
                  pl.BlockSpec((1, qb, BRANCH_WIDTH), lambda bi, qi_: (bi, qi_, 0)),
                  pl.BlockSpec((1, skp, BRANCH_WIDTH), lambda bi, qi_: (bi, 0, 0),
                               pipeline_mode=pl.Buffered(1)),
                  pl.BlockSpec((1, BRANCH_WIDTH, skp), lambda bi, qi_: (bi, 0, 0),
                               pipeline_mode=pl.Buffered(1))],
        out_specs=pl.BlockSpec((1, BRANCH_WIDTH, qb), lambda bi, qi_: (bi, 0, qi_)),
        out_shape=jax.ShapeDtypeStruct((b, BRANCH_WIDTH, sq), F32),
        scratch_shapes=[pltpu.VMEM((skp, qb), jnp.int32), pltpu.VMEM((skp, qb), jnp.int16),
                        pltpu.VMEM((1, qb), jnp.int32),
                        pltpu.VMEM((N_HEADS, kb // 2, qb), F32), pltpu.VMEM((N_HEADS, kb // 2, qb), F32)],
        compiler_params=_cparams(2),
        name="dsa_attention",
    )(qi, w, kx, q, k, vt)


def _layer_norm(z, g, b):
    mu = jnp.mean(z, axis=-1, keepdims=True)
    zc = z - mu
    var = jnp.mean(zc * zc, axis=-1, keepdims=True)
    return zc * lax.rsqrt(var + LN_EPS) * g + b


def _merge_kernel(x_ref, wgate_ref, oa_ref, ob_ref, oc_ref, od_ref, wb_ref, wo_ref, g_ref, b_ref,
                  y_ref):
    xb = x_ref[...].astype(BF16)
    logits = [jnp.dot(xb, wgate_ref[:, n * D_MODEL:(n + 1) * D_MODEL], preferred_element_type=F32)
              for n in range(N_BRANCH)]
    branch = [jnp.dot(o_ref[...].astype(BF16), wb_ref[n], preferred_element_type=F32)
              for n, o_ref in enumerate((oa_ref, ob_ref, oc_ref, od_ref))]
    merged = jax.nn.sigmoid(logits[0]) * branch[0]
    for n in range(1, N_BRANCH):
        merged = merged + jax.nn.sigmoid(logits[n]) * branch[n]
    y = jnp.dot(merged.astype(BF16), wo_ref[...], preferred_element_type=F32)
    y_ref[...] = _layer_norm(ALPHA * x_ref[...] + y, g_ref[...], b_ref[...])


def _merge_out_ln(x, w_gate, o_a, o_b, o_c, o_d, w_branch, w_out, g, b):
    n = x.shape[0]
    ts = _pick(n, (512, 256, 128, 64))
    row = lambda i: (i, 0)
    obs = pl.BlockSpec((ts, BRANCH_WIDTH), row)
    return pl.pallas_call(
        _merge_kernel,
        grid=(n // ts,),
        in_specs=[pl.BlockSpec((ts, D_MODEL), row),
                  pl.BlockSpec((D_MODEL, GATE_WIDTH), lambda i: (0, 0), pipeline_mode=pl.Buffered(1)),
                  obs, obs, obs, obs,
                  pl.BlockSpec((N_BRANCH, BRANCH_WIDTH, D_MODEL), lambda i: (0, 0, 0)),
                  pl.BlockSpec((D_MODEL, D_MODEL), lambda i: (0, 0)),
                  pl.BlockSpec((1, D_MODEL), lambda i: (0, 0)),
                  pl.BlockSpec((1, D_MODEL), lambda i: (0, 0))],
        out_specs=pl.BlockSpec((ts, D_MODEL), row),
        out_shape=jax.ShapeDtypeStruct((n, D_MODEL), F32),
        compiler_params=_cparams(1),
        name="merge_out_ln",
    )(x, w_gate, o_a, o_b, o_c, o_d, w_branch, w_out, g, b)


def _ffn_kernel(x_ref, wg_ref, wu_ref, wd_ref, g_ref, b_ref, y_ref, acc_ref):
    f = pl.program_id(1)
    xb = x_ref[...].astype(BF16)
    gate = jnp.dot(xb, wg_ref[...], preferred_element_type=F32)
    up = jnp.dot(xb, wu_ref[...], preferred_element_type=F32)
    hidden = (gate * jax.nn.sigmoid(gate) * up).astype(BF16)
    part = jnp.dot(hidden, wd_ref[...], preferred_element_type=F32)

    @pl.when(f == 0)
    def _():
        acc_ref[...] = part

    @pl.when(f > 0)
    def _():
        acc_ref[...] += part

    @pl.when(f == pl.num_programs(1) - 1)
    def _():
        y_ref[...] = _layer_norm(ALPHA * x_ref[...] + acc_ref[...], g_ref[...], b_ref[...])


def _ffn_ln(x, w_gate, w_up, w_down, g, b):
    n = x.shape[0]
    ts = _pick(n, (1024, 512, 256, 128, 64))
    tf = D_FF // 2
    return pl.pallas_call(
        _ffn_kernel,
        grid=(n // ts, D_FF // tf),
        in_specs=[pl.BlockSpec((ts, D_MODEL), lambda i, f: (i, 0)),
                  pl.BlockSpec((D_MODEL, tf), lambda i, f: (0, f)),
                  pl.BlockSpec((D_MODEL, tf), lambda i, f: (0, f)),
                  pl.BlockSpec((tf, D_MODEL), lambda i, f: (f, 0)),
                  pl.BlockSpec((1, D_MODEL), lambda i, f: (0, 0)),
                  pl.BlockSpec((1, D_MODEL), lambda i, f: (0, 0))],
        out_specs=pl.BlockSpec((ts, D_MODEL), lambda i, f: (i, 0)),
        out_shape=jax.ShapeDtypeStruct((n, D_MODEL), F32),
        scratch_shapes=[pltpu.VMEM((ts, D_MODEL), F32)],
        compiler_params=_cparams(2),
        name="ffn_ln",
    )(x, w_gate, w_up, w_down, g, b)


def _pad_keys(a, axis=1):
    n = a.shape[axis]
    pad = (-n) % KEY_ALIGN
    if pad == 0:
        return a
    widths = [(0, 0)] * a.ndim
    widths[axis] = (0, pad)
    return jnp.pad(a, widths)


def _prepare_weights(w_in, mla_w_uk, mla_w_uv, w_branch, w_out, w_gate_up, w_down):
    w_rest = jnp.concatenate([w_in, jnp.zeros((D_MODEL, 1), w_in.dtype)], axis=1)
    w_rest = jnp.take(w_rest, _projection_columns(), axis=1).astype(BF16)
    w_uk = jnp.pad(mla_w_uk, ((0, 0), (0, 0), (0, LANES - MLA_D_NOPE))).reshape(MLA_D_C, -1)
    w_uv = jnp.pad(mla_w_uv, ((0, 0), (0, 0), (0, LANES - HEAD_DIM))).reshape(MLA_D_C, -1)
    w_ukv = jnp.concatenate([w_uk, w_uv], axis=1).astype(BF16)
    return dict(w_rest=w_rest, w_gates=w_in[:, REST_WIDTH:].astype(BF16), w_ukv=w_ukv,
                w_branch=w_branch.astype(BF16),
                w_out=w_out.astype(BF16), w_gate=w_gate_up[:, :D_FF].astype(BF16),
                w_up=w_gate_up[:, D_FF:].astype(BF16), w_down=w_down.astype(BF16))


def _projection_columns():
    zero_col = REST_WIDTH + GATE_WIDTH
    src = np.full(PROJ_WIDTH, zero_col, np.int32)
    o_aq, o_ckv, o_kr, o_sb, o_bd, o_ds, o_ixq, o_ixk, o_ixw = np.concatenate(
        [[0], np.cumsum(IN_SIZES[:-1])])[:9]
    hd = MLA_D_NOPE + MLA_D_ROPE

    def put(lane0, src0, n):
        src[lane0:lane0 + n] = src0 + np.arange(n)

    for h in range(N_HEADS):
        put((SEG_AQ + h) * LANES, o_aq + h * hd, hd)
    put(SEG_CKV * LANES, o_ckv, MLA_D_C)
    put(SEG_KR * LANES + MLA_ROPE_LANE, o_kr, MLA_D_ROPE)
    put(SEG_SB * LANES, o_sb, 3 * BRANCH_WIDTH)
    put(SEG_BD * LANES, o_bd, 3 * BRANCH_WIDTH)
    put(SEG_DS * LANES, o_ds, 3 * BRANCH_WIDTH)
    put(SEG_IXQ * LANES, o_ixq, IDX_HEADS * IDX_DIM)
    put(SEG_IXK * LANES, o_ixk, IDX_DIM)
    put(SEG_IXK * LANES + IXW_LANE, o_ixw, IDX_HEADS)
    return src


def _trunk_layer(x, q_off, past, wts, mla_kv_norm, band_rel_bias, ln1_g, ln1_b, ln2_g, ln2_b):
    bsz, s_len, _ = x.shape
    n = bsz * s_len
    pos = jnp.tile(q_off + jnp.arange(s_len, dtype=jnp.int32), bsz)
    x2 = x.reshape(n, D_MODEL)
    (q_a, lat_new, k_a, v_a, q_b, kv_b, sb_kv_new, q_c, kv_c, bd_kv_new,
     q_d, k_d, vt_d, ds_kv_new, q_ix, kidx_new, kx_d, w_ix) = _project(
        x2, pos, wts['w_rest'], wts['w_ukv'], mla_kv_norm[None])
    seq = lambda a: a.reshape(bsz, s_len, a.shape[-1])
    seq_t = lambda a: jnp.transpose(a.reshape(a.shape[0], bsz, s_len), (1, 0, 2))
    q_a, k_a, v_a, q_b, kv_b, q_c, kv_c, q_d, k_d, q_ix, kx_d = map(
        seq, (q_a, k_a, v_a, q_b, kv_b, q_c, kv_c, q_d, k_d, q_ix, kx_d))
    vt_d, w_ix = seq_t(vt_d), seq_t(w_ix)
    kv_state = lambda a: a.reshape(bsz, s_len, 2, N_HEADS, HEAD_DIM)
    lat_new, kidx_new = seq(lat_new), seq(kidx_new)
    sb_kv_new, bd_kv_new, ds_kv_new = kv_state(sb_kv_new), kv_state(bd_kv_new), kv_state(ds_kv_new)

    if past is None:
        kv_c = jnp.pad(kv_c, ((0, 0), (BAND_WINDOW, 0), (0, 0)))
        band_rows = bd_kv_new[:, s_len - min(BAND_WINDOW, s_len):]
        s_k = s_len
    else:
        past_lat, past_sb, past_band, past_ds, past_kidx = past
        p_len = past_lat.shape[1]
        s_k = p_len + s_len
        rows_bf16 = lambda a: a.reshape(bsz, a.shape[1], -1).astype(BF16)
        kv_past = _matmul(past_lat[..., :MLA_D_C].reshape(bsz * p_len, MLA_D_C), wts['w_ukv'])
        kv_past = kv_past.reshape(bsz, p_len, -1)
        kr_past = jnp.pad(past_lat[..., MLA_D_C:],
                          ((0, 0), (0, 0), (MLA_ROPE_LANE, LANES - MLA_ROPE_LANE - MLA_D_ROPE)))
        k_past = kv_past[..., :N_HEADS * LANES].reshape(bsz, p_len, N_HEADS, LANES) + kr_past[:, :, None]
        k_a = jnp.concatenate([rows_bf16(k_past), k_a], axis=1)
        v_past = kv_past[..., N_HEADS * LANES:] + _ones_upper_half(p_len)
        v_a = jnp.concatenate([v_past.astype(BF16), v_a], axis=1)
        kv_b = jnp.concatenate([rows_bf16(past_sb), kv_b], axis=1)
        kv_c = jnp.concatenate([rows_bf16(past_band), kv_c], axis=1)
        past_ds = past_ds.reshape(bsz, p_len, 2 * BRANCH_WIDTH)
        k_d = jnp.concatenate([past_ds[..., :BRANCH_WIDTH].astype(BF16), k_d], axis=1)
        vt_d = jnp.concatenate(
            [jnp.transpose(past_ds[..., BRANCH_WIDTH:], (0, 2, 1)).astype(BF16), vt_d], axis=2)
        kx_d = jnp.concatenate([past_kidx.astype(BF16), kx_d], axis=1)
        band_rows = bd_kv_new

    o_a = _mla_attention(q_a, _pad_keys(k_a), _pad_keys(v_a), q_off)
    o_b = _sb_attention(q_b, _pad_keys(kv_b), q_off)
    o_c = _band_attention(q_c, kv_c, band_rel_bias, q_off)
    n_sel = min(DSA_TOPK, s_k // 4)
    o_d = _dsa_attention(q_ix, w_ix, _pad_keys(kx_d), q_d, _pad_keys(k_d), _pad_keys(vt_d, axis=2),
                         q_off, n_sel)
    o_d = jnp.transpose(o_d, (0, 2, 1))

    flat = lambda o: o.reshape(n, BRANCH_WIDTH)
    x1 = _merge_out_ln(x2, wts['w_gates'], flat(o_a), flat(o_b), flat(o_c), flat(o_d),
                       wts['w_branch'], wts['w_out'], ln1_g[None], ln1_b[None])
    x_out = _ffn_ln(x1, wts['w_gate'], wts['w_up'], wts['w_down'], ln2_g[None], ln2_b[None])
    return x_out.reshape(bsz, s_len, D_MODEL), (lat_new, sb_kv_new, band_rows, ds_kv_new, kidx_new)


def kernel(x_prompt, x_sample, cache_mla_latent, cache_sb_kv, cache_band_kv, cache_dsa_kv, cache_dsa_kidx, w_in, mla_kv_norm, mla_w_uk, mla_w_uv, band_rel_bias, w_branch, w_out, ln1_g, ln1_b, w_gate_up, w_down, ln2_g, ln2_b):
    past_len = cache_mla_latent.shape[2]
    xp, xs = x_prompt, x_sample
    st_p, st_s = [], []
    for l in range(w_in.shape[0]):
        wts = _prepare_weights(w_in[l], mla_w_uk[l], mla_w_uv[l], w_branch[l], w_out[l],
                               w_gate_up[l], w_down[l])
        params = (wts, mla_kv_norm[l], band_rel_bias[l], ln1_g[l], ln1_b[l], ln2_g[l], ln2_b[l])
        xp, new_p = _trunk_layer(xp, 0, None, *params)
        past = (cache_mla_latent[l], cache_sb_kv[l], cache_band_kv[l], cache_dsa_kv[l],
                cache_dsa_kidx[l])
        xs, new_s = _trunk_layer(xs, past_len, past, *params)
        st_p.append(new_p)
        st_s.append(new_s)
    stack = lambda st, i: jnp.stack([s[i] for s in st])
    return (xp, xs) + tuple(stack(st_p, i) for i in range(5)) + tuple(stack(st_s, i) for i in range(5))
```

```python
import functools
import math

import numpy as np
import jax
import jax.numpy as jnp
from jax import lax
from jax.experimental import pallas as pl
from jax.experimental.pallas import tpu as pltpu

D_MODEL = 1024
CHUNK = 64
CHUNK_SHIFT = 6
N_BRANCH = 4
N_HEADS = 4
HEAD_DIM = 64
BRANCH_WIDTH = N_HEADS * HEAD_DIM
MLA_D_C = 128
MLA_D_NOPE = 64
MLA_D_ROPE = 32
MLA_THETA = 10000.0
ROPE_THETA = 500000.0
BAND_LEFT_CHUNKS = 8
BAND_WINDOW = BAND_LEFT_CHUNKS * CHUNK
REL_CLIP = 128
IDX_HEADS = 8
IDX_DIM = 64
DSA_TOPK = 256
D_FF = ((8 * D_MODEL // 3 + 255) // 256) * 256
DEPTH = 2
ALPHA = (2 * DEPTH) ** 0.25
NEG_INF = -1e30
LOG2_E = math.log2(math.e)
SB_RUN_FLOOR = -150.0
LN_EPS = 1e-5
IN_SIZES = (N_HEADS * (MLA_D_NOPE + MLA_D_ROPE), MLA_D_C, MLA_D_ROPE,
            3 * BRANCH_WIDTH, 3 * BRANCH_WIDTH, 3 * BRANCH_WIDTH,
            IDX_HEADS * IDX_DIM, IDX_DIM, IDX_HEADS, N_BRANCH * D_MODEL)
REST_WIDTH = sum(IN_SIZES[:-1])
GATE_WIDTH = IN_SIZES[-1]

LANES = 128
KEY_ALIGN = 512
INT_MIN = -2 ** 31
VMEM_LIMIT = 56 * 1024 * 1024

F32 = jnp.float32
BF16 = jnp.bfloat16


def _cparams(n_axes):
    return pltpu.CompilerParams(dimension_semantics=("arbitrary",) * n_axes,
                                vmem_limit_bytes=VMEM_LIMIT)


def _pick(n, candidates):
    for c in candidates:
        if n % c == 0:
            return c
    return n


def _ones_upper_half(rows):
    lane = lax.broadcasted_iota(jnp.int32, (rows, N_HEADS * LANES), 1)
    return jnp.where((lane & (LANES - 1)) >= HEAD_DIM, 1.0, 0.0)


def _dot_nt(a, b):
    return lax.dot_general(a, b, (((1,), (1,)), ((), ())), preferred_element_type=F32)


def _latent_kernel(lat_ref, w_ref, k_ref, v_ref):
    res = jnp.dot(lat_ref[...].astype(BF16), w_ref[...], preferred_element_type=F32)
    k_ref[...] = res[:, :N_HEADS * LANES].astype(BF16)
    v_ref[...] = (res[:, N_HEADS * LANES:] + _ones_upper_half(res.shape[0])).astype(BF16)


def _latent_keys_values(lat, w_lat):
    m = lat.shape[0]
    tm = _pick(m, (1024, 512, 256, 128, 64, 32, 16, 8))
    width = N_HEADS * LANES
    return pl.pallas_call(
        _latent_kernel,
        grid=(m // tm,),
        in_specs=[pl.BlockSpec((tm, LAT_WIDTH), lambda i: (i, 0)),
                  pl.BlockSpec((LAT_WIDTH, 2 * width), lambda i: (0, 0))],
        out_specs=[pl.BlockSpec((tm, width), lambda i: (i, 0))] * 2,
        out_shape=[jax.ShapeDtypeStruct((m, width), BF16)] * 2,
        compiler_params=_cparams(1),
        name="latent_keys_values",
    )(lat, w_lat)


SEG_AQ = 0
SEG_CKV = 4
SEG_KR = 5
SEG_SB = 6
SEG_BD = 12
SEG_DS = 18
SEG_IXQ = 24
SEG_IXK = 28
N_SEG = 30
IXW_LANE = 96
PROJ_WIDTH = N_SEG * LANES
MLA_ROPE_LANE = 64
LAT_WIDTH = MLA_D_C + MLA_D_ROPE


def _proj_kernel(x_ref, w_ref, wukv_ref, g_ref, ca_ref, sma_ref, spa_ref, cp_ref, smp_ref, spp_ref,
                 qa_ref, lat_ref, ka_ref, va_ref,
                 qb_ref, kvb_ref, sbkv_ref, qc_ref, kvc_ref, bdkv_ref,
                 qd_ref, kd_ref, vtd_ref, dskv_ref, qix_ref, kidx_ref, kx_ref, wix_ref):
    proj = jnp.dot(x_ref[...].astype(BF16), w_ref[...], preferred_element_type=F32)
    seg = lambda s, n=1: proj[:, s * LANES:(s + n) * LANES]

    def rope(t, c_ref, sm_ref, sp_ref, half):
        return (t * c_ref[...] + pltpu.roll(t, LANES - half, 1) * sm_ref[...]
                + pltpu.roll(t, half, 1) * sp_ref[...])

    rope_a = functools.partial(rope, c_ref=ca_ref, sm_ref=sma_ref, sp_ref=spa_ref, half=MLA_D_ROPE // 2)
    rope_p = functools.partial(rope, c_ref=cp_ref, sm_ref=smp_ref, sp_ref=spp_ref, half=HEAD_DIM // 8)
    head_scale = HEAD_DIM ** -0.5 * LOG2_E

    mla_scale = (MLA_D_NOPE + MLA_D_ROPE) ** -0.5 * LOG2_E
    for h in range(N_HEADS):
        qa_ref[:, h * LANES:(h + 1) * LANES] = (rope_a(seg(SEG_AQ + h)) * mla_scale).astype(BF16)
    ckv = seg(SEG_CKV)
    ckv = ckv * lax.rsqrt(jnp.mean(ckv * ckv, axis=-1, keepdims=True) + LN_EPS) * g_ref[...]
    kr = rope_a(seg(SEG_KR))
    lat_ref[:, :MLA_D_C] = ckv
    lat_ref[:, MLA_D_C:] = kr[:, MLA_ROPE_LANE:MLA_ROPE_LANE + MLA_D_ROPE]
    kv_a = jnp.dot(ckv.astype(BF16), wukv_ref[...], preferred_element_type=F32)
    for h in range(N_HEADS):
        ka_ref[:, h * LANES:(h + 1) * LANES] = (kv_a[:, h * LANES:(h + 1) * LANES] + kr).astype(BF16)
    va_ref[...] = (kv_a[:, N_HEADS * LANES:] + _ones_upper_half(kv_a.shape[0])).astype(BF16)

    for s0, q_ref, kv_ref, new_ref in ((SEG_SB, qb_ref, kvb_ref, sbkv_ref),
                                       (SEG_BD, qc_ref, kvc_ref, bdkv_ref)):
        q_ref[...] = (seg(s0, 2) * head_scale).astype(BF16)
        kv = seg(s0 + 2, 4)
        new_ref[...] = kv
        kv_ref[...] = kv.astype(BF16)

    for p in range(2):
        qd_ref[:, p * LANES:(p + 1) * LANES] = (rope_p(seg(SEG_DS + p)) * head_scale).astype(BF16)
        k_rot = rope_p(seg(SEG_DS + 2 + p))
        dskv_ref[:, p * LANES:(p + 1) * LANES] = k_rot
        kd_ref[:, p * LANES:(p + 1) * LANES] = k_rot.astype(BF16)
    v_d = seg(SEG_DS + 4, 2)
    dskv_ref[:, BRANCH_WIDTH:] = v_d
    vtd_ref[...] = v_d.T.astype(BF16)
    for p in range(IDX_HEADS // 2):
        qix_ref[:, p * LANES:(p + 1) * LANES] = (rope_p(seg(SEG_IXQ + p)) * IDX_DIM ** -0.5).astype(BF16)
    ixk = rope_p(seg(SEG_IXK))
    kidx_ref[...] = ixk[:, :IDX_DIM]
    kx_ref[...] = ixk[:, :IDX_DIM].astype(BF16)
    wix_ref[...] = ixk.T[IXW_LANE:IXW_LANE + IDX_HEADS, :]


def _rope_tables(pos, theta, width, starts):
    half = width // 2
    inv = jnp.exp(jnp.arange(half, dtype=F32) * (-2.0 * math.log(theta) / width))
    ang = pos.astype(F32)[:, None] * inv[None, :]
    cos, sin = jnp.cos(ang), jnp.sin(ang)
    n = pos.shape[0]
    fill = lambda v, w: jnp.full((n, w), v, F32)
    c, sm, sp, lane = [], [], [], 0
    for s in sorted(starts):
        c += [fill(1.0, s - lane), cos, cos]
        sm += [fill(0.0, s - lane), -sin, fill(0.0, half)]
        sp += [fill(0.0, s - lane + half), sin]
        lane = s + width
    cat = lambda parts, v: jnp.concatenate(parts + [fill(v, LANES - lane)], axis=1)
    return cat(c, 1.0), cat(sm, 0.0), cat(sp, 0.0)


def _project(x, pos, w_rest, w_ukv, kv_norm):
    n = x.shape[0]
    ts = _pick(n, (512, 256, 128, 64))
    tables = (_rope_tables(pos, MLA_THETA, MLA_D_ROPE, (MLA_ROPE_LANE,))
              + _rope_tables(pos, ROPE_THETA, HEAD_DIM // 4, (0, HEAD_DIM)))
    row = lambda i: (i, 0)
    fixed = lambda i: (0, 0)
    rows = lambda w: pl.BlockSpec((ts, w), row)
    out = lambda w, dt: jax.ShapeDtypeStruct((n, w), dt)
    specs = [
        (N_HEADS * LANES, BF16), (LAT_WIDTH, F32), (N_HEADS * LANES, BF16), (N_HEADS * LANES, BF16),
        (BRANCH_WIDTH, BF16), (2 * BRANCH_WIDTH, BF16), (2 * BRANCH_WIDTH, F32),
        (BRANCH_WIDTH, BF16), (2 * BRANCH_WIDTH, BF16), (2 * BRANCH_WIDTH, F32),
        (BRANCH_WIDTH, BF16), (BRANCH_WIDTH, BF16), None, (2 * BRANCH_WIDTH, F32),
        (IDX_HEADS * IDX_DIM, BF16), (IDX_DIM, F32), (IDX_DIM, BF16), None]
    out_specs, out_shape = [], []
    for k, spec in enumerate(specs):
        if spec is None:
            rows_t, dt = ((BRANCH_WIDTH, BF16), (IDX_HEADS, F32))[k > 12]
            out_specs.append(pl.BlockSpec((rows_t, ts), lambda i: (0, i)))
            out_shape.append(jax.ShapeDtypeStruct((rows_t, n), dt))
        else:
            out_specs.append(rows(spec[0]))
            out_shape.append(out(*spec))
    return pl.pallas_call(
        _proj_kernel,
        grid=(n // ts,),
        in_specs=[rows(D_MODEL),
                  pl.BlockSpec((D_MODEL, PROJ_WIDTH), fixed, pipeline_mode=pl.Buffered(1)),
                  pl.BlockSpec((MLA_D_C, 2 * N_HEADS * LANES), fixed),
                  pl.BlockSpec((1, MLA_D_C), fixed)] + [rows(LANES)] * 6,
        out_specs=out_specs,
        out_shape=out_shape,
        compiler_params=_cparams(1),
        name="project_prepare",
    )(x, w_rest, w_ukv, kv_norm, *tables)


def _mla_kernel(q_ref, k_ref, v_ref, o_ref, *, qb, kb, q_off, skp):
    q0 = pl.program_id(1) * qb
    hi = jnp.minimum((((q_off + q0 + qb - 1) >> CHUNK_SHIFT) + 1) * CHUNK, skp)
    nkb = (hi + kb - 1) // kb
    qpos = q_off + q0 + lax.broadcasted_iota(jnp.int32, (qb, 1), 0)
    cend = ((qpos >> CHUNK_SHIFT) + 1) * CHUNK
    col = lax.broadcasted_iota(jnp.int32, (qb, kb), 1)
    lane = lax.broadcasted_iota(jnp.int32, (qb, LANES), 1)
    n_full = jnp.minimum(((((q_off + q0) >> CHUNK_SHIFT) + 1) * CHUNK) // kb, nkb)

    def body(i, carry, masked):
        k0 = pl.multiple_of(i * kb, kb)
        s = [_dot_nt(q_ref[0, :, h * LANES:(h + 1) * LANES],
                     k_ref[0, pl.ds(k0, kb), h * LANES:(h + 1) * LANES]) for h in range(N_HEADS)]
        if masked:
            vis = col + k0 < cend
            s = [jnp.where(vis, sh, NEG_INF) for sh in s]
        p, new = [], []
        for h in range(N_HEADS):
            m, acc = carry[2 * h:2 * h + 2]
            m_new = jnp.maximum(m, jnp.max(s[h], axis=1, keepdims=True))
            p.append(jnp.exp2((s[h] - m_new).astype(BF16)))
            new += [m_new, jnp.exp2(m - m_new) * acc]
        for h in range(N_HEADS):
            vblk = v_ref[0, pl.ds(k0, kb), h * LANES:(h + 1) * LANES]
            new[2 * h + 1] = new[2 * h + 1] + jnp.dot(p[h], vblk, preferred_element_type=F32)
        return tuple(new)

    init = (jnp.full((qb, 1), NEG_INF, F32), jnp.zeros((qb, LANES), F32)) * N_HEADS
    carry = lax.fori_loop(0, n_full, functools.partial(body, masked=False), init)
    carry = lax.fori_loop(n_full, nkb, functools.partial(body, masked=True), carry)
    outs = [carry[2 * h + 1] / pltpu.roll(carry[2 * h + 1], HEAD_DIM, 1) for h in range(N_HEADS)]
    for pair in range(N_HEADS // 2):
        o_ref[0, :, pair * LANES:(pair + 1) * LANES] = jnp.where(
            lane < HEAD_DIM, outs[2 * pair], pltpu.roll(outs[2 * pair + 1], HEAD_DIM, 1))


def _mla_attention(q, k, v, q_off):
    b, sq, _ = q.shape
    skp = k.shape[1]
    qb = _pick(sq, (512, 256, 128, 64))
    kb = _pick(skp, (1024, KEY_ALIGN))
    kern = functools.partial(_mla_kernel, qb=qb, kb=kb, q_off=q_off, skp=skp)
    return pl.pallas_call(
        kern,
        grid=(b, sq // qb),
        in_specs=[pl.BlockSpec((1, qb, N_HEADS * LANES), lambda bi, qi: (bi, qi, 0)),
                  pl.BlockSpec((1, skp, N_HEADS * LANES), lambda bi, qi: (bi, 0, 0),
                               pipeline_mode=pl.Buffered(1)),
                  pl.BlockSpec((1, skp, N_HEADS * LANES), lambda bi, qi: (bi, 0, 0),
                               pipeline_mode=pl.Buffered(1))],
        out_specs=pl.BlockSpec((1, qb, BRANCH_WIDTH), lambda bi, qi: (bi, qi, 0)),
        out_shape=jax.ShapeDtypeStruct((b, sq, BRANCH_WIDTH), F32),
        compiler_params=_cparams(2),
        name="mla_attention",
    )(q, k, v)


def _head_mask(x_pair, h):
    lane = lax.broadcasted_iota(jnp.int32, x_pair.shape, 1)
    keep = (lane < HEAD_DIM) if h % 2 == 0 else (lane >= HEAD_DIM)
    return jnp.where(keep, x_pair, jnp.zeros_like(x_pair))


def _sb_kernel(q_ref, kv_ref, t_ref, o_ref, *, qb, kb, sub, q_off):
    q0 = pl.program_id(1) * qb
    hi = q_off + q0 + qb - 1
    nkb = (hi + kb - 1) // kb
    qpos = q_off + q0 + lax.broadcasted_iota(jnp.int32, (qb, 1), 0)
    col = lax.broadcasted_iota(jnp.int32, (qb, sub), 1)
    lane = lax.broadcasted_iota(jnp.int32, (qb, LANES), 1)
    tri = t_ref[...]
    n_full = jnp.minimum((q_off + q0) // kb, nkb)
    qh = [_head_mask(q_ref[0, :, (h // 2) * LANES:(h // 2 + 1) * LANES], h) for h in range(N_HEADS)]

    def body(i, carry, masked):
        k0 = pl.multiple_of(i * kb, kb)
        carry = list(carry)
        units = [(j, h) for j in reversed(range(kb // sub)) for h in range(N_HEADS)]
        strict = {j: col + (k0 + j * sub) < qpos for j in range(kb // sub)} if masked else None
        z = {}
        for j, h in units:
            pair = h // 2
            kblk = kv_ref[0, pl.ds(k0 + j * sub, sub), pair * LANES:(pair + 1) * LANES]
            z[j, h] = _dot_nt(qh[h], kblk)
        log_1m, suffix = {}, {}
        for u in units:
            nz = -z[u]
            t = jnp.minimum(nz, 0.0) - jnp.log2(1.0 + jnp.exp2(jnp.minimum(z[u], nz)))
            if masked:
                t = jnp.where(strict[u[0]], t, 0.0)
            log_1m[u] = t
        for u in units:
            hi_part = log_1m[u].astype(BF16)
            lo_part = (log_1m[u] - hi_part.astype(F32)).astype(BF16)
            suffix[u] = (jnp.dot(hi_part, tri, preferred_element_type=F32)
                         + jnp.dot(lo_part, tri, preferred_element_type=F32))
        for j, h in units:
            u = (j, h)
            run, acc = carry[2 * h:2 * h + 2]
            expo = z[u] + log_1m[u] + suffix[u] + run
            if masked:
                expo = jnp.where(strict[j], expo, NEG_INF)
            a = jnp.exp2(expo).astype(BF16)
            pair = h // 2
            vblk = kv_ref[0, pl.ds(k0 + j * sub, sub),
                          BRANCH_WIDTH + pair * LANES:BRANCH_WIDTH + (pair + 1) * LANES]
            carry[2 * h + 1] = acc + jnp.dot(a, vblk, preferred_element_type=F32)
            carry[2 * h] = run + jnp.sum(log_1m[u], axis=1, keepdims=True)
        return tuple(carry)

    def live(state):
        i, carry = state[0], state[1:]
        top = carry[0]
        for h in range(1, N_HEADS):
            top = jnp.maximum(top, carry[2 * h])
        return (i >= 0) & (jnp.max(top) >= SB_RUN_FLOOR)

    def step(masked):
        return lambda state: (state[0] - 1,) + body(state[0], state[1:], masked)

    init = (jnp.zeros((qb, 1), F32), jnp.zeros((qb, LANES), F32)) * N_HEADS
    carry = lax.fori_loop(0, nkb - n_full, lambda i, c: body(nkb - 1 - i, c, True), init)
    state = lax.while_loop(live, step(False), (n_full - 1,) + tuple(carry))
    carry = state[1:]
    outs = [carry[2 * h + 1] for h in range(N_HEADS)]
    for pair in range(N_HEADS // 2):
        o_ref[0, :, pair * LANES:(pair + 1) * LANES] = jnp.where(
            lane < HEAD_DIM, outs[2 * pair], outs[2 * pair + 1])


def _sb_attention(q, kv, q_off):
    b, sq, _ = q.shape
    skp = kv.shape[1]
    qb = _pick(sq, (256, 128, 64))
    sub = 256
    kb = KEY_ALIGN
    tri =jnp.asarray(np.tril(np.ones((sub, sub), np.float32), -1), BF16)
    kern = functools.partial(_sb_kernel, qb=qb, kb=kb, sub=sub, q_off=q_off)
    return pl.pallas_call(
        kern,
        grid=(b, sq // qb),
        in_specs=[pl.BlockSpec((1, qb, BRANCH_WIDTH), lambda bi, qi: (bi, qi, 0)),
                  pl.BlockSpec((1, skp, 2 * BRANCH_WIDTH), lambda bi, qi: (bi, 0, 0),
                               pipeline_mode=pl.Buffered(1)),
                  pl.BlockSpec((sub, sub), lambda bi, qi: (0, 0))],
        out_specs=pl.BlockSpec((1, qb, BRANCH_WIDTH), lambda bi, qi: (bi, qi, 0)),
        out_shape=jax.ShapeDtypeStruct((b, sq, BRANCH_WIDTH), F32),
        compiler_params=_cparams(2),
        name="stick_breaking_attention",
    )(q, kv, tri)


def _band_kernel(bias_ref, q_ref, kv_ref, o_ref, tile_ref, *, qb, win, q_off):
    first = (pl.program_id(0) == 0) & (pl.program_id(1) == 0)
    row = lax.broadcasted_iota(jnp.int32, (qb, win), 0)
    col = lax.broadcasted_iota(jnp.int32, (qb, win), 1)

    @pl.when(first)
    def _():
        width = win + qb
        c = lax.broadcasted_iota(jnp.int32, (8, width), 1)
        rel = jnp.where(c < win, jnp.clip(BAND_WINDOW - c, -REL_CLIP, REL_CLIP) + REL_CLIP, 2 * REL_CLIP)

        def fill(r, rows):
            hit = rel == r
            return tuple(jnp.where(hit, bias_ref[h, r], rows[h]) for h in range(N_HEADS))

        rows = lax.fori_loop(0, 2 * REL_CLIP + 1, fill, (jnp.zeros((8, width), F32),) * N_HEADS)
        qch = row >> CHUNK_SHIFT
        kch = col >> CHUNK_SHIFT
        in_band = (kch >= qch) & (kch <= qch + BAND_LEFT_CHUNKS)
        for h in range(N_HEADS):
            table = jnp.broadcast_to(rows[h][:1], (qb, width))
            skewed = pltpu.roll(table, 0, 1, stride=1, stride_axis=0)[:, :win]
            tile_ref[h] = jnp.where(in_band, skewed * LOG2_E, NEG_INF)

    q0 = pl.multiple_of(pl.program_id(1) * qb, qb)
    kpos = col + (q_off + q0 - BAND_WINDOW)
    lane = lax.broadcasted_iota(jnp.int32, (qb, LANES), 1)
    outs = []
    for h in range(N_HEADS):
        pair = h // 2
        qh = _head_mask(q_ref[0, :, pair * LANES:(pair + 1) * LANES], h)
        kwin = kv_ref[0, pl.ds(q0, win), pair * LANES:(pair + 1) * LANES]
        s = _dot_nt(qh, kwin) + tile_ref[h]
        s = jnp.where(kpos >= 0, s, NEG_INF)
        m = jnp.max(s, axis=1, keepdims=True)
        p = jnp.exp2(s - m)
        l = jnp.sum(p, axis=1, keepdims=True)
        vwin = kv_ref[0, pl.ds(q0, win),
                      BRANCH_WIDTH + pair * LANES:BRANCH_WIDTH + (pair + 1) * LANES]
        outs.append(jnp.dot(p.astype(BF16), vwin, preferred_element_type=F32) / l)
    for pair in range(N_HEADS // 2):
        o_ref[0, :, pair * LANES:(pair + 1) * LANES] = jnp.where(
            lane < HEAD_DIM, outs[2 * pair], outs[2 * pair + 1])


def _band_attention(q, kv, rel_bias, q_off):
    b, sq, _ = q.shape
    sk = kv.shape[1]
    qb = _pick(sq, (256, 128, 64))
    win = qb + BAND_WINDOW
    kern = functools.partial(_band_kernel, qb=qb, win=win, q_off=q_off)
    return pl.pallas_call(
        kern,
        grid=(b, sq // qb),
        in_specs=[pl.BlockSpec(memory_space=pltpu.SMEM),
                  pl.BlockSpec((1, qb, BRANCH_WIDTH), lambda bi, qi: (bi, qi, 0)),
                  pl.BlockSpec((1, sk, 2 * BRANCH_WIDTH), lambda bi, qi: (bi, 0, 0),
                               pipeline_mode=pl.Buffered(1))],
        out_specs=pl.BlockSpec((1, qb, BRANCH_WIDTH), lambda bi, qi: (bi, qi, 0)),
        out_shape=jax.ShapeDtypeStruct((b, sq, BRANCH_WIDTH), F32),
        scratch_shapes=[pltpu.VMEM((N_HEADS, qb, win), F32)],
        compiler_params=_cparams(2),
        name="band_attention",
    )(rel_bias, q, kv)


def _dsa_kernel(qi_ref, w_ref, kx_ref, q_ref, k_ref, vt_ref, o_ref, key_ref, key16_ref, cut_ref,
                sa_ref, sb_ref,
                *, qb, kb, q_off, skp, n_sel):
    q0 = pl.program_id(1) * qb
    hi = jnp.minimum((((q_off + q0 + qb - 1) >> CHUNK_SHIFT) + 1) * CHUNK, skp)
    nkb = (hi + kb - 1) // kb
    qpos = q_off + q0 + lax.broadcasted_iota(jnp.int32, (1, qb), 1)
    cend = ((qpos >> CHUNK_SHIFT) + 1) * CHUNK
    row = lax.broadcasted_iota(jnp.int32, (kb, qb), 0)
    float_key = lambda bits: jnp.where(bits < 0, INT_MIN - bits, bits)
    neg_key = float_key(lax.bitcast_convert_type(jnp.full((1, 1), NEG_INF, F32), jnp.int32))
    n_full = jnp.minimum(((((q_off + q0) >> CHUNK_SHIFT) + 1) * CHUNK) // kb, nkb)

    def score_block(i, c, masked):
        k0 = pl.multiple_of(i * kb, kb)
        kx = kx_ref[0, pl.ds(k0, kb), :]
        r = [_dot_nt(kx, qi_ref[0, :, j * IDX_DIM:(j + 1) * IDX_DIM]) for j in range(IDX_HEADS)]
        score = jnp.zeros((kb, qb), F32)
        for j in range(IDX_HEADS):
            wj = w_ref[0, j:j + 1, :] * (IDX_HEADS ** -0.5)
            score = score + wj * jnp.maximum(r[j], 0.0)
        if masked:
            score = jnp.where(row + k0 < cend, score, NEG_INF)
        keys = float_key(lax.bitcast_convert_type(score, jnp.int32))
        key_ref[pl.ds(k0, kb), :] = keys
        key16_ref[pl.ds(k0, kb), :] = (keys >> 16).astype(jnp.int16)
        return c

    lax.fori_loop(0, n_full, functools.partial(score_block, masked=False), 0)
    lax.fori_loop(n_full, nkb, functools.partial(score_block, masked=True), 0)

    slab = 64 if kb % 64 == 0 else kb

    def count_ge(cand):
        def blk(i, cnt):
            k0 = pl.multiple_of(i * kb, kb)
            for s in range(kb // slab):
                keys = key_ref[pl.ds(k0 + s * slab, slab), :]
                cnt = cnt + jnp.where(keys >= cand, 1.0, 0.0)
            return cnt
        part = lax.fori_loop(0, nkb, blk, jnp.zeros((slab, qb), F32))
        return jnp.sum(part, axis=0, keepdims=True)

    slab16 = 128 if kb % 128 == 0 else kb

    def count16_ge(cand16):
        def blk(i, cnt):
            k0 = pl.multiple_of(i * kb, kb)
            for s in range(kb // slab16):
                k16 = key16_ref[pl.ds(k0 + s * slab16, slab16), :]
                cnt = cnt + jnp.where(k16 >= cand16, jnp.int16(1), jnp.int16(0))
            return cnt
        part = lax.fori_loop(0, nkb, blk, jnp.zeros((slab16, qb), jnp.int16))
        return jnp.sum(part.astype(F32), axis=0, keepdims=True)

    def undecided(t_end):
        def cond(state):
            t, _, n_ge = state
            return (t < t_end) & (jnp.max(jnp.abs(n_ge - n_sel)) > 0.0)
        return cond

    def bisect(to_16bit):
        def step(state):
            t, thr, n_ge = state
            cand = thr + jnp.left_shift(jnp.int32(1), 31 - t)
            cnt = count16_ge(to_16bit(cand))
            ok = cnt >= n_sel
            return t + 1, jnp.where(ok, cand, thr), jnp.where(ok, cnt, n_ge)
        return step

    stored = (nkb * kb).astype(F32)
    high_step = bisect(lambda cand: (cand >> 16).astype(jnp.int16))
    state = lax.fori_loop(
        0, 16, lambda _, st: high_step(st),
        (jnp.int32(0), jnp.full((1, qb), INT_MIN, jnp.int32), jnp.full((1, qb), stored, F32)))

    @pl.when(undecided(32)(state))
    def _():
        thr_hi = state[1] >> 16

        def low_halves(i, c):
            k0 = pl.multiple_of(i * kb, kb)
            keys = key_ref[pl.ds(k0, kb), :]
            hi16 = keys >> 16
            low = jnp.where(hi16 == thr_hi, (keys & 0xFFFF) - 0x8000,
                            jnp.where(hi16 > thr_hi, 0x7FFF, -0x8000))
            key16_ref[pl.ds(k0, kb), :] = low.astype(jnp.int16)
            return c

        lax.fori_loop(0, nkb, low_halves, 0)

    low_step = bisect(lambda cand: ((cand & 0xFFFF) - 0x8000).astype(jnp.int16))
    _, thr, n_ge = lax.while_loop(
        undecided(32), lambda st: low_step(low_step(low_step(low_step(st)))), state)
    excess = jnp.where((n_ge > n_sel) & (thr > neg_key), 1.0, 0.0)
    cut_ref[...] = jnp.full((1, qb), skp, jnp.int32)

    @pl.when(jnp.max(excess) > 0.0)
    def _():
        need = n_sel - count_ge(thr + 1)

        def count_tied_below(limit):
            def blk(i, cnt):
                k0 = pl.multiple_of(i * kb, kb)
                tied = (key_ref[pl.ds(k0, kb), :] == thr) & (row + k0 < limit)
                return cnt + jnp.sum(jnp.where(tied, 1.0, 0.0), axis=0, keepdims=True)
            return lax.fori_loop(0, nkb, blk, jnp.zeros((1, qb), F32))

        n_bits = max(1, (skp - 1).bit_length())

        def bisect_cut(t, cut):
            cand = cut + jnp.left_shift(jnp.int32(1), n_bits - 1 - t)
            return jnp.where(count_tied_below(cand) < need, cand, cut)

        cut = lax.fori_loop(0, n_bits, bisect_cut, jnp.zeros((1, qb), jnp.int32))
        cut_ref[...] = jnp.where(excess > 0.0, cut + 1, skp)

    cut = cut_ref[...]

    qh = [_head_mask(q_ref[0, :, (h // 2) * LANES:(h // 2 + 1) * LANES], h) for h in range(N_HEADS)]

    kb3 = kb // 2
    row3 = lax.broadcasted_iota(jnp.int32, (kb3, qb), 0)

    def scores(i):
        k0 = pl.multiple_of(i * kb3, kb3)
        return tuple(_dot_nt(k_ref[0, pl.ds(k0, kb3), (h // 2) * LANES:(h // 2 + 1) * LANES], qh[h])
                     for h in range(N_HEADS))

    def attend(i, s, carry):
        k0 = pl.multiple_of(i * kb3, kb3)
        keys = key_ref[pl.ds(k0, kb3), :]
        kpos = row3 + k0
        sel = (keys >= thr) & ((keys != thr) | (kpos < cut)) & (kpos < cend)
        bias = jnp.where(sel, 0.0, NEG_INF)
        p, new = [], []
        for h in range(N_HEADS):
            m, acc = carry[2 * h:2 * h + 2]
            sh = s[h] + bias
            m_new = jnp.maximum(m, jnp.max(sh, axis=0, keepdims=True))
            p.append(jnp.exp2((sh - m_new).astype(BF16)))
            new += [m_new, jnp.exp2(m - m_new) * acc]
        for h in range(N_HEADS):
            vblk = jnp.concatenate([vt_ref[0, h * HEAD_DIM:(h + 1) * HEAD_DIM, pl.ds(k0, kb3)],
                                    jnp.ones((LANES - HEAD_DIM, kb3), BF16)], axis=0)
            new[2 * h + 1] = new[2 * h + 1] + jnp.dot(vblk, p[h], preferred_element_type=F32)
        return tuple(new)

    def put_scores(slot_ref, i):
        for h, sh in enumerate(scores(i)):
            slot_ref[h] = sh

    def attend_from(slot_ref, i, carry):
        return attend(i, tuple(slot_ref[h] for h in range(N_HEADS)), carry)

    def body(j, carry):
        put_scores(sb_ref, 2 * j + 1)
        carry = attend_from(sa_ref, 2 * j, carry)
        put_scores(sa_ref, 2 * j + 2)
        return attend_from(sb_ref, 2 * j + 1, carry)

    init = (jnp.full((1, qb), NEG_INF, F32), jnp.zeros((LANES, qb), F32)) * N_HEADS
    n_pairs = nkb * (kb // (2 * kb3))
    put_scores(sa_ref, 0)
    carry = lax.fori_loop(0, n_pairs - 1, body, init)
    last = 2 * (n_pairs - 1)
    put_scores(sb_ref, last + 1)
    carry = attend_from(sa_ref, last, carry)
    carry = attend_from(sb_ref, last + 1, carry)
    for h in range(N_HEADS):
        acc = carry[2 * h + 1]
        o_ref[0, h * HEAD_DIM:(h + 1) * HEAD_DIM, :] = acc[:HEAD_DIM, :] / acc[HEAD_DIM:HEAD_DIM + 1, :]


def _dsa_attention(qi, w, kx, q, k, vt, q_off, n_sel):
    b, sq, _ = q.shape
    skp = k.shape[1]
    qb = _pick(sq, (256, 128, 64))
    kb = _pick(skp, (1024, KEY_ALIGN))
    kern = functools.partial(_dsa_kernel, qb=qb, kb=kb, q_off=q_off, skp=skp, n_sel=n_sel)
    return pl.pallas_call(
        kern,
        grid=(b, sq // qb),
        in_specs=[pl.BlockSpec((1, qb, IDX_HEADS * IDX_DIM), lambda bi, qi_: (bi, qi_, 0)),
                  pl.BlockSpec((1, IDX_HEADS, qb), lambda bi, qi_: (bi, 0, qi_)),
                  pl.BlockSpec((1, skp, IDX_DIM), lambda bi, qi_: (bi, 0, 0),
                               pipeline_mode=pl.Buffered(1)),
                  pl.BlockSpec((1, qb, BRANCH_WIDTH), lambda bi, qi_: (bi, qi_, 0)),
                  pl.BlockSpec((1, skp, BRANCH_WIDTH), lambda bi, qi_: (bi, 0, 0),
                               pipeline_mode=pl.Buffered(1)),
                  pl.BlockSpec((1, BRANCH_WIDTH, skp), lambda bi, qi_: (bi, 0, 0),
                               pipeline_mode=pl.Buffered(1))],
        out_specs=pl.BlockSpec((1, BRANCH_WIDTH, qb), lambda bi, qi_: (bi, 0, qi_)),
        out_shape=jax.ShapeDtypeStruct((b, BRANCH_WIDTH, sq), F32),
        scratch_shapes=[pltpu.VMEM((skp, qb), jnp.int32), pltpu.VMEM((skp, qb), jnp.int16),
                        pltpu.VMEM((1, qb), jnp.int32),
                        pltpu.VMEM((N_HEADS, kb // 2, qb), F32), pltpu.VMEM((N_HEADS, kb // 2, qb), F32)],
        compiler_params=_cparams(2),
        name="dsa_attention",
    )(qi, w, kx, q, k, vt)


def _layer_norm(z, g, b):
    mu = jnp.mean(z, axis=-1, keepdims=True)
    zc = z - mu
    var = jnp.mean(zc * zc, axis=-1, keepdims=True)
    return zc * lax.rsqrt(var + LN_EPS) * g + b


def _merge_kernel(x_ref, wgate_ref, oa_ref, ob_ref, oc_ref, od_ref, wb_ref, wo_ref, g_ref, b_ref,
                  y_ref):
    xb = x_ref[...].astype(BF16)
    logits = [jnp.dot(xb, wgate_ref[:, n * D_MODEL:(n + 1) * D_MODEL], preferred_element_type=F32)
              for n in range(N_BRANCH)]
    branch = [jnp.dot(o_ref[...].astype(BF16), wb_ref[n], preferred_element_type=F32)
              for n, o_ref in enumerate((oa_ref, ob_ref, oc_ref, od_ref))]
    merged = jax.nn.sigmoid(logits[0]) * branch[0]
    for n in range(1, N_BRANCH):
        merged = merged + jax.nn.sigmoid(logits[n]) * branch[n]
    y = jnp.dot(merged.astype(BF16), wo_ref[...], preferred_element_type=F32)
    y_ref[...] = _layer_norm(ALPHA * x_ref[...] + y, g_ref[...], b_ref[...])


def _merge_out_ln(x, w_gate, o_a, o_b, o_c, o_d, w_branch, w_out, g, b):
    n = x.shape[0]
    ts = _pick(n, (512, 256, 128, 64))
    row = lambda i: (i, 0)
    obs = pl.BlockSpec((ts, BRANCH_WIDTH), row)
    return pl.pallas_call(
        _merge_kernel,
        grid=(n // ts,),
        in_specs=[pl.BlockSpec((ts, D_MODEL), row),
                  pl.BlockSpec((D_MODEL, GATE_WIDTH), lambda i: (0, 0), pipeline_mode=pl.Buffered(1)),
                  obs, obs, obs, obs,
                  pl.BlockSpec((N_BRANCH, BRANCH_WIDTH, D_MODEL), lambda i: (0, 0, 0)),
                  pl.BlockSpec((D_MODEL, D_MODEL), lambda i: (0, 0)),
                  pl.BlockSpec((1, D_MODEL), lambda i: (0, 0)),
                  pl.BlockSpec((1, D_MODEL), lambda i: (0, 0))],
        out_specs=pl.BlockSpec((ts, D_MODEL), row),
        out_shape=jax.ShapeDtypeStruct((n, D_MODEL), F32),
        compiler_params=_cparams(1),
        name="merge_out_ln",
    )(x, w_gate, o_a, o_b, o_c, o_d, w_branch, w_out, g, b)


def _ffn_kernel(x_ref, wg_ref, wu_ref, wd_ref, g_ref, b_ref, y_ref, acc_ref):
    f = pl.program_id(1)
    xb = x_ref[...].astype(BF16)
    gate = jnp.dot(xb, wg_ref[...], preferred_element_type=F32)
    up = jnp.dot(xb, wu_ref[...], preferred_element_type=F32)
    hidden = (gate * jax.nn.sigmoid(gate) * up).astype(BF16)
    part = jnp.dot(hidden, wd_ref[...], preferred_element_type=F32)

    @pl.when(f == 0)
    def _():
        acc_ref[...] = part

    @pl.when(f > 0)
    def _():
        acc_ref[...] += part

    @pl.when(f == pl.num_programs(1) - 1)
    def _():
        y_ref[...] = _layer_norm(ALPHA * x_ref[...] + acc_ref[...], g_ref[...], b_ref[...])


def _ffn_ln(x, w_gate, w_up, w_down, g, b):
    n = x.shape[0]
    ts = _pick(n, (1024, 512, 256, 128, 64))
    tf = D_FF // 2
    return pl.pallas_call(
        _ffn_kernel,
        grid=(n // ts, D_FF // tf),
        in_specs=[pl.BlockSpec((ts, D_MODEL), lambda i, f: (i, 0)),
                  pl.BlockSpec((D_MODEL, tf), lambda i, f: (0, f)),
                  pl.BlockSpec((D_MODEL, tf), lambda i, f: (0, f)),
                  pl.BlockSpec((tf, D_MODEL), lambda i, f: (f, 0)),
                  pl.BlockSpec((1, D_MODEL), lambda i, f: (0, 0)),
                  pl.BlockSpec((1, D_MODEL), lambda i, f: (0, 0))],
        out_specs=pl.BlockSpec((ts, D_MODEL), lambda i, f: (i, 0)),
        out_shape=jax.ShapeDtypeStruct((n, D_MODEL), F32),
        scratch_shapes=[pltpu.VMEM((ts, D_MODEL), F32)],
        compiler_params=_cparams(2),
        name="ffn_ln",
    )(x, w_gate, w_up, w_down, g, b)


def _pad_keys(a, axis=1):
    n = a.shape[axis]
    pad = (-n) % KEY_ALIGN
    if pad == 0:
        return a
    widths = [(0, 0)] * a.ndim
    widths[axis] = (0, pad)
    return jnp.pad(a, widths)


def _prepare_weights(w_in, mla_w_uk, mla_w_uv, w_branch, w_out, w_gate_up, w_down):
    w_rest = jnp.concatenate([w_in, jnp.zeros((D_MODEL, 1), w_in.dtype)], axis=1)
    w_rest = jnp.take(w_rest, _projection_columns(), axis=1).astype(BF16)
    w_uk = jnp.pad(mla_w_uk, ((0, 0), (0, 0), (0, LANES - MLA_D_NOPE))).reshape(MLA_D_C, -1)
    w_uv = jnp.pad(mla_w_uv, ((0, 0), (0, 0), (0, LANES - HEAD_DIM))).reshape(MLA_D_C, -1)
    w_ukv = jnp.concatenate([w_uk, w_uv], axis=1).astype(BF16)
    place = np.zeros((MLA_D_ROPE, 2 * N_HEADS * LANES), np.float32)
    for h in range(N_HEADS):
        place[np.arange(MLA_D_ROPE), h * LANES + MLA_ROPE_LANE + np.arange(MLA_D_ROPE)] = 1.0
    w_lat = jnp.concatenate([w_ukv, jnp.asarray(place, BF16)], axis=0)
    return dict(w_rest=w_rest, w_gates=w_in[:, REST_WIDTH:].astype(BF16), w_ukv=w_ukv, w_lat=w_lat,
                w_branch=w_branch.astype(BF16),
                w_out=w_out.astype(BF16), w_gate=w_gate_up[:, :D_FF].astype(BF16),
                w_up=w_gate_up[:, D_FF:].astype(BF16), w_down=w_down.astype(BF16))


def _projection_columns():
    zero_col = REST_WIDTH + GATE_WIDTH
    src = np.full(PROJ_WIDTH, zero_col, np.int32)
    o_aq, o_ckv, o_kr, o_sb, o_bd, o_ds, o_ixq, o_ixk, o_ixw = np.concatenate(
        [[0], np.cumsum(IN_SIZES[:-1])])[:9]
    hd = MLA_D_NOPE + MLA_D_ROPE

    def put(lane0, src0, n):
        src[lane0:lane0 + n] = src0 + np.arange(n)

    for h in range(N_HEADS):
        put((SEG_AQ + h) * LANES, o_aq + h * hd, hd)
    put(SEG_CKV * LANES, o_ckv, MLA_D_C)
    put(SEG_KR * LANES + MLA_ROPE_LANE, o_kr, MLA_D_ROPE)
    put(SEG_SB * LANES, o_sb, 3 * BRANCH_WIDTH)
    put(SEG_BD * LANES, o_bd, 3 * BRANCH_WIDTH)
    put(SEG_DS * LANES, o_ds, 3 * BRANCH_WIDTH)
    put(SEG_IXQ * LANES, o_ixq, IDX_HEADS * IDX_DIM)
    put(SEG_IXK * LANES, o_ixk, IDX_DIM)
    put(SEG_IXK * LANES + IXW_LANE, o_ixw, IDX_HEADS)
    return src


def _trunk_layer(x, q_off, past, wts, mla_kv_norm, band_rel_bias, ln1_g, ln1_b, ln2_g, ln2_b):
    bsz, s_len, _ = x.shape
    n = bsz * s_len
    pos = jnp.tile(q_off + jnp.arange(s_len, dtype=jnp.int32), bsz)
    x2 = x.reshape(n, D_MODEL)
    (q_a, lat_new, k_a, v_a, q_b, kv_b, sb_kv_new, q_c, kv_c, bd_kv_new,
     q_d, k_d, vt_d, ds_kv_new, q_ix, kidx_new, kx_d, w_ix) = _project(
        x2, pos, wts['w_rest'], wts['w_ukv'], mla_kv_norm[None])
    seq = lambda a: a.reshape(bsz, s_len, a.shape[-1])
    seq_t = lambda a: jnp.transpose(a.reshape(a.shape[0], bsz, s_len), (1, 0, 2))
    q_a, k_a, v_a, q_b, kv_b, q_c, kv_c, q_d, k_d, q_ix, kx_d = map(
        seq, (q_a, k_a, v_a, q_b, kv_b, q_c, kv_c, q_d, k_d, q_ix, kx_d))
    vt_d, w_ix = seq_t(vt_d), seq_t(w_ix)
    kv_state = lambda a: a.reshape(bsz, s_len, 2, N_HEADS, HEAD_DIM)
    lat_new, kidx_new = seq(lat_new), seq(kidx_new)
    sb_kv_new, bd_kv_new, ds_kv_new = kv_state(sb_kv_new), kv_state(bd_kv_new), kv_state(ds_kv_new)

    if past is None:
        kv_c = jnp.pad(kv_c, ((0, 0), (BAND_WINDOW, 0), (0, 0)))
        band_rows = bd_kv_new[:, s_len - min(BAND_WINDOW, s_len):]
        s_k = s_len
    else:
        past_lat, past_sb, past_band, past_ds, past_kidx = past
        p_len = past_lat.shape[1]
        s_k = p_len + s_len
        rows_bf16 = lambda a: a.reshape(bsz, a.shape[1], -1).astype(BF16)
        k_past, v_past = _latent_keys_values(past_lat.reshape(bsz * p_len, LAT_WIDTH), wts['w_lat'])
        k_a = jnp.concatenate([k_past.reshape(bsz, p_len, -1), k_a], axis=1)
        v_a = jnp.concatenate([v_past.reshape(bsz, p_len, -1), v_a], axis=1)
        kv_b = jnp.concatenate([rows_bf16(past_sb), kv_b], axis=1)
        kv_c = jnp.concatenate([rows_bf16(past_band), kv_c], axis=1)
        past_ds = past_ds.reshape(bsz, p_len, 2 * BRANCH_WIDTH)
        k_d = jnp.concatenate([past_ds[..., :BRANCH_WIDTH].astype(BF16), k_d], axis=1)
        vt_d = jnp.concatenate(
            [jnp.transpose(past_ds[..., BRANCH_WIDTH:], (0, 2, 1)).astype(BF16), vt_d], axis=2)
        kx_d = jnp.concatenate([past_kidx.astype(BF16), kx_d], axis=1)
        band_rows = bd_kv_new

    o_a = _mla_attention(q_a, _pad_keys(k_a), _pad_keys(v_a), q_off)
    o_b = _sb_attention(q_b, _pad_keys(kv_b), q_off)
    o_c = _band_attention(q_c, kv_c, band_rel_bias, q_off)
    n_sel = min(DSA_TOPK, s_k // 4)
    o_d = _dsa_attention(q_ix, w_ix, _pad_keys(kx_d), q_d, _pad_keys(k_d), _pad_keys(vt_d, axis=2),
                         q_off, n_sel)
    o_d = jnp.transpose(o_d, (0, 2, 1))

    flat = lambda o: o.reshape(n, BRANCH_WIDTH)
    x1 = _merge_out_ln(x2, wts['w_gates'], flat(o_a), flat(o_b), flat(o_c), flat(o_d),
                       wts['w_branch'], wts['w_out'], ln1_g[None], ln1_b[None])
    x_out = _ffn_ln(x1, wts['w_gate'], wts['w_up'], wts['w_down'], ln2_g[None], ln2_b[None])
    return x_out.reshape(bsz, s_len, D_MODEL), (lat_new, sb_kv_new, band_rows, ds_kv_new, kidx_new)


def kernel(x_prompt, x_sample, cache_mla_latent, cache_sb_kv, cache_band_kv, cache_dsa_kv, cache_dsa_kidx, w_in, mla_kv_norm, mla_w_uk, mla_w_uv, band_rel_bias, w_branch, w_out, ln1_g, ln1_b, w_gate_up, w_down, ln2_g, ln2_b):
    past_len = cache_mla_latent.shape[2]
    xp, xs = x_prompt, x_sample
    st_p, st_s = [], []
    for l in range(w_in.shape[0]):
        wts = _prepare_weights(w_in[l], mla_w_uk[l], mla_w_uv[l], w_branch[l], w_out[l],
                               w_gate_up[l], w_down[l])
        params = (wts, mla_kv_norm[l], band_rel_bias[l], ln1_g[l], ln1_b[l], ln2_g[l], ln2_b[l])
        xp, new_p = _trunk_layer(xp, 0, None, *params)
        past = (cache_mla_latent[l], cache_sb_kv[l], cache_band_kv[l], cache_dsa_kv[l],
                cache_dsa_kidx[l])
        xs, new_s = _trunk_layer(xs, past_len, past, *params)
        st_p.append(new_p)
        st_s.append(new_s)
    stack = lambda st, i: jnp.stack([s[i] for s in st])
    return (xp, xs) + tuple(stack(st_p, i) for i in range(5)) + tuple(stack(st_s, i) for i in range(5))
```

```python
import functools
import math

import numpy as np
import jax
import jax.numpy as jnp
from jax import lax
from jax.experimental import pallas as pl
from jax.experimental.pallas import tpu as pltpu

D_MODEL = 1024
CHUNK = 64
CHUNK_SHIFT = 6
N_BRANCH = 4
N_HEADS = 4
HEAD_DIM = 64
BRANCH_WIDTH = N_HEADS * HEAD_DIM
MLA_D_C = 128
MLA_D_NOPE = 64
MLA_D_ROPE = 32
MLA_THETA = 10000.0
ROPE_THETA = 500000.0
BAND_LEFT_CHUNKS = 8
BAND_WINDOW = BAND_LEFT_CHUNKS * CHUNK
REL_CLIP = 128
IDX_HEADS = 8
IDX_DIM = 64
DSA_TOPK = 256
D_FF = ((8 * D_MODEL // 3 + 255) // 256) * 256
DEPTH = 2
ALPHA = (2 * DEPTH) ** 0.25
NEG_INF = -1e30
LOG2_E = math.log2(math.e)
SB_RUN_FLOOR = -150.0
LN_EPS = 1e-5
IN_SIZES = (N_HEADS * (MLA_D_NOPE + MLA_D_ROPE), MLA_D_C, MLA_D_ROPE,
            3 * BRANCH_WIDTH, 3 * BRANCH_WIDTH, 3 * BRANCH_WIDTH,
            IDX_HEADS * IDX_DIM, IDX_DIM, IDX_HEADS, N_BRANCH * D_MODEL)
REST_WIDTH = sum(IN_SIZES[:-1])
GATE_WIDTH = IN_SIZES[-1]

LANES = 128
KEY_ALIGN = 512
INT_MIN = -2 ** 31
VMEM_LIMIT = 56 * 1024 * 1024

F32 = jnp.float32
BF16 = jnp.bfloat16


def _cparams(n_axes):
    return pltpu.CompilerParams(dimension_semantics=("arbitrary",) * n_axes,
                                vmem_limit_bytes=VMEM_LIMIT)


def _pick(n, candidates):
    for c in candidates:
        if n % c == 0:
            return c
    return n


def _ones_upper_half(rows):
    lane = lax.broadcasted_iota(jnp.int32, (rows, N_HEADS * LANES), 1)
    return jnp.where((lane & (LANES - 1)) >= HEAD_DIM, 1.0, 0.0)


def _dot_nt(a, b):
    return lax.dot_general(a, b, (((1,), (1,)), ((), ())), preferred_element_type=F32)


def _latent_kernel(lat_ref, w_ref, k_ref, v_ref):
    res = jnp.dot(lat_ref[...].astype(BF16), w_ref[...], preferred_element_type=F32)
    k_ref[...] = res[:, :N_HEADS * LANES].astype(BF16)
    v_ref[...] = (res[:, N_HEADS * LANES:] + _ones_upper_half(res.shape[0])).astype(BF16)


def _latent_keys_values(lat, w_lat):
    m = lat.shape[0]
    tm = _pick(m, (1024, 512, 256, 128, 64, 32, 16, 8))
    width = N_HEADS * LANES
    return pl.pallas_call(
        _latent_kernel,
        grid=(m // tm,),
        in_specs=[pl.BlockSpec((tm, LAT_WIDTH), lambda i: (i, 0)),
                  pl.BlockSpec((LAT_WIDTH, 2 * width), lambda i: (0, 0))],
        out_specs=[pl.BlockSpec((tm, width), lambda i: (i, 0))] * 2,
        out_shape=[jax.ShapeDtypeStruct((m, width), BF16)] * 2,
        compiler_params=_cparams(1),
        name="latent_keys_values",
    )(lat, w_lat)


SEG_AQ = 0
SEG_CKV = 4
SEG_KR = 5
SEG_SB = 6
SEG_BD = 12
SEG_DS = 18
SEG_IXQ = 24
SEG_IXK = 28
N_SEG = 30
IXW_LANE = 96
PROJ_WIDTH = N_SEG * LANES
MLA_ROPE_LANE = 64
LAT_WIDTH = MLA_D_C + MLA_D_ROPE


def _proj_kernel(x_ref, w_ref, wukv_ref, g_ref, ca_ref, sma_ref, spa_ref, cp_ref, smp_ref, spp_ref,
                 qa_ref, lat_ref, ka_ref, va_ref,
                 qb_ref, kvb_ref, sbkv_ref, qc_ref, kvc_ref, bdkv_ref,
                 qd_ref, kd_ref, vtd_ref, dskv_ref, qix_ref, kidx_ref, kx_ref, wix_ref):
    proj = jnp.dot(x_ref[...].astype(BF16), w_ref[...], preferred_element_type=F32)
    seg = lambda s, n=1: proj[:, s * LANES:(s + n) * LANES]

    def rope(t, c_ref, sm_ref, sp_ref, half):
        return (t * c_ref[...] + pltpu.roll(t, LANES - half, 1) * sm_ref[...]
                + pltpu.roll(t, half, 1) * sp_ref[...])

    rope_a = functools.partial(rope, c_ref=ca_ref, sm_ref=sma_ref, sp_ref=spa_ref, half=MLA_D_ROPE // 2)
    rope_p = functools.partial(rope, c_ref=cp_ref, sm_ref=smp_ref, sp_ref=spp_ref, half=HEAD_DIM // 8)
    head_scale = HEAD_DIM ** -0.5 * LOG2_E

    mla_scale = (MLA_D_NOPE + MLA_D_ROPE) ** -0.5 * LOG2_E
    for h in range(N_HEADS):
        qa_ref[:, h * LANES:(h + 1) * LANES] = (rope_a(seg(SEG_AQ + h)) * mla_scale).astype(BF16)
    ckv = seg(SEG_CKV)
    ckv = ckv * lax.rsqrt(jnp.mean(ckv * ckv, axis=-1, keepdims=True) + LN_EPS) * g_ref[...]
    kr = rope_a(seg(SEG_KR))
    lat_ref[:, :MLA_D_C] = ckv
    lat_ref[:, MLA_D_C:] = kr[:, MLA_ROPE_LANE:MLA_ROPE_LANE + MLA_D_ROPE]
    kv_a = jnp.dot(ckv.astype(BF16), wukv_ref[...], preferred_element_type=F32)
    for h in range(N_HEADS):
        ka_ref[:, h * LANES:(h + 1) * LANES] = (kv_a[:, h * LANES:(h + 1) * LANES] + kr).astype(BF16)
    va_ref[...] = (kv_a[:, N_HEADS * LANES:] + _ones_upper_half(kv_a.shape[0])).astype(BF16)

    for s0, q_ref, kv_ref, new_ref in ((SEG_SB, qb_ref, kvb_ref, sbkv_ref),
                                       (SEG_BD, qc_ref, kvc_ref, bdkv_ref)):
        q_ref[...] = (seg(s0, 2) * head_scale).astype(BF16)
        kv = seg(s0 + 2, 4)
        new_ref[...] = kv
        kv_ref[...] = kv.astype(BF16)

    for p in range(2):
        qd_ref[:, p * LANES:(p + 1) * LANES] = (rope_p(seg(SEG_DS + p)) * head_scale).astype(BF16)
        k_rot = rope_p(seg(SEG_DS + 2 + p))
        dskv_ref[:, p * LANES:(p + 1) * LANES] = k_rot
        kd_ref[:, p * LANES:(p + 1) * LANES] = k_rot.astype(BF16)
    v_d = seg(SEG_DS + 4, 2)
    dskv_ref[:, BRANCH_WIDTH:] = v_d
    vtd_ref[...] = v_d.T.astype(BF16)
    for p in range(IDX_HEADS // 2):
        qix_ref[:, p * LANES:(p + 1) * LANES] = (rope_p(seg(SEG_IXQ + p)) * IDX_DIM ** -0.5).astype(BF16)
    ixk = rope_p(seg(SEG_IXK))
    kidx_ref[...] = ixk[:, :IDX_DIM]
    kx_ref[...] = ixk[:, :IDX_DIM].astype(BF16)
    wix_ref[...] = ixk.T[IXW_LANE:IXW_LANE + IDX_HEADS, :]


def _rope_tables(pos, theta, width, starts):
    half = width // 2
    inv = jnp.exp(jnp.arange(half, dtype=F32) * (-2.0 * math.log(theta) / width))
    ang = pos.astype(F32)[:, None] * inv[None, :]
    cos, sin = jnp.cos(ang), jnp.sin(ang)
    n = pos.shape[0]
    fill = lambda v, w: jnp.full((n, w), v, F32)
    c, sm, sp, lane = [], [], [], 0
    for s in sorted(starts):
        c += [fill(1.0, s - lane), cos, cos]
        sm += [fill(0.0, s - lane), -sin, fill(0.0, half)]
        sp += [fill(0.0, s - lane + half), sin]
        lane = s + width
    cat = lambda parts, v: jnp.concatenate(parts + [fill(v, LANES - lane)], axis=1)
    return cat(c, 1.0), cat(sm, 0.0), cat(sp, 0.0)


def _project(x, pos, w_rest, w_ukv, kv_norm):
    n = x.shape[0]
    ts = _pick(n, (512, 256, 128, 64))
    tables = (_rope_tables(pos, MLA_THETA, MLA_D_ROPE, (MLA_ROPE_LANE,))
              + _rope_tables(pos, ROPE_THETA, HEAD_DIM // 4, (0, HEAD_DIM)))
    row = lambda i: (i, 0)
    fixed = lambda i: (0, 0)
    rows = lambda w: pl.BlockSpec((ts, w), row)
    out = lambda w, dt: jax.ShapeDtypeStruct((n, w), dt)
    specs = [
        (N_HEADS * LANES, BF16), (LAT_WIDTH, F32), (N_HEADS * LANES, BF16), (N_HEADS * LANES, BF16),
        (BRANCH_WIDTH, BF16), (2 * BRANCH_WIDTH, BF16), (2 * BRANCH_WIDTH, F32),
        (BRANCH_WIDTH, BF16), (2 * BRANCH_WIDTH, BF16), (2 * BRANCH_WIDTH, F32),
        (BRANCH_WIDTH, BF16), (BRANCH_WIDTH, BF16), None, (2 * BRANCH_WIDTH, F32),
        (IDX_HEADS * IDX_DIM, BF16), (IDX_DIM, F32), (IDX_DIM, BF16), None]
    out_specs, out_shape = [], []
    for k, spec in enumerate(specs):
        if spec is None:
            rows_t, dt = ((BRANCH_WIDTH, BF16), (IDX_HEADS, F32))[k > 12]
            out_specs.append(pl.BlockSpec((rows_t, ts), lambda i: (0, i)))
            out_shape.append(jax.ShapeDtypeStruct((rows_t, n), dt))
        else:
            out_specs.append(rows(spec[0]))
            out_shape.append(out(*spec))
    return pl.pallas_call(
        _proj_kernel,
        grid=(n // ts,),
        in_specs=[rows(D_MODEL),
                  pl.BlockSpec((D_MODEL, PROJ_WIDTH), fixed, pipeline_mode=pl.Buffered(1)),
                  pl.BlockSpec((MLA_D_C, 2 * N_HEADS * LANES), fixed),
                  pl.BlockSpec((1, MLA_D_C), fixed)] + [rows(LANES)] * 6,
        out_specs=out_specs,
        out_shape=out_shape,
        compiler_params=_cparams(1),
        name="project_prepare",
    )(x, w_rest, w_ukv, kv_norm, *tables)


def _mla_kernel(q_ref, k_ref, v_ref, o_ref, *, qb, kb, q_off, skp):
    q0 = pl.program_id(1) * qb
    hi = jnp.minimum((((q_off + q0 + qb - 1) >> CHUNK_SHIFT) + 1) * CHUNK, skp)
    nkb = (hi + kb - 1) // kb
    qpos = q_off + q0 + lax.broadcasted_iota(jnp.int32, (qb, 1), 0)
    cend = ((qpos >> CHUNK_SHIFT) + 1) * CHUNK
    col = lax.broadcasted_iota(jnp.int32, (qb, kb), 1)
    lane = lax.broadcasted_iota(jnp.int32, (qb, LANES), 1)
    n_full = jnp.minimum(((((q_off + q0) >> CHUNK_SHIFT) + 1) * CHUNK) // kb, nkb)

    def body(i, carry, masked):
        k0 = pl.multiple_of(i * kb, kb)
        s = [_dot_nt(q_ref[0, :, h * LANES:(h + 1) * LANES],
                     k_ref[0, pl.ds(k0, kb), h * LANES:(h + 1) * LANES]) for h in range(N_HEADS)]
        if masked:
            vis = col + k0 < cend
            s = [jnp.where(vis, sh, NEG_INF) for sh in s]
        p, new = [], []
        for h in range(N_HEADS):
            m, acc = carry[2 * h:2 * h + 2]
            m_new = jnp.maximum(m, jnp.max(s[h], axis=1, keepdims=True))
            p.append(jnp.exp2((s[h] - m_new).astype(BF16)))
            new += [m_new, jnp.exp2(m - m_new) * acc]
        for h in range(N_HEADS):
            vblk = v_ref[0, pl.ds(k0, kb), h * LANES:(h + 1) * LANES]
            new[2 * h + 1] = new[2 * h + 1] + jnp.dot(p[h], vblk, preferred_element_type=F32)
        return tuple(new)

    init = (jnp.full((qb, 1), NEG_INF, F32), jnp.zeros((qb, LANES), F32)) * N_HEADS
    carry = lax.fori_loop(0, n_full, functools.partial(body, masked=False), init)
    carry = lax.fori_loop(n_full, nkb, functools.partial(body, masked=True), carry)
    outs = [carry[2 * h + 1] / pltpu.roll(carry[2 * h + 1], HEAD_DIM, 1) for h in range(N_HEADS)]
    for pair in range(N_HEADS // 2):
        o_ref[0, :, pair * LANES:(pair + 1) * LANES] = jnp.where(
            lane < HEAD_DIM, outs[2 * pair], pltpu.roll(outs[2 * pair + 1], HEAD_DIM, 1))


def _mla_attention(q, k, v, q_off):
    b, sq, _ = q.shape
    skp = k.shape[1]
    qb = _pick(sq, (512, 256, 128, 64))
    kb = _pick(skp, (1024, KEY_ALIGN))
    kern = functools.partial(_mla_kernel, qb=qb, kb=kb, q_off=q_off, skp=skp)
    return pl.pallas_call(
        kern,
        grid=(b, sq // qb),
        in_specs=[pl.BlockSpec((1, qb, N_HEADS * LANES), lambda bi, qi: (bi, qi, 0)),
                  pl.BlockSpec((1, skp, N_HEADS * LANES), lambda bi, qi: (bi, 0, 0),
                               pipeline_mode=pl.Buffered(1)),
                  pl.BlockSpec((1, skp, N_HEADS * LANES), lambda bi, qi: (bi, 0, 0),
                               pipeline_mode=pl.Buffered(1))],
        out_specs=pl.BlockSpec((1, qb, BRANCH_WIDTH), lambda bi, qi: (bi, qi, 0)),
        out_shape=jax.ShapeDtypeStruct((b, sq, BRANCH_WIDTH), F32),
        compiler_params=_cparams(2),
        name="mla_attention",
    )(q, k, v)


def _head_mask(x_pair, h):
    lane = lax.broadcasted_iota(jnp.int32, x_pair.shape, 1)
    keep = (lane < HEAD_DIM) if h % 2 == 0 else (lane >= HEAD_DIM)
    return jnp.where(keep, x_pair, jnp.zeros_like(x_pair))


def _sb_kernel(q_ref, kv_ref, t_ref, o_ref, *, qb, kb, sub, q_off):
    q0 = pl.program_id(1) * qb
    hi = q_off + q0 + qb - 1
    nkb = (hi + kb - 1) // kb
    qpos = q_off + q0 + lax.broadcasted_iota(jnp.int32, (qb, 1), 0)
    col = lax.broadcasted_iota(jnp.int32, (qb, sub), 1)
    lane = lax.broadcasted_iota(jnp.int32, (qb, LANES), 1)
    tri = t_ref[...]
    n_full = jnp.minimum((q_off + q0) // kb, nkb)
    qh = [_head_mask(q_ref[0, :, (h // 2) * LANES:(h // 2 + 1) * LANES], h) for h in range(N_HEADS)]

    def body(i, carry, masked):
        k0 = pl.multiple_of(i * kb, kb)
        carry = list(carry)
        units = [(j, h) for j in reversed(range(kb // sub)) for h in range(N_HEADS)]
        strict = {j: col + (k0 + j * sub) < qpos for j in range(kb // sub)} if masked else None
        z = {}
        for j, h in units:
            pair = h // 2
            kblk = kv_ref[0, pl.ds(k0 + j * sub, sub), pair * LANES:(pair + 1) * LANES]
            z[j, h] = _dot_nt(qh[h], kblk)
        log_1m, suffix = {}, {}
        for u in units:
            nz = -z[u]
            t = jnp.minimum(nz, 0.0) - jnp.log2(1.0 + jnp.exp2(jnp.minimum(z[u], nz)))
            if masked:
                t = jnp.where(strict[u[0]], t, 0.0)
            log_1m[u] = t
        for u in units:
            hi_part = log_1m[u].astype(BF16)
            lo_part = (log_1m[u] - hi_part.astype(F32)).astype(BF16)
            suffix[u] = (jnp.dot(hi_part, tri, preferred_element_type=F32)
                         + jnp.dot(lo_part, tri, preferred_element_type=F32))
        for j, h in units:
            u = (j, h)
            run, acc = carry[2 * h:2 * h + 2]
            expo = z[u] + log_1m[u] + suffix[u] + run
            if masked:
                expo = jnp.where(strict[j], expo, NEG_INF)
            a = jnp.exp2(expo).astype(BF16)
            pair = h // 2
            vblk = kv_ref[0, pl.ds(k0 + j * sub, sub),
                          BRANCH_WIDTH + pair * LANES:BRANCH_WIDTH + (pair + 1) * LANES]
            carry[2 * h + 1] = acc + jnp.dot(a, vblk, preferred_element_type=F32)
            carry[2 * h] = run + jnp.sum(log_1m[u], axis=1, keepdims=True)
        return tuple(carry)

    def live(state):
        i, carry = state[0], state[1:]
        top = carry[0]
        for h in range(1, N_HEADS):
            top = jnp.maximum(top, carry[2 * h])
        return (i >= 0) & (jnp.max(top) >= SB_RUN_FLOOR)

    def step(masked):
        return lambda state: (state[0] - 1,) + body(state[0], state[1:], masked)

    init = (jnp.zeros((qb, 1), F32), jnp.zeros((qb, LANES), F32)) * N_HEADS
    carry = lax.fori_loop(0, nkb - n_full, lambda i, c: body(nkb - 1 - i, c, True), init)
    state = lax.while_loop(live, step(False), (n_full - 1,) + tuple(carry))
    carry = state[1:]
    outs = [carry[2 * h + 1] for h in range(N_HEADS)]
    for pair in range(N_HEADS // 2):
        o_ref[0, :, pair * LANES:(pair + 1) * LANES] = jnp.where(
            lane < HEAD_DIM, outs[2 * pair], outs[2 * pair + 1])


def _sb_attention(q, kv, q_off):
    b, sq, _ = q.shape
    skp = kv.shape[1]
    qb = _pick(sq, (256, 128, 64))
    sub = 256
    kb = KEY_ALIGN
    tri =jnp.asarray(np.tril(np.ones((sub, sub), np.float32), -1), BF16)
    kern = functools.partial(_sb_kernel, qb=qb, kb=kb, sub=sub, q_off=q_off)
    return pl.pallas_call(
        kern,
        grid=(b, sq // qb),
        in_specs=[pl.BlockSpec((1, qb, BRANCH_WIDTH), lambda bi, qi: (bi, qi, 0)),
                  pl.BlockSpec((1, skp, 2 * BRANCH_WIDTH), lambda bi, qi: (bi, 0, 0),
                               pipeline_mode=pl.Buffered(1)),
                  pl.BlockSpec((sub, sub), lambda bi, qi: (0, 0))],
        out_specs=pl.BlockSpec((1, qb, BRANCH_WIDTH), lambda bi, qi: (bi, qi, 0)),
        out_shape=jax.ShapeDtypeStruct((b, sq, BRANCH_WIDTH), F32),
        compiler_params=_cparams(2),
        name="stick_breaking_attention",
    )(q, kv, tri)


def _band_kernel(bias_ref, q_ref, kv_ref, o_ref, tile_ref, *, qb, win, q_off):
    first = (pl.program_id(0) == 0) & (pl.program_id(1) == 0)
    row = lax.broadcasted_iota(jnp.int32, (qb, win), 0)
    col = lax.broadcasted_iota(jnp.int32, (qb, win), 1)

    @pl.when(first)
    def _():
        width = win + qb
        c = lax.broadcasted_iota(jnp.int32, (8, width), 1)
        rel = jnp.where(c < win, jnp.clip(BAND_WINDOW - c, -REL_CLIP, REL_CLIP) + REL_CLIP, 2 * REL_CLIP)

        def fill(r, rows):
            hit = rel == r
            return tuple(jnp.where(hit, bias_ref[h, r], rows[h]) for h in range(N_HEADS))

        rows = lax.fori_loop(0, 2 * REL_CLIP + 1, fill, (jnp.zeros((8, width), F32),) * N_HEADS)
        qch = row >> CHUNK_SHIFT
        kch = col >> CHUNK_SHIFT
        in_band = (kch >= qch) & (kch <= qch + BAND_LEFT_CHUNKS)
        for h in range(N_HEADS):
            table = jnp.broadcast_to(rows[h][:1], (qb, width))
            skewed = pltpu.roll(table, 0, 1, stride=1, stride_axis=0)[:, :win]
            tile_ref[h] = jnp.where(in_band, skewed * LOG2_E, NEG_INF)

    q0 = pl.multiple_of(pl.program_id(1) * qb, qb)
    kpos = col + (q_off + q0 - BAND_WINDOW)
    lane = lax.broadcasted_iota(jnp.int32, (qb, LANES), 1)
    outs = []
    for h in range(N_HEADS):
        pair = h // 2
        qh = _head_mask(q_ref[0, :, pair * LANES:(pair + 1) * LANES], h)
        kwin = kv_ref[0, pl.ds(q0, win), pair * LANES:(pair + 1) * LANES]
        s = _dot_nt(qh, kwin) + tile_ref[h]
        s = jnp.where(kpos >= 0, s, NEG_INF)
        m = jnp.max(s, axis=1, keepdims=True)
        p = jnp.exp2(s - m)
        l = jnp.sum(p, axis=1, keepdims=True)
        vwin = kv_ref[0, pl.ds(q0, win),
                      BRANCH_WIDTH + pair * LANES:BRANCH_WIDTH + (pair + 1) * LANES]
        outs.append(jnp.dot(p.astype(BF16), vwin, preferred_element_type=F32) / l)
    for pair in range(N_HEADS // 2):
        o_ref[0, :, pair * LANES:(pair + 1) * LANES] = jnp.where(
            lane < HEAD_DIM, outs[2 * pair], outs[2 * pair + 1])


def _band_attention(q, kv, rel_bias, q_off):
    b, sq, _ = q.shape
    sk = kv.shape[1]
    qb = _pick(sq, (256, 128, 64))
    win = qb + BAND_WINDOW
    kern = functools.partial(_band_kernel, qb=qb, win=win, q_off=q_off)
    return pl.pallas_call(
        kern,
        grid=(b, sq // qb),
        in_specs=[pl.BlockSpec(memory_space=pltpu.SMEM),
                  pl.BlockSpec((1, qb, BRANCH_WIDTH), lambda bi, qi: (bi, qi, 0)),
                  pl.BlockSpec((1, sk, 2 * BRANCH_WIDTH), lambda bi, qi: (bi, 0, 0),
                               pipeline_mode=pl.Buffered(1))],
        out_specs=pl.BlockSpec((1, qb, BRANCH_WIDTH), lambda bi, qi: (bi, qi, 0)),
        out_shape=jax.ShapeDtypeStruct((b, sq, BRANCH_WIDTH), F32),
        scratch_shapes=[pltpu.VMEM((N_HEADS, qb, win), F32)],
        compiler_params=_cparams(2),
        name="band_attention",
    )(rel_bias, q, kv)


def _dsa_kernel(qi_ref, w_ref, kx_ref, q_ref, k_ref, vt_ref, o_ref, key_ref, key16_ref, cut_ref,
                sa_ref, sb_ref,
                *, qb, kb, q_off, skp, n_sel):
    q0 = pl.program_id(1) * qb
    hi = jnp.minimum((((q_off + q0 + qb - 1) >> CHUNK_SHIFT) + 1) * CHUNK, skp)
    nkb = (hi + kb - 1) // kb
    qpos = q_off + q0 + lax.broadcasted_iota(jnp.int32, (1, qb), 1)
    cend = ((qpos >> CHUNK_SHIFT) + 1) * CHUNK
    row = lax.broadcasted_iota(jnp.int32, (kb, qb), 0)
    float_key = lambda bits: jnp.where(bits < 0, INT_MIN - bits, bits)
    neg_key = float_key(lax.bitcast_convert_type(jnp.full((1, 1), NEG_INF, F32), jnp.int32))
    n_full = jnp.minimum(((((q_off + q0) >> CHUNK_SHIFT) + 1) * CHUNK) // kb, nkb)

    def score_block(i, c, masked):
        k0 = pl.multiple_of(i * kb, kb)
        kx = kx_ref[0, pl.ds(k0, kb), :]
        r = [_dot_nt(kx, qi_ref[0, :, j * IDX_DIM:(j + 1) * IDX_DIM]) for j in range(IDX_HEADS)]
        score = jnp.zeros((kb, qb), F32)
        for j in range(IDX_HEADS):
            wj = w_ref[0, j:j + 1, :] * (IDX_HEADS ** -0.5)
            score = score + wj * jnp.maximum(r[j], 0.0)
        if masked:
            score = jnp.where(row + k0 < cend, score, NEG_INF)
        keys = float_key(lax.bitcast_convert_type(score, jnp.int32))
        key_ref[pl.ds(k0, kb), :] = keys
        key16_ref[pl.ds(k0, kb), :] = (keys >> 16).astype(jnp.int16)
        return c

    lax.fori_loop(0, n_full, functools.partial(score_block, masked=False), 0)
    lax.fori_loop(n_full, nkb, functools.partial(score_block, masked=True), 0)

    slab = 64 if kb % 64 == 0 else kb

    def count_ge(cand):
        def blk(i, cnt):
            k0 = pl.multiple_of(i * kb, kb)
            for s in range(kb // slab):
                keys = key_ref[pl.ds(k0 + s * slab, slab), :]
                cnt = cnt + jnp.where(keys >= cand, 1.0, 0.0)
            return cnt
        part = lax.fori_loop(0, nkb, blk, jnp.zeros((slab, qb), F32))
        return jnp.sum(part, axis=0, keepdims=True)

    slab16 = 128 if kb % 128 == 0 else kb

    def count16_ge(cand16):
        def blk(i, cnt):
            k0 = pl.multiple_of(i * kb, kb)
            for s in range(kb // slab16):
                k16 = key16_ref[pl.ds(k0 + s * slab16, slab16), :]
                cnt = cnt + jnp.where(k16 >= cand16, jnp.int16(1), jnp.int16(0))
            return cnt
        part = lax.fori_loop(0, nkb, blk, jnp.zeros((slab16, qb), jnp.int16))
        return jnp.sum(part.astype(F32), axis=0, keepdims=True)

    def undecided(t_end):
        def cond(state):
            t, _, n_ge = state
            return (t < t_end) & (jnp.max(jnp.abs(n_ge - n_sel)) > 0.0)
        return cond

    def bisect(to_16bit):
        def step(state):
            t, thr, n_ge = state
            cand = thr + jnp.left_shift(jnp.int32(1), 31 - t)
            cnt = count16_ge(to_16bit(cand))
            ok = cnt >= n_sel
            return t + 1, jnp.where(ok, cand, thr), jnp.where(ok, cnt, n_ge)
        return step

    stored = (nkb * kb).astype(F32)
    high_step = bisect(lambda cand: (cand >> 16).astype(jnp.int16))
    state = lax.fori_loop(
        0, 16, lambda _, st: high_step(st),
        (jnp.int32(0), jnp.full((1, qb), INT_MIN, jnp.int32), jnp.full((1, qb), stored, F32)))

    @pl.when(undecided(32)(state))
    def _():
        thr_hi = state[1] >> 16

        def low_halves(i, c):
            k0 = pl.multiple_of(i * kb, kb)
            keys = key_ref[pl.ds(k0, kb), :]
            hi16 = keys >> 16
            low = jnp.where(hi16 == thr_hi, (keys & 0xFFFF) - 0x8000,
                            jnp.where(hi16 > thr_hi, 0x7FFF, -0x8000))
            key16_ref[pl.ds(k0, kb), :] = low.astype(jnp.int16)
            return c

        lax.fori_loop(0, nkb, low_halves, 0)

    low_step = bisect(lambda cand: ((cand & 0xFFFF) - 0x8000).astype(jnp.int16))
    _, thr, n_ge = lax.while_loop(
        undecided(32), lambda st: low_step(low_step(low_step(low_step(st)))), state)
    excess = jnp.where((n_ge > n_sel) & (thr > neg_key), 1.0, 0.0)
    cut_ref[...] = jnp.full((1, qb), skp, jnp.int32)

    @pl.when(jnp.max(excess) > 0.0)
    def _():
        need = n_sel - count_ge(thr + 1)

        def count_tied_below(limit):
            def blk(i, cnt):
                k0 = pl.multiple_of(i * kb, kb)
                tied = (key_ref[pl.ds(k0, kb), :] == thr) & (row + k0 < limit)
                return cnt + jnp.sum(jnp.where(tied, 1.0, 0.0), axis=0, keepdims=True)
            return lax.fori_loop(0, nkb, blk, jnp.zeros((1, qb), F32))

        n_bits = max(1, (skp - 1).bit_length())

        def bisect_cut(t, cut):
            cand = cut + jnp.left_shift(jnp.int32(1), n_bits - 1 - t)
            return jnp.where(count_tied_below(cand) < need, cand, cut)

        cut = lax.fori_loop(0, n_bits, bisect_cut, jnp.zeros((1, qb), jnp.int32))
        cut_ref[...] = jnp.where(excess > 0.0, cut + 1, skp)

    cut = cut_ref[...]

    qh = [_head_mask(q_ref[0, :, (h // 2) * LANES:(h // 2 + 1) * LANES], h) for h in range(N_HEADS)]

    kb3 = kb // 2
    row3 = lax.broadcasted_iota(jnp.int32, (kb3, qb), 0)

    def scores(i):
        k0 = pl.multiple_of(i * kb3, kb3)
        return tuple(_dot_nt(k_ref[0, pl.ds(k0, kb3), (h // 2) * LANES:(h // 2 + 1) * LANES], qh[h])
                     for h in range(N_HEADS))

    def attend(i, s, carry, plain):
        k0 = pl.multiple_of(i * kb3, kb3)
        keys = key_ref[pl.ds(k0, kb3), :]
        if plain:
            sel = keys >= thr
        else:
            kpos = row3 + k0
            sel = (keys >= thr) & ((keys != thr) | (kpos < cut)) & (kpos < cend)
        bias = jnp.where(sel, 0.0, NEG_INF)
        p, new = [], []
        for h in range(N_HEADS):
            m, acc = carry[2 * h:2 * h + 2]
            sh = s[h] + bias
            m_new = jnp.maximum(m, jnp.max(sh, axis=0, keepdims=True))
            p.append(jnp.exp2((sh - m_new).astype(BF16)))
            new += [m_new, jnp.exp2(m - m_new) * acc]
        for h in range(N_HEADS):
            vblk = jnp.concatenate([vt_ref[0, h * HEAD_DIM:(h + 1) * HEAD_DIM, pl.ds(k0, kb3)],
                                    jnp.ones((LANES - HEAD_DIM, kb3), BF16)], axis=0)
            new[2 * h + 1] = new[2 * h + 1] + jnp.dot(vblk, p[h], preferred_element_type=F32)
        return tuple(new)

    def put_scores(slot_ref, i):
        for h, sh in enumerate(scores(i)):
            slot_ref[h] = sh

    def attend_from(slot_ref, i, carry, plain):
        return attend(i, tuple(slot_ref[h] for h in range(N_HEADS)), carry, plain)

    def body(j, carry, plain):
        put_scores(sb_ref, 2 * j + 1)
        carry = attend_from(sa_ref, 2 * j, carry, plain)
        put_scores(sa_ref, 2 * j + 2)
        return attend_from(sb_ref, 2 * j + 1, carry, plain)

    init = (jnp.full((1, qb), NEG_INF, F32), jnp.zeros((LANES, qb), F32)) * N_HEADS
    n_pairs = nkb
    n_plain = jnp.where(jnp.max(excess) > 0.0, 0, jnp.minimum(n_full, n_pairs - 1))
    put_scores(sa_ref, 0)
    carry = lax.fori_loop(0, n_plain, functools.partial(body, plain=True), init)
    carry = lax.fori_loop(n_plain, n_pairs - 1, functools.partial(body, plain=False), carry)
    last = 2 * (n_pairs - 1)
    put_scores(sb_ref, last + 1)
    carry = attend_from(sa_ref, last, carry, False)
    carry = attend_from(sb_ref, last + 1, carry, False)
    for h in range(N_HEADS):
        acc = carry[2 * h + 1]
        o_ref[0, h * HEAD_DIM:(h + 1) * HEAD_DIM, :] = acc[:HEAD_DIM, :] / acc[HEAD_DIM:HEAD_DIM + 1, :]


def _dsa_attention(qi, w, kx, q, k, vt, q_off, n_sel):
    b, sq, _ = q.shape
    skp = k.shape[1]
    qb = _pick(sq, (256, 128, 64))
    kb = _pick(skp, (1024, KEY_ALIGN))
    kern = functools.partial(_dsa_kernel, qb=qb, kb=kb, q_off=q_off, skp=skp, n_sel=n_sel)
    return pl.pallas_call(
        kern,
        grid=(b, sq // qb),
        in_specs=[pl.BlockSpec((1, qb, IDX_HEADS * IDX_DIM), lambda bi, qi_: (bi, qi_, 0)),
                  pl.BlockSpec((1, IDX_HEADS, qb), lambda bi, qi_: (bi, 0, qi_)),
                  pl.BlockSpec((1, skp, IDX_DIM), lambda bi, qi_: (bi, 0, 0),
                               pipeline_mode=pl.Buffered(1)),
                  pl.BlockSpec((1, qb, BRANCH_WIDTH), lambda bi, qi_: (bi, qi_, 0)),
                  pl.BlockSpec((1, skp, BRANCH_WIDTH), lambda bi, qi_: (bi, 0, 0),
                               pipeline_mode=pl.Buffered(1)),
                  pl.BlockSpec((1, BRANCH_WIDTH, skp), lambda bi, qi_: (bi, 0, 0),
                               pipeline_mode=pl.Buffered(1))],
        out_specs=pl.BlockSpec((1, BRANCH_WIDTH, qb), lambda bi, qi_: (bi, 0, qi_)),
        out_shape=jax.ShapeDtypeStruct((b, BRANCH_WIDTH, sq), F32),
        scratch_shapes=[pltpu.VMEM((skp, qb), jnp.int32), pltpu.VMEM((skp, qb), jnp.int16),
                        pltpu.VMEM((1, qb), jnp.int32),
                        pltpu.VMEM((N_HEADS, kb // 2, qb), F32), pltpu.VMEM((N_HEADS, kb // 2, qb), F32)],
        compiler_params=_cparams(2),
        name="dsa_attention",
    )(qi, w, kx, q, k, vt)


def _layer_norm(z, g, b):
    mu = jnp.mean(z, axis=-1, keepdims=True)
    zc = z - mu
    var = jnp.mean(zc * zc, axis=-1, keepdims=True)
    return zc * lax.rsqrt(var + LN_EPS) * g + b


def _merge_kernel(x_ref, wgate_ref, oa_ref, ob_ref, oc_ref, od_ref, wb_ref, wo_ref, g_ref, b_ref,
                  y_ref):
    xb = x_ref[...].astype(BF16)
    logits = [jnp.dot(xb, wgate_ref[:, n * D_MODEL:(n + 1) * D_MODEL], preferred_element_type=F32)
              for n in range(N_BRANCH)]
    branch = [jnp.dot(o_ref[...].astype(BF16), wb_ref[n], preferred_element_type=F32)
              for n, o_ref in enumerate((oa_ref, ob_ref, oc_ref, od_ref))]
    merged = jax.nn.sigmoid(logits[0]) * branch[0]
    for n in range(1, N_BRANCH):
        merged = merged + jax.nn.sigmoid(logits[n]) * branch[n]
    y = jnp.dot(merged.astype(BF16), wo_ref[...], preferred_element_type=F32)
    y_ref[...] = _layer_norm(ALPHA * x_ref[...] + y, g_ref[...], b_ref[...])


def _merge_out_ln(x, w_gate, o_a, o_b, o_c, o_d, w_branch, w_out, g, b):
    n = x.shape[0]
    ts = _pick(n, (512, 256, 128, 64))
    row = lambda i: (i, 0)
    obs = pl.BlockSpec((ts, BRANCH_WIDTH), row)
    return pl.pallas_call(
        _merge_kernel,
        grid=(n // ts,),
        in_specs=[pl.BlockSpec((ts, D_MODEL), row),
                  pl.BlockSpec((D_MODEL, GATE_WIDTH), lambda i: (0, 0), pipeline_mode=pl.Buffered(1)),
                  obs, obs, obs, obs,
                  pl.BlockSpec((N_BRANCH, BRANCH_WIDTH, D_MODEL), lambda i: (0, 0, 0)),
                  pl.BlockSpec((D_MODEL, D_MODEL), lambda i: (0, 0)),
                  pl.BlockSpec((1, D_MODEL), lambda i: (0, 0)),
                  pl.BlockSpec((1, D_MODEL), lambda i: (0, 0))],
        out_specs=pl.BlockSpec((ts, D_MODEL), row),
        out_shape=jax.ShapeDtypeStruct((n, D_MODEL), F32),
        compiler_params=_cparams(1),
        name="merge_out_ln",
    )(x, w_gate, o_a, o_b, o_c, o_d, w_branch, w_out, g, b)


def _ffn_kernel(x_ref, wg_ref, wu_ref, wd_ref, g_ref, b_ref, y_ref, acc_ref):
    f = pl.program_id(1)
    xb = x_ref[...].astype(BF16)
    gate = jnp.dot(xb, wg_ref[...], preferred_element_type=F32)
    up = jnp.dot(xb, wu_ref[...], preferred_element_type=F32)
    hidden = (gate * jax.nn.sigmoid(gate) * up).astype(BF16)
    part = jnp.dot(hidden, wd_ref[...], preferred_element_type=F32)

    @pl.when(f == 0)
    def _():
        acc_ref[...] = part

    @pl.when(f > 0)
    def _():
        acc_ref[...] += part

    @pl.when(f == pl.num_programs(1) - 1)
    def _():
        y_ref[...] = _layer_norm(ALPHA * x_ref[...] + acc_ref[...], g_ref[...], b_ref[...])


def _ffn_ln(x, w_gate, w_up, w_down, g, b):
    n = x.shape[0]
    ts = _pick(n, (1024, 512, 256, 128, 64))
    tf = D_FF // 2
    return pl.pallas_call(
        _ffn_kernel,
        grid=(n // ts, D_FF // tf),
        in_specs=[pl.BlockSpec((ts, D_MODEL), lambda i, f: (i, 0)),
                  pl.BlockSpec((D_MODEL, tf), lambda i, f: (0, f)),
                  pl.BlockSpec((D_MODEL, tf), lambda i, f: (0, f)),
                  pl.BlockSpec((tf, D_MODEL), lambda i, f: (f, 0)),
                  pl.BlockSpec((1, D_MODEL), lambda i, f: (0, 0)),
                  pl.BlockSpec((1, D_MODEL), lambda i, f: (0, 0))],
        out_specs=pl.BlockSpec((ts, D_MODEL), lambda i, f: (i, 0)),
        out_shape=jax.ShapeDtypeStruct((n, D_MODEL), F32),
        scratch_shapes=[pltpu.VMEM((ts, D_MODEL), F32)],
        compiler_params=_cparams(2),
        name="ffn_ln",
    )(x, w_gate, w_up, w_down, g, b)


def _pad_keys(a, axis=1):
    n = a.shape[axis]
    pad = (-n) % KEY_ALIGN
    if pad == 0:
        return a
    widths = [(0, 0)] * a.ndim
    widths[axis] = (0, pad)
    return jnp.pad(a, widths)


def _prepare_weights(w_in, mla_w_uk, mla_w_uv, w_branch, w_out, w_gate_up, w_down):
    w_rest = jnp.concatenate([w_in, jnp.zeros((D_MODEL, 1), w_in.dtype)], axis=1)
    w_rest = jnp.take(w_rest, _projection_columns(), axis=1).astype(BF16)
    w_uk = jnp.pad(mla_w_uk, ((0, 0), (0, 0), (0, LANES - MLA_D_NOPE))).reshape(MLA_D_C, -1)
    w_uv = jnp.pad(mla_w_uv, ((0, 0), (0, 0), (0, LANES - HEAD_DIM))).reshape(MLA_D_C, -1)
    w_ukv = jnp.concatenate([w_uk, w_uv], axis=1).astype(BF16)
    place = np.zeros((MLA_D_ROPE, 2 * N_HEADS * LANES), np.float32)
    for h in range(N_HEADS):
        place[np.arange(MLA_D_ROPE), h * LANES + MLA_ROPE_LANE + np.arange(MLA_D_ROPE)] = 1.0
    w_lat = jnp.concatenate([w_ukv, jnp.asarray(place, BF16)], axis=0)
    return dict(w_rest=w_rest, w_gates=w_in[:, REST_WIDTH:].astype(BF16), w_ukv=w_ukv, w_lat=w_lat,
                w_branch=w_branch.astype(BF16),
                w_out=w_out.astype(BF16), w_gate=w_gate_up[:, :D_FF].astype(BF16),
                w_up=w_gate_up[:, D_FF:].astype(BF16), w_down=w_down.astype(BF16))


def _projection_columns():
    zero_col = REST_WIDTH + GATE_WIDTH
    src = np.full(PROJ_WIDTH, zero_col, np.int32)
    o_aq, o_ckv, o_kr, o_sb, o_bd, o_ds, o_ixq, o_ixk, o_ixw = np.concatenate(
        [[0], np.cumsum(IN_SIZES[:-1])])[:9]
    hd = MLA_D_NOPE + MLA_D_ROPE

    def put(lane0, src0, n):
        src[lane0:lane0 + n] = src0 + np.arange(n)

    for h in range(N_HEADS):
        put((SEG_AQ + h) * LANES, o_aq + h * hd, hd)
    put(SEG_CKV * LANES, o_ckv, MLA_D_C)
    put(SEG_KR * LANES + MLA_ROPE_LANE, o_kr, MLA_D_ROPE)
    put(SEG_SB * LANES, o_sb, 3 * BRANCH_WIDTH)
    put(SEG_BD * LANES, o_bd, 3 * BRANCH_WIDTH)
    put(SEG_DS * LANES, o_ds, 3 * BRANCH_WIDTH)
    put(SEG_IXQ * LANES, o_ixq, IDX_HEADS * IDX_DIM)
    put(SEG_IXK * LANES, o_ixk, IDX_DIM)
    put(SEG_IXK * LANES + IXW_LANE, o_ixw, IDX_HEADS)
    return src


def _trunk_layer(x, q_off, past, wts, mla_kv_norm, band_rel_bias, ln1_g, ln1_b, ln2_g, ln2_b):
    bsz, s_len, _ = x.shape
    n = bsz * s_len
    pos = jnp.tile(q_off + jnp.arange(s_len, dtype=jnp.int32), bsz)
    x2 = x.reshape(n, D_MODEL)
    (q_a, lat_new, k_a, v_a, q_b, kv_b, sb_kv_new, q_c, kv_c, bd_kv_new,
     q_d, k_d, vt_d, ds_kv_new, q_ix, kidx_new, kx_d, w_ix) = _project(
        x2, pos, wts['w_rest'], wts['w_ukv'], mla_kv_norm[None])
    seq = lambda a: a.reshape(bsz, s_len, a.shape[-1])
    seq_t = lambda a: jnp.transpose(a.reshape(a.shape[0], bsz, s_len), (1, 0, 2))
    q_a, k_a, v_a, q_b, kv_b, q_c, kv_c, q_d, k_d, q_ix, kx_d = map(
        seq, (q_a, k_a, v_a, q_b, kv_b, q_c, kv_c, q_d, k_d, q_ix, kx_d))
    vt_d, w_ix = seq_t(vt_d), seq_t(w_ix)
    kv_state = lambda a: a.reshape(bsz, s_len, 2, N_HEADS, HEAD_DIM)
    lat_new, kidx_new = seq(lat_new), seq(kidx_new)
    sb_kv_new, bd_kv_new, ds_kv_new = kv_state(sb_kv_new), kv_state(bd_kv_new), kv_state(ds_kv_new)

    if past is None:
        kv_c = jnp.pad(kv_c, ((0, 0), (BAND_WINDOW, 0), (0, 0)))
        band_rows = bd_kv_new[:, s_len - min(BAND_WINDOW, s_len):]
        s_k = s_len
    else:
        past_lat, past_sb, past_band, past_ds, past_kidx = past
        p_len = past_lat.shape[1]
        s_k = p_len + s_len
        rows_bf16 = lambda a: a.reshape(bsz, a.shape[1], -1).astype(BF16)
        k_past, v_past = _latent_keys_values(past_lat.reshape(bsz * p_len, LAT_WIDTH), wts['w_lat'])
        k_a = jnp.concatenate([k_past.reshape(bsz, p_len, -1), k_a], axis=1)
        v_a = jnp.concatenate([v_past.reshape(bsz, p_len, -1), v_a], axis=1)
        kv_b = jnp.concatenate([rows_bf16(past_sb), kv_b], axis=1)
        kv_c = jnp.concatenate([rows_bf16(past_band), kv_c], axis=1)
        past_ds = past_ds.reshape(bsz, p_len, 2 * BRANCH_WIDTH)
        k_d = jnp.concatenate([past_ds[..., :BRANCH_WIDTH].astype(BF16), k_d], axis=1)
        vt_d = jnp.concatenate(
            [jnp.transpose(past_ds[..., BRANCH_WIDTH:], (0, 2, 1)).astype(BF16), vt_d], axis=2)
        kx_d = jnp.concatenate([past_kidx.astype(BF16), kx_d], axis=1)
        band_rows = bd_kv_new

    o_a = _mla_attention(q_a, _pad_keys(k_a), _pad_keys(v_a), q_off)
    o_b = _sb_attention(q_b, _pad_keys(kv_b), q_off)
    o_c = _band_attention(q_c, kv_c, band_rel_bias, q_off)
    n_sel = min(DSA_TOPK, s_k // 4)
    o_d = _dsa_attention(q_ix, w_ix, _pad_keys(kx_d), q_d, _pad_keys(k_d), _pad_keys(vt_d, axis=2),
                         q_off, n_sel)
    o_d = jnp.transpose(o_d, (0, 2, 1))

    flat = lambda o: o.reshape(n, BRANCH_WIDTH)
    x1 = _merge_out_ln(x2, wts['w_gates'], flat(o_a), flat(o_b), flat(o_c), flat(o_d),
                       wts['w_branch'], wts['w_out'], ln1_g[None], ln1_b[None])
    x_out = _ffn_ln(x1, wts['w_gate'], wts['w_up'], wts['w_down'], ln2_g[None], ln2_b[None])
    return x_out.reshape(bsz, s_len, D_MODEL), (lat_new, sb_kv_new, band_rows, ds_kv_new, kidx_new)


def kernel(x_prompt, x_sample, cache_mla_latent, cache_sb_kv, cache_band_kv, cache_dsa_kv, cache_dsa_kidx, w_in, mla_kv_norm, mla_w_uk, mla_w_uv, band_rel_bias, w_branch, w_out, ln1_g, ln1_b, w_gate_up, w_down, ln2_g, ln2_b):
    past_len = cache_mla_latent.shape[2]
    xp, xs = x_prompt, x_sample
    st_p, st_s = [], []
    for l in range(w_in.shape[0]):
        wts = _prepare_weights(w_in[l], mla_w_uk[l], mla_w_uv[l], w_branch[l], w_out[l],
                               w_gate_up[l], w_down[l])
        params = (wts, mla_kv_norm[l], band_rel_bias[l], ln1_g[l], ln1_b[l], ln2_g[l], ln2_b[l])
        xp, new_p = _trunk_layer(xp, 0, None, *params)
        past = (cache_mla_latent[l], cache_sb_kv[l], cache_band_kv[l], cache_dsa_kv[l],
                cache_dsa_kidx[l])
        xs, new_s = _trunk_layer(xs, past_len, past, *params)
        st_p.append(new_p)
        st_s.append(new_s)
    stack = lambda st, i: jnp.stack([s[i] for s in st])
    return (xp, xs) + tuple(stack(st_p, i) for i in range(5)) + tuple(stack(st_s, i) for i in range(5))
```

```python
import functools
import math

import numpy as np
import jax
import jax.numpy as jnp
from jax import lax
from jax.experimental import pallas as pl
from jax.experimental.pallas import tpu as pltpu

D_MODEL = 1024
CHUNK = 64
CHUNK_SHIFT = 6
N_BRANCH = 4
N_HEADS = 4
HEAD_DIM = 64
BRANCH_WIDTH = N_HEADS * HEAD_DIM
MLA_D_C = 128
MLA_D_NOPE = 64
MLA_D_ROPE = 32
MLA_THETA = 10000.0
ROPE_THETA = 500000.0
BAND_LEFT_CHUNKS = 8
BAND_WINDOW = BAND_LEFT_CHUNKS * CHUNK
REL_CLIP = 128
IDX_HEADS = 8
IDX_DIM = 64
DSA_TOPK = 256
D_FF = ((8 * D_MODEL // 3 + 255) // 256) * 256
DEPTH = 2
ALPHA = (2 * DEPTH) ** 0.25
NEG_INF = -1e30
LOG2_E = math.log2(math.e)
GROUP = 8
WINDOW_BITS = 5
SB_RUN_FLOOR = -150.0
LN_EPS = 1e-5
IN_SIZES = (N_HEADS * (MLA_D_NOPE + MLA_D_ROPE), MLA_D_C, MLA_D_ROPE,
            3 * BRANCH_WIDTH, 3 * BRANCH_WIDTH, 3 * BRANCH_WIDTH,
            IDX_HEADS * IDX_DIM, IDX_DIM, IDX_HEADS, N_BRANCH * D_MODEL)
REST_WIDTH = sum(IN_SIZES[:-1])
GATE_WIDTH = IN_SIZES[-1]

LANES = 128
KEY_ALIGN = 512
INT_MIN = -2 ** 31
VMEM_LIMIT = 56 * 1024 * 1024

F32 = jnp.float32
BF16 = jnp.bfloat16


def _cparams(n_axes):
    return pltpu.CompilerParams(dimension_semantics=("arbitrary",) * n_axes,
                                vmem_limit_bytes=VMEM_LIMIT)


def _pick(n, candidates):
    for c in candidates:
        if n % c == 0:
            return c
    return n


def _ones_upper_half(rows):
    lane = lax.broadcasted_iota(jnp.int32, (rows, N_HEADS * LANES), 1)
    return jnp.where((lane & (LANES - 1)) >= HEAD_DIM, 1.0, 0.0)


def _dot_nt(a, b):
    return lax.dot_general(a, b, (((1,), (1,)), ((), ())), preferred_element_type=F32)


def _latent_kernel(lat_ref, w_ref, k_ref, v_ref):
    res = jnp.dot(lat_ref[...].astype(BF16), w_ref[...], preferred_element_type=F32)
    k_ref[...] = res[:, :N_HEADS * LANES].astype(BF16)
    v_ref[...] = (res[:, N_HEADS * LANES:] + _ones_upper_half(res.shape[0])).astype(BF16)


def _latent_keys_values(lat, w_lat):
    m = lat.shape[0]
    tm = _pick(m, (1024, 512, 256, 128, 64, 32, 16, 8))
    width = N_HEADS * LANES
    return pl.pallas_call(
        _latent_kernel,
        grid=(m // tm,),
        in_specs=[pl.BlockSpec((tm, LAT_WIDTH), lambda i: (i, 0)),
                  pl.BlockSpec((LAT_WIDTH, 2 * width), lambda i: (0, 0))],
        out_specs=[pl.BlockSpec((tm, width), lambda i: (i, 0))] * 2,
        out_shape=[jax.ShapeDtypeStruct((m, width), BF16)] * 2,
        compiler_params=_cparams(1),
        name="latent_keys_values",
    )(lat, w_lat)


SEG_AQ = 0
SEG_CKV = 4
SEG_KR = 5
SEG_SB = 6
SEG_BD = 12
SEG_DS = 18
SEG_IXQ = 24
SEG_IXK = 28
N_SEG = 30
IXW_LANE = 96
PROJ_WIDTH = N_SEG * LANES
MLA_ROPE_LANE = 64
LAT_WIDTH = MLA_D_C + MLA_D_ROPE


def _proj_kernel(x_ref, w_ref, wukv_ref, g_ref, ca_ref, sma_ref, spa_ref, cp_ref, smp_ref, spp_ref,
                 qa_ref, lat_ref, ka_ref, va_ref,
                 qb_ref, kvb_ref, sbkv_ref, qc_ref, kvc_ref, bdkv_ref,
                 qd_ref, kd_ref, vtd_ref, dskv_ref, qix_ref, kidx_ref, kx_ref, wix_ref):
    proj = jnp.dot(x_ref[...].astype(BF16), w_ref[...], preferred_element_type=F32)
    seg = lambda s, n=1: proj[:, s * LANES:(s + n) * LANES]

    def rope(t, c_ref, sm_ref, sp_ref, half):
        return (t * c_ref[...] + pltpu.roll(t, LANES - half, 1) * sm_ref[...]
                + pltpu.roll(t, half, 1) * sp_ref[...])

    rope_a = functools.partial(rope, c_ref=ca_ref, sm_ref=sma_ref, sp_ref=spa_ref, half=MLA_D_ROPE // 2)
    rope_p = functools.partial(rope, c_ref=cp_ref, sm_ref=smp_ref, sp_ref=spp_ref, half=HEAD_DIM // 8)
    head_scale = HEAD_DIM ** -0.5 * LOG2_E

    mla_scale = (MLA_D_NOPE + MLA_D_ROPE) ** -0.5 * LOG2_E
    for h in range(N_HEADS):
        qa_ref[:, h * LANES:(h + 1) * LANES] = (rope_a(seg(SEG_AQ + h)) * mla_scale).astype(BF16)
    ckv = seg(SEG_CKV)
    ckv = ckv * lax.rsqrt(jnp.mean(ckv * ckv, axis=-1, keepdims=True) + LN_EPS) * g_ref[...]
    kr = rope_a(seg(SEG_KR))
    lat_ref[:, :MLA_D_C] = ckv
    lat_ref[:, MLA_D_C:] = kr[:, MLA_ROPE_LANE:MLA_ROPE_LANE + MLA_D_ROPE]
    kv_a = jnp.dot(ckv.astype(BF16), wukv_ref[...], preferred_element_type=F32)
    for h in range(N_HEADS):
        ka_ref[:, h * LANES:(h + 1) * LANES] = (kv_a[:, h * LANES:(h + 1) * LANES] + kr).astype(BF16)
    va_ref[...] = (kv_a[:, N_HEADS * LANES:] + _ones_upper_half(kv_a.shape[0])).astype(BF16)

    for s0, q_ref, kv_ref, new_ref in ((SEG_SB, qb_ref, kvb_ref, sbkv_ref),
                                       (SEG_BD, qc_ref, kvc_ref, bdkv_ref)):
        q_ref[...] = (seg(s0, 2) * head_scale).astype(BF16)
        kv = seg(s0 + 2, 4)
        new_ref[...] = kv
        kv_ref[...] = kv.astype(BF16)

    for p in range(2):
        qd_ref[:, p * LANES:(p + 1) * LANES] = (rope_p(seg(SEG_DS + p)) * head_scale).astype(BF16)
        k_rot = rope_p(seg(SEG_DS + 2 + p))
        dskv_ref[:, p * LANES:(p + 1) * LANES] = k_rot
        kd_ref[:, p * LANES:(p + 1) * LANES] = k_rot.astype(BF16)
    v_d = seg(SEG_DS + 4, 2)
    dskv_ref[:, BRANCH_WIDTH:] = v_d
    vtd_ref[...] = v_d.T.astype(BF16)
    for p in range(IDX_HEADS // 2):
        qix_ref[:, p * LANES:(p + 1) * LANES] = (rope_p(seg(SEG_IXQ + p)) * IDX_DIM ** -0.5).astype(BF16)
    ixk = rope_p(seg(SEG_IXK))
    kidx_ref[...] = ixk[:, :IDX_DIM]
    kx_ref[...] = ixk[:, :IDX_DIM].astype(BF16)
    wix_ref[...] = ixk.T[IXW_LANE:IXW_LANE + IDX_HEADS, :]


def _rope_tables(pos, theta, width, starts):
    half = width // 2
    inv = jnp.exp(jnp.arange(half, dtype=F32) * (-2.0 * math.log(theta) / width))
    ang = pos.astype(F32)[:, None] * inv[None, :]
    cos, sin = jnp.cos(ang), jnp.sin(ang)
    n = pos.shape[0]
    fill = lambda v, w: jnp.full((n, w), v, F32)
    c, sm, sp, lane = [], [], [], 0
    for s in sorted(starts):
        c += [fill(1.0, s - lane), cos, cos]
        sm += [fill(0.0, s - lane), -sin, fill(0.0, half)]
        sp += [fill(0.0, s - lane + half), sin]
        lane = s + width
    cat = lambda parts, v: jnp.concatenate(parts + [fill(v, LANES - lane)], axis=1)
    return cat(c, 1.0), cat(sm, 0.0), cat(sp, 0.0)


def _project(x, pos, w_rest, w_ukv, kv_norm):
    n = x.shape[0]
    ts = _pick(n, (512, 256, 128, 64))
    tables = (_rope_tables(pos, MLA_THETA, MLA_D_ROPE, (MLA_ROPE_LANE,))
              + _rope_tables(pos, ROPE_THETA, HEAD_DIM // 4, (0, HEAD_DIM)))
    row = lambda i: (i, 0)
    fixed = lambda i: (0, 0)
    rows = lambda w: pl.BlockSpec((ts, w), row)
    out = lambda w, dt: jax.ShapeDtypeStruct((n, w), dt)
    specs = [
        (N_HEADS * LANES, BF16), (LAT_WIDTH, F32), (N_HEADS * LANES, BF16), (N_HEADS * LANES, BF16),
        (BRANCH_WIDTH, BF16), (2 * BRANCH_WIDTH, BF16), (2 * BRANCH_WIDTH, F32),
        (BRANCH_WIDTH, BF16), (2 * BRANCH_WIDTH, BF16), (2 * BRANCH_WIDTH, F32),
        (BRANCH_WIDTH, BF16), (BRANCH_WIDTH, BF16), None, (2 * BRANCH_WIDTH, F32),
        (IDX_HEADS * IDX_DIM, BF16), (IDX_DIM, F32), (IDX_DIM, BF16), None]
    out_specs, out_shape = [], []
    for k, spec in enumerate(specs):
        if spec is None:
            rows_t, dt = ((BRANCH_WIDTH, BF16), (IDX_HEADS, F32))[k > 12]
            out_specs.append(pl.BlockSpec((rows_t, ts), lambda i: (0, i)))
            out_shape.append(jax.ShapeDtypeStruct((rows_t, n), dt))
        else:
            out_specs.append(rows(spec[0]))
            out_shape.append(out(*spec))
    return pl.pallas_call(
        _proj_kernel,
        grid=(n // ts,),
        in_specs=[rows(D_MODEL),
                  pl.BlockSpec((D_MODEL, PROJ_WIDTH), fixed, pipeline_mode=pl.Buffered(1)),
                  pl.BlockSpec((MLA_D_C, 2 * N_HEADS * LANES), fixed),
                  pl.BlockSpec((1, MLA_D_C), fixed)] + [rows(LANES)] * 6,
        out_specs=out_specs,
        out_shape=out_shape,
        compiler_params=_cparams(1),
        name="project_prepare",
    )(x, w_rest, w_ukv, kv_norm, *tables)


def _mla_kernel(q_ref, k_ref, v_ref, o_ref, *, qb, kb, q_off, skp):
    q0 = pl.program_id(1) * qb
    hi = jnp.minimum((((q_off + q0 + qb - 1) >> CHUNK_SHIFT) + 1) * CHUNK, skp)
    nkb = (hi + kb - 1) // kb
    qpos = q_off + q0 + lax.broadcasted_iota(jnp.int32, (qb, 1), 0)
    cend = ((qpos >> CHUNK_SHIFT) + 1) * CHUNK
    col = lax.broadcasted_iota(jnp.int32, (qb, kb), 1)
    lane = lax.broadcasted_iota(jnp.int32, (qb, LANES), 1)
    n_full = jnp.minimum(((((q_off + q0) >> CHUNK_SHIFT) + 1) * CHUNK) // kb, nkb)

    def body(i, carry, masked):
        k0 = pl.multiple_of(i * kb, kb)
        s = [_dot_nt(q_ref[0, :, h * LANES:(h + 1) * LANES],
                     k_ref[0, pl.ds(k0, kb), h * LANES:(h + 1) * LANES]) for h in range(N_HEADS)]
        if masked:
            vis = col + k0 < cend
            s = [jnp.where(vis, sh, NEG_INF) for sh in s]
        p, new = [], []
        for h in range(N_HEADS):
            m, acc = carry[2 * h:2 * h + 2]
            m_new = jnp.maximum(m, jnp.max(s[h], axis=1, keepdims=True))
            p.append(jnp.exp2((s[h] - m_new).astype(BF16)))
            new += [m_new, jnp.exp2(m - m_new) * acc]
        for h in range(N_HEADS):
            vblk = v_ref[0, pl.ds(k0, kb), h * LANES:(h + 1) * LANES]
            new[2 * h + 1] = new[2 * h + 1] + jnp.dot(p[h], vblk, preferred_element_type=F32)
        return tuple(new)

    init = (jnp.full((qb, 1), NEG_INF, F32), jnp.zeros((qb, LANES), F32)) * N_HEADS
    carry = lax.fori_loop(0, n_full, functools.partial(body, masked=False), init)
    carry = lax.fori_loop(n_full, nkb, functools.partial(body, masked=True), carry)
    outs = [carry[2 * h + 1] / pltpu.roll(carry[2 * h + 1], HEAD_DIM, 1) for h in range(N_HEADS)]
    for pair in range(N_HEADS // 2):
        o_ref[0, :, pair * LANES:(pair + 1) * LANES] = jnp.where(
            lane < HEAD_DIM, outs[2 * pair], pltpu.roll(outs[2 * pair + 1], HEAD_DIM, 1))


def _mla_attention(q, k, v, q_off):
    b, sq, _ = q.shape
    skp = k.shape[1]
    qb = _pick(sq, (512, 256, 128, 64))
    kb = _pick(skp, (1024, KEY_ALIGN))
    kern = functools.partial(_mla_kernel, qb=qb, kb=kb, q_off=q_off, skp=skp)
    return pl.pallas_call(
        kern,
        grid=(b, sq // qb),
        in_specs=[pl.BlockSpec((1, qb, N_HEADS * LANES), lambda bi, qi: (bi, qi, 0)),
                  pl.BlockSpec((1, skp, N_HEADS * LANES), lambda bi, qi: (bi, 0, 0),
                               pipeline_mode=pl.Buffered(1)),
                  pl.BlockSpec((1, skp, N_HEADS * LANES), lambda bi, qi: (bi, 0, 0),
                               pipeline_mode=pl.Buffered(1))],
        out_specs=pl.BlockSpec((1, qb, BRANCH_WIDTH), lambda bi, qi: (bi, qi, 0)),
        out_shape=jax.ShapeDtypeStruct((b, sq, BRANCH_WIDTH), F32),
        compiler_params=_cparams(2),
        name="mla_attention",
    )(q, k, v)


def _head_mask(x_pair, h):
    lane = lax.broadcasted_iota(jnp.int32, x_pair.shape, 1)
    keep = (lane < HEAD_DIM) if h % 2 == 0 else (lane >= HEAD_DIM)
    return jnp.where(keep, x_pair, jnp.zeros_like(x_pair))


def _sb_kernel(q_ref, kv_ref, t_ref, o_ref, *, qb, kb, sub, q_off):
    q0 = pl.program_id(1) * qb
    hi = q_off + q0 + qb - 1
    nkb = (hi + kb - 1) // kb
    qpos = q_off + q0 + lax.broadcasted_iota(jnp.int32, (qb, 1), 0)
    col = lax.broadcasted_iota(jnp.int32, (qb, sub), 1)
    lane = lax.broadcasted_iota(jnp.int32, (qb, LANES), 1)
    tri = t_ref[...]
    n_full = jnp.minimum((q_off + q0) // kb, nkb)
    qh = [_head_mask(q_ref[0, :, (h // 2) * LANES:(h // 2 + 1) * LANES], h) for h in range(N_HEADS)]

    def body(i, carry, masked):
        k0 = pl.multiple_of(i * kb, kb)
        carry = list(carry)
        units = [(j, h) for j in reversed(range(kb // sub)) for h in range(N_HEADS)]
        strict = {j: col + (k0 + j * sub) < qpos for j in range(kb // sub)} if masked else None
        z = {}
        for j, h in units:
            pair = h // 2
            kblk = kv_ref[0, pl.ds(k0 + j * sub, sub), pair * LANES:(pair + 1) * LANES]
            z[j, h] = _dot_nt(qh[h], kblk)
        log_1m, suffix = {}, {}
        for u in units:
            nz = -z[u]
            t = jnp.minimum(nz, 0.0) - jnp.log2(1.0 + jnp.exp2(jnp.minimum(z[u], nz)))
            if masked:
                t = jnp.where(strict[u[0]], t, 0.0)
            log_1m[u] = t
        for u in units:
            hi_part = log_1m[u].astype(BF16)
            lo_part = (log_1m[u] - hi_part.astype(F32)).astype(BF16)
            suffix[u] = (jnp.dot(hi_part, tri, preferred_element_type=F32)
                         + jnp.dot(lo_part, tri, preferred_element_type=F32))
        for j, h in units:
            u = (j, h)
            run, acc = carry[2 * h:2 * h + 2]
            expo = z[u] + log_1m[u] + suffix[u] + run
            if masked:
                expo = jnp.where(strict[j], expo, NEG_INF)
            a = jnp.exp2(expo).astype(BF16)
            pair = h // 2
            vblk = kv_ref[0, pl.ds(k0 + j * sub, sub),
                          BRANCH_WIDTH + pair * LANES:BRANCH_WIDTH + (pair + 1) * LANES]
            carry[2 * h + 1] = acc + jnp.dot(a, vblk, preferred_element_type=F32)
            carry[2 * h] = run + jnp.sum(log_1m[u], axis=1, keepdims=True)
        return tuple(carry)

    def live(state):
        i, carry = state[0], state[1:]
        top = carry[0]
        for h in range(1, N_HEADS):
            top = jnp.maximum(top, carry[2 * h])
        return (i >= 0) & (jnp.max(top) >= SB_RUN_FLOOR)

    def step(masked):
        return lambda state: (state[0] - 1,) + body(state[0], state[1:], masked)

    init = (jnp.zeros((qb, 1), F32), jnp.zeros((qb, LANES), F32)) * N_HEADS
    carry = lax.fori_loop(0, nkb - n_full, lambda i, c: body(nkb - 1 - i, c, True), init)
    state = lax.while_loop(live, step(False), (n_full - 1,) + tuple(carry))
    carry = state[1:]
    outs = [carry[2 * h + 1] for h in range(N_HEADS)]
    for pair in range(N_HEADS // 2):
        o_ref[0, :, pair * LANES:(pair + 1) * LANES] = jnp.where(
            lane < HEAD_DIM, outs[2 * pair], outs[2 * pair + 1])


def _sb_attention(q, kv, q_off):
    b, sq, _ = q.shape
    skp = kv.shape[1]
    qb = _pick(sq, (256, 128, 64))
    sub = 256
    kb = KEY_ALIGN
    tri =jnp.asarray(np.tril(np.ones((sub, sub), np.float32), -1), BF16)
    kern = functools.partial(_sb_kernel, qb=qb, kb=kb, sub=sub, q_off=q_off)
    return pl.pallas_call(
        kern,
        grid=(b, sq // qb),
        in_specs=[pl.BlockSpec((1, qb, BRANCH_WIDTH), lambda bi, qi: (bi, qi, 0)),
                  pl.BlockSpec((1, skp, 2 * BRANCH_WIDTH), lambda bi, qi: (bi, 0, 0),
                               pipeline_mode=pl.Buffered(1)),
                  pl.BlockSpec((sub, sub), lambda bi, qi: (0, 0))],
        out_specs=pl.BlockSpec((1, qb, BRANCH_WIDTH), lambda bi, qi: (bi, qi, 0)),
        out_shape=jax.ShapeDtypeStruct((b, sq, BRANCH_WIDTH), F32),
        compiler_params=_cparams(2),
        name="stick_breaking_attention",
    )(q, kv, tri)


def _band_kernel(bias_ref, q_ref, kv_ref, o_ref, tile_ref, *, qb, win, q_off):
    first = (pl.program_id(0) == 0) & (pl.program_id(1) == 0)
    row = lax.broadcasted_iota(jnp.int32, (qb, win), 0)
    col = lax.broadcasted_iota(jnp.int32, (qb, win), 1)

    @pl.when(first)
    def _():
        width = win + qb
        c = lax.broadcasted_iota(jnp.int32, (8, width), 1)
        rel = jnp.where(c < win, jnp.clip(BAND_WINDOW - c, -REL_CLIP, REL_CLIP) + REL_CLIP, 2 * REL_CLIP)

        def fill(r, rows):
            hit = rel == r
            return tuple(jnp.where(hit, bias_ref[h, r], rows[h]) for h in range(N_HEADS))

        rows = lax.fori_loop(0, 2 * REL_CLIP + 1, fill, (jnp.zeros((8, width), F32),) * N_HEADS)
        qch = row >> CHUNK_SHIFT
        kch = col >> CHUNK_SHIFT
        in_band = (kch >= qch) & (kch <= qch + BAND_LEFT_CHUNKS)
        for h in range(N_HEADS):
            table = jnp.broadcast_to(rows[h][:1], (qb, width))
            skewed = pltpu.roll(table, 0, 1, stride=1, stride_axis=0)[:, :win]
            tile_ref[h] = jnp.where(in_band, skewed * LOG2_E, NEG_INF)

    q0 = pl.multiple_of(pl.program_id(1) * qb, qb)
    kpos = col + (q_off + q0 - BAND_WINDOW)
    lane = lax.broadcasted_iota(jnp.int32, (qb, LANES), 1)
    outs = []
    for h in range(N_HEADS):
        pair = h // 2
        qh = _head_mask(q_ref[0, :, pair * LANES:(pair + 1) * LANES], h)
        kwin = kv_ref[0, pl.ds(q0, win), pair * LANES:(pair + 1) * LANES]
        s = _dot_nt(qh, kwin) + tile_ref[h]
        s = jnp.where(kpos >= 0, s, NEG_INF)
        m = jnp.max(s, axis=1, keepdims=True)
        p = jnp.exp2(s - m)
        l = jnp.sum(p, axis=1, keepdims=True)
        vwin = kv_ref[0, pl.ds(q0, win),
                      BRANCH_WIDTH + pair * LANES:BRANCH_WIDTH + (pair + 1) * LANES]
        outs.append(jnp.dot(p.astype(BF16), vwin, preferred_element_type=F32) / l)
    for pair in range(N_HEADS // 2):
        o_ref[0, :, pair * LANES:(pair + 1) * LANES] = jnp.where(
            lane < HEAD_DIM, outs[2 * pair], outs[2 * pair + 1])


def _band_attention(q, kv, rel_bias, q_off):
    b, sq, _ = q.shape
    sk = kv.shape[1]
    qb = _pick(sq, (256, 128, 64))
    win = qb + BAND_WINDOW
    kern = functools.partial(_band_kernel, qb=qb, win=win, q_off=q_off)
    return pl.pallas_call(
        kern,
        grid=(b, sq // qb),
        in_specs=[pl.BlockSpec(memory_space=pltpu.SMEM),
                  pl.BlockSpec((1, qb, BRANCH_WIDTH), lambda bi, qi: (bi, qi, 0)),
                  pl.BlockSpec((1, sk, 2 * BRANCH_WIDTH), lambda bi, qi: (bi, 0, 0),
                               pipeline_mode=pl.Buffered(1))],
        out_specs=pl.BlockSpec((1, qb, BRANCH_WIDTH), lambda bi, qi: (bi, qi, 0)),
        out_shape=jax.ShapeDtypeStruct((b, sq, BRANCH_WIDTH), F32),
        scratch_shapes=[pltpu.VMEM((N_HEADS, qb, win), F32)],
        compiler_params=_cparams(2),
        name="band_attention",
    )(rel_bias, q, kv)


def _dsa_kernel(qi_ref, w_ref, kx_ref, q_ref, k_ref, vt_ref, o_ref, key_ref, key16_ref, cut_ref,
                sa_ref, sb_ref, gm_ref, thr_ref, nge_ref,
                *, qb, kb, q_off, skp, n_sel):
    q0 = pl.program_id(1) * qb
    hi = jnp.minimum((((q_off + q0 + qb - 1) >> CHUNK_SHIFT) + 1) * CHUNK, skp)
    nkb = (hi + kb - 1) // kb
    qpos = q_off + q0 + lax.broadcasted_iota(jnp.int32, (1, qb), 1)
    cend = ((qpos >> CHUNK_SHIFT) + 1) * CHUNK
    row = lax.broadcasted_iota(jnp.int32, (kb, qb), 0)
    float_key = lambda bits: jnp.where(bits < 0, INT_MIN - bits, bits)
    neg_key = float_key(lax.bitcast_convert_type(jnp.full((1, 1), NEG_INF, F32), jnp.int32))
    n_full = jnp.minimum(((((q_off + q0) >> CHUNK_SHIFT) + 1) * CHUNK) // kb, nkb)

    def score_block(i, c, masked):
        k0 = pl.multiple_of(i * kb, kb)
        kx = kx_ref[0, pl.ds(k0, kb), :]
        r = [_dot_nt(kx, qi_ref[0, :, j * IDX_DIM:(j + 1) * IDX_DIM]) for j in range(IDX_HEADS)]
        score = jnp.zeros((kb, qb), F32)
        for j in range(IDX_HEADS):
            wj = w_ref[0, j:j + 1, :] * (IDX_HEADS ** -0.5)
            score = score + wj * jnp.maximum(r[j], 0.0)
        if masked:
            score = jnp.where(row + k0 < cend, score, NEG_INF)
        keys = float_key(lax.bitcast_convert_type(score, jnp.int32))
        key_ref[pl.ds(k0, kb), :] = keys
        k16 = (keys >> 16).astype(jnp.int16)
        key16_ref[pl.ds(k0, kb), :] = k16
        g = kb // GROUP
        gmax = k16[:g]
        for s in range(1, GROUP):
            part = k16[s * g:(s + 1) * g]
            gmax = jnp.where(part > gmax, part, gmax)
        gm_ref[pl.ds(pl.multiple_of(i * g, g), g), :] = gmax
        return c

    lax.fori_loop(0, n_full, functools.partial(score_block, masked=False), 0)
    lax.fori_loop(n_full, nkb, functools.partial(score_block, masked=True), 0)

    slab = 64 if kb % 64 == 0 else kb

    def count_ge(cand):
        def blk(i, cnt):
            k0 = pl.multiple_of(i * kb, kb)
            for s in range(kb // slab):
                keys = key_ref[pl.ds(k0 + s * slab, slab), :]
                cnt = cnt + jnp.where(keys >= cand, 1.0, 0.0)
            return cnt
        part = lax.fori_loop(0, nkb, blk, jnp.zeros((slab, qb), F32))
        return jnp.sum(part, axis=0, keepdims=True)

    slab16 = 128 if kb % 128 == 0 else kb

    def count16_ge(cand16, src_ref=key16_ref, rows=kb):
        slab_rows = min(slab16, rows)

        def blk(i, cnt):
            k0 = pl.multiple_of(i * rows, rows)
            for s in range(rows // slab_rows):
                k16 = src_ref[pl.ds(k0 + s * slab_rows, slab_rows), :]
                cnt = cnt + jnp.where(k16 >= cand16, jnp.int16(1), jnp.int16(0))
            return cnt
        part = lax.fori_loop(0, nkb, blk, jnp.zeros((slab_rows, qb), jnp.int16))
        return jnp.sum(part.astype(F32), axis=0, keepdims=True)

    def undecided(t_end):
        def cond(state):
            t, _, n_ge = state
            return (t < t_end) & (jnp.max(jnp.abs(n_ge - n_sel)) > 0.0)
        return cond

    def bisect(to_16bit, **count_args):
        def step(state):
            t, thr, n_ge = state
            cand = thr + jnp.left_shift(jnp.int32(1), 31 - t)
            cnt = count16_ge(to_16bit(cand), **count_args)
            ok = cnt >= n_sel
            return t + 1, jnp.where(ok, cand, thr), jnp.where(ok, cnt, n_ge)
        return step

    high_half = lambda cand: (cand >> 16).astype(jnp.int16)
    start = (jnp.int32(0), jnp.full((1, qb), INT_MIN, jnp.int32), jnp.full((1, qb), 0.0, F32))

    _, base, _ = lax.fori_loop(0, 16, lambda _, st: bisect(high_half, src_ref=gm_ref, rows=kb // GROUP)(st), start)
    base_hi = base >> 16

    def window_step(b, st):
        delta, n_ge = st
        cand_delta = delta + jnp.left_shift(jnp.int32(1), WINDOW_BITS - 1 - b)
        cnt = count16_ge(jnp.minimum(base_hi + cand_delta, 0x7FFF).astype(jnp.int16))
        ok = cnt >= n_sel
        return jnp.where(ok, cand_delta, delta), jnp.where(ok, cnt, n_ge)

    delta, n_ge_fast = lax.fori_loop(0, WINDOW_BITS, window_step,
                                     (jnp.zeros((1, qb), jnp.int32), count16_ge(base_hi.astype(jnp.int16))))
    thr_ref[...] = base + jnp.left_shift(delta, 16)
    nge_ref[...] = n_ge_fast

    @pl.when(jnp.max(delta) >= 2 ** WINDOW_BITS - 1)
    def _():
        stored = (nkb * kb).astype(F32)
        _, thr_full, n_ge_full = lax.fori_loop(
            0, 16, lambda _, st: bisect(high_half)(st),
            (jnp.int32(0), jnp.full((1, qb), INT_MIN, jnp.int32), jnp.full((1, qb), stored, F32)))
        thr_ref[...] = thr_full
        nge_ref[...] = n_ge_full

    state = (jnp.int32(16), thr_ref[...], nge_ref[...])

    @pl.when(undecided(32)(state))
    def _():
        thr_hi = state[1] >> 16

        def low_halves(i, c):
            k0 = pl.multiple_of(i * kb, kb)
            keys = key_ref[pl.ds(k0, kb), :]
            hi16 = keys >> 16
            low = jnp.where(hi16 == thr_hi, (keys & 0xFFFF) - 0x8000,
                            jnp.where(hi16 > thr_hi, 0x7FFF, -0x8000))
            key16_ref[pl.ds(k0, kb), :] = low.astype(jnp.int16)
            return c

        lax.fori_loop(0, nkb, low_halves, 0)

    low_step = bisect(lambda cand: ((cand & 0xFFFF) - 0x8000).astype(jnp.int16))
    _, thr, n_ge = lax.while_loop(
        undecided(32), lambda st: low_step(low_step(low_step(low_step(st)))), state)
    excess = jnp.where((n_ge > n_sel) & (thr > neg_key), 1.0, 0.0)
    cut_ref[...] = jnp.full((1, qb), skp, jnp.int32)

    @pl.when(jnp.max(excess) > 0.0)
    def _():
        need = n_sel - count_ge(thr + 1)

        def count_tied_below(limit):
            def blk(i, cnt):
                k0 = pl.multiple_of(i * kb, kb)
                tied = (key_ref[pl.ds(k0, kb), :] == thr) & (row + k0 < limit)
                return cnt + jnp.sum(jnp.where(tied, 1.0, 0.0), axis=0, keepdims=True)
            return lax.fori_loop(0, nkb, blk, jnp.zeros((1, qb), F32))

        n_bits = max(1, (skp - 1).bit_length())

        def bisect_cut(t, cut):
            cand = cut + jnp.left_shift(jnp.int32(1), n_bits - 1 - t)
            return jnp.where(count_tied_below(cand) < need, cand, cut)

        cut = lax.fori_loop(0, n_bits, bisect_cut, jnp.zeros((1, qb), jnp.int32))
        cut_ref[...] = jnp.where(excess > 0.0, cut + 1, skp)

    cut = cut_ref[...]

    qh = [_head_mask(q_ref[0, :, (h // 2) * LANES:(h // 2 + 1) * LANES], h) for h in range(N_HEADS)]

    kb3 = kb // 2
    row3 = lax.broadcasted_iota(jnp.int32, (kb3, qb), 0)

    def scores(i):
        k0 = pl.multiple_of(i * kb3, kb3)
        return tuple(_dot_nt(k_ref[0, pl.ds(k0, kb3), (h // 2) * LANES:(h // 2 + 1) * LANES], qh[h])
                     for h in range(N_HEADS))

    def attend(i, s, carry, plain):
        k0 = pl.multiple_of(i * kb3, kb3)
        keys = key_ref[pl.ds(k0, kb3), :]
        if plain:
            sel = keys >= thr
        else:
            kpos = row3 + k0
            sel = (keys >= thr) & ((keys != thr) | (kpos < cut)) & (kpos < cend)
        bias = jnp.where(sel, 0.0, NEG_INF)
        p, new = [], []
        for h in range(N_HEADS):
            m, acc = carry[2 * h:2 * h + 2]
            sh = s[h] + bias
            m_new = jnp.maximum(m, jnp.max(sh, axis=0, keepdims=True))
            p.append(jnp.exp2((sh - m_new).astype(BF16)))
            new += [m_new, jnp.exp2(m - m_new) * acc]
        for h in range(N_HEADS):
            vblk = jnp.concatenate([vt_ref[0, h * HEAD_DIM:(h + 1) * HEAD_DIM, pl.ds(k0, kb3)],
                                    jnp.ones((LANES - HEAD_DIM, kb3), BF16)], axis=0)
            new[2 * h + 1] = new[2 * h + 1] + jnp.dot(vblk, p[h], preferred_element_type=F32)
        return tuple(new)

    def put_scores(slot_ref, i):
        for h, sh in enumerate(scores(i)):
            slot_ref[h] = sh

    def attend_from(slot_ref, i, carry, plain):
        return attend(i, tuple(slot_ref[h] for h in range(N_HEADS)), carry, plain)

    def body(j, carry, plain):
        put_scores(sb_ref, 2 * j + 1)
        carry = attend_from(sa_ref, 2 * j, carry, plain)
        put_scores(sa_ref, 2 * j + 2)
        return attend_from(sb_ref, 2 * j + 1, carry, plain)

    init = (jnp.full((1, qb), NEG_INF, F32), jnp.zeros((LANES, qb), F32)) * N_HEADS
    n_pairs = nkb
    n_plain = jnp.where(jnp.max(excess) > 0.0, 0, jnp.minimum(n_full, n_pairs - 1))
    put_scores(sa_ref, 0)
    carry = lax.fori_loop(0, n_plain, functools.partial(body, plain=True), init)
    carry = lax.fori_loop(n_plain, n_pairs - 1, functools.partial(body, plain=False), carry)
    last = 2 * (n_pairs - 1)
    put_scores(sb_ref, last + 1)
    carry = attend_from(sa_ref, last, carry, False)
    carry = attend_from(sb_ref, last + 1, carry, False)
    for h in range(N_HEADS):
        acc = carry[2 * h + 1]
        o_ref[0, h * HEAD_DIM:(h + 1) * HEAD_DIM, :] = acc[:HEAD_DIM, :] / acc[HEAD_DIM:HEAD_DIM + 1, :]


def _dsa_attention(qi, w, kx, q, k, vt, q_off, n_sel):
    b, sq, _ = q.shape
    skp = k.shape[1]
    qb = _pick(sq, (256, 128, 64))
    kb = _pick(skp, (1024, KEY_ALIGN))
    kern = functools.partial(_dsa_kernel, qb=qb, kb=kb, q_off=q_off, skp=skp, n_sel=n_sel)
    return pl.pallas_call(
        kern,
        grid=(b, sq // qb),
        in_specs=[pl.BlockSpec((1, qb, IDX_HEADS * IDX_DIM), lambda bi, qi_: (bi, qi_, 0)),
                  pl.BlockSpec((1, IDX_HEADS, qb), lambda bi, qi_: (bi, 0, qi_)),
                  pl.BlockSpec((1, skp, IDX_DIM), lambda bi, qi_: (bi, 0, 0),
                               pipeline_mode=pl.Buffered(1)),
                  pl.BlockSpec((1, qb, BRANCH_WIDTH), lambda bi, qi_: (bi, qi_, 0)),
                  pl.BlockSpec((1, skp, BRANCH_WIDTH), lambda bi, qi_: (bi, 0, 0),
                               pipeline_mode=pl.Buffered(1)),
                  pl.BlockSpec((1, BRANCH_WIDTH, skp), lambda bi, qi_: (bi, 0, 0),
                               pipeline_mode=pl.Buffered(1))],
        out_specs=pl.BlockSpec((1, BRANCH_WIDTH, qb), lambda bi, qi_: (bi, 0, qi_)),
        out_shape=jax.ShapeDtypeStruct((b, BRANCH_WIDTH, sq), F32),
        scratch_shapes=[pltpu.VMEM((skp, qb), jnp.int32), pltpu.VMEM((skp, qb), jnp.int16),
                        pltpu.VMEM((1, qb), jnp.int32),
                        pltpu.VMEM((N_HEADS, kb // 2, qb), F32), pltpu.VMEM((N_HEADS, kb // 2, qb), F32),
                        pltpu.VMEM((skp // GROUP, qb), jnp.int16),
                        pltpu.VMEM((1, qb), jnp.int32), pltpu.VMEM((1, qb), F32)],
        compiler_params=_cparams(2),
        name="dsa_attention",
    )(qi, w, kx, q, k, vt)


def _layer_norm(z, g, b):
    mu = jnp.mean(z, axis=-1, keepdims=True)
    zc = z - mu
    var = jnp.mean(zc * zc, axis=-1, keepdims=True)
    return zc * lax.rsqrt(var + LN_EPS) * g + b


def _merge_kernel(x_ref, wgate_ref, oa_ref, ob_ref, oc_ref, od_ref, wb_ref, wo_ref, g_ref, b_ref,
                  y_ref):
    xb = x_ref[...].astype(BF16)
    logits = [jnp.dot(xb, wgate_ref[:, n * D_MODEL:(n + 1) * D_MODEL], preferred_element_type=F32)
              for n in range(N_BRANCH)]
    branch = [jnp.dot(o_ref[...].astype(BF16), wb_ref[n], preferred_element_type=F32)
              for n, o_ref in enumerate((oa_ref, ob_ref, oc_ref, od_ref))]
    merged = jax.nn.sigmoid(logits[0]) * branch[0]
    for n in range(1, N_BRANCH):
        merged = merged + jax.nn.sigmoid(logits[n]) * branch[n]
    y = jnp.dot(merged.astype(BF16), wo_ref[...], preferred_element_type=F32)
    y_ref[...] = _layer_norm(ALPHA * x_ref[...] + y, g_ref[...], b_ref[...])


def _merge_out_ln(x, w_gate, o_a, o_b, o_c, o_d, w_branch, w_out, g, b):
    n = x.shape[0]
    ts = _pick(n, (512, 256, 128, 64))
    row = lambda i: (i, 0)
    obs = pl.BlockSpec((ts, BRANCH_WIDTH), row)
    return pl.pallas_call(
        _merge_kernel,
        grid=(n // ts,),
        in_specs=[pl.BlockSpec((ts, D_MODEL), row),
                  pl.BlockSpec((D_MODEL, GATE_WIDTH), lambda i: (0, 0), pipeline_mode=pl.Buffered(1)),
                  obs, obs, obs, obs,
                  pl.BlockSpec((N_BRANCH, BRANCH_WIDTH, D_MODEL), lambda i: (0, 0, 0)),
                  pl.BlockSpec((D_MODEL, D_MODEL), lambda i: (0, 0)),
                  pl.BlockSpec((1, D_MODEL), lambda i: (0, 0)),
                  pl.BlockSpec((1, D_MODEL), lambda i: (0, 0))],
        out_specs=pl.BlockSpec((ts, D_MODEL), row),
        out_shape=jax.ShapeDtypeStruct((n, D_MODEL), F32),
        compiler_params=_cparams(1),
        name="merge_out_ln",
    )(x, w_gate, o_a, o_b, o_c, o_d, w_branch, w_out, g, b)


def _ffn_kernel(x_ref, wg_ref, wu_ref, wd_ref, g_ref, b_ref, y_ref, acc_ref):
    f = pl.program_id(1)
    xb = x_ref[...].astype(BF16)
    gate = jnp.dot(xb, wg_ref[...], preferred_element_type=F32)
    up = jnp.dot(xb, wu_ref[...], preferred_element_type=F32)
    hidden = (gate * jax.nn.sigmoid(gate) * up).astype(BF16)
    part = jnp.dot(hidden, wd_ref[...], preferred_element_type=F32)

    @pl.when(f == 0)
    def _():
        acc_ref[...] = part

    @pl.when(f > 0)
    def _():
        acc_ref[...] += part

    @pl.when(f == pl.num_programs(1) - 1)
    def _():
        y_ref[...] = _layer_norm(ALPHA * x_ref[...] + acc_ref[...], g_ref[...], b_ref[...])


def _ffn_ln(x, w_gate, w_up, w_down, g, b):
    n = x.shape[0]
    ts = _pick(n, (1024, 512, 256, 128, 64))
    tf = D_FF // 2
    return pl.pallas_call(
        _ffn_kernel,
        grid=(n // ts, D_FF // tf),
        in_specs=[pl.BlockSpec((ts, D_MODEL), lambda i, f: (i, 0)),
                  pl.BlockSpec((D_MODEL, tf), lambda i, f: (0, f)),
                  pl.BlockSpec((D_MODEL, tf), lambda i, f: (0, f)),
                  pl.BlockSpec((tf, D_MODEL), lambda i, f: (f, 0)),
                  pl.BlockSpec((1, D_MODEL), lambda i, f: (0, 0)),
                  pl.BlockSpec((1, D_MODEL), lambda i, f: (0, 0))],
        out_specs=pl.BlockSpec((ts, D_MODEL), lambda i, f: (i, 0)),
        out_shape=jax.ShapeDtypeStruct((n, D_MODEL), F32),
        scratch_shapes=[pltpu.VMEM((ts, D_MODEL), F32)],
        compiler_params=_cparams(2),
        name="ffn_ln",
    )(x, w_gate, w_up, w_down, g, b)


def _pad_keys(a, axis=1):
    n = a.shape[axis]
    pad = (-n) % KEY_ALIGN
    if pad == 0:
        return a
    widths = [(0, 0)] * a.ndim
    widths[axis] = (0, pad)
    return jnp.pad(a, widths)


def _prepare_weights(w_in, mla_w_uk, mla_w_uv, w_branch, w_out, w_gate_up, w_down):
    w_rest = jnp.concatenate([w_in, jnp.zeros((D_MODEL, 1), w_in.dtype)], axis=1)
    w_rest = jnp.take(w_rest, _projection_columns(), axis=1).astype(BF16)
    w_uk = jnp.pad(mla_w_uk, ((0, 0), (0, 0), (0, LANES - MLA_D_NOPE))).reshape(MLA_D_C, -1)
    w_uv = jnp.pad(mla_w_uv, ((0, 0), (0, 0), (0, LANES - HEAD_DIM))).reshape(MLA_D_C, -1)
    w_ukv = jnp.concatenate([w_uk, w_uv], axis=1).astype(BF16)
    place = np.zeros((MLA_D_ROPE, 2 * N_HEADS * LANES), np.float32)
    for h in range(N_HEADS):
        place[np.arange(MLA_D_ROPE), h * LANES + MLA_ROPE_LANE + np.arange(MLA_D_ROPE)] = 1.0
    w_lat = jnp.concatenate([w_ukv, jnp.asarray(place, BF16)], axis=0)
    return dict(w_rest=w_rest, w_gates=w_in[:, REST_WIDTH:].astype(BF16), w_ukv=w_ukv, w_lat=w_lat,
                w_branch=w_branch.astype(BF16),
                w_out=w_out.astype(BF16), w_gate=w_gate_up[:, :D_FF].astype(BF16),
                w_up=w_gate_up[:, D_FF:].astype(BF16), w_down=w_down.astype(BF16))


def _projection_columns():
    zero_col = REST_WIDTH + GATE_WIDTH
    src = np.full(PROJ_WIDTH, zero_col, np.int32)
    o_aq, o_ckv, o_kr, o_sb, o_bd, o_ds, o_ixq, o_ixk, o_ixw = np.concatenate(
        [[0], np.cumsum(IN_SIZES[:-1])])[:9]
    hd = MLA_D_NOPE + MLA_D_ROPE

    def put(lane0, src0, n):
        src[lane0:lane0 + n] = src0 + np.arange(n)

    for h in range(N_HEADS):
        put((SEG_AQ + h) * LANES, o_aq + h * hd, hd)
    put(SEG_CKV * LANES, o_ckv, MLA_D_C)
    put(SEG_KR * LANES + MLA_ROPE_LANE, o_kr, MLA_D_ROPE)
    put(SEG_SB * LANES, o_sb, 3 * BRANCH_WIDTH)
    put(SEG_BD * LANES, o_bd, 3 * BRANCH_WIDTH)
    put(SEG_DS * LANES, o_ds, 3 * BRANCH_WIDTH)
    put(SEG_IXQ * LANES, o_ixq, IDX_HEADS * IDX_DIM)
    put(SEG_IXK * LANES, o_ixk, IDX_DIM)
    put(SEG_IXK * LANES + IXW_LANE, o_ixw, IDX_HEADS)
    return src


def _trunk_layer(x, q_off, past, wts, mla_kv_norm, band_rel_bias, ln1_g, ln1_b, ln2_g, ln2_b):
    bsz, s_len, _ = x.shape
    n = bsz * s_len
    pos = jnp.tile(q_off + jnp.arange(s_len, dtype=jnp.int32), bsz)
    x2 = x.reshape(n, D_MODEL)
    (q_a, lat_new, k_a, v_a, q_b, kv_b, sb_kv_new, q_c, kv_c, bd_kv_new,
     q_d, k_d, vt_d, ds_kv_new, q_ix, kidx_new, kx_d, w_ix) = _project(
        x2, pos, wts['w_rest'], wts['w_ukv'], mla_kv_norm[None])
    seq = lambda a: a.reshape(bsz, s_len, a.shape[-1])
    seq_t = lambda a: jnp.transpose(a.reshape(a.shape[0], bsz, s_len), (1, 0, 2))
    q_a, k_a, v_a, q_b, kv_b, q_c, kv_c, q_d, k_d, q_ix, kx_d = map(
        seq, (q_a, k_a, v_a, q_b, kv_b, q_c, kv_c, q_d, k_d, q_ix, kx_d))
    vt_d, w_ix = seq_t(vt_d), seq_t(w_ix)
    kv_state = lambda a: a.reshape(bsz, s_len, 2, N_HEADS, HEAD_DIM)
    lat_new, kidx_new = seq(lat_new), seq(kidx_new)
    sb_kv_new, bd_kv_new, ds_kv_new = kv_state(sb_kv_new), kv_state(bd_kv_new), kv_state(ds_kv_new)

    if past is None:
        kv_c = jnp.pad(kv_c, ((0, 0), (BAND_WINDOW, 0), (0, 0)))
        band_rows = bd_kv_new[:, s_len - min(BAND_WINDOW, s_len):]
        s_k = s_len
    else:
        past_lat, past_sb, past_band, past_ds, past_kidx = past
        p_len = past_lat.shape[1]
        s_k = p_len + s_len
        rows_bf16 = lambda a: a.reshape(bsz, a.shape[1], -1).astype(BF16)
        k_past, v_past = _latent_keys_values(past_lat.reshape(bsz * p_len, LAT_WIDTH), wts['w_lat'])
        k_a = jnp.concatenate([k_past.reshape(bsz, p_len, -1), k_a], axis=1)
        v_a = jnp.concatenate([v_past.reshape(bsz, p_len, -1), v_a], axis=1)
        kv_b = jnp.concatenate([rows_bf16(past_sb), kv_b], axis=1)
        kv_c = jnp.concatenate([rows_bf16(past_band), kv_c], axis=1)
        past_ds = past_ds.reshape(bsz, p_len, 2 * BRANCH_WIDTH)
        k_d = jnp.concatenate([past_ds[..., :BRANCH_WIDTH].astype(BF16), k_d], axis=1)
        vt_d = jnp.concatenate(
            [jnp.transpose(past_ds[..., BRANCH_WIDTH:], (0, 2, 1)).astype(BF16), vt_d], axis=2)
        kx_d = jnp.concatenate([past_kidx.astype(BF16), kx_d], axis=1)
        band_rows = bd_kv_new

    o_a = _mla_attention(q_a, _pad_keys(k_a), _pad_keys(v_a), q_off)
    o_b = _sb_attention(q_b, _pad_keys(kv_b), q_off)
    o_c = _band_attention(q_c, kv_c, band_rel_bias, q_off)
    n_sel = min(DSA_TOPK, s_k // 4)
    o_d = _dsa_attention(q_ix, w_ix, _pad_keys(kx_d), q_d, _pad_keys(k_d), _pad_keys(vt_d, axis=2),
                         q_off, n_sel)
    o_d = jnp.transpose(o_d, (0, 2, 1))

    flat = lambda o: o.reshape(n, BRANCH_WIDTH)
    x1 = _merge_out_ln(x2, wts['w_gates'], flat(o_a), flat(o_b), flat(o_c), flat(o_d),
                       wts['w_branch'], wts['w_out'], ln1_g[None], ln1_b[None])
    x_out = _ffn_ln(x1, wts['w_gate'], wts['w_up'], wts['w_down'], ln2_g[None], ln2_b[None])
    return x_out.reshape(bsz, s_len, D_MODEL), (lat_new, sb_kv_new, band_rows, ds_kv_new, kidx_new)


def kernel(x_prompt, x_sample, cache_mla_latent, cache_sb_kv, cache_band_kv, cache_dsa_kv, cache_dsa_kidx, w_in, mla_kv_norm, mla_w_uk, mla_w_uv, band_rel_bias, w_branch, w_out, ln1_g, ln1_b, w_gate_up, w_down, ln2_g, ln2_b):
    past_len = cache_mla_latent.shape[2]
    xp, xs = x_prompt, x_sample
    st_p, st_s = [], []
    for l in range(w_in.shape[0]):
        wts = _prepare_weights(w_in[l], mla_w_uk[l], mla_w_uv[l], w_branch[l], w_out[l],
                               w_gate_up[l], w_down[l])
        params = (wts, mla_kv_norm[l], band_rel_bias[l], ln1_g[l], ln1_b[l], ln2_g[l], ln2_b[l])
        xp, new_p = _trunk_layer(xp, 0, None, *params)
        past = (cache_mla_latent[l], cache_sb_kv[l], cache_band_kv[l], cache_dsa_kv[l],
                cache_dsa_kidx[l])
        xs, new_s = _trunk_layer(xs, past_len, past, *params)
        st_p.append(new_p)
        st_s.append(new_s)
    stack = lambda st, i: jnp.stack([s[i] for s in st])
    return (xp, xs) + tuple(stack(st_p, i) for i in range(5)) + tuple(stack(st_s, i) for i in range(5))
```

```python
import functools
import math

import numpy as np
import jax
import jax.numpy as jnp
from jax import lax
from jax.experimental import pallas as pl
from jax.experimental.pallas import tpu as pltpu

D_MODEL = 1024
CHUNK = 64
CHUNK_SHIFT = 6
N_BRANCH = 4
N_HEADS = 4
HEAD_DIM = 64
BRANCH_WIDTH = N_HEADS * HEAD_DIM
MLA_D_C = 128
MLA_D_NOPE = 64
MLA_D_ROPE = 32
MLA_THETA = 10000.0
ROPE_THETA = 500000.0
BAND_LEFT_CHUNKS = 8
BAND_WINDOW = BAND_LEFT_CHUNKS * CHUNK
REL_CLIP = 128
IDX_HEADS = 8
IDX_DIM = 64
DSA_TOPK = 256
D_FF = ((8 * D_MODEL // 3 + 255) // 256) * 256
DEPTH = 2
ALPHA = (2 * DEPTH) ** 0.25
NEG_INF = -1e30
LOG2_E = math.log2(math.e)
SB_RUN_FLOOR = -150.0
LN_EPS = 1e-5
IN_SIZES = (N_HEADS * (MLA_D_NOPE + MLA_D_ROPE), MLA_D_C, MLA_D_ROPE,
            3 * BRANCH_WIDTH, 3 * BRANCH_WIDTH, 3 * BRANCH_WIDTH,
            IDX_HEADS * IDX_DIM, IDX_DIM, IDX_HEADS, N_BRANCH * D_MODEL)
REST_WIDTH = sum(IN_SIZES[:-1])
GATE_WIDTH = IN_SIZES[-1]

LANES = 128
KEY_ALIGN = 512
INT_MIN = -2 ** 31
VMEM_LIMIT = 56 * 1024 * 1024

F32 = jnp.float32
BF16 = jnp.bfloat16


def _cparams(n_axes):
    return pltpu.CompilerParams(dimension_semantics=("arbitrary",) * n_axes,
                                vmem_limit_bytes=VMEM_LIMIT)


def _pick(n, candidates):
    for c in candidates:
        if n % c == 0:
            return c
    return n


def _ones_upper_half(rows):
    lane = lax.broadcasted_iota(jnp.int32, (rows, N_HEADS * LANES), 1)
    return jnp.where((lane & (LANES - 1)) >= HEAD_DIM, 1.0, 0.0)


def _dot_nt(a, b):
    return lax.dot_general(a, b, (((1,), (1,)), ((), ())), preferred_element_type=F32)


def _latent_kernel(lat_ref, w_ref, k_ref, v_ref):
    res = lax.dot_general(lat_ref[0].astype(BF16), w_ref[...], (((0,), (0,)), ((), ())),
                          preferred_element_type=F32)
    k_ref[0] = res[:, :N_HEADS * LANES].astype(BF16)
    v_ref[0] = (res[:, N_HEADS * LANES:] + _ones_upper_half(res.shape[0])).astype(BF16)


def _latent_keys_values(lat_t, w_lat):
    b, _, p = lat_t.shape
    tm = _pick(p, (1024, 512, 256, 128))
    width = N_HEADS * LANES
    return pl.pallas_call(
        _latent_kernel,
        grid=(b, p // tm),
        in_specs=[pl.BlockSpec((1, LAT_WIDTH, tm), lambda bi, i: (bi, 0, i)),
                  pl.BlockSpec((LAT_WIDTH, 2 * width), lambda bi, i: (0, 0))],
        out_specs=[pl.BlockSpec((1, tm, width), lambda bi, i: (bi, i, 0))] * 2,
        out_shape=[jax.ShapeDtypeStruct((b, p, width), BF16)] * 2,
        compiler_params=_cparams(2),
        name="latent_keys_values",
    )(lat_t, w_lat)


SEG_AQ = 0
SEG_CKV = 4
SEG_KR = 5
SEG_SB = 6
SEG_BD = 12
SEG_DS = 18
SEG_IXQ = 24
SEG_IXK = 28
N_SEG = 30
IXW_LANE = 96
PROJ_WIDTH = N_SEG * LANES
MLA_ROPE_LANE = 64
LAT_WIDTH = MLA_D_C + MLA_D_ROPE


def _proj_kernel(x_ref, w_ref, wukv_ref, g_ref, ca_ref, sma_ref, spa_ref, cp_ref, smp_ref, spp_ref,
                 qa_ref, lat_ref, ka_ref, va_ref,
                 qb_ref, kvb_ref, sbkv_ref, qc_ref, kvc_ref, bdkv_ref,
                 qd_ref, kd_ref, vtd_ref, dskv_ref, qix_ref, kidx_ref, kx_ref, wix_ref):
    proj = jnp.dot(x_ref[...].astype(BF16), w_ref[...], preferred_element_type=F32)
    seg = lambda s, n=1: proj[:, s * LANES:(s + n) * LANES]

    def rope(t, c_ref, sm_ref, sp_ref, half):
        return (t * c_ref[...] + pltpu.roll(t, LANES - half, 1) * sm_ref[...]
                + pltpu.roll(t, half, 1) * sp_ref[...])

    rope_a = functools.partial(rope, c_ref=ca_ref, sm_ref=sma_ref, sp_ref=spa_ref, half=MLA_D_ROPE // 2)
    rope_p = functools.partial(rope, c_ref=cp_ref, sm_ref=smp_ref, sp_ref=spp_ref, half=HEAD_DIM // 8)
    head_scale = HEAD_DIM ** -0.5 * LOG2_E

    mla_scale = (MLA_D_NOPE + MLA_D_ROPE) ** -0.5 * LOG2_E
    for h in range(N_HEADS):
        qa_ref[:, h * LANES:(h + 1) * LANES] = (rope_a(seg(SEG_AQ + h)) * mla_scale).astype(BF16)
    ckv = seg(SEG_CKV)
    ckv = ckv * lax.rsqrt(jnp.mean(ckv * ckv, axis=-1, keepdims=True) + LN_EPS) * g_ref[...]
    kr = rope_a(seg(SEG_KR))
    lat_ref[:MLA_D_C, :] = ckv.T
    lat_ref[MLA_D_C:, :] = kr.T[MLA_ROPE_LANE:MLA_ROPE_LANE + MLA_D_ROPE, :]
    kv_a = jnp.dot(ckv.astype(BF16), wukv_ref[...], preferred_element_type=F32)
    for h in range(N_HEADS):
        ka_ref[:, h * LANES:(h + 1) * LANES] = (kv_a[:, h * LANES:(h + 1) * LANES] + kr).astype(BF16)
    va_ref[...] = (kv_a[:, N_HEADS * LANES:] + _ones_upper_half(kv_a.shape[0])).astype(BF16)

    for s0, q_ref, kv_ref, new_ref in ((SEG_SB, qb_ref, kvb_ref, sbkv_ref),
                                       (SEG_BD, qc_ref, kvc_ref, bdkv_ref)):
        q_ref[...] = (seg(s0, 2) * head_scale).astype(BF16)
        kv = seg(s0 + 2, 4)
        new_ref[...] = kv.T
        kv_ref[...] = kv.astype(BF16)

    for p in range(2):
        qd_ref[:, p * LANES:(p + 1) * LANES] = (rope_p(seg(SEG_DS + p)) * head_scale).astype(BF16)
        k_rot = rope_p(seg(SEG_DS + 2 + p))
        dskv_ref[p * LANES:(p + 1) * LANES, :] = k_rot.T
        kd_ref[:, p * LANES:(p + 1) * LANES] = k_rot.astype(BF16)
    vt = seg(SEG_DS + 4, 2).T
    dskv_ref[BRANCH_WIDTH:, :] = vt
    vtd_ref[...] = vt.astype(BF16)
    for p in range(IDX_HEADS // 2):
        qix_ref[:, p * LANES:(p + 1) * LANES] = (rope_p(seg(SEG_IXQ + p)) * IDX_DIM ** -0.5).astype(BF16)
    ixk = rope_p(seg(SEG_IXK))
    ixk_t = ixk.T
    kidx_ref[...] = ixk_t[:IDX_DIM, :]
    kx_ref[...] = ixk[:, :IDX_DIM].astype(BF16)
    wix_ref[...] = ixk_t[IXW_LANE:IXW_LANE + IDX_HEADS, :]


def _rope_tables(pos, theta, width, starts):
    half = width // 2
    inv = jnp.exp(jnp.arange(half, dtype=F32) * (-2.0 * math.log(theta) / width))
    ang = pos.astype(F32)[:, None] * inv[None, :]
    cos, sin = jnp.cos(ang), jnp.sin(ang)
    n = pos.shape[0]
    fill = lambda v, w: jnp.full((n, w), v, F32)
    c, sm, sp, lane = [], [], [], 0
    for s in sorted(starts):
        c += [fill(1.0, s - lane), cos, cos]
        sm += [fill(0.0, s - lane), -sin, fill(0.0, half)]
        sp += [fill(0.0, s - lane + half), sin]
        lane = s + width
    cat = lambda parts, v: jnp.concatenate(parts + [fill(v, LANES - lane)], axis=1)
    return cat(c, 1.0), cat(sm, 0.0), cat(sp, 0.0)


def _project(x, pos, w_rest, w_ukv, kv_norm):
    n = x.shape[0]
    ts = _pick(n, (512, 256, 128, 64))
    tables = (_rope_tables(pos, MLA_THETA, MLA_D_ROPE, (MLA_ROPE_LANE,))
              + _rope_tables(pos, ROPE_THETA, HEAD_DIM // 4, (0, HEAD_DIM)))
    row = lambda i: (i, 0)
    fixed = lambda i: (0, 0)
    rows = lambda w: pl.BlockSpec((ts, w), row)
    out = lambda w, dt: jax.ShapeDtypeStruct((n, w), dt)
    t = True
    specs = [
        (N_HEADS * LANES, BF16, 0), (LAT_WIDTH, F32, t), (N_HEADS * LANES, BF16, 0), (N_HEADS * LANES, BF16, 0),
        (BRANCH_WIDTH, BF16, 0), (2 * BRANCH_WIDTH, BF16, 0), (2 * BRANCH_WIDTH, F32, t),
        (BRANCH_WIDTH, BF16, 0), (2 * BRANCH_WIDTH, BF16, 0), (2 * BRANCH_WIDTH, F32, t),
        (BRANCH_WIDTH, BF16, 0), (BRANCH_WIDTH, BF16, 0), (BRANCH_WIDTH, BF16, t), (2 * BRANCH_WIDTH, F32, t),
        (IDX_HEADS * IDX_DIM, BF16, 0), (IDX_DIM, F32, t), (IDX_DIM, BF16, 0), (IDX_HEADS, F32, t)]
    out_specs, out_shape = [], []
    for width, dt, feature_major in specs:
        if feature_major:
            out_specs.append(pl.BlockSpec((width, ts), lambda i: (0, i)))
            out_shape.append(jax.ShapeDtypeStruct((width, n), dt))
        else:
            out_specs.append(rows(width))
            out_shape.append(out(width, dt))
    return pl.pallas_call(
        _proj_kernel,
        grid=(n // ts,),
        in_specs=[rows(D_MODEL),
                  pl.BlockSpec((D_MODEL, PROJ_WIDTH), fixed, pipeline_mode=pl.Buffered(1)),
                  pl.BlockSpec((MLA_D_C, 2 * N_HEADS * LANES), fixed),
                  pl.BlockSpec((1, MLA_D_C), fixed)] + [rows(LANES)] * 6,
        out_specs=out_specs,
        out_shape=out_shape,
        compiler_params=_cparams(1),
        name="project_prepare",
    )(x, w_rest, w_ukv, kv_norm, *tables)


def _mla_kernel(q_ref, k_ref, v_ref, o_ref, *, qb, kb, q_off, skp):
    q0 = pl.program_id(1) * qb
    hi = jnp.minimum((((q_off + q0 + qb - 1) >> CHUNK_SHIFT) + 1) * CHUNK, skp)
    nkb = (hi + kb - 1) // kb
    qpos = q_off + q0 + lax.broadcasted_iota(jnp.int32, (qb, 1), 0)
    cend = ((qpos >> CHUNK_SHIFT) + 1) * CHUNK
    col = lax.broadcasted_iota(jnp.int32, (qb, kb), 1)
    lane = lax.broadcasted_iota(jnp.int32, (qb, LANES), 1)
    n_full = jnp.minimum(((((q_off + q0) >> CHUNK_SHIFT) + 1) * CHUNK) // kb, nkb)

    def body(i, carry, masked):
        k0 = pl.multiple_of(i * kb, kb)
        s = [_dot_nt(q_ref[0, :, h * LANES:(h + 1) * LANES],
                     k_ref[0, pl.ds(k0, kb), h * LANES:(h + 1) * LANES]) for h in range(N_HEADS)]
        if masked:
            vis = col + k0 < cend
            s = [jnp.where(vis, sh, NEG_INF) for sh in s]
        p, new = [], []
        for h in range(N_HEADS):
            m, acc = carry[2 * h:2 * h + 2]
            m_new = jnp.maximum(m, jnp.max(s[h], axis=1, keepdims=True))
            p.append(jnp.exp2((s[h] - m_new).astype(BF16)))
            new += [m_new, jnp.exp2(m - m_new) * acc]
        for h in range(N_HEADS):
            vblk = v_ref[0, pl.ds(k0, kb), h * LANES:(h + 1) * LANES]
            new[2 * h + 1] = new[2 * h + 1] + jnp.dot(p[h], vblk, preferred_element_type=F32)
        return tuple(new)

    init = (jnp.full((qb, 1), NEG_INF, F32), jnp.zeros((qb, LANES), F32)) * N_HEADS
    carry = lax.fori_loop(0, n_full, functools.partial(body, masked=False), init)
    carry = lax.fori_loop(n_full, nkb, functools.partial(body, masked=True), carry)
    outs = [carry[2 * h + 1] / pltpu.roll(carry[2 * h + 1], HEAD_DIM, 1) for h in range(N_HEADS)]
    for pair in range(N_HEADS // 2):
        o_ref[0, :, pair * LANES:(pair + 1) * LANES] = jnp.where(
            lane < HEAD_DIM, outs[2 * pair], pltpu.roll(outs[2 * pair + 1], HEAD_DIM, 1))


def _mla_attention(q, k, v, q_off):
    b, sq, _ = q.shape
    skp = k.shape[1]
    qb = _pick(sq, (512, 256, 128, 64))
    kb = _pick(skp, (1024, KEY_ALIGN))
    kern = functools.partial(_mla_kernel, qb=qb, kb=kb, q_off=q_off, skp=skp)
    return pl.pallas_call(
        kern,
        grid=(b, sq // qb),
        in_specs=[pl.BlockSpec((1, qb, N_HEADS * LANES), lambda bi, qi: (bi, qi, 0)),
                  pl.BlockSpec((1, skp, N_HEADS * LANES), lambda bi, qi: (bi, 0, 0),
                               pipeline_mode=pl.Buffered(1)),
                  pl.BlockSpec((1, skp, N_HEADS * LANES), lambda bi, qi: (bi, 0, 0),
                               pipeline_mode=pl.Buffered(1))],
        out_specs=pl.BlockSpec((1, qb, BRANCH_WIDTH), lambda bi, qi: (bi, qi, 0)),
        out_shape=jax.ShapeDtypeStruct((b, sq, BRANCH_WIDTH), F32),
        compiler_params=_cparams(2),
        name="mla_attention",
    )(q, k, v)


def _head_mask(x_pair, h):
    lane = lax.broadcasted_iota(jnp.int32, x_pair.shape, 1)
    keep = (lane < HEAD_DIM) if h % 2 == 0 else (lane >= HEAD_DIM)
    return jnp.where(keep, x_pair, jnp.zeros_like(x_pair))


def _sb_kernel(q_ref, kv_ref, t_ref, o_ref, *, qb, kb, sub, q_off):
    q0 = pl.program_id(1) * qb
    hi = q_off + q0 + qb - 1
    nkb = (hi + kb - 1) // kb
    qpos = q_off + q0 + lax.broadcasted_iota(jnp.int32, (qb, 1), 0)
    col = lax.broadcasted_iota(jnp.int32, (qb, sub), 1)
    lane = lax.broadcasted_iota(jnp.int32, (qb, LANES), 1)
    tri = t_ref[...]
    n_full = jnp.minimum((q_off + q0) // kb, nkb)
    qh = [_head_mask(q_ref[0, :, (h // 2) * LANES:(h // 2 + 1) * LANES], h) for h in range(N_HEADS)]

    def body(i, carry, masked):
        k0 = pl.multiple_of(i * kb, kb)
        carry = list(carry)
        units = [(j, h) for j in reversed(range(kb // sub)) for h in range(N_HEADS)]
        strict = {j: col + (k0 + j * sub) < qpos for j in range(kb // sub)} if masked else None
        z = {}
        for j, h in units:
            pair = h // 2
            kblk = kv_ref[0, pl.ds(k0 + j * sub, sub), pair * LANES:(pair + 1) * LANES]
            z[j, h] = _dot_nt(qh[h], kblk)
        log_1m, suffix = {}, {}
        for u in units:
            nz = -z[u]
            t = jnp.minimum(nz, 0.0) - jnp.log2(1.0 + jnp.exp2(jnp.minimum(z[u], nz)))
            if masked:
                t = jnp.where(strict[u[0]], t, 0.0)
            log_1m[u] = t
        for u in units:
            hi_part = log_1m[u].astype(BF16)
            lo_part = (log_1m[u] - hi_part.astype(F32)).astype(BF16)
            suffix[u] = (jnp.dot(hi_part, tri, preferred_element_type=F32)
                         + jnp.dot(lo_part, tri, preferred_element_type=F32))
        for j, h in units:
            u = (j, h)
            run, acc = carry[2 * h:2 * h + 2]
            expo = z[u] + log_1m[u] + suffix[u] + run
            if masked:
                expo = jnp.where(strict[j], expo, NEG_INF)
            a = jnp.exp2(expo).astype(BF16)
            pair = h // 2
            vblk = kv_ref[0, pl.ds(k0 + j * sub, sub),
                          BRANCH_WIDTH + pair * LANES:BRANCH_WIDTH + (pair + 1) * LANES]
            carry[2 * h + 1] = acc + jnp.dot(a, vblk, preferred_element_type=F32)
            carry[2 * h] = run + jnp.sum(log_1m[u], axis=1, keepdims=True)
        return tuple(carry)

    def live(state):
        i, carry = state[0], state[1:]
        top = carry[0]
        for h in range(1, N_HEADS):
            top = jnp.maximum(top, carry[2 * h])
        return (i >= 0) & (jnp.max(top) >= SB_RUN_FLOOR)

    def step(masked):
        return lambda state: (state[0] - 1,) + body(state[0], state[1:], masked)

    init = (jnp.zeros((qb, 1), F32), jnp.zeros((qb, LANES), F32)) * N_HEADS
    carry = lax.fori_loop(0, nkb - n_full, lambda i, c: body(nkb - 1 - i, c, True), init)
    state = lax.while_loop(live, step(False), (n_full - 1,) + tuple(carry))
    carry = state[1:]
    outs = [carry[2 * h + 1] for h in range(N_HEADS)]
    for pair in range(N_HEADS // 2):
        o_ref[0, :, pair * LANES:(pair + 1) * LANES] = jnp.where(
            lane < HEAD_DIM, outs[2 * pair], outs[2 * pair + 1])


def _sb_attention(q, kv, q_off):
    b, sq, _ = q.shape
    skp = kv.shape[1]
    qb = _pick(sq, (256, 128, 64))
    sub = 256
    kb = KEY_ALIGN
    tri =jnp.asarray(np.tril(np.ones((sub, sub), np.float32), -1), BF16)
    kern = functools.partial(_sb_kernel, qb=qb, kb=kb, sub=sub, q_off=q_off)
    return pl.pallas_call(
        kern,
        grid=(b, sq // qb),
        in_specs=[pl.BlockSpec((1, qb, BRANCH_WIDTH), lambda bi, qi: (bi, qi, 0)),
                  pl.BlockSpec((1, skp, 2 * BRANCH_WIDTH), lambda bi, qi: (bi, 0, 0),
                               pipeline_mode=pl.Buffered(1)),
                  pl.BlockSpec((sub, sub), lambda bi, qi: (0, 0))],
        out_specs=pl.BlockSpec((1, qb, BRANCH_WIDTH), lambda bi, qi: (bi, qi, 0)),
        out_shape=jax.ShapeDtypeStruct((b, sq, BRANCH_WIDTH), F32),
        compiler_params=_cparams(2),
        name="stick_breaking_attention",
    )(q, kv, tri)


def _band_kernel(bias_ref, q_ref, kv_ref, o_ref, tile_ref, *, qb, win, q_off):
    first = (pl.program_id(0) == 0) & (pl.program_id(1) == 0)
    row = lax.broadcasted_iota(jnp.int32, (qb, win), 0)
    col = lax.broadcasted_iota(jnp.int32, (qb, win), 1)

    @pl.when(first)
    def _():
        width = win + qb
        c = lax.broadcasted_iota(jnp.int32, (8, width), 1)
        rel = jnp.where(c < win, jnp.clip(BAND_WINDOW - c, -REL_CLIP, REL_CLIP) + REL_CLIP, 2 * REL_CLIP)

        def fill(r, rows):
            hit = rel == r
            return tuple(jnp.where(hit, bias_ref[h, r], rows[h]) for h in range(N_HEADS))

        rows = lax.fori_loop(0, 2 * REL_CLIP + 1, fill, (jnp.zeros((8, width), F32),) * N_HEADS)
        qch = row >> CHUNK_SHIFT
        kch = col >> CHUNK_SHIFT
        in_band = (kch >= qch) & (kch <= qch + BAND_LEFT_CHUNKS)
        for h in range(N_HEADS):
            table = jnp.broadcast_to(rows[h][:1], (qb, width))
            skewed = pltpu.roll(table, 0, 1, stride=1, stride_axis=0)[:, :win]
            tile_ref[h] = jnp.where(in_band, skewed * LOG2_E, NEG_INF)

    q0 = pl.multiple_of(pl.program_id(1) * qb, qb)
    kpos = col + (q_off + q0 - BAND_WINDOW)
    lane = lax.broadcasted_iota(jnp.int32, (qb, LANES), 1)
    outs = []
    for h in range(N_HEADS):
        pair = h // 2
        qh = _head_mask(q_ref[0, :, pair * LANES:(pair + 1) * LANES], h)
        kwin = kv_ref[0, pl.ds(q0, win), pair * LANES:(pair + 1) * LANES]
        s = _dot_nt(qh, kwin) + tile_ref[h]
        s = jnp.where(kpos >= 0, s, NEG_INF)
        m = jnp.max(s, axis=1, keepdims=True)
        p = jnp.exp2(s - m)
        l = jnp.sum(p, axis=1, keepdims=True)
        vwin = kv_ref[0, pl.ds(q0, win),
                      BRANCH_WIDTH + pair * LANES:BRANCH_WIDTH + (pair + 1) * LANES]
        outs.append(jnp.dot(p.astype(BF16), vwin, preferred_element_type=F32) / l)
    for pair in range(N_HEADS // 2):
        o_ref[0, :, pair * LANES:(pair + 1) * LANES] = jnp.where(
            lane < HEAD_DIM, outs[2 * pair], outs[2 * pair + 1])


def _band_attention(q, kv, rel_bias, q_off):
    b, sq, _ = q.shape
    sk = kv.shape[1]
    qb = _pick(sq, (256, 128, 64))
    win = qb + BAND_WINDOW
    kern = functools.partial(_band_kernel, qb=qb, win=win, q_off=q_off)
    return pl.pallas_call(
        kern,
        grid=(b, sq // qb),
        in_specs=[pl.BlockSpec(memory_space=pltpu.SMEM),
                  pl.BlockSpec((1, qb, BRANCH_WIDTH), lambda bi, qi: (bi, qi, 0)),
                  pl.BlockSpec((1, sk, 2 * BRANCH_WIDTH), lambda bi, qi: (bi, 0, 0),
                               pipeline_mode=pl.Buffered(1))],
        out_specs=pl.BlockSpec((1, qb, BRANCH_WIDTH), lambda bi, qi: (bi, qi, 0)),
        out_shape=jax.ShapeDtypeStruct((b, sq, BRANCH_WIDTH), F32),
        scratch_shapes=[pltpu.VMEM((N_HEADS, qb, win), F32)],
        compiler_params=_cparams(2),
        name="band_attention",
    )(rel_bias, q, kv)


def _dsa_kernel(qi_ref, w_ref, kx_ref, q_ref, k_ref, vt_ref, o_ref, key_ref, key16_ref, cut_ref,
                sa_ref, sb_ref,
                *, qb, kb, q_off, skp, n_sel):
    q0 = pl.program_id(1) * qb
    hi = jnp.minimum((((q_off + q0 + qb - 1) >> CHUNK_SHIFT) + 1) * CHUNK, skp)
    nkb = (hi + kb - 1) // kb
    qpos = q_off + q0 + lax.broadcasted_iota(jnp.int32, (1, qb), 1)
    cend = ((qpos >> CHUNK_SHIFT) + 1) * CHUNK
    row = lax.broadcasted_iota(jnp.int32, (kb, qb), 0)
    float_key = lambda bits: jnp.where(bits < 0, INT_MIN - bits, bits)
    neg_key = float_key(lax.bitcast_convert_type(jnp.full((1, 1), NEG_INF, F32), jnp.int32))
    n_full = jnp.minimum(((((q_off + q0) >> CHUNK_SHIFT) + 1) * CHUNK) // kb, nkb)

    def score_block(i, c, masked):
        k0 = pl.multiple_of(i * kb, kb)
        kx = kx_ref[0, pl.ds(k0, kb), :]
        r = [_dot_nt(kx, qi_ref[0, :, j * IDX_DIM:(j + 1) * IDX_DIM]) for j in range(IDX_HEADS)]
        score = jnp.zeros((kb, qb), F32)
        for j in range(IDX_HEADS):
            wj = w_ref[0, j:j + 1, :] * (IDX_HEADS ** -0.5)
            score = score + wj * jnp.maximum(r[j], 0.0)
        if masked:
            score = jnp.where(row + k0 < cend, score, NEG_INF)
        keys = float_key(lax.bitcast_convert_type(score, jnp.int32))
        key_ref[pl.ds(k0, kb), :] = keys
        key16_ref[pl.ds(k0, kb), :] = (keys >> 16).astype(jnp.int16)
        return c

    lax.fori_loop(0, n_full, functools.partial(score_block, masked=False), 0)
    lax.fori_loop(n_full, nkb, functools.partial(score_block, masked=True), 0)

    slab = 64 if kb % 64 == 0 else kb

    def count_ge(cand):
        def blk(i, cnt):
            k0 = pl.multiple_of(i * kb, kb)
            for s in range(kb // slab):
                keys = key_ref[pl.ds(k0 + s * slab, slab), :]
                cnt = cnt + jnp.where(keys >= cand, 1.0, 0.0)
            return cnt
        part = lax.fori_loop(0, nkb, blk, jnp.zeros((slab, qb), F32))
        return jnp.sum(part, axis=0, keepdims=True)

    slab16 = 128 if kb % 128 == 0 else kb

    def count16_ge(cand16):
        def blk(i, cnt):
            k0 = pl.multiple_of(i * kb, kb)
            for s in range(kb // slab16):
                k16 = key16_ref[pl.ds(k0 + s * slab16, slab16), :]
                cnt = cnt + jnp.where(k16 >= cand16, jnp.int16(1), jnp.int16(0))
            return cnt
        part = lax.fori_loop(0, nkb, blk, jnp.zeros((slab16, qb), jnp.int16))
        return jnp.sum(part.astype(F32), axis=0, keepdims=True)

    def undecided(t_end):
        def cond(state):
            t, _, n_ge = state
            return (t < t_end) & (jnp.max(jnp.abs(n_ge - n_sel)) > 0.0)
        return cond

    def bisect(to_16bit):
        def step(state):
            t, thr, n_ge = state
            cand = thr + jnp.left_shift(jnp.int32(1), 31 - t)
            cnt = count16_ge(to_16bit(cand))
            ok = cnt >= n_sel
            return t + 1, jnp.where(ok, cand, thr), jnp.where(ok, cnt, n_ge)
        return step

    stored = (nkb * kb).astype(F32)
    high_step = bisect(lambda cand: (cand >> 16).astype(jnp.int16))
    state = lax.fori_loop(
        0, 16, lambda _, st: high_step(st),
        (jnp.int32(0), jnp.full((1, qb), INT_MIN, jnp.int32), jnp.full((1, qb), stored, F32)))

    @pl.when(undecided(32)(state))
    def _():
        thr_hi = state[1] >> 16

        def low_halves(i, c):
            k0 = pl.multiple_of(i * kb, kb)
            keys = key_ref[pl.ds(k0, kb), :]
            hi16 = keys >> 16
            low = jnp.where(hi16 == thr_hi, (keys & 0xFFFF) - 0x8000,
                            jnp.where(hi16 > thr_hi, 0x7FFF, -0x8000))
            key16_ref[pl.ds(k0, kb), :] = low.astype(jnp.int16)
            return c

        lax.fori_loop(0, nkb, low_halves, 0)

    low_step = bisect(lambda cand: ((cand & 0xFFFF) - 0x8000).astype(jnp.int16))
    _, thr, n_ge = lax.while_loop(
        undecided(32), lambda st: low_step(low_step(low_step(low_step(st)))), state)
    excess = jnp.where((n_ge > n_sel) & (thr > neg_key), 1.0, 0.0)
    cut_ref[...] = jnp.full((1, qb), skp, jnp.int32)

    @pl.when(jnp.max(excess) > 0.0)
    def _():
        need = n_sel - count_ge(thr + 1)

        def count_tied_below(limit):
            def blk(i, cnt):
                k0 = pl.multiple_of(i * kb, kb)
                tied = (key_ref[pl.ds(k0, kb), :] == thr) & (row + k0 < limit)
                return cnt + jnp.sum(jnp.where(tied, 1.0, 0.0), axis=0, keepdims=True)
            return lax.fori_loop(0, nkb, blk, jnp.zeros((1, qb), F32))

        n_bits = max(1, (skp - 1).bit_length())

        def bisect_cut(t, cut):
            cand = cut + jnp.left_shift(jnp.int32(1), n_bits - 1 - t)
            return jnp.where(count_tied_below(cand) < need, cand, cut)

        cut = lax.fori_loop(0, n_bits, bisect_cut, jnp.zeros((1, qb), jnp.int32))
        cut_ref[...] = jnp.where(excess > 0.0, cut + 1, skp)

    cut = cut_ref[...]

    qh = [_head_mask(q_ref[0, :, (h // 2) * LANES:(h // 2 + 1) * LANES], h) for h in range(N_HEADS)]

    kb3 = kb // 2
    row3 = lax.broadcasted_iota(jnp.int32, (kb3, qb), 0)

    def scores(i):
        k0 = pl.multiple_of(i * kb3, kb3)
        return tuple(_dot_nt(k_ref[0, pl.ds(k0, kb3), (h // 2) * LANES:(h // 2 + 1) * LANES], qh[h])
                     for h in range(N_HEADS))

    def attend(i, s, carry, plain):
        k0 = pl.multiple_of(i * kb3, kb3)
        keys = key_ref[pl.ds(k0, kb3), :]
        if plain:
            sel = keys >= thr
        else:
            kpos = row3 + k0
            sel = (keys >= thr) & ((keys != thr) | (kpos < cut)) & (kpos < cend)
        bias = jnp.where(sel, 0.0, NEG_INF)
        p, new = [], []
        for h in range(N_HEADS):
            m, acc = carry[2 * h:2 * h + 2]
            sh = s[h] + bias
            m_new = jnp.maximum(m, jnp.max(sh, axis=0, keepdims=True))
            p.append(jnp.exp2((sh - m_new).astype(BF16)))
            new += [m_new, jnp.exp2(m - m_new) * acc]
        for h in range(N_HEADS):
            vblk = jnp.concatenate([vt_ref[0, h * HEAD_DIM:(h + 1) * HEAD_DIM, pl.ds(k0, kb3)],
                                    jnp.ones((LANES - HEAD_DIM, kb3), BF16)], axis=0)
            new[2 * h + 1] = new[2 * h + 1] + jnp.dot(vblk, p[h], preferred_element_type=F32)
        return tuple(new)

    def put_scores(slot_ref, i):
        for h, sh in enumerate(scores(i)):
            slot_ref[h] = sh

    def attend_from(slot_ref, i, carry, plain):
        return attend(i, tuple(slot_ref[h] for h in range(N_HEADS)), carry, plain)

    def body(j, carry, plain):
        put_scores(sb_ref, 2 * j + 1)
        carry = attend_from(sa_ref, 2 * j, carry, plain)
        put_scores(sa_ref, 2 * j + 2)
        return attend_from(sb_ref, 2 * j + 1, carry, plain)

    init = (jnp.full((1, qb), NEG_INF, F32), jnp.zeros((LANES, qb), F32)) * N_HEADS
    n_pairs = nkb
    n_plain = jnp.where(jnp.max(excess) > 0.0, 0, jnp.minimum(n_full, n_pairs - 1))
    put_scores(sa_ref, 0)
    carry = lax.fori_loop(0, n_plain, functools.partial(body, plain=True), init)
    carry = lax.fori_loop(n_plain, n_pairs - 1, functools.partial(body, plain=False), carry)
    last = 2 * (n_pairs - 1)
    put_scores(sb_ref, last + 1)
    carry = attend_from(sa_ref, last, carry, False)
    carry = attend_from(sb_ref, last + 1, carry, False)
    for h in range(N_HEADS):
        acc = carry[2 * h + 1]
        o_ref[0, h * HEAD_DIM:(h + 1) * HEAD_DIM, :] = acc[:HEAD_DIM, :] / acc[HEAD_DIM:HEAD_DIM + 1, :]


def _dsa_attention(qi, w, kx, q, k, vt, q_off, n_sel):
    b, sq, _ = q.shape
    skp = k.shape[1]
    qb = _pick(sq, (256, 128, 64))
    kb = _pick(skp, (1024, KEY_ALIGN))
    kern = functools.partial(_dsa_kernel, qb=qb, kb=kb, q_off=q_off, skp=skp, n_sel=n_sel)
    return pl.pallas_call(
        kern,
        grid=(b, sq // qb),
        in_specs=[pl.BlockSpec((1, qb, IDX_HEADS * IDX_DIM), lambda bi, qi_: (bi, qi_, 0)),
                  pl.BlockSpec((1, IDX_HEADS, qb), lambda bi, qi_: (bi, 0, qi_)),
                  pl.BlockSpec((1, skp, IDX_DIM), lambda bi, qi_: (bi, 0, 0),
                               pipeline_mode=pl.Buffered(1)),
                  pl.BlockSpec((1, qb, BRANCH_WIDTH), lambda bi, qi_: (bi, qi_, 0)),
                  pl.BlockSpec((1, skp, BRANCH_WIDTH), lambda bi, qi_: (bi, 0, 0),
                               pipeline_mode=pl.Buffered(1)),
                  pl.BlockSpec((1, BRANCH_WIDTH, skp), lambda bi, qi_: (bi, 0, 0),
                               pipeline_mode=pl.Buffered(1))],
        out_specs=pl.BlockSpec((1, BRANCH_WIDTH, qb), lambda bi, qi_: (bi, 0, qi_)),
        out_shape=jax.ShapeDtypeStruct((b, BRANCH_WIDTH, sq), F32),
        scratch_shapes=[pltpu.VMEM((skp, qb), jnp.int32), pltpu.VMEM((skp, qb), jnp.int16),
                        pltpu.VMEM((1, qb), jnp.int32),
                        pltpu.VMEM((N_HEADS, kb // 2, qb), F32), pltpu.VMEM((N_HEADS, kb // 2, qb), F32)],
        compiler_params=_cparams(2),
        name="dsa_attention",
    )(qi, w, kx, q, k, vt)


def _layer_norm(z, g, b):
    mu = jnp.mean(z, axis=-1, keepdims=True)
    zc = z - mu
    var = jnp.mean(zc * zc, axis=-1, keepdims=True)
    return zc * lax.rsqrt(var + LN_EPS) * g + b


def _merge_kernel(x_ref, wgate_ref, oa_ref, ob_ref, oc_ref, od_ref, wb_ref, wo_ref, g_ref, b_ref,
                  y_ref):
    xb = x_ref[...].astype(BF16)
    logits = [jnp.dot(xb, wgate_ref[:, n * D_MODEL:(n + 1) * D_MODEL], preferred_element_type=F32)
              for n in range(N_BRANCH)]
    branch = [jnp.dot(o_ref[...].astype(BF16), wb_ref[n], preferred_element_type=F32)
              for n, o_ref in enumerate((oa_ref, ob_ref, oc_ref, od_ref))]
    merged = jax.nn.sigmoid(logits[0]) * branch[0]
    for n in range(1, N_BRANCH):
        merged = merged + jax.nn.sigmoid(logits[n]) * branch[n]
    y = jnp.dot(merged.astype(BF16), wo_ref[...], preferred_element_type=F32)
    y_ref[...] = _layer_norm(ALPHA * x_ref[...] + y, g_ref[...], b_ref[...])


def _merge_out_ln(x, w_gate, o_a, o_b, o_c, o_d, w_branch, w_out, g, b):
    n = x.shape[0]
    ts = _pick(n, (512, 256, 128, 64))
    row = lambda i: (i, 0)
    obs = pl.BlockSpec((ts, BRANCH_WIDTH), row)
    return pl.pallas_call(
        _merge_kernel,
        grid=(n // ts,),
        in_specs=[pl.BlockSpec((ts, D_MODEL), row),
                  pl.BlockSpec((D_MODEL, GATE_WIDTH), lambda i: (0, 0), pipeline_mode=pl.Buffered(1)),
                  obs, obs, obs, obs,
                  pl.BlockSpec((N_BRANCH, BRANCH_WIDTH, D_MODEL), lambda i: (0, 0, 0)),
                  pl.BlockSpec((D_MODEL, D_MODEL), lambda i: (0, 0)),
                  pl.BlockSpec((1, D_MODEL), lambda i: (0, 0)),
                  pl.BlockSpec((1, D_MODEL), lambda i: (0, 0))],
        out_specs=pl.BlockSpec((ts, D_MODEL), row),
        out_shape=jax.ShapeDtypeStruct((n, D_MODEL), F32),
        compiler_params=_cparams(1),
        name="merge_out_ln",
    )(x, w_gate, o_a, o_b, o_c, o_d, w_branch, w_out, g, b)


def _ffn_kernel(x_ref, wg_ref, wu_ref, wd_ref, g_ref, b_ref, y_ref, acc_ref):
    f = pl.program_id(1)
    xb = x_ref[...].astype(BF16)
    gate = jnp.dot(xb, wg_ref[...], preferred_element_type=F32)
    up = jnp.dot(xb, wu_ref[...], preferred_element_type=F32)
    hidden = (gate * jax.nn.sigmoid(gate) * up).astype(BF16)
    part = jnp.dot(hidden, wd_ref[...], preferred_element_type=F32)

    @pl.when(f == 0)
    def _():
        acc_ref[...] = part

    @pl.when(f > 0)
    def _():
        acc_ref[...] += part

    @pl.when(f == pl.num_programs(1) - 1)
    def _():
        y_ref[...] = _layer_norm(ALPHA * x_ref[...] + acc_ref[...], g_ref[...], b_ref[...])


def _ffn_ln(x, w_gate, w_up, w_down, g, b):
    n = x.shape[0]
    ts = _pick(n, (1024, 512, 256, 128, 64))
    tf = D_FF // 2
    return pl.pallas_call(
        _ffn_kernel,
        grid=(n // ts, D_FF // tf),
        in_specs=[pl.BlockSpec((ts, D_MODEL), lambda i, f: (i, 0)),
                  pl.BlockSpec((D_MODEL, tf), lambda i, f: (0, f)),
                  pl.BlockSpec((D_MODEL, tf), lambda i, f: (0, f)),
                  pl.BlockSpec((tf, D_MODEL), lambda i, f: (f, 0)),
                  pl.BlockSpec((1, D_MODEL), lambda i, f: (0, 0)),
                  pl.BlockSpec((1, D_MODEL), lambda i, f: (0, 0))],
        out_specs=pl.BlockSpec((ts, D_MODEL), lambda i, f: (i, 0)),
        out_shape=jax.ShapeDtypeStruct((n, D_MODEL), F32),
        scratch_shapes=[pltpu.VMEM((ts, D_MODEL), F32)],
        compiler_params=_cparams(2),
        name="ffn_ln",
    )(x, w_gate, w_up, w_down, g, b)


def _pad_keys(a, axis=1):
    n = a.shape[axis]
    pad = (-n) % KEY_ALIGN
    if pad == 0:
        return a
    widths = [(0, 0)] * a.ndim
    widths[axis] = (0, pad)
    return jnp.pad(a, widths)


def _prepare_weights(w_in, mla_w_uk, mla_w_uv, w_branch, w_out, w_gate_up, w_down):
    w_rest = jnp.concatenate([w_in, jnp.zeros((D_MODEL, 1), w_in.dtype)], axis=1)
    w_rest = jnp.take(w_rest, _projection_columns(), axis=1).astype(BF16)
    w_uk = jnp.pad(mla_w_uk, ((0, 0), (0, 0), (0, LANES - MLA_D_NOPE))).reshape(MLA_D_C, -1)
    w_uv = jnp.pad(mla_w_uv, ((0, 0), (0, 0), (0, LANES - HEAD_DIM))).reshape(MLA_D_C, -1)
    w_ukv = jnp.concatenate([w_uk, w_uv], axis=1).astype(BF16)
    place = np.zeros((MLA_D_ROPE, 2 * N_HEADS * LANES), np.float32)
    for h in range(N_HEADS):
        place[np.arange(MLA_D_ROPE), h * LANES + MLA_ROPE_LANE + np.arange(MLA_D_ROPE)] = 1.0
    w_lat = jnp.concatenate([w_ukv, jnp.asarray(place, BF16)], axis=0)
    return dict(w_rest=w_rest, w_gates=w_in[:, REST_WIDTH:].astype(BF16), w_ukv=w_ukv, w_lat=w_lat,
                w_branch=w_branch.astype(BF16),
                w_out=w_out.astype(BF16), w_gate=w_gate_up[:, :D_FF].astype(BF16),
                w_up=w_gate_up[:, D_FF:].astype(BF16), w_down=w_down.astype(BF16))


def _projection_columns():
    zero_col = REST_WIDTH + GATE_WIDTH
    src = np.full(PROJ_WIDTH, zero_col, np.int32)
    o_aq, o_ckv, o_kr, o_sb, o_bd, o_ds, o_ixq, o_ixk, o_ixw = np.concatenate(
        [[0], np.cumsum(IN_SIZES[:-1])])[:9]
    hd = MLA_D_NOPE + MLA_D_ROPE

    def put(lane0, src0, n):
        src[lane0:lane0 + n] = src0 + np.arange(n)

    for h in range(N_HEADS):
        put((SEG_AQ + h) * LANES, o_aq + h * hd, hd)
    put(SEG_CKV * LANES, o_ckv, MLA_D_C)
    put(SEG_KR * LANES + MLA_ROPE_LANE, o_kr, MLA_D_ROPE)
    put(SEG_SB * LANES, o_sb, 3 * BRANCH_WIDTH)
    put(SEG_BD * LANES, o_bd, 3 * BRANCH_WIDTH)
    put(SEG_DS * LANES, o_ds, 3 * BRANCH_WIDTH)
    put(SEG_IXQ * LANES, o_ixq, IDX_HEADS * IDX_DIM)
    put(SEG_IXK * LANES, o_ixk, IDX_DIM)
    put(SEG_IXK * LANES + IXW_LANE, o_ixw, IDX_HEADS)
    return src


def _trunk_layer(x, q_off, past, wts, mla_kv_norm, band_rel_bias, ln1_g, ln1_b, ln2_g, ln2_b):
    bsz, s_len, _ = x.shape
    n = bsz * s_len
    pos = jnp.tile(q_off + jnp.arange(s_len, dtype=jnp.int32), bsz)
    x2 = x.reshape(n, D_MODEL)
    (q_a, lat_new, k_a, v_a, q_b, kv_b, sb_kv_new, q_c, kv_c, bd_kv_new,
     q_d, k_d, vt_d, ds_kv_new, q_ix, kidx_new, kx_d, w_ix) = _project(
        x2, pos, wts['w_rest'], wts['w_ukv'], mla_kv_norm[None])
    seq = lambda a: a.reshape(bsz, s_len, a.shape[-1])
    seq_t = lambda a: jnp.transpose(a.reshape(a.shape[0], bsz, s_len), (1, 0, 2))
    q_a, k_a, v_a, q_b, kv_b, q_c, kv_c, q_d, k_d, q_ix, kx_d = map(
        seq, (q_a, k_a, v_a, q_b, kv_b, q_c, kv_c, q_d, k_d, q_ix, kx_d))
    vt_d, w_ix = seq_t(vt_d), seq_t(w_ix)
    rows_last = lambda a: jnp.moveaxis(a.reshape(a.shape[0], bsz, s_len), 0, -1)
    kv_state = lambda a: rows_last(a).reshape(bsz, s_len, 2, N_HEADS, HEAD_DIM)
    lat_new, kidx_new = rows_last(lat_new), rows_last(kidx_new)
    sb_kv_new, bd_kv_new, ds_kv_new = kv_state(sb_kv_new), kv_state(bd_kv_new), kv_state(ds_kv_new)

    if past is None:
        kv_c = jnp.pad(kv_c, ((0, 0), (BAND_WINDOW, 0), (0, 0)))
        band_rows = bd_kv_new[:, s_len - min(BAND_WINDOW, s_len):]
        s_k = s_len
    else:
        past_lat, past_sb, past_band, past_ds, past_kidx = past
        p_len = past_lat.shape[1]
        s_k = p_len + s_len
        rows_bf16 = lambda a: a.reshape(bsz, a.shape[1], -1).astype(BF16)
        k_past, v_past = _latent_keys_values(jnp.transpose(past_lat, (0, 2, 1)), wts['w_lat'])
        k_a = jnp.concatenate([k_past, k_a], axis=1)
        v_a = jnp.concatenate([v_past, v_a], axis=1)
        kv_b = jnp.concatenate([rows_bf16(past_sb), kv_b], axis=1)
        kv_c = jnp.concatenate([rows_bf16(past_band), kv_c], axis=1)
        past_ds = past_ds.reshape(bsz, p_len, 2 * BRANCH_WIDTH)
        k_d = jnp.concatenate([past_ds[..., :BRANCH_WIDTH].astype(BF16), k_d], axis=1)
        vt_d = jnp.concatenate(
            [jnp.transpose(past_ds[..., BRANCH_WIDTH:], (0, 2, 1)).astype(BF16), vt_d], axis=2)
        kx_d = jnp.concatenate([past_kidx.astype(BF16), kx_d], axis=1)
        band_rows = bd_kv_new

    o_a = _mla_attention(q_a, _pad_keys(k_a), _pad_keys(v_a), q_off)
    o_b = _sb_attention(q_b, _pad_keys(kv_b), q_off)
    o_c = _band_attention(q_c, kv_c, band_rel_bias, q_off)
    n_sel = min(DSA_TOPK, s_k // 4)
    o_d = _dsa_attention(q_ix, w_ix, _pad_keys(kx_d), q_d, _pad_keys(k_d), _pad_keys(vt_d, axis=2),
                         q_off, n_sel)
    o_d = jnp.transpose(o_d, (0, 2, 1))

    flat = lambda o: o.reshape(n, BRANCH_WIDTH)
    x1 = _merge_out_ln(x2, wts['w_gates'], flat(o_a), flat(o_b), flat(o_c), flat(o_d),
                       wts['w_branch'], wts['w_out'], ln1_g[None], ln1_b[None])
    x_out = _ffn_ln(x1, wts['w_gate'], wts['w_up'], wts['w_down'], ln2_g[None], ln2_b[None])
    return x_out.reshape(bsz, s_len, D_MODEL), (lat_new, sb_kv_new, band_rows, ds_kv_new, kidx_new)


def kernel(x_prompt, x_sample, cache_mla_latent, cache_sb_kv, cache_band_kv, cache_dsa_kv, cache_dsa_kidx, w_in, mla_kv_norm, mla_w_uk, mla_w_uv, band_rel_bias, w_branch, w_out, ln1_g, ln1_b, w_gate_up, w_down, ln2_g, ln2_b):
    past_len = cache_mla_latent.shape[2]
    xp, xs = x_prompt, x_sample
    st_p, st_s = [], []
    for l in range(w_in.shape[0]):
        wts = _prepare_weights(w_in[l], mla_w_uk[l], mla_w_uv[l], w_branch[l], w_out[l],
                               w_gate_up[l], w_down[l])
        params = (wts, mla_kv_norm[l], band_rel_bias[l], ln1_g[l], ln1_b[l], ln2_g[l], ln2_b[l])
        xp, new_p = _trunk_layer(xp, 0, None, *params)
        past = (cache_mla_latent[l], cache_sb_kv[l], cache_band_kv[l], cache_dsa_kv[l],
                cache_dsa_kidx[l])
        xs, new_s = _trunk_layer(xs, past_len, past, *params)
        st_p.append(new_p)
        st_s.append(new_s)
    stack = lambda st, i: jnp.stack([s[i] for s in st])
    return (xp, xs) + tuple(stack(st_p, i) for i in range(5)) + tuple(stack(st_s, i) for i in range(5))
```

```python
import functools
import math

import numpy as np
import jax
import jax.numpy as jnp
from jax import lax
from jax.experimental import pallas as pl
from jax.experimental.pallas import tpu as pltpu

D_MODEL = 1024
CHUNK = 64
CHUNK_SHIFT = 6
N_BRANCH = 4
N_HEADS = 4
HEAD_DIM = 64
BRANCH_WIDTH = N_HEADS * HEAD_DIM
MLA_D_C = 128
MLA_D_NOPE = 64
MLA_D_ROPE = 32
MLA_THETA = 10000.0
ROPE_THETA = 500000.0
BAND_LEFT_CHUNKS = 8
BAND_WINDOW = BAND_LEFT_CHUNKS * CHUNK
REL_CLIP = 128
IDX_HEADS = 8
IDX_DIM = 64
DSA_TOPK = 256
D_FF = ((8 * D_MODEL // 3 + 255) // 256) * 256
DEPTH = 2
ALPHA = (2 * DEPTH) ** 0.25
NEG_INF = -1e30
LOG2_E = math.log2(math.e)
SB_RUN_FLOOR = -150.0
LN_EPS = 1e-5
IN_SIZES = (N_HEADS * (MLA_D_NOPE + MLA_D_ROPE), MLA_D_C, MLA_D_ROPE,
            3 * BRANCH_WIDTH, 3 * BRANCH_WIDTH, 3 * BRANCH_WIDTH,
            IDX_HEADS * IDX_DIM, IDX_DIM, IDX_HEADS, N_BRANCH * D_MODEL)
REST_WIDTH = sum(IN_SIZES[:-1])
GATE_WIDTH = IN_SIZES[-1]

LANES = 128
KEY_ALIGN = 512
INT_MIN = -2 ** 31
VMEM_LIMIT = 56 * 1024 * 1024

F32 = jnp.float32
BF16 = jnp.bfloat16


def _cparams(n_axes):
    return pltpu.CompilerParams(dimension_semantics=("arbitrary",) * n_axes,
                                vmem_limit_bytes=VMEM_LIMIT)


def _pick(n, candidates):
    for c in candidates:
        if n % c == 0:
            return c
    return n


def _ones_upper_half(rows):
    lane = lax.broadcasted_iota(jnp.int32, (rows, N_HEADS * LANES), 1)
    return jnp.where((lane & (LANES - 1)) >= HEAD_DIM, 1.0, 0.0)


def _dot_nt(a, b):
    return lax.dot_general(a, b, (((1,), (1,)), ((), ())), preferred_element_type=F32)


def _latent_kernel(lat_ref, w_ref, k_ref, v_ref):
    res = lax.dot_general(lat_ref[0].astype(BF16), w_ref[...], (((0,), (0,)), ((), ())),
                          preferred_element_type=F32)
    k_ref[0] = res[:, :N_HEADS * LANES].astype(BF16)
    v_ref[0] = (res[:, N_HEADS * LANES:] + _ones_upper_half(res.shape[0])).astype(BF16)


def _latent_keys_values(lat_t, w_lat):
    b, _, p = lat_t.shape
    tm = _pick(p, (1024, 512, 256, 128))
    width = N_HEADS * LANES
    return pl.pallas_call(
        _latent_kernel,
        grid=(b, p // tm),
        in_specs=[pl.BlockSpec((1, LAT_WIDTH, tm), lambda bi, i: (bi, 0, i)),
                  pl.BlockSpec((LAT_WIDTH, 2 * width), lambda bi, i: (0, 0))],
        out_specs=[pl.BlockSpec((1, tm, width), lambda bi, i: (bi, i, 0))] * 2,
        out_shape=[jax.ShapeDtypeStruct((b, p, width), BF16)] * 2,
        compiler_params=_cparams(2),
        name="latent_keys_values",
    )(lat_t, w_lat)


SEG_AQ = 0
SEG_CKV = 4
SEG_KR = 5
SEG_SB = 6
SEG_BD = 12
SEG_DS = 18
SEG_IXQ = 24
SEG_IXK = 28
N_SEG = 30
IXW_LANE = 96
PROJ_WIDTH = N_SEG * LANES
MLA_ROPE_LANE = 64
LAT_WIDTH = MLA_D_C + MLA_D_ROPE


def _proj_kernel(x_ref, w_ref, wukv_ref, g_ref, ca_ref, sma_ref, spa_ref, cp_ref, smp_ref, spp_ref,
                 qa_ref, lat_ref, ka_ref, va_ref,
                 qb_ref, kvb_ref, sbkv_ref, qc_ref, kvc_ref, bdkv_ref,
                 qd_ref, kd_ref, vtd_ref, dskv_ref, qix_ref, kidx_ref, kx_ref, wix_ref):
    proj = _dot_nt(x_ref[...].astype(BF16), w_ref[...])
    seg = lambda s, n=1: proj[:, s * LANES:(s + n) * LANES]

    def rope(t, c_ref, sm_ref, sp_ref, half):
        return (t * c_ref[...] + pltpu.roll(t, LANES - half, 1) * sm_ref[...]
                + pltpu.roll(t, half, 1) * sp_ref[...])

    rope_a = functools.partial(rope, c_ref=ca_ref, sm_ref=sma_ref, sp_ref=spa_ref, half=MLA_D_ROPE // 2)
    rope_p = functools.partial(rope, c_ref=cp_ref, sm_ref=smp_ref, sp_ref=spp_ref, half=HEAD_DIM // 8)
    head_scale = HEAD_DIM ** -0.5 * LOG2_E

    mla_scale = (MLA_D_NOPE + MLA_D_ROPE) ** -0.5 * LOG2_E
    for h in range(N_HEADS):
        qa_ref[:, h * LANES:(h + 1) * LANES] = (rope_a(seg(SEG_AQ + h)) * mla_scale).astype(BF16)
    ckv = seg(SEG_CKV)
    ckv = ckv * lax.rsqrt(jnp.mean(ckv * ckv, axis=-1, keepdims=True) + LN_EPS) * g_ref[...]
    kr = rope_a(seg(SEG_KR))
    lat_ref[:MLA_D_C, :] = ckv.T
    lat_ref[MLA_D_C:, :] = kr.T[MLA_ROPE_LANE:MLA_ROPE_LANE + MLA_D_ROPE, :]
    kv_a = jnp.dot(ckv.astype(BF16), wukv_ref[...], preferred_element_type=F32)
    for h in range(N_HEADS):
        ka_ref[:, h * LANES:(h + 1) * LANES] = (kv_a[:, h * LANES:(h + 1) * LANES] + kr).astype(BF16)
    va_ref[...] = (kv_a[:, N_HEADS * LANES:] + _ones_upper_half(kv_a.shape[0])).astype(BF16)

    for s0, q_ref, kv_ref, new_ref in ((SEG_SB, qb_ref, kvb_ref, sbkv_ref),
                                       (SEG_BD, qc_ref, kvc_ref, bdkv_ref)):
        q_ref[...] = (seg(s0, 2) * head_scale).astype(BF16)
        kv = seg(s0 + 2, 4)
        new_ref[...] = kv.T
        kv_ref[...] = kv.astype(BF16)

    for p in range(2):
        qd_ref[:, p * LANES:(p + 1) * LANES] = (rope_p(seg(SEG_DS + p)) * head_scale).astype(BF16)
        k_rot = rope_p(seg(SEG_DS + 2 + p))
        dskv_ref[p * LANES:(p + 1) * LANES, :] = k_rot.T
        kd_ref[:, p * LANES:(p + 1) * LANES] = k_rot.astype(BF16)
    vt = seg(SEG_DS + 4, 2).T
    dskv_ref[BRANCH_WIDTH:, :] = vt
    vtd_ref[...] = vt.astype(BF16)
    for p in range(IDX_HEADS // 2):
        qix_ref[:, p * LANES:(p + 1) * LANES] = (rope_p(seg(SEG_IXQ + p)) * IDX_DIM ** -0.5).astype(BF16)
    ixk = rope_p(seg(SEG_IXK))
    ixk_t = ixk.T
    kidx_ref[...] = ixk_t[:IDX_DIM, :]
    kx_ref[...] = ixk[:, :IDX_DIM].astype(BF16)
    wix_ref[...] = ixk_t[IXW_LANE:IXW_LANE + IDX_HEADS, :]


def _rope_tables(pos, theta, width, starts):
    half = width // 2
    inv = jnp.exp(jnp.arange(half, dtype=F32) * (-2.0 * math.log(theta) / width))
    ang = pos.astype(F32)[:, None] * inv[None, :]
    cos, sin = jnp.cos(ang), jnp.sin(ang)
    n = pos.shape[0]
    fill = lambda v, w: jnp.full((n, w), v, F32)
    c, sm, sp, lane = [], [], [], 0
    for s in sorted(starts):
        c += [fill(1.0, s - lane), cos, cos]
        sm += [fill(0.0, s - lane), -sin, fill(0.0, half)]
        sp += [fill(0.0, s - lane + half), sin]
        lane = s + width
    cat = lambda parts, v: jnp.concatenate(parts + [fill(v, LANES - lane)], axis=1)
    return cat(c, 1.0), cat(sm, 0.0), cat(sp, 0.0)


def _project(x, pos, w_rest, w_ukv, kv_norm):
    n = x.shape[0]
    ts = _pick(n, (512, 256, 128, 64))
    tables = (_rope_tables(pos, MLA_THETA, MLA_D_ROPE, (MLA_ROPE_LANE,))
              + _rope_tables(pos, ROPE_THETA, HEAD_DIM // 4, (0, HEAD_DIM)))
    row = lambda i: (i, 0)
    fixed = lambda i: (0, 0)
    rows = lambda w: pl.BlockSpec((ts, w), row)
    out = lambda w, dt: jax.ShapeDtypeStruct((n, w), dt)
    t = True
    specs = [
        (N_HEADS * LANES, BF16, 0), (LAT_WIDTH, F32, t), (N_HEADS * LANES, BF16, 0), (N_HEADS * LANES, BF16, 0),
        (BRANCH_WIDTH, BF16, 0), (2 * BRANCH_WIDTH, BF16, 0), (2 * BRANCH_WIDTH, F32, t),
        (BRANCH_WIDTH, BF16, 0), (2 * BRANCH_WIDTH, BF16, 0), (2 * BRANCH_WIDTH, F32, t),
        (BRANCH_WIDTH, BF16, 0), (BRANCH_WIDTH, BF16, 0), (BRANCH_WIDTH, BF16, t), (2 * BRANCH_WIDTH, F32, t),
        (IDX_HEADS * IDX_DIM, BF16, 0), (IDX_DIM, F32, t), (IDX_DIM, BF16, 0), (IDX_HEADS, F32, t)]
    out_specs, out_shape = [], []
    for width, dt, feature_major in specs:
        if feature_major:
            out_specs.append(pl.BlockSpec((width, ts), lambda i: (0, i)))
            out_shape.append(jax.ShapeDtypeStruct((width, n), dt))
        else:
            out_specs.append(rows(width))
            out_shape.append(out(width, dt))
    return pl.pallas_call(
        _proj_kernel,
        grid=(n // ts,),
        in_specs=[rows(D_MODEL),
                  pl.BlockSpec((PROJ_WIDTH, D_MODEL), fixed, pipeline_mode=pl.Buffered(1)),
                  pl.BlockSpec((MLA_D_C, 2 * N_HEADS * LANES), fixed),
                  pl.BlockSpec((1, MLA_D_C), fixed)] + [rows(LANES)] * 6,
        out_specs=out_specs,
        out_shape=out_shape,
        compiler_params=_cparams(1),
        name="project_prepare",
    )(x, w_rest, w_ukv, kv_norm, *tables)


def _mla_kernel(q_ref, k_ref, v_ref, o_ref, *, qb, kb, q_off, skp):
    q0 = pl.program_id(1) * qb
    hi = jnp.minimum((((q_off + q0 + qb - 1) >> CHUNK_SHIFT) + 1) * CHUNK, skp)
    nkb = (hi + kb - 1) // kb
    qpos = q_off + q0 + lax.broadcasted_iota(jnp.int32, (qb, 1), 0)
    cend = ((qpos >> CHUNK_SHIFT) + 1) * CHUNK
    col = lax.broadcasted_iota(jnp.int32, (qb, kb), 1)
    lane = lax.broadcasted_iota(jnp.int32, (qb, LANES), 1)
    n_full = jnp.minimum(((((q_off + q0) >> CHUNK_SHIFT) + 1) * CHUNK) // kb, nkb)

    def body(i, carry, masked):
        k0 = pl.multiple_of(i * kb, kb)
        s = [_dot_nt(q_ref[0, :, h * LANES:(h + 1) * LANES],
                     k_ref[0, pl.ds(k0, kb), h * LANES:(h + 1) * LANES]) for h in range(N_HEADS)]
        if masked:
            vis = col + k0 < cend
            s = [jnp.where(vis, sh, NEG_INF) for sh in s]
        p, new = [], []
        for h in range(N_HEADS):
            m, acc = carry[2 * h:2 * h + 2]
            m_new = jnp.maximum(m, jnp.max(s[h], axis=1, keepdims=True))
            p.append(jnp.exp2((s[h] - m_new).astype(BF16)))
            new += [m_new, jnp.exp2(m - m_new) * acc]
        for h in range(N_HEADS):
            vblk = v_ref[0, pl.ds(k0, kb), h * LANES:(h + 1) * LANES]
            new[2 * h + 1] = new[2 * h + 1] + jnp.dot(p[h], vblk, preferred_element_type=F32)
        return tuple(new)

    init = (jnp.full((qb, 1), NEG_INF, F32), jnp.zeros((qb, LANES), F32)) * N_HEADS
    carry = lax.fori_loop(0, n_full, functools.partial(body, masked=False), init)
    carry = lax.fori_loop(n_full, nkb, functools.partial(body, masked=True), carry)
    outs = [carry[2 * h + 1] / pltpu.roll(carry[2 * h + 1], HEAD_DIM, 1) for h in range(N_HEADS)]
    for pair in range(N_HEADS // 2):
        o_ref[0, :, pair * LANES:(pair + 1) * LANES] = jnp.where(
            lane < HEAD_DIM, outs[2 * pair], pltpu.roll(outs[2 * pair + 1], HEAD_DIM, 1))


def _mla_attention(q, k, v, q_off):
    b, sq, _ = q.shape
    skp = k.shape[1]
    qb = _pick(sq, (512, 256, 128, 64))
    kb = _pick(skp, (1024, KEY_ALIGN))
    kern = functools.partial(_mla_kernel, qb=qb, kb=kb, q_off=q_off, skp=skp)
    return pl.pallas_call(
        kern,
        grid=(b, sq // qb),
        in_specs=[pl.BlockSpec((1, qb, N_HEADS * LANES), lambda bi, qi: (bi, qi, 0)),
                  pl.BlockSpec((1, skp, N_HEADS * LANES), lambda bi, qi: (bi, 0, 0),
                               pipeline_mode=pl.Buffered(1)),
                  pl.BlockSpec((1, skp, N_HEADS * LANES), lambda bi, qi: (bi, 0, 0),
                               pipeline_mode=pl.Buffered(1))],
        out_specs=pl.BlockSpec((1, qb, BRANCH_WIDTH), lambda bi, qi: (bi, qi, 0)),
        out_shape=jax.ShapeDtypeStruct((b, sq, BRANCH_WIDTH), F32),
        compiler_params=_cparams(2),
        name="mla_attention",
    )(q, k, v)


def _head_mask(x_pair, h):
    lane = lax.broadcasted_iota(jnp.int32, x_pair.shape, 1)
    keep = (lane < HEAD_DIM) if h % 2 == 0 else (lane >= HEAD_DIM)
    return jnp.where(keep, x_pair, jnp.zeros_like(x_pair))


def _sb_kernel(q_ref, kv_ref, t_ref, o_ref, *, qb, kb, sub, q_off):
    q0 = pl.program_id(1) * qb
    hi = q_off + q0 + qb - 1
    nkb = (hi + kb - 1) // kb
    qpos = q_off + q0 + lax.broadcasted_iota(jnp.int32, (qb, 1), 0)
    col = lax.broadcasted_iota(jnp.int32, (qb, sub), 1)
    lane = lax.broadcasted_iota(jnp.int32, (qb, LANES), 1)
    tri = t_ref[...]
    n_full = jnp.minimum((q_off + q0) // kb, nkb)
    qh = [_head_mask(q_ref[0, :, (h // 2) * LANES:(h // 2 + 1) * LANES], h) for h in range(N_HEADS)]

    def body(i, carry, masked):
        k0 = pl.multiple_of(i * kb, kb)
        carry = list(carry)
        units = [(j, h) for j in reversed(range(kb // sub)) for h in range(N_HEADS)]
        strict = {j: col + (k0 + j * sub) < qpos for j in range(kb // sub)} if masked else None
        z = {}
        for j, h in units:
            pair = h // 2
            kblk = kv_ref[0, pl.ds(k0 + j * sub, sub), pair * LANES:(pair + 1) * LANES]
            z[j, h] = _dot_nt(qh[h], kblk)
        log_1m, suffix = {}, {}
        for u in units:
            nz = -z[u]
            t = jnp.minimum(nz, 0.0) - jnp.log2(1.0 + jnp.exp2(jnp.minimum(z[u], nz)))
            if masked:
                t = jnp.where(strict[u[0]], t, 0.0)
            log_1m[u] = t
        for u in units:
            hi_part = log_1m[u].astype(BF16)
            lo_part = (log_1m[u] - hi_part.astype(F32)).astype(BF16)
            suffix[u] = (jnp.dot(hi_part, tri, preferred_element_type=F32)
                         + jnp.dot(lo_part, tri, preferred_element_type=F32))
        for j, h in units:
            u = (j, h)
            run, acc = carry[2 * h:2 * h + 2]
            expo = z[u] + log_1m[u] + suffix[u] + run
            if masked:
                expo = jnp.where(strict[j], expo, NEG_INF)
            a = jnp.exp2(expo).astype(BF16)
            pair = h // 2
            vblk = kv_ref[0, pl.ds(k0 + j * sub, sub),
                          BRANCH_WIDTH + pair * LANES:BRANCH_WIDTH + (pair + 1) * LANES]
            carry[2 * h + 1] = acc + jnp.dot(a, vblk, preferred_element_type=F32)
            carry[2 * h] = run + jnp.sum(log_1m[u], axis=1, keepdims=True)
        return tuple(carry)

    def live(state):
        i, carry = state[0], state[1:]
        top = carry[0]
        for h in range(1, N_HEADS):
            top = jnp.maximum(top, carry[2 * h])
        return (i >= 0) & (jnp.max(top) >= SB_RUN_FLOOR)

    def step(masked):
        return lambda state: (state[0] - 1,) + body(state[0], state[1:], masked)

    init = (jnp.zeros((qb, 1), F32), jnp.zeros((qb, LANES), F32)) * N_HEADS
    carry = lax.fori_loop(0, nkb - n_full, lambda i, c: body(nkb - 1 - i, c, True), init)
    state = lax.while_loop(live, step(False), (n_full - 1,) + tuple(carry))
    carry = state[1:]
    outs = [carry[2 * h + 1] for h in range(N_HEADS)]
    for pair in range(N_HEADS // 2):
        o_ref[0, :, pair * LANES:(pair + 1) * LANES] = jnp.where(
            lane < HEAD_DIM, outs[2 * pair], outs[2 * pair + 1])


def _sb_attention(q, kv, q_off):
    b, sq, _ = q.shape
    skp = kv.shape[1]
    qb = _pick(sq, (256, 128, 64))
    sub = 256
    kb = KEY_ALIGN
    tri =jnp.asarray(np.tril(np.ones((sub, sub), np.float32), -1), BF16)
    kern = functools.partial(_sb_kernel, qb=qb, kb=kb, sub=sub, q_off=q_off)
    return pl.pallas_call(
        kern,
        grid=(b, sq // qb),
        in_specs=[pl.BlockSpec((1, qb, BRANCH_WIDTH), lambda bi, qi: (bi, qi, 0)),
                  pl.BlockSpec((1, skp, 2 * BRANCH_WIDTH), lambda bi, qi: (bi, 0, 0),
                               pipeline_mode=pl.Buffered(1)),
                  pl.BlockSpec((sub, sub), lambda bi, qi: (0, 0))],
        out_specs=pl.BlockSpec((1, qb, BRANCH_WIDTH), lambda bi, qi: (bi, qi, 0)),
        out_shape=jax.ShapeDtypeStruct((b, sq, BRANCH_WIDTH), F32),
        compiler_params=_cparams(2),
        name="stick_breaking_attention",
    )(q, kv, tri)


def _band_kernel(bias_ref, q_ref, kv_ref, o_ref, tile_ref, *, qb, win, q_off):
    first = (pl.program_id(0) == 0) & (pl.program_id(1) == 0)
    row = lax.broadcasted_iota(jnp.int32, (qb, win), 0)
    col = lax.broadcasted_iota(jnp.int32, (qb, win), 1)

    @pl.when(first)
    def _():
        width = win + qb
        c = lax.broadcasted_iota(jnp.int32, (8, width), 1)
        rel = jnp.where(c < win, jnp.clip(BAND_WINDOW - c, -REL_CLIP, REL_CLIP) + REL_CLIP, 2 * REL_CLIP)

        def fill(r, rows):
            hit = rel == r
            return tuple(jnp.where(hit, bias_ref[h, r], rows[h]) for h in range(N_HEADS))

        rows = lax.fori_loop(0, 2 * REL_CLIP + 1, fill, (jnp.zeros((8, width), F32),) * N_HEADS)
        qch = row >> CHUNK_SHIFT
        kch = col >> CHUNK_SHIFT
        in_band = (kch >= qch) & (kch <= qch + BAND_LEFT_CHUNKS)
        for h in range(N_HEADS):
            table = jnp.broadcast_to(rows[h][:1], (qb, width))
            skewed = pltpu.roll(table, 0, 1, stride=1, stride_axis=0)[:, :win]
            tile_ref[h] = jnp.where(in_band, skewed * LOG2_E, NEG_INF)

    q0 = pl.multiple_of(pl.program_id(1) * qb, qb)
    kpos = col + (q_off + q0 - BAND_WINDOW)
    lane = lax.broadcasted_iota(jnp.int32, (qb, LANES), 1)
    outs = []
    for h in range(N_HEADS):
        pair = h // 2
        qh = _head_mask(q_ref[0, :, pair * LANES:(pair + 1) * LANES], h)
        kwin = kv_ref[0, pl.ds(q0, win), pair * LANES:(pair + 1) * LANES]
        s = _dot_nt(qh, kwin) + tile_ref[h]
        s = jnp.where(kpos >= 0, s, NEG_INF)
        m = jnp.max(s, axis=1, keepdims=True)
        p = jnp.exp2(s - m)
        l = jnp.sum(p, axis=1, keepdims=True)
        vwin = kv_ref[0, pl.ds(q0, win),
                      BRANCH_WIDTH + pair * LANES:BRANCH_WIDTH + (pair + 1) * LANES]
        outs.append(jnp.dot(p.astype(BF16), vwin, preferred_element_type=F32) / l)
    for pair in range(N_HEADS // 2):
        o_ref[0, :, pair * LANES:(pair + 1) * LANES] = jnp.where(
            lane < HEAD_DIM, outs[2 * pair], outs[2 * pair + 1])


def _band_attention(q, kv, rel_bias, q_off):
    b, sq, _ = q.shape
    sk = kv.shape[1]
    qb = _pick(sq, (256, 128, 64))
    win = qb + BAND_WINDOW
    kern = functools.partial(_band_kernel, qb=qb, win=win, q_off=q_off)
    return pl.pallas_call(
        kern,
        grid=(b, sq // qb),
        in_specs=[pl.BlockSpec(memory_space=pltpu.SMEM),
                  pl.BlockSpec((1, qb, BRANCH_WIDTH), lambda bi, qi: (bi, qi, 0)),
                  pl.BlockSpec((1, sk, 2 * BRANCH_WIDTH), lambda bi, qi: (bi, 0, 0),
                               pipeline_mode=pl.Buffered(1))],
        out_specs=pl.BlockSpec((1, qb, BRANCH_WIDTH), lambda bi, qi: (bi, qi, 0)),
        out_shape=jax.ShapeDtypeStruct((b, sq, BRANCH_WIDTH), F32),
        scratch_shapes=[pltpu.VMEM((N_HEADS, qb, win), F32)],
        compiler_params=_cparams(2),
        name="band_attention",
    )(rel_bias, q, kv)


def _dsa_kernel(qi_ref, w_ref, kx_ref, q_ref, k_ref, vt_ref, o_ref, key_ref, key16_ref, cut_ref,
                sa_ref, sb_ref,
                *, qb, kb, q_off, skp, n_sel):
    q0 = pl.program_id(1) * qb
    hi = jnp.minimum((((q_off + q0 + qb - 1) >> CHUNK_SHIFT) + 1) * CHUNK, skp)
    nkb = (hi + kb - 1) // kb
    qpos = q_off + q0 + lax.broadcasted_iota(jnp.int32, (1, qb), 1)
    cend = ((qpos >> CHUNK_SHIFT) + 1) * CHUNK
    row = lax.broadcasted_iota(jnp.int32, (kb, qb), 0)
    float_key = lambda bits: jnp.where(bits < 0, INT_MIN - bits, bits)
    neg_key = float_key(lax.bitcast_convert_type(jnp.full((1, 1), NEG_INF, F32), jnp.int32))
    n_full = jnp.minimum(((((q_off + q0) >> CHUNK_SHIFT) + 1) * CHUNK) // kb, nkb)

    def score_block(i, c, masked):
        k0 = pl.multiple_of(i * kb, kb)
        kx = kx_ref[0, pl.ds(k0, kb), :]
        r = [_dot_nt(kx, qi_ref[0, :, j * IDX_DIM:(j + 1) * IDX_DIM]) for j in range(IDX_HEADS)]
        score = jnp.zeros((kb, qb), F32)
        for j in range(IDX_HEADS):
            wj = w_ref[0, j:j + 1, :] * (IDX_HEADS ** -0.5)
            score = score + wj * jnp.maximum(r[j], 0.0)
        if masked:
            score = jnp.where(row + k0 < cend, score, NEG_INF)
        keys = float_key(lax.bitcast_convert_type(score, jnp.int32))
        key_ref[pl.ds(k0, kb), :] = keys
        key16_ref[pl.ds(k0, kb), :] = (keys >> 16).astype(jnp.int16)
        return c

    lax.fori_loop(0, n_full, functools.partial(score_block, masked=False), 0)
    lax.fori_loop(n_full, nkb, functools.partial(score_block, masked=True), 0)

    slab = 64 if kb % 64 == 0 else kb

    def count_ge(cand):
        def blk(i, cnt):
            k0 = pl.multiple_of(i * kb, kb)
            for s in range(kb // slab):
                keys = key_ref[pl.ds(k0 + s * slab, slab), :]
                cnt = cnt + jnp.where(keys >= cand, 1.0, 0.0)
            return cnt
        part = lax.fori_loop(0, nkb, blk, jnp.zeros((slab, qb), F32))
        return jnp.sum(part, axis=0, keepdims=True)

    slab16 = 128 if kb % 128 == 0 else kb

    def count16_ge(cand16):
        def blk(i, cnt):
            k0 = pl.multiple_of(i * kb, kb)
            for s in range(kb // slab16):
                k16 = key16_ref[pl.ds(k0 + s * slab16, slab16), :]
                cnt = cnt + jnp.where(k16 >= cand16, jnp.int16(1), jnp.int16(0))
            return cnt
        part = lax.fori_loop(0, nkb, blk, jnp.zeros((slab16, qb), jnp.int16))
        return jnp.sum(part.astype(F32), axis=0, keepdims=True)

    def undecided(t_end):
        def cond(state):
            t, _, n_ge = state
            return (t < t_end) & (jnp.max(jnp.abs(n_ge - n_sel)) > 0.0)
        return cond

    def bisect(to_16bit):
        def step(state):
            t, thr, n_ge = state
            cand = thr + jnp.left_shift(jnp.int32(1), 31 - t)
            cnt = count16_ge(to_16bit(cand))
            ok = cnt >= n_sel
            return t + 1, jnp.where(ok, cand, thr), jnp.where(ok, cnt, n_ge)
        return step

    stored = (nkb * kb).astype(F32)
    high_step = bisect(lambda cand: (cand >> 16).astype(jnp.int16))
    state = lax.fori_loop(
        0, 16, lambda _, st: high_step(st),
        (jnp.int32(0), jnp.full((1, qb), INT_MIN, jnp.int32), jnp.full((1, qb), stored, F32)))

    @pl.when(undecided(32)(state))
    def _():
        thr_hi = state[1] >> 16

        def low_halves(i, c):
            k0 = pl.multiple_of(i * kb, kb)
            keys = key_ref[pl.ds(k0, kb), :]
            hi16 = keys >> 16
            low = jnp.where(hi16 == thr_hi, (keys & 0xFFFF) - 0x8000,
                            jnp.where(hi16 > thr_hi, 0x7FFF, -0x8000))
            key16_ref[pl.ds(k0, kb), :] = low.astype(jnp.int16)
            return c

        lax.fori_loop(0, nkb, low_halves, 0)

    low_step = bisect(lambda cand: ((cand & 0xFFFF) - 0x8000).astype(jnp.int16))
    _, thr, n_ge = lax.while_loop(
        undecided(32), lambda st: low_step(low_step(low_step(low_step(st)))), state)
    excess = jnp.where((n_ge > n_sel) & (thr > neg_key), 1.0, 0.0)
    cut_ref[...] = jnp.full((1, qb), skp, jnp.int32)

    @pl.when(jnp.max(excess) > 0.0)
    def _():
        need = n_sel - count_ge(thr + 1)

        def count_tied_below(limit):
            def blk(i, cnt):
                k0 = pl.multiple_of(i * kb, kb)
                tied = (key_ref[pl.ds(k0, kb), :] == thr) & (row + k0 < limit)
                return cnt + jnp.sum(jnp.where(tied, 1.0, 0.0), axis=0, keepdims=True)
            return lax.fori_loop(0, nkb, blk, jnp.zeros((1, qb), F32))

        n_bits = max(1, (skp - 1).bit_length())

        def bisect_cut(t, cut):
            cand = cut + jnp.left_shift(jnp.int32(1), n_bits - 1 - t)
            return jnp.where(count_tied_below(cand) < need, cand, cut)

        cut = lax.fori_loop(0, n_bits, bisect_cut, jnp.zeros((1, qb), jnp.int32))
        cut_ref[...] = jnp.where(excess > 0.0, cut + 1, skp)

    cut = cut_ref[...]

    qh = [_head_mask(q_ref[0, :, (h // 2) * LANES:(h // 2 + 1) * LANES], h) for h in range(N_HEADS)]

    kb3 = kb // 2
    row3 = lax.broadcasted_iota(jnp.int32, (kb3, qb), 0)

    def scores(i):
        k0 = pl.multiple_of(i * kb3, kb3)
        return tuple(_dot_nt(k_ref[0, pl.ds(k0, kb3), (h // 2) * LANES:(h // 2 + 1) * LANES], qh[h])
                     for h in range(N_HEADS))

    def attend(i, s, carry, plain):
        k0 = pl.multiple_of(i * kb3, kb3)
        keys = key_ref[pl.ds(k0, kb3), :]
        if plain:
            sel = keys >= thr
        else:
            kpos = row3 + k0
            sel = (keys >= thr) & ((keys != thr) | (kpos < cut)) & (kpos < cend)
        bias = jnp.where(sel, 0.0, NEG_INF)
        p, new = [], []
        for h in range(N_HEADS):
            m, acc = carry[2 * h:2 * h + 2]
            sh = s[h] + bias
            m_new = jnp.maximum(m, jnp.max(sh, axis=0, keepdims=True))
            p.append(jnp.exp2((sh - m_new).astype(BF16)))
            new += [m_new, jnp.exp2(m - m_new) * acc]
        for h in range(N_HEADS):
            vblk = jnp.concatenate([vt_ref[0, h * HEAD_DIM:(h + 1) * HEAD_DIM, pl.ds(k0, kb3)],
                                    jnp.ones((LANES - HEAD_DIM, kb3), BF16)], axis=0)
            new[2 * h + 1] = new[2 * h + 1] + jnp.dot(vblk, p[h], preferred_element_type=F32)
        return tuple(new)

    def put_scores(slot_ref, i):
        for h, sh in enumerate(scores(i)):
            slot_ref[h] = sh

    def attend_from(slot_ref, i, carry, plain):
        return attend(i, tuple(slot_ref[h] for h in range(N_HEADS)), carry, plain)

    def body(j, carry, plain):
        put_scores(sb_ref, 2 * j + 1)
        carry = attend_from(sa_ref, 2 * j, carry, plain)
        put_scores(sa_ref, 2 * j + 2)
        return attend_from(sb_ref, 2 * j + 1, carry, plain)

    init = (jnp.full((1, qb), NEG_INF, F32), jnp.zeros((LANES, qb), F32)) * N_HEADS
    n_pairs = nkb
    n_plain = jnp.where(jnp.max(excess) > 0.0, 0, jnp.minimum(n_full, n_pairs - 1))
    put_scores(sa_ref, 0)
    carry = lax.fori_loop(0, n_plain, functools.partial(body, plain=True), init)
    carry = lax.fori_loop(n_plain, n_pairs - 1, functools.partial(body, plain=False), carry)
    last = 2 * (n_pairs - 1)
    put_scores(sb_ref, last + 1)
    carry = attend_from(sa_ref, last, carry, False)
    carry = attend_from(sb_ref, last + 1, carry, False)
    for h in range(N_HEADS):
        acc = carry[2 * h + 1]
        o_ref[0, h * HEAD_DIM:(h + 1) * HEAD_DIM, :] = acc[:HEAD_DIM, :] / acc[HEAD_DIM:HEAD_DIM + 1, :]


def _dsa_attention(qi, w, kx, q, k, vt, q_off, n_sel):
    b, sq, _ = q.shape
    skp = k.shape[1]
    qb = _pick(sq, (256, 128, 64))
    kb = _pick(skp, (1024, KEY_ALIGN))
    kern = functools.partial(_dsa_kernel, qb=qb, kb=kb, q_off=q_off, skp=skp, n_sel=n_sel)
    return pl.pallas_call(
        kern,
        grid=(b, sq // qb),
        in_specs=[pl.BlockSpec((1, qb, IDX_HEADS * IDX_DIM), lambda bi, qi_: (bi, qi_, 0)),
                  pl.BlockSpec((1, IDX_HEADS, qb), lambda bi, qi_: (bi, 0, qi_)),
                  pl.BlockSpec((1, skp, IDX_DIM), lambda bi, qi_: (bi, 0, 0),
                               pipeline_mode=pl.Buffered(1)),
                  pl.BlockSpec((1, qb, BRANCH_WIDTH), lambda bi, qi_: (bi, qi_, 0)),
                  pl.BlockSpec((1, skp, BRANCH_WIDTH), lambda bi, qi_: (bi, 0, 0),
                               pipeline_mode=pl.Buffered(1)),
                  pl.BlockSpec((1, BRANCH_WIDTH, skp), lambda bi, qi_: (bi, 0, 0),
                               pipeline_mode=pl.Buffered(1))],
        out_specs=pl.BlockSpec((1, BRANCH_WIDTH, qb), lambda bi, qi_: (bi, 0, qi_)),
        out_shape=jax.ShapeDtypeStruct((b, BRANCH_WIDTH, sq), F32),
        scratch_shapes=[pltpu.VMEM((skp, qb), jnp.int32), pltpu.VMEM((skp, qb), jnp.int16),
                        pltpu.VMEM((1, qb), jnp.int32),
                        pltpu.VMEM((N_HEADS, kb // 2, qb), F32), pltpu.VMEM((N_HEADS, kb // 2, qb), F32)],
        compiler_params=_cparams(2),
        name="dsa_attention",
    )(qi, w, kx, q, k, vt)


def _layer_norm(z, g, b):
    mu = jnp.mean(z, axis=-1, keepdims=True)
    zc = z - mu
    var = jnp.mean(zc * zc, axis=-1, keepdims=True)
    return zc * lax.rsqrt(var + LN_EPS) * g + b


def _merge_kernel(x_ref, wgate_ref, oa_ref, ob_ref, oc_ref, od_ref, wb_ref, wo_ref, g_ref, b_ref,
                  y_ref):
    xb = x_ref[...].astype(BF16)
    logits = [_dot_nt(xb, wgate_ref[n * D_MODEL:(n + 1) * D_MODEL, :]) for n in range(N_BRANCH)]
    branch = [jnp.dot(o_ref[...].astype(BF16), wb_ref[n], preferred_element_type=F32)
              for n, o_ref in enumerate((oa_ref, ob_ref, oc_ref, od_ref))]
    merged = jax.nn.sigmoid(logits[0]) * branch[0]
    for n in range(1, N_BRANCH):
        merged = merged + jax.nn.sigmoid(logits[n]) * branch[n]
    y = jnp.dot(merged.astype(BF16), wo_ref[...], preferred_element_type=F32)
    y_ref[...] = _layer_norm(ALPHA * x_ref[...] + y, g_ref[...], b_ref[...])


def _merge_out_ln(x, w_gate, o_a, o_b, o_c, o_d, w_branch, w_out, g, b):
    n = x.shape[0]
    ts = _pick(n, (512, 256, 128, 64))
    row = lambda i: (i, 0)
    obs = pl.BlockSpec((ts, BRANCH_WIDTH), row)
    return pl.pallas_call(
        _merge_kernel,
        grid=(n // ts,),
        in_specs=[pl.BlockSpec((ts, D_MODEL), row),
                  pl.BlockSpec((GATE_WIDTH, D_MODEL), lambda i: (0, 0), pipeline_mode=pl.Buffered(1)),
                  obs, obs, obs, obs,
                  pl.BlockSpec((N_BRANCH, BRANCH_WIDTH, D_MODEL), lambda i: (0, 0, 0)),
                  pl.BlockSpec((D_MODEL, D_MODEL), lambda i: (0, 0)),
                  pl.BlockSpec((1, D_MODEL), lambda i: (0, 0)),
                  pl.BlockSpec((1, D_MODEL), lambda i: (0, 0))],
        out_specs=pl.BlockSpec((ts, D_MODEL), row),
        out_shape=jax.ShapeDtypeStruct((n, D_MODEL), F32),
        compiler_params=_cparams(1),
        name="merge_out_ln",
    )(x, w_gate, o_a, o_b, o_c, o_d, w_branch, w_out, g, b)


def _ffn_kernel(x_ref, wg_ref, wu_ref, wd_ref, g_ref, b_ref, y_ref, acc_ref):
    f = pl.program_id(1)
    xb = x_ref[...].astype(BF16)
    gate = jnp.dot(xb, wg_ref[...], preferred_element_type=F32)
    up = jnp.dot(xb, wu_ref[...], preferred_element_type=F32)
    hidden = (gate * jax.nn.sigmoid(gate) * up).astype(BF16)
    part = jnp.dot(hidden, wd_ref[...], preferred_element_type=F32)

    @pl.when(f == 0)
    def _():
        acc_ref[...] = part

    @pl.when(f > 0)
    def _():
        acc_ref[...] += part

    @pl.when(f == pl.num_programs(1) - 1)
    def _():
        y_ref[...] = _layer_norm(ALPHA * x_ref[...] + acc_ref[...], g_ref[...], b_ref[...])


def _ffn_ln(x, w_gate, w_up, w_down, g, b):
    n = x.shape[0]
    ts = _pick(n, (1024, 512, 256, 128, 64))
    tf = D_FF // 2
    return pl.pallas_call(
        _ffn_kernel,
        grid=(n // ts, D_FF // tf),
        in_specs=[pl.BlockSpec((ts, D_MODEL), lambda i, f: (i, 0)),
                  pl.BlockSpec((D_MODEL, tf), lambda i, f: (0, f)),
                  pl.BlockSpec((D_MODEL, tf), lambda i, f: (0, f)),
                  pl.BlockSpec((tf, D_MODEL), lambda i, f: (f, 0)),
                  pl.BlockSpec((1, D_MODEL), lambda i, f: (0, 0)),
                  pl.BlockSpec((1, D_MODEL), lambda i, f: (0, 0))],
        out_specs=pl.BlockSpec((ts, D_MODEL), lambda i, f: (i, 0)),
        out_shape=jax.ShapeDtypeStruct((n, D_MODEL), F32),
        scratch_shapes=[pltpu.VMEM((ts, D_MODEL), F32)],
        compiler_params=_cparams(2),
        name="ffn_ln",
    )(x, w_gate, w_up, w_down, g, b)


def _pad_keys(a, axis=1):
    n = a.shape[axis]
    pad = (-n) % KEY_ALIGN
    if pad == 0:
        return a
    widths = [(0, 0)] * a.ndim
    widths[axis] = (0, pad)
    return jnp.pad(a, widths)


def _prepare_weights(w_in, mla_w_uk, mla_w_uv, w_branch, w_out, w_gate_up, w_down):
    w_in_t = jnp.transpose(w_in).astype(BF16)
    w_rest = jnp.concatenate([w_in_t, jnp.zeros((1, D_MODEL), BF16)], axis=0)
    w_rest = jnp.take(w_rest, _projection_columns(), axis=0)
    w_uk = jnp.pad(mla_w_uk, ((0, 0), (0, 0), (0, LANES - MLA_D_NOPE))).reshape(MLA_D_C, -1)
    w_uv = jnp.pad(mla_w_uv, ((0, 0), (0, 0), (0, LANES - HEAD_DIM))).reshape(MLA_D_C, -1)
    w_ukv = jnp.concatenate([w_uk, w_uv], axis=1).astype(BF16)
    place = np.zeros((MLA_D_ROPE, 2 * N_HEADS * LANES), np.float32)
    for h in range(N_HEADS):
        place[np.arange(MLA_D_ROPE), h * LANES + MLA_ROPE_LANE + np.arange(MLA_D_ROPE)] = 1.0
    w_lat = jnp.concatenate([w_ukv, jnp.asarray(place, BF16)], axis=0)
    return dict(w_rest=w_rest, w_gates=w_in_t[REST_WIDTH:], w_ukv=w_ukv, w_lat=w_lat,
                w_branch=w_branch.astype(BF16),
                w_out=w_out.astype(BF16), w_gate=w_gate_up[:, :D_FF].astype(BF16),
                w_up=w_gate_up[:, D_FF:].astype(BF16), w_down=w_down.astype(BF16))


def _projection_columns():
    zero_col = REST_WIDTH + GATE_WIDTH
    src = np.full(PROJ_WIDTH, zero_col, np.int32)
    o_aq, o_ckv, o_kr, o_sb, o_bd, o_ds, o_ixq, o_ixk, o_ixw = np.concatenate(
        [[0], np.cumsum(IN_SIZES[:-1])])[:9]
    hd = MLA_D_NOPE + MLA_D_ROPE

    def put(lane0, src0, n):
        src[lane0:lane0 + n] = src0 + np.arange(n)

    for h in range(N_HEADS):
        put((SEG_AQ + h) * LANES, o_aq + h * hd, hd)
    put(SEG_CKV * LANES, o_ckv, MLA_D_C)
    put(SEG_KR * LANES + MLA_ROPE_LANE, o_kr, MLA_D_ROPE)
    put(SEG_SB * LANES, o_sb, 3 * BRANCH_WIDTH)
    put(SEG_BD * LANES, o_bd, 3 * BRANCH_WIDTH)
    put(SEG_DS * LANES, o_ds, 3 * BRANCH_WIDTH)
    put(SEG_IXQ * LANES, o_ixq, IDX_HEADS * IDX_DIM)
    put(SEG_IXK * LANES, o_ixk, IDX_DIM)
    put(SEG_IXK * LANES + IXW_LANE, o_ixw, IDX_HEADS)
    return src


def _trunk_layer(x, q_off, past, wts, mla_kv_norm, band_rel_bias, ln1_g, ln1_b, ln2_g, ln2_b):
    bsz, s_len, _ = x.shape
    n = bsz * s_len
    pos = jnp.tile(q_off + jnp.arange(s_len, dtype=jnp.int32), bsz)
    x2 = x.reshape(n, D_MODEL)
    (q_a, lat_new, k_a, v_a, q_b, kv_b, sb_kv_new, q_c, kv_c, bd_kv_new,
     q_d, k_d, vt_d, ds_kv_new, q_ix, kidx_new, kx_d, w_ix) = _project(
        x2, pos, wts['w_rest'], wts['w_ukv'], mla_kv_norm[None])
    seq = lambda a: a.reshape(bsz, s_len, a.shape[-1])
    seq_t = lambda a: jnp.transpose(a.reshape(a.shape[0], bsz, s_len), (1, 0, 2))
    q_a, k_a, v_a, q_b, kv_b, q_c, kv_c, q_d, k_d, q_ix, kx_d = map(
        seq, (q_a, k_a, v_a, q_b, kv_b, q_c, kv_c, q_d, k_d, q_ix, kx_d))
    vt_d, w_ix = seq_t(vt_d), seq_t(w_ix)
    rows_last = lambda a: jnp.moveaxis(a.reshape(a.shape[0], bsz, s_len), 0, -1)
    kv_state = lambda a: rows_last(a).reshape(bsz, s_len, 2, N_HEADS, HEAD_DIM)
    lat_new, kidx_new = rows_last(lat_new), rows_last(kidx_new)
    sb_kv_new, bd_kv_new, ds_kv_new = kv_state(sb_kv_new), kv_state(bd_kv_new), kv_state(ds_kv_new)

    if past is None:
        kv_c = jnp.pad(kv_c, ((0, 0), (BAND_WINDOW, 0), (0, 0)))
        band_rows = bd_kv_new[:, s_len - min(BAND_WINDOW, s_len):]
        s_k = s_len
    else:
        past_lat, past_sb, past_band, past_ds, past_kidx = past
        p_len = past_lat.shape[1]
        s_k = p_len + s_len
        rows_bf16 = lambda a: a.reshape(bsz, a.shape[1], -1).astype(BF16)
        k_past, v_past = _latent_keys_values(jnp.transpose(past_lat, (0, 2, 1)), wts['w_lat'])
        k_a = jnp.concatenate([k_past, k_a], axis=1)
        v_a = jnp.concatenate([v_past, v_a], axis=1)
        kv_b = jnp.concatenate([rows_bf16(past_sb), kv_b], axis=1)
        kv_c = jnp.concatenate([rows_bf16(past_band), kv_c], axis=1)
        past_ds = past_ds.reshape(bsz, p_len, 2 * BRANCH_WIDTH)
        k_d = jnp.concatenate([past_ds[..., :BRANCH_WIDTH].astype(BF16), k_d], axis=1)
        vt_d = jnp.concatenate(
            [jnp.transpose(past_ds[..., BRANCH_WIDTH:], (0, 2, 1)).astype(BF16), vt_d], axis=2)
        kx_d = jnp.concatenate([past_kidx.astype(BF16), kx_d], axis=1)
        band_rows = bd_kv_new

    o_a = _mla_attention(q_a, _pad_keys(k_a), _pad_keys(v_a), q_off)
    o_b = _sb_attention(q_b, _pad_keys(kv_b), q_off)
    o_c = _band_attention(q_c, kv_c, band_rel_bias, q_off)
    n_sel = min(DSA_TOPK, s_k // 4)
    o_d = _dsa_attention(q_ix, w_ix, _pad_keys(kx_d), q_d, _pad_keys(k_d), _pad_keys(vt_d, axis=2),
                         q_off, n_sel)
    o_d = jnp.transpose(o_d, (0, 2, 1))

    flat = lambda o: o.reshape(n, BRANCH_WIDTH)
    x1 = _merge_out_ln(x2, wts['w_gates'], flat(o_a), flat(o_b), flat(o_c), flat(o_d),
                       wts['w_branch'], wts['w_out'], ln1_g[None], ln1_b[None])
    x_out = _ffn_ln(x1, wts['w_gate'], wts['w_up'], wts['w_down'], ln2_g[None], ln2_b[None])
    return x_out.reshape(bsz, s_len, D_MODEL), (lat_new, sb_kv_new, band_rows, ds_kv_new, kidx_new)


def kernel(x_prompt, x_sample, cache_mla_latent, cache_sb_kv, cache_band_kv, cache_dsa_kv, cache_dsa_kidx, w_in, mla_kv_norm, mla_w_uk, mla_w_uv, band_rel_bias, w_branch, w_out, ln1_g, ln1_b, w_gate_up, w_down, ln2_g, ln2_b):
    past_len = cache_mla_latent.shape[2]
    xp, xs = x_prompt, x_sample
    st_p, st_s = [], []
    for l in range(w_in.shape[0]):
        wts = _prepare_weights(w_in[l], mla_w_uk[l], mla_w_uv[l], w_branch[l], w_out[l],
                               w_gate_up[l], w_down[l])
        params = (wts, mla_kv_norm[l], band_rel_bias[l], ln1_g[l], ln1_b[l], ln2_g[l], ln2_b[l])
        xp, new_p = _trunk_layer(xp, 0, None, *params)
        past = (cache_mla_latent[l], cache_sb_kv[l], cache_band_kv[l], cache_dsa_kv[l],
                cache_dsa_kidx[l])
        xs, new_s = _trunk_layer(xs, past_len, past, *params)
        st_p.append(new_p)
        st_s.append(new_s)
    stack = lambda st, i: jnp.stack([s[i] for s in st])
    return (xp, xs) + tuple(stack(st_p, i) for i in range(5)) + tuple(stack(st_s, i) for i in range(5))
```
